```python
import math
import jax
import jax.numpy as jnp
from jax import lax
import numpy as np

D_MODEL = 1024
BATCH = 8
SEQ = 4096
DEPTH = 4

A_HEADS = 8
A_KV_GROUPS = 2
A_HEAD_DIM = 64
CMP_BLOCK = 32
CMP_STRIDE = 16
SLC_BLOCK = 64
SLC_TOPK = 16
N_LOCAL_BLOCKS = 2
WINDOW = 512
NSA_Q_BLOCK = 64
FORCE_SCORE = 1e9
B_HEADS = 8
Q_LORA = 256
KV_LORA = 128
NOPE_DIM = 64
ROPE_DIM = 32
V_DIM = 64
ROPE_THETA = 10000.0
Q_BLOCK = 128
C_HEADS = 8
C_HEAD_DIM = 64
C_WIDTH = C_HEADS * C_HEAD_DIM
DECAY_LORA = 64
AAA_LORA = 64
GATE_LORA = 128
GN_EPS = 64e-5
N_BUCKETS = 32
MAX_DISTANCE = 128
D_FF = 2816
CONV_WIDTH = 3
EPS = 1e-6

A_Q = A_HEADS * A_HEAD_DIM
A_KV = A_KV_GROUPS * A_HEAD_DIM
A_GATE = 3 * A_HEADS
A_COLS = A_Q + 6 * A_KV + A_GATE
B_COLS = Q_LORA + KV_LORA + ROPE_DIM
C_COLS = 3 * C_WIDTH + DECAY_LORA + AAA_LORA + GATE_LORA
MERGE_COLS = 3 * D_MODEL
IN_COLS = A_COLS + B_COLS + C_COLS + MERGE_COLS
A_OUT = A_HEADS * A_HEAD_DIM
B_OUT = B_HEADS * V_DIM
C_OUT = C_WIDTH
MIX_WIDTH = A_OUT + B_OUT + C_OUT

kernel_name = 'hybrid_nsa_mla_rwkv7_gated_merge'


def _split(x, sizes):
    return jnp.split(x, np.cumsum(sizes)[:-1].tolist(), axis=-1)


def rms_norm(x, gain):
    xf = x.astype(jnp.float32)
    y = xf * lax.rsqrt(jnp.mean(xf * xf, axis=-1, keepdims=True) + EPS)
    return (y * gain.astype(jnp.float32)).astype(x.dtype)


def masked_softmax(logits, mask):
    logits = jnp.where(mask, logits.astype(jnp.float32), -1e30)
    m = jnp.max(logits, axis=-1, keepdims=True)
    p = jnp.where(mask, jnp.exp(logits - m), 0.0)
    return p / jnp.maximum(jnp.sum(p, axis=-1, keepdims=True), 1e-30)


def t5_bucket(dist):
    max_exact = N_BUCKETS // 2
    d = jnp.maximum(dist, 0)
    large = max_exact + (jnp.log(jnp.maximum(d, 1).astype(jnp.float32) / max_exact)
                         / math.log(MAX_DISTANCE / max_exact) * (N_BUCKETS - max_exact)).astype(jnp.int32)
    return jnp.where(d < max_exact, d, jnp.minimum(large, N_BUCKETS - 1))


def shift_right(x):
    return jnp.pad(x, ((0, 0), (1, 0), (0, 0)))[:, :x.shape[1]]


def causal_dwconv(x, w, b):
    K, C = w.shape
    y = lax.conv_general_dilated(x, w[:, None, :].astype(x.dtype), window_strides=(1,),
                                 padding=[(K - 1, 0)], dimension_numbers=('NWC', 'WIO', 'NWC'),
                                 feature_group_count=C)
    return y + b


def rope(x, pos):
    half = x.shape[-1] // 2
    inv = ROPE_THETA ** (-jnp.arange(half, dtype=jnp.float32) / half)
    ang = pos.astype(jnp.float32)[:, None] * inv
    cos, sin = jnp.cos(ang)[None, :, None, :], jnp.sin(ang)[None, :, None, :]
    x1, x2 = x[..., :half], x[..., half:]
    return jnp.concatenate([x1 * cos - x2 * sin, x1 * sin + x2 * cos], axis=-1).astype(x.dtype)


def nsa_attention(q, k_cmp, v_cmp, k_slc, v_slc, k_win, v_win, gates, cmp_pos, cmp_w1, cmp_w2, rel_bias):
    B, S = q.shape[:2]
    G, HPG, Dh = A_KV_GROUPS, A_HEADS // A_KV_GROUPS, A_HEAD_DIM
    Qb = NSA_Q_BLOCK
    scale = Dh ** -0.5
    n_cmp = (S - CMP_BLOCK) // CMP_STRIDE + 1
    blk = np.arange(n_cmp)[:, None] * CMP_STRIDE + np.arange(CMP_BLOCK)[None, :]

    def compress(t, j):
        tb = t[:, blk] + cmp_pos[j][None, None, :, None, :]
        tb = tb.transpose(0, 1, 3, 2, 4).reshape(B, n_cmp, G, CMP_BLOCK * Dh)
        return jax.nn.gelu(tb @ cmp_w1[j]) @ cmp_w2[j]

    kc, vc = compress(k_cmp, 0), compress(v_cmp, 1)
    cmp_end = jnp.arange(n_cmp) * CMP_STRIDE + CMP_BLOCK - 1
    n_slc = S // SLC_BLOCK
    r_s, r_c = SLC_BLOCK // CMP_STRIDE, CMP_BLOCK // CMP_STRIDE
    cidx = (r_s * np.arange(n_slc)[:, None, None] - np.arange(r_s)[None, :, None]
            - np.arange(r_c)[None, None, :]).reshape(n_slc, -1)
    M = jnp.asarray((cidx[:, :, None] == np.arange(n_cmp)[None, None, :]).sum(1).T.astype(np.float32))
    k_sel = min(SLC_TOPK, n_slc)
    ks_blocks = k_slc.reshape(B, n_slc, SLC_BLOCK, G, Dh).transpose(0, 3, 1, 2, 4)
    vs_blocks = v_slc.reshape(B, n_slc, SLC_BLOCK, G, Dh).transpose(0, 3, 1, 2, 4)
    kw_pad = jnp.pad(k_win, ((0, 0), (WINDOW, 0), (0, 0), (0, 0)))
    vw_pad = jnp.pad(v_win, ((0, 0), (WINDOW, 0), (0, 0), (0, 0)))
    rel_g = rel_bias.reshape(N_BUCKETS, G, HPG)
    b_ar = jnp.arange(B)[:, None, None, None]
    g_ar = jnp.arange(G)[None, :, None, None]
    blk_id = jnp.arange(n_slc)

    def bias_hq(dist):
        return rel_bias[t5_bucket(dist)].transpose(2, 0, 1).reshape(G, HPG, *dist.shape)

    def block(i):
        start = i * Qb
        qb = lax.dynamic_slice_in_dim(q, start, Qb, 1).reshape(B, Qb, G, HPG, Dh)
        gb = lax.dynamic_slice_in_dim(gates, start, Qb, 1).reshape(B, Qb, G, HPG, 3)
        t = start + jnp.arange(Qb)
        dist_c = t[:, None] - cmp_end[None, :]
        s_c = jnp.einsum('bqghd,bngd->bghqn', qb, kc) * scale + bias_hq(dist_c)
        p_c = masked_softmax(s_c, dist_c >= 0)
        o_c = jnp.einsum('bghqn,bngd->bqghd', p_c.astype(vc.dtype), vc)
        imp = jnp.einsum('bghqn,nj->bgqj', p_c, M)
        cur = t // SLC_BLOCK
        back = cur[:, None] - blk_id[None, :]
        forced = (blk_id[None, :] == 0) | ((back >= 0) & (back < N_LOCAL_BLOCKS))
        score = jnp.where(forced, FORCE_SCORE, jnp.where(back >= 0, imp, -1.0))
        _, idx = lax.top_k(score, k_sel)
        ks = ks_blocks[b_ar, g_ar, idx].reshape(B, G, Qb, k_sel * SLC_BLOCK, Dh)
        vs = vs_blocks[b_ar, g_ar, idx].reshape(B, G, Qb, k_sel * SLC_BLOCK, Dh)
        kpos = (idx[..., None] * SLC_BLOCK + jnp.arange(SLC_BLOCK)).reshape(B, G, Qb, k_sel * SLC_BLOCK)
        dist_s = t[None, None, :, None] - kpos
        bias_s = rel_g[t5_bucket(dist_s), g_ar].transpose(0, 1, 4, 2, 3)
        s_s = jnp.einsum('bqghd,bgqkd->bghqk', qb, ks) * scale + bias_s
        p_s = masked_softmax(s_s, (dist_s >= 0)[:, :, None])
        o_s = jnp.einsum('bghqk,bgqkd->bqghd', p_s.astype(vs.dtype), vs)
        kw = lax.dynamic_slice_in_dim(kw_pad, start, Qb + WINDOW, 1)
        vw = lax.dynamic_slice_in_dim(vw_pad, start, Qb + WINDOW, 1)
        wpos = start - WINDOW + jnp.arange(Qb + WINDOW)
        dist_w = t[:, None] - wpos[None, :]
        mask_w = (dist_w >= 0) & (dist_w < WINDOW) & (wpos[None, :] >= 0)
        s_w = jnp.einsum('bqghd,bkgd->bghqk', qb, kw) * scale + bias_hq(dist_w)
        p_w = masked_softmax(s_w, mask_w)
        o_w = jnp.einsum('bghqk,bkgd->bqghd', p_w.astype(vw.dtype), vw)
        return gb[..., 0:1] * o_c + gb[..., 1:2] * o_s + gb[..., 2:3] * o_w

    out = lax.map(block, jnp.arange(S // Qb))
    return out.transpose(1, 0, 2, 3, 4, 5).reshape(B, S, A_HEADS * Dh)


def causal_attention(q, k, v):
    B, S, H, Dqk = q.shape
    scale = Dqk ** -0.5
    kpos = jnp.arange(S)

    def block(i):
        qb = lax.dynamic_slice_in_dim(q, i * Q_BLOCK, Q_BLOCK, 1)
        s = jnp.einsum('bqhd,bkhd->bhqk', qb, k) * scale
        qpos = i * Q_BLOCK + jnp.arange(Q_BLOCK)
        p = masked_softmax(s, kpos[None, :] <= qpos[:, None])
        return jnp.einsum('bhqk,bkhd->bqhd', p.astype(v.dtype), v)

    out = lax.map(block, jnp.arange(S // Q_BLOCK))
    return out.transpose(1, 0, 2, 3, 4).reshape(B, S, H * v.shape[-1])


def mla_branch(qa, kva, kr, q_norm, kv_norm, w_uq, w_ukv):
    B, S = qa.shape[:2]
    pos = jnp.arange(S)
    q = (rms_norm(qa, q_norm) @ w_uq).reshape(B, S, B_HEADS, NOPE_DIM + ROPE_DIM)
    kv = (rms_norm(kva, kv_norm) @ w_ukv).reshape(B, S, B_HEADS, NOPE_DIM + V_DIM)
    q = jnp.concatenate([q[..., :NOPE_DIM], rope(q[..., NOPE_DIM:], pos)], axis=-1)
    k_rope = jnp.broadcast_to(rope(kr[:, :, None, :], pos), (B, S, B_HEADS, ROPE_DIM))
    k = jnp.concatenate([kv[..., :NOPE_DIM], k_rope], axis=-1)
    return causal_attention(q, k, kv[..., NOPE_DIM:])


def rwkv7_branch(p, mu, w0, a0, k_k, k_a, w2, a2, g2, r_k, ln_x):
    B, S = p.shape[:2]
    p = p + (shift_right(p) - p) * mu
    r, k, v, wd, ad, gd = _split(p, [C_WIDTH] * 3 + [DECAY_LORA, AAA_LORA, GATE_LORA])
    w = w0 + jnp.tanh(wd) @ w2
    decay = jnp.exp(-jnp.exp(-jax.nn.softplus(-w) - 0.5))
    a = jax.nn.sigmoid(a0 + ad @ a2)
    g = jax.nn.sigmoid(gd) @ g2
    heads = lambda t: t.reshape(B, S, C_HEADS, C_HEAD_DIM)
    kk = heads(k * k_k)
    kk = kk / jnp.maximum(jnp.linalg.norm(kk, axis=-1, keepdims=True), 1e-12)
    k = k * (1.0 + (a - 1.0) * k_a)
    r_h, k_h, v_h, a_h, d_h = heads(r), heads(k), heads(v), heads(a), heads(decay)

    def step(state, inp):
        r_t, d_t, k_t, v_t, kk_t, a_t = inp
        sa = jnp.einsum('bhij,bhj->bhi', state, -kk_t)
        state = (state * d_t[:, :, None, :] + sa[..., None] * (kk_t * a_t)[:, :, None, :]
                 + v_t[..., None] * k_t[:, :, None, :])
        return state, jnp.einsum('bhij,bhj->bhi', state, r_t)

    xs = tuple(jnp.moveaxis(t.astype(jnp.float32), 1, 0) for t in (r_h, d_h, k_h, v_h, kk, a_h))
    state0 = jnp.zeros((B, C_HEADS, C_HEAD_DIM, C_HEAD_DIM), jnp.float32)
    _, y = lax.scan(step, state0, xs)
    y = jnp.moveaxis(y, 0, 1)
    mean = jnp.mean(y, axis=-1, keepdims=True)
    var = jnp.mean(jnp.square(y - mean), axis=-1, keepdims=True)
    y = ((y - mean) * lax.rsqrt(var + GN_EPS)).reshape(B, S, C_WIDTH) * ln_x[0] + ln_x[1]
    bonus = (jnp.sum(r_h * k_h * r_k, axis=-1, keepdims=True) * v_h).reshape(B, S, C_WIDTH)
    return (y.astype(p.dtype) + bonus) * g


def token_mixer(h, w_in, cmp_pos, cmp_w1, cmp_w2, rel_bias, q_norm, kv_norm, w_uq, w_ukv,
                mu, w0, a0, k_k, k_a, w2, a2, g2, r_k, ln_x, w_branch, w_out):
    B, S, _ = h.shape
    z = h @ w_in
    za, zb, zc, zg = _split(z, [A_COLS, B_COLS, C_COLS, MERGE_COLS])
    aq, akc, avc, aks, avs, akw, avw, agate = _split(za, [A_Q] + [A_KV] * 6 + [A_GATE])
    grp = lambda t: t.reshape(B, S, A_KV_GROUPS, A_HEAD_DIM)
    y_a = nsa_attention(aq.reshape(B, S, A_HEADS, A_HEAD_DIM), grp(akc), grp(avc), grp(aks), grp(avs),
                        grp(akw), grp(avw), jax.nn.sigmoid(agate).reshape(B, S, A_HEADS, 3),
                        cmp_pos, cmp_w1, cmp_w2, rel_bias)
    qa, kva, kr = _split(zb, [Q_LORA, KV_LORA, ROPE_DIM])
    y_b = mla_branch(qa, kva, kr, q_norm, kv_norm, w_uq, w_ukv)
    y_c = rwkv7_branch(zc, mu, w0, a0, k_k, k_a, w2, a2, g2, r_k, ln_x)
    g_a, g_b, g_c = jnp.split(jax.nn.sigmoid(zg), 3, axis=-1)
    w_a, w_b, w_c = w_branch[:A_OUT], w_branch[A_OUT:A_OUT + B_OUT], w_branch[A_OUT + B_OUT:]
    merged = g_a * (y_a @ w_a) + g_b * (y_b @ w_b) + g_c * (y_c @ w_c)
    return merged @ w_out


def conv_ffn(h, w_up, conv_w, conv_b, w_down):
    u = causal_dwconv(h @ w_up, conv_w, conv_b)
    gate, val = jnp.split(u, 2, axis=-1)
    return (jax.nn.gelu(gate) * val) @ w_down


def setup_inputs(seed: int = 0) -> dict:
    key = jax.random.key(seed)
    keys = iter(jax.random.split(key, 40))
    L, D = DEPTH, D_MODEL

    def nrm(shape, scale):
        return jax.random.normal(next(keys), shape, jnp.float32) * scale

    return {
        'x': nrm((BATCH, SEQ, D), 1.0),
        'c': nrm((BATCH, D), 1.0),
        'rel_bias': nrm((N_BUCKETS, A_HEADS), 0.5),
        'ada_w': nrm((L, D, 6 * D), 0.5 * D ** -0.5),
        'ada_b': nrm((L, 6 * D), 0.02),
        'norm_gain': 1.0 + nrm((L, 4, D), 0.05),
        'w_in': nrm((L, D, IN_COLS), D ** -0.5),
        'nsa_cmp_pos': nrm((L, 2, CMP_BLOCK, A_HEAD_DIM), 0.1),
        'nsa_cmp_w1': nrm((L, 2, CMP_BLOCK * A_HEAD_DIM, A_HEAD_DIM), (CMP_BLOCK * A_HEAD_DIM) ** -0.5),
        'nsa_cmp_w2': nrm((L, 2, A_HEAD_DIM, A_HEAD_DIM), A_HEAD_DIM ** -0.5),
        'mla_q_norm': 1.0 + nrm((L, Q_LORA), 0.05),
        'mla_kv_norm': 1.0 + nrm((L, KV_LORA), 0.05),
        'mla_w_uq': nrm((L, Q_LORA, B_HEADS * (NOPE_DIM + ROPE_DIM)), Q_LORA ** -0.5),
        'mla_w_ukv': nrm((L, KV_LORA, B_HEADS * (NOPE_DIM + V_DIM)), KV_LORA ** -0.5),
        'rwkv_mu': jax.random.uniform(next(keys), (L, C_COLS), jnp.float32),
        'rwkv_w0': jax.random.uniform(next(keys), (L, C_WIDTH), jnp.float32, -6.0, 1.0),
        'rwkv_a0': nrm((L, C_WIDTH), 0.1),
        'rwkv_k_k': 0.85 + nrm((L, C_WIDTH), 0.05),
        'rwkv_k_a': 1.0 + nrm((L, C_WIDTH), 0.05),
        'rwkv_w2': nrm((L, DECAY_LORA, C_WIDTH), DECAY_LORA ** -0.5),
        'rwkv_a2': nrm((L, AAA_LORA, C_WIDTH), AAA_LORA ** -0.5),
        'rwkv_g2': nrm((L, GATE_LORA, C_WIDTH), GATE_LORA ** -0.5),
        'rwkv_r_k': nrm((L, C_HEADS, C_HEAD_DIM), 0.1),
        'rwkv_ln': jnp.stack([1.0 + nrm((L, C_WIDTH), 0.05), nrm((L, C_WIDTH), 0.02)], axis=1),
        'w_branch': nrm((L, MIX_WIDTH, D), A_OUT ** -0.5),
        'w_out': nrm((L, D, D), D ** -0.5),
        'ffn_up': nrm((L, D, 2 * D_FF), D ** -0.5),
        'ffn_conv_w': nrm((L, CONV_WIDTH, 2 * D_FF), CONV_WIDTH ** -0.5),
        'ffn_conv_b': nrm((L, 2 * D_FF), 0.02),
        'ffn_down': nrm((L, D_FF, D), D_FF ** -0.5),
    }


def reference(x, c, rel_bias, ada_w, ada_b, norm_gain, w_in, nsa_cmp_pos, nsa_cmp_w1, nsa_cmp_w2,
              mla_q_norm, mla_kv_norm, mla_w_uq, mla_w_ukv, rwkv_mu, rwkv_w0, rwkv_a0, rwkv_k_k,
              rwkv_k_a, rwkv_w2, rwkv_a2, rwkv_g2, rwkv_r_k, rwkv_ln, w_branch, w_out,
              ffn_up, ffn_conv_w, ffn_conv_b, ffn_down):
    cond = jax.nn.silu(c)
    for l in range(DEPTH):
        mod = (cond @ ada_w[l] + ada_b[l])[:, None, :]
        sh1, sc1, gt1, sh2, sc2, gt2 = jnp.split(mod, 6, axis=-1)
        h = rms_norm(x, norm_gain[l, 0]) * (1.0 + sc1) + sh1
        y = token_mixer(h, w_in[l], nsa_cmp_pos[l], nsa_cmp_w1[l], nsa_cmp_w2[l], rel_bias,
                        mla_q_norm[l], mla_kv_norm[l], mla_w_uq[l], mla_w_ukv[l],
                        rwkv_mu[l], rwkv_w0[l], rwkv_a0[l], rwkv_k_k[l], rwkv_k_a[l],
                        rwkv_w2[l], rwkv_a2[l], rwkv_g2[l], rwkv_r_k[l], rwkv_ln[l],
                        w_branch[l], w_out[l])
        x = x + gt1 * rms_norm(y, norm_gain[l, 1])
        h = rms_norm(x, norm_gain[l, 2]) * (1.0 + sc2) + sh2
        f = conv_ffn(h, ffn_up[l], ffn_conv_w[l], ffn_conv_b[l], ffn_down[l])
        x = x + gt2 * rms_norm(f, norm_gain[l, 3])
    return x
```

```python
import functools
import math

import jax
import jax.numpy as jnp
import numpy as np
from jax import lax
from jax.experimental import pallas as pl
from jax.experimental.pallas import tpu as pltpu

F32 = jnp.float32
BF16 = jnp.bfloat16
HIGHEST = lax.Precision.HIGHEST

D_MODEL = 1024
DEPTH = 4
A_HEADS, A_KV_GROUPS, A_HEAD_DIM = 8, 2, 64
CMP_BLOCK, CMP_STRIDE = 32, 16
SLC_BLOCK, SLC_TOPK, N_LOCAL_BLOCKS = 64, 16, 2
WINDOW = 512
FORCE_SCORE = 1e9
B_HEADS, Q_LORA, KV_LORA, NOPE_DIM, ROPE_DIM, V_DIM = 8, 256, 128, 64, 32, 64
ROPE_THETA = 10000.0
C_HEADS, C_HEAD_DIM = 8, 64
C_WIDTH = C_HEADS * C_HEAD_DIM
DECAY_LORA, AAA_LORA, GATE_LORA = 64, 64, 128
GN_EPS = 64e-5
N_BUCKETS, MAX_DISTANCE = 32, 128
D_FF = 2816
EPS = 1e-6
NEG = -1e30

A_Q = A_HEADS * A_HEAD_DIM
A_KV = A_KV_GROUPS * A_HEAD_DIM
A_GATE = 3 * A_HEADS
A_COLS = A_Q + 6 * A_KV + A_GATE
B_COLS = Q_LORA + KV_LORA + ROPE_DIM
C_COLS = 3 * C_WIDTH + DECAY_LORA + AAA_LORA + GATE_LORA

LANES = 128
VMEM_LIMIT = 48 * 1024 * 1024

Z_AQ = 0
Z_QA = 512
Z_AKC, Z_AVC, Z_AKS, Z_AVS, Z_AKW, Z_AVW = 768, 896, 1024, 1152, 1280, 1408
Z_AG = 1536
Z_KVA = 1664
Z_KRA = 1792
Z_KRB = 1920
Z_CR, Z_CK, Z_CV = 2048, 2560, 3072
Z_CWA = 3584
Z_CG = 3712
Z_ZG = 4096
Z_COLS = 7168

TQ = 256
CH = 64


def _cparams(*sem):
    return pltpu.CompilerParams(dimension_semantics=sem, vmem_limit_bytes=VMEM_LIMIT)


def _gelu(x):
    return 0.5 * x * (1.0 + jnp.tanh(0.7978845608028654 * (x + 0.044715 * (x * x * x))))


def _dot(a, b, **kw):
    return jnp.dot(a, b, preferred_element_type=F32, **kw)


def _dot_t(a, b):
    return lax.dot_general(a, b, (((1,), (1,)), ((), ())), preferred_element_type=F32)


def _adaln_kernel(c_ref, w_ref, b_ref, o_ref):
    c = c_ref[...]
    cond = c * jax.nn.sigmoid(c)
    o_ref[0] = _dot(cond.astype(BF16), w_ref[0].astype(BF16)) + b_ref[0]


def _adaln(c, ada_w, ada_b):
    L, D, N6 = ada_w.shape
    B = c.shape[0]
    tn = 1536
    return pl.pallas_call(
        _adaln_kernel,
        grid=(L, N6 // tn),
        in_specs=[pl.BlockSpec((B, D), lambda l, j: (0, 0)),
                  pl.BlockSpec((1, D, tn), lambda l, j: (l, 0, j)),
                  pl.BlockSpec((1, 1, tn), lambda l, j: (l, 0, j))],
        out_specs=pl.BlockSpec((1, B, tn), lambda l, j: (l, 0, j)),
        out_shape=jax.ShapeDtypeStruct((L, B, N6), F32),
        compiler_params=_cparams("parallel", "parallel"),
        name="adaln",
    )(c, ada_w, ada_b.reshape(L, 1, N6))


def _nmm_kernel(x_ref, g_ref, sc_ref, sh_ref, w_ref, o_ref, h_ref):
    @pl.when(pl.program_id(1) == 0)
    def _():
        x = x_ref[...]
        y = x * lax.rsqrt(jnp.mean(x * x, axis=-1, keepdims=True) + EPS)
        h = (y * g_ref[...]) * (1.0 + sc_ref[0]) + sh_ref[0]
        h_ref[...] = h.astype(BF16)

    o_ref[...] = _dot(h_ref[...], w_ref[...]).astype(o_ref.dtype)


def _norm_mod_matmul(x2, gain, sc, sh, w, S, tm, tn, out_dtype=F32):
    N, D = x2.shape
    NC = w.shape[1]
    tpb = S // tm
    return pl.pallas_call(
        _nmm_kernel,
        grid=(N // tm, NC // tn),
        in_specs=[pl.BlockSpec((tm, D), lambda i, j: (i, 0)),
                  pl.BlockSpec((1, D), lambda i, j: (0, 0)),
                  pl.BlockSpec((1, 1, D), lambda i, j: (i // tpb, 0, 0)),
                  pl.BlockSpec((1, 1, D), lambda i, j: (i // tpb, 0, 0)),
                  pl.BlockSpec((D, tn), lambda i, j: (0, j))],
        out_specs=pl.BlockSpec((tm, tn), lambda i, j: (i, j)),
        out_shape=jax.ShapeDtypeStruct((N, NC), out_dtype),
        scratch_shapes=[pltpu.VMEM((tm, D), BF16)],
        compiler_params=_cparams("parallel", "arbitrary"),
        name="norm_mod_matmul",
    )(x2, gain.reshape(1, D), sc, sh, w)


def _compress_kernel(zk_ref, zv_ref, w1_ref, pos_ref, w1c_ref, w2_ref, kc_ref, vc_ref, *, NC):
    for kind, (z_ref, o_ref) in enumerate(((zk_ref, kc_ref), (zv_ref, vc_ref))):
        pa = jnp.zeros((NC, LANES), F32)
        pb = jnp.zeros((NC, LANES), F32)
        for l in range(CMP_STRIDE):
            xl = z_ref[pl.ds(l, NC, stride=CMP_STRIDE), :].astype(BF16)
            pa = pa + _dot(xl, w1_ref[kind, l])
            pb = pb + _dot(xl, w1_ref[kind, CMP_STRIDE + l])
        posb = _dot(pos_ref[kind].astype(BF16), w1c_ref[kind])[0:1, :]
        h = pa + pltpu.roll(pb, NC - 1, axis=0) + posb
        act = _gelu(h).astype(BF16)
        row = lax.broadcasted_iota(jnp.int32, (NC, LANES), 0)
        for g in range(A_KV_GROUPS):
            for eo in range(2):
                out = _dot(act, w2_ref[kind, g, eo])
                out = jnp.where(row < NC - 1, out, 0.0)
                o_ref[g, eo, pl.ds(0, NC), :] = jnp.zeros((NC, LANES), BF16)
                o_ref[g, eo, pl.ds(NC, NC), :] = out.astype(BF16)


def _compress(z, w1bd, posrow, w1cat, w2v, B, S):
    NC = S // CMP_STRIDE
    out = jax.ShapeDtypeStruct((B, A_KV_GROUPS, 2, 2 * NC, LANES), BF16)
    ospec = pl.BlockSpec((None, A_KV_GROUPS, 2, 2 * NC, LANES), lambda b: (b, 0, 0, 0, 0))
    full = lambda a: pl.BlockSpec(a.shape, lambda b: (0,) * a.ndim)
    return pl.pallas_call(
        functools.partial(_compress_kernel, NC=NC),
        grid=(B,),
        in_specs=[pl.BlockSpec((S, LANES), lambda b: (b, Z_AKC // LANES)),
                  pl.BlockSpec((S, LANES), lambda b: (b, Z_AVC // LANES)),
                  full(w1bd), full(posrow), full(w1cat), full(w2v)],
        out_specs=[ospec, ospec],
        out_shape=[out, out],
        compiler_params=_cparams("parallel"),
        name="nsa_compress",
    )(z, z, w1bd, posrow, w1cat, w2v)


def _stack_pairs(zq, scale):
    q = zq * scale
    return jnp.concatenate([q[:, :LANES], q[:, LANES:]], axis=0).astype(BF16)


def _cmpsel_kernel(zq_ref, kc_ref, vc_ref, cb_ref, oc_ref, sb_ref, *, NC, NS):
    qt = pl.program_id(2)
    r = TQ // CMP_STRIDE
    lhs = _stack_pairs(zq_ref[...], A_HEAD_DIM ** -0.5)
    st = pl.multiple_of(r * (qt + 1), 16)
    col = lax.broadcasted_iota(jnp.int32, (2 * TQ, NC), 1)
    exists = col >= NC - r * (qt + 1)
    acc = jnp.zeros((2 * TQ, LANES), F32)
    psum = jnp.zeros((TQ, NC), F32)
    for eo in range(2):
        kwin = kc_ref[eo, pl.ds(st, NC), :]
        vwin = vc_ref[eo, pl.ds(st, NC), :]
        logits = _dot_t(lhs, kwin) + cb_ref[eo]
        logits = jnp.where(exists, logits, NEG)
        m = jnp.max(logits, axis=-1, keepdims=True)
        p = jnp.where(logits > 0.1 * NEG, jnp.exp(logits - m), 0.0)
        l = jnp.maximum(jnp.sum(p, axis=-1, keepdims=True), 1e-30)
        p = p * (1.0 / l)
        acc = acc + _dot(p.astype(BF16), vwin)
        psum = psum + p[:TQ] + p[TQ:]
    oc_ref[:, :LANES] = acc[:TQ]
    oc_ref[:, LANES:] = acc[TQ:]

    jj = lax.broadcasted_iota(jnp.int32, (LANES, NC), 0)
    nn = lax.broadcasted_iota(jnp.int32, (LANES, NC), 1) + (r * (qt + 1) - NC)
    delta = 4 * jj - nn
    mt = jnp.where((delta == 0) | (delta == 4), 1.0, 0.0) + jnp.where((delta >= 1) & (delta <= 3), 2.0, 0.0)
    mt = jnp.where(jj < NS, mt, 0.0)
    imp_t = lax.dot_general(mt, psum, (((1,), (1,)), ((), ())), preferred_element_type=F32,
                            precision=HIGHEST)
    jb = lax.broadcasted_iota(jnp.int32, (LANES, TQ), 0)
    tpos = qt * TQ + lax.broadcasted_iota(jnp.int32, (LANES, TQ), 1)
    back = (tpos >> 6) - jb
    forced = (jb == 0) | ((back >= 0) & (back < N_LOCAL_BLOCKS))
    score = jnp.where(forced, FORCE_SCORE, jnp.where(back >= 0, imp_t, -1.0))
    score = jnp.where(jb < NS, score, -2.0)
    rank = jnp.zeros((LANES, TQ), F32)
    for jp in range(NS):
        row = score[jp:jp + 1, :]
        beats = (row > score) | ((row == score) & (jb > jp))
        rank = rank + jnp.where(beats, 1.0, 0.0)
    sel = (rank < float(min(SLC_TOPK, NS))) & (back >= 0) & (jb < NS)
    sb_t = jnp.where(sel, 0.0, jnp.where(jb < NS, NEG, 0.0))
    sb_ref[...] = sb_t.T.astype(BF16)


def _cmpsel(z, kcp, vcp, cb, B, S):
    NC = S // CMP_STRIDE
    NS = S // SLC_BLOCK
    nq = S // TQ
    kspec = pl.BlockSpec((None, None, 2, 2 * NC, LANES), lambda b, g, q: (b, g, 0, 0, 0))
    return pl.pallas_call(
        functools.partial(_cmpsel_kernel, NC=NC, NS=NS),
        grid=(B, A_KV_GROUPS, nq),
        in_specs=[pl.BlockSpec((TQ, 2 * LANES), lambda b, g, q: (b * nq + q, g)),
                  kspec, kspec,
                  pl.BlockSpec((None, 2, 2 * TQ, NC), lambda b, g, q: (g, 0, 0, 0))],
        out_specs=[pl.BlockSpec((TQ, 2 * LANES), lambda b, g, q: (b * nq + q, g)),
                   pl.BlockSpec((None, None, TQ, LANES), lambda b, g, q: (b, g, q, 0))],
        out_shape=[jax.ShapeDtypeStruct((B * S, A_Q), F32),
                   jax.ShapeDtypeStruct((B, A_KV_GROUPS, S, LANES), BF16)],
        compiler_params=_cparams("parallel", "parallel", "parallel"),
        name="nsa_cmp_select",
    )(z, kcp, vcp, cb)


def _flash_update(s, v, m_ref, l_ref, acc_ref):
    m_prev = m_ref[...]
    m_new = jnp.maximum(m_prev, jnp.max(s, axis=-1, keepdims=True))
    alpha = jnp.exp(m_prev - m_new)
    p = jnp.exp(s - m_new)
    l_ref[...] = alpha * l_ref[...] + jnp.sum(p, axis=-1, keepdims=True)
    acc_ref[...] = alpha * acc_ref[...] + _dot(p.astype(BF16), v)
    m_ref[...] = m_new


def _split_heads_kv(slab, g):
    lane = lax.broadcasted_iota(jnp.int32, slab.shape, 1)
    own = jnp.where((lane < A_HEAD_DIM) == (g == 0), slab, 0.0)
    other = pltpu.roll(own, A_HEAD_DIM, axis=1)
    is0 = g == 0
    return jnp.where(is0, own, other), jnp.where(is0, other, own)


def _nsa_flash_kernel(*refs, S, selected):
    if selected:
        (zq_ref, sb_ref, zk_ref, zv_ref, tab_ref, o_ref,
         ke_ref, ko_ref, ve_ref, vo_ref, m_ref, l_ref, acc_ref) = refs
    else:
        (zq_ref, zk_ref, zv_ref, tab_ref, o_ref,
         ke_ref, ko_ref, ve_ref, vo_ref, m_ref, l_ref, acc_ref) = refs
    g = pl.program_id(1)
    qt = pl.program_id(2)
    k_refs = (ke_ref, ko_ref)
    v_refs = (ve_ref, vo_ref)

    @pl.when(qt == 0)
    def _():
        ke, ko = _split_heads_kv(zk_ref[...], g)
        ve, vo = _split_heads_kv(zv_ref[...], g)
        ke_ref[:, :LANES] = ke.astype(BF16)
        ko_ref[:, :LANES] = ko.astype(BF16)
        ve_ref[...] = ve.astype(BF16)
        vo_ref[...] = vo.astype(BF16)
        if selected:
            rowb = lax.broadcasted_iota(jnp.int32, (S, LANES), 0) >> 6
            lane = lax.broadcasted_iota(jnp.int32, (S, LANES), 1)
            onehot = jnp.where(rowb == lane, 1.0, 0.0).astype(BF16)
            ke_ref[:, LANES:] = onehot
            ko_ref[:, LANES:] = onehot

    lhs = _stack_pairs(zq_ref[...], A_HEAD_DIM ** -0.5)
    if selected:
        sb = sb_ref[...]
        lhs = jnp.concatenate([lhs, jnp.concatenate([sb, sb], axis=0)], axis=1)

    m_ref[...] = jnp.full(m_ref.shape, NEG, F32)
    l_ref[...] = jnp.zeros(l_ref.shape, F32)
    acc_ref[...] = jnp.zeros(acc_ref.shape, F32)

    def chunk(kc, ti):
        ks = pl.multiple_of(kc * TQ, TQ)
        for eo in range(2):
            s = _dot_t(lhs, k_refs[eo][pl.ds(ks, TQ), :]) + tab_ref[eo, ti]
            _flash_update(s, v_refs[eo][pl.ds(ks, TQ), :], m_ref.at[eo], l_ref.at[eo], acc_ref.at[eo])

    chunk(qt, 0)

    @pl.when(qt >= 1)
    def _():
        chunk(qt - 1, 1)

    if selected:
        def body(kc, carry):
            chunk(kc, 2)
            return carry
        lax.fori_loop(0, jnp.maximum(qt - 1, 0), body, 0)
    else:
        @pl.when(qt >= 2)
        def _():
            chunk(qt - 2, 2)

    out = acc_ref[0] * (1.0 / l_ref[0]) + acc_ref[1] * (1.0 / l_ref[1])
    o_ref[:, :LANES] = out[:TQ]
    o_ref[:, LANES:] = out[TQ:]


def _nsa_flash(z, sb, tab, zk_col, zv_col, B, S, selected):
    nq = S // TQ
    kw = 2 * LANES if selected else LANES
    in_specs = [pl.BlockSpec((TQ, 2 * LANES), lambda b, g, q: (b * nq + q, g))]
    args = [z]
    if selected:
        in_specs.append(pl.BlockSpec((None, None, TQ, LANES), lambda b, g, q: (b, g, q, 0)))
        args.append(sb)
    in_specs += [pl.BlockSpec((S, LANES), lambda b, g, q: (b, zk_col // LANES)),
                 pl.BlockSpec((S, LANES), lambda b, g, q: (b, zv_col // LANES)),
                 pl.BlockSpec((None, 2, 3, 2 * TQ, TQ), lambda b, g, q: (g, 0, 0, 0, 0))]
    args += [z, z, tab]
    return pl.pallas_call(
        functools.partial(_nsa_flash_kernel, S=S, selected=selected),
        grid=(B, A_KV_GROUPS, nq),
        in_specs=in_specs,
        out_specs=pl.BlockSpec((TQ, 2 * LANES), lambda b, g, q: (b * nq + q, g)),
        out_shape=jax.ShapeDtypeStruct((B * S, A_Q), F32),
        scratch_shapes=[pltpu.VMEM((S, kw), BF16), pltpu.VMEM((S, kw), BF16),
                        pltpu.VMEM((S, LANES), BF16), pltpu.VMEM((S, LANES), BF16),
                        pltpu.VMEM((2, 2 * TQ, 1), F32), pltpu.VMEM((2, 2 * TQ, 1), F32),
                        pltpu.VMEM((2, 2 * TQ, LANES), F32)],
        compiler_params=_cparams("parallel", "parallel", "arbitrary"),
        name="nsa_selected" if selected else "nsa_window",
    )(*args)


def _mla_proj_kernel(zqa_ref, zkva_ref, zkra_ref, zkrb_ref, qn_ref, kvn_ref, wqa_ref, wqb_ref,
                     wk_ref, wv_ref, caq_ref, cbq_ref, cak_ref, cbk_ref, q_ref, k_ref, v_ref):
    def rms(x, gain):
        return x * lax.rsqrt(jnp.mean(x * x, axis=-1, keepdims=True) + EPS) * gain

    nq = rms(zqa_ref[...], qn_ref[...]).astype(BF16)
    nkv = rms(zkva_ref[...], kvn_ref[...]).astype(BF16)
    qa = _dot(nq, wqa_ref[...])
    qb = _dot(nq, wqb_ref[...])
    kn = _dot(nkv, wk_ref[...])
    vv = _dot(nkv, wv_ref[...])
    caq, cbq = caq_ref[...], cbq_ref[...]
    kr = zkra_ref[...] * cak_ref[...] + zkrb_ref[...] * cbk_ref[...]
    for h in range(B_HEADS):
        sl = slice(h * LANES, (h + 1) * LANES)
        q_ref[h] = (qa[:, sl] * caq + qb[:, sl] * cbq).astype(BF16)
        k_ref[h] = (kn[:, sl] + kr).astype(BF16)
        v_ref[h] = vv[:, sl].astype(BF16)


def _mla_proj(z, qn, kvn, wqa, wqb, wk, wv, caq, cbq, cak, cbk, B, S, tm=256):
    nt = S // tm
    zspec = lambda w, col: pl.BlockSpec((tm, w), lambda b, i: (b * nt + i, col // w))
    full = lambda a: pl.BlockSpec(a.shape, lambda b, i: (0,) * a.ndim)
    tspec = pl.BlockSpec((tm, LANES), lambda b, i: (i, 0))
    ospec = pl.BlockSpec((None, B_HEADS, tm, LANES), lambda b, i: (b, 0, i, 0))
    oshape = jax.ShapeDtypeStruct((B, B_HEADS, S, LANES), BF16)
    return pl.pallas_call(
        _mla_proj_kernel,
        grid=(B, nt),
        in_specs=[zspec(Q_LORA, Z_QA), zspec(LANES, Z_KVA), zspec(LANES, Z_KRA), zspec(LANES, Z_KRB),
                  full(qn), full(kvn), full(wqa), full(wqb), full(wk), full(wv),
                  tspec, tspec, tspec, tspec],
        out_specs=[ospec, ospec, ospec],
        out_shape=[oshape, oshape, oshape],
        compiler_params=_cparams("parallel", "parallel"),
        name="mla_proj",
    )(z, z, z, z, qn, kvn, wqa, wqb, wk, wv, caq, cbq, cak, cbk)


def _mla_flash_kernel(q_ref, k_ref, v_ref, o_ref, m_ref, l_ref, acc_ref):
    qt = pl.program_id(2)
    m_ref[...] = jnp.full(m_ref.shape, NEG, F32)
    l_ref[...] = jnp.zeros(l_ref.shape, F32)
    acc_ref[...] = jnp.zeros(acc_ref.shape, F32)
    ri = lax.broadcasted_iota(jnp.int32, (TQ, TQ), 0)
    ci = lax.broadcasted_iota(jnp.int32, (TQ, TQ), 1)
    causal = ci <= ri
    for hh in range(2):
        q = q_ref[hh]
        ks = pl.multiple_of(qt * TQ, TQ)
        s = _dot_t(q, k_ref[hh, pl.ds(ks, TQ), :])
        s = jnp.where(causal, s, NEG)
        _flash_update(s, v_ref[hh, pl.ds(ks, TQ), :], m_ref.at[hh], l_ref.at[hh], acc_ref.at[hh])

        def body(kc, carry):
            k0 = pl.multiple_of(kc * TQ, TQ)
            sc = _dot_t(q, k_ref[hh, pl.ds(k0, TQ), :])
            _flash_update(sc, v_ref[hh, pl.ds(k0, TQ), :], m_ref.at[hh], l_ref.at[hh], acc_ref.at[hh])
            return carry
        lax.fori_loop(0, qt, body, 0)
    o_ref[...] = acc_ref[0] * (1.0 / l_ref[0]) + acc_ref[1] * (1.0 / l_ref[1])


def _mla_flash(q, k, v, B, S):
    nq = S // TQ
    hp = B_HEADS // 2
    return pl.pallas_call(
        _mla_flash_kernel,
        grid=(B, hp, nq),
        in_specs=[pl.BlockSpec((None, 2, TQ, LANES), lambda b, h, i: (b, h, i, 0)),
                  pl.BlockSpec((None, 2, S, LANES), lambda b, h, i: (b, h, 0, 0)),
                  pl.BlockSpec((None, 2, S, LANES), lambda b, h, i: (b, h, 0, 0))],
        out_specs=pl.BlockSpec((TQ, LANES), lambda b, h, i: (b * nq + i, h)),
        out_shape=jax.ShapeDtypeStruct((B * S, B_HEADS * V_DIM), F32),
        scratch_shapes=[pltpu.VMEM((2, TQ, 1), F32), pltpu.VMEM((2, TQ, 1), F32),
                        pltpu.VMEM((2, TQ, LANES), F32)],
        compiler_params=_cparams("parallel", "parallel", "arbitrary"),
        name="mla_flash",
    )(q, k, v)


def _rwkv_prep_kernel(zr_ref, zk_ref, zv_ref, zwa_ref, zg_ref, mu_r, mu_k, mu_v, mu_wa, mu_g,
                      w0_ref, a0_ref, kk_ref, ka_ref, rk_ref, w2_ref, a2_ref, g2_ref, bd_ref,
                      r_o, k_o, v_o, kk_o, be_o, ld_o, g_o, bo_o,
                      c_r, c_k, c_v, c_wa, c_g):
    t = pl.program_id(1)
    tm = zr_ref.shape[0]

    def shifted(z_ref, mu_ref, c_ref):
        x = z_ref[...]
        row = lax.broadcasted_iota(jnp.int32, x.shape, 0)
        prev = jnp.where(t == 0, 0.0, c_ref[0:1, :])
        xs = jnp.where(row == 0, prev, pltpu.roll(x, 1, axis=0))
        c_ref[0:1, :] = x[tm - 1:tm, :]
        return x + (xs - x) * mu_ref[...]

    r = shifted(zr_ref, mu_r, c_r)
    k = shifted(zk_ref, mu_k, c_k)
    v = shifted(zv_ref, mu_v, c_v)
    wa = shifted(zwa_ref, mu_wa, c_wa)
    gd = shifted(zg_ref, mu_g, c_g)
    w = w0_ref[...] + _dot(jnp.tanh(wa).astype(BF16), w2_ref[...])
    ld_o[...] = -jax.nn.sigmoid(w) * math.exp(-0.5)
    a = jax.nn.sigmoid(a0_ref[...] + _dot(wa.astype(BF16), a2_ref[...]))
    g_o[...] = _dot(jax.nn.sigmoid(gd).astype(BF16), g2_ref[...])
    kk = k * kk_ref[...]
    nsq = _dot(kk * kk, bd_ref[...], precision=HIGHEST)
    kk = kk / jnp.maximum(jnp.sqrt(nsq), 1e-12)
    k2 = k * (1.0 + (a - 1.0) * ka_ref[...])
    rks = _dot(r * k2 * rk_ref[...], bd_ref[...], precision=HIGHEST)
    r_o[...] = r
    k_o[...] = k2
    v_o[...] = v
    kk_o[...] = kk
    be_o[...] = kk * a
    bo_o[...] = rks * v


def _rwkv_prep(z, mus, w0, a0, k_k, k_a, r_k, w2p, a2p, g2, bd, B, S, tm=256):
    nt = S // tm
    W = C_WIDTH
    zspec = lambda w, col: pl.BlockSpec((tm, w), lambda b, i: (b * nt + i, col // w))
    full = lambda a: pl.BlockSpec(a.shape, lambda b, i: (0,) * a.ndim)
    ospec = pl.BlockSpec((tm, W), lambda b, i: (b * nt + i, 0))
    oshape = jax.ShapeDtypeStruct((B * S, W), F32)
    consts = list(mus) + [w0, a0, k_k, k_a, r_k, w2p, a2p, g2, bd]
    return pl.pallas_call(
        _rwkv_prep_kernel,
        grid=(B, nt),
        in_specs=[zspec(W, Z_CR), zspec(W, Z_CK), zspec(W, Z_CV), zspec(LANES, Z_CWA), zspec(LANES, Z_CG)]
                 + [full(a) for a in consts],
        out_specs=[ospec] * 8,
        out_shape=[oshape] * 8,
        scratch_shapes=[pltpu.VMEM((8, W), F32)] * 3 + [pltpu.VMEM((8, LANES), F32)] * 2,
        compiler_params=_cparams("parallel", "arbitrary"),
        name="rwkv_prep",
    )(z, z, z, z, z, *consts)


def _rwkv_chunk_kernel(r_ref, k_ref, v_ref, kk_ref, be_ref, ld_ref,
                       qa_o, bkt_o, yu_o, vp_o, pcb_o):
    C = CH
    ld = ld_ref[...]
    ri = lax.broadcasted_iota(jnp.int32, (C, C), 0)
    ci = lax.broadcasted_iota(jnp.int32, (C, C), 1)
    incl = ci <= ri
    strict = ci < ri
    cs = _dot(jnp.where(incl, 1.0, 0.0), ld, precision=HIGHEST)
    cl = cs[C - 1:C, :]
    ex, exn, exx, exc = jnp.exp(cs), jnp.exp(-cs), jnp.exp(cs - ld), jnp.exp(cl - cs)
    kk, be, k2, v = kk_ref[...], be_ref[...], k_ref[...], v_ref[...]
    at = -kk * exx
    rt = r_ref[...] * ex
    bt = be * exn
    kt = k2 * exn
    btc = be * exc
    ktc = k2 * exc
    pc = jnp.exp(cl)
    eye = jnp.where(ci == ri, 1.0, 0.0)
    lane = lax.broadcasted_iota(jnp.int32, (C, LANES), 1)
    for p in range(C_HEADS // 2):
        sl = slice(p * LANES, (p + 1) * LANES)
        at_p, rt_p, bt_p, kt_p, v_p = at[:, sl], rt[:, sl], bt[:, sl], kt[:, sl], v[:, sl]
        lhs2 = jnp.concatenate([at_p, rt_p], axis=0).astype(BF16)
        ahat = jnp.zeros((C, LANES), F32)
        uhat = jnp.zeros((C, LANES), F32)
        qadd = jnp.zeros((C, LANES), F32)
        yi = jnp.zeros((C, LANES), F32)
        for hh in range(2):
            msk = (lane < C_HEAD_DIM) if hh == 0 else (lane >= C_HEAD_DIM)
            bm = jnp.where(msk, bt_p, 0.0).astype(BF16)
            km = jnp.where(msk, kt_p, 0.0).astype(BF16)
            vm = jnp.where(msk, v_p, 0.0).astype(BF16)
            am = jnp.where(msk, at_p, 0.0).astype(BF16)
            g1 = _dot_t(lhs2, bm)
            g2 = _dot_t(lhs2, km)
            lab = jnp.where(strict, g1[:C], 0.0)
            mrb = jnp.where(incl, g1[C:], 0.0).astype(BF16)
            lak = jnp.where(strict, g2[:C], 0.0).astype(BF16)
            mrk = jnp.where(incl, g2[C:], 0.0).astype(BF16)
            tinv = eye + lab
            lp = lab
            for _ in range(int(math.log2(C)) - 1):
                lpb = lp.astype(BF16)
                lp = _dot(lpb, lpb)
                tinv = tinv + _dot(tinv.astype(BF16), lp.astype(BF16))
            w2 = _dot(lak, vm)
            au = _dot(tinv.astype(BF16), jnp.concatenate([am, w2.astype(BF16)], axis=1))
            qy = _dot(mrb, au.astype(BF16))
            ahat = ahat + au[:, :LANES]
            uhat = uhat + au[:, LANES:]
            qadd = qadd + qy[:, :LANES]
            yi = yi + qy[:, LANES:] + _dot(mrk, vm)
        qa_o[p] = jnp.concatenate([rt_p + qadd, ahat], axis=0).astype(BF16)
        yu_o[p] = jnp.concatenate([yi, uhat], axis=0)
        bkt_o[p] = jnp.concatenate([btc[:, sl], ktc[:, sl]], axis=0).T.astype(BF16)
        vp_o[p] = v_p.astype(BF16)
        pcb_o[p] = jnp.broadcast_to(pc[:, sl], (LANES, LANES)).T


def _rwkv_chunk(r, k2, v, kk, be, ld, B, S):
    nch = S // CH
    W = C_WIDTH
    P = C_HEADS // 2
    ispec = pl.BlockSpec((CH, W), lambda b, c: (b * nch + c, 0))
    sq = lambda rows: pl.BlockSpec((None, None, P, rows, LANES), lambda b, c: (b, c, 0, 0, 0))
    shp = lambda rows, dt: jax.ShapeDtypeStruct((B, nch, P, rows, LANES), dt)
    return pl.pallas_call(
        _rwkv_chunk_kernel,
        grid=(B, nch),
        in_specs=[ispec] * 6,
        out_specs=[sq(2 * CH), sq(LANES), sq(2 * CH), sq(CH), sq(LANES)],
        out_shape=[shp(2 * CH, BF16), shp(LANES, BF16), shp(2 * CH, F32), shp(CH, BF16), shp(LANES, F32)],
        compiler_params=_cparams("parallel", "parallel"),
        name="rwkv_chunk",
    )(r, k2, v, kk, be, ld)


def _rwkv_state_kernel(qa_ref, bkt_ref, yu_ref, vp_ref, pcb_ref, y_ref, ap_ref):
    @pl.when(pl.program_id(1) == 0)
    def _():
        ap_ref[...] = jnp.zeros(ap_ref.shape, F32)

    ri = lax.broadcasted_iota(jnp.int32, (LANES, LANES), 0)
    ci = lax.broadcasted_iota(jnp.int32, (LANES, LANES), 1)
    same_head = (ri < C_HEAD_DIM) == (ci < C_HEAD_DIM)
    for p in range(C_HEADS // 2):
        a = ap_ref[p]
        x = _dot(qa_ref[p], a.astype(BF16))
        yu = yu_ref[p]
        y_ref[:, p * LANES:(p + 1) * LANES] = x[:CH] + yu[:CH]
        u = x[CH:] + yu[CH:]
        uv = jnp.concatenate([u.astype(BF16), vp_ref[p]], axis=0)
        upd = _dot(bkt_ref[p], uv)
        ap_ref[p] = pcb_ref[p] * a + jnp.where(same_head, upd, 0.0)


def _rwkv_state(qa, bkt, yu, vp, pcb, B, S):
    nch = S // CH
    P = C_HEADS // 2
    sq = lambda rows: pl.BlockSpec((None, None, P, rows, LANES), lambda b, c: (b, c, 0, 0, 0))
    return pl.pallas_call(
        _rwkv_state_kernel,
        grid=(B, nch),
        in_specs=[sq(2 * CH), sq(LANES), sq(2 * CH), sq(CH), sq(LANES)],
        out_specs=pl.BlockSpec((CH, C_WIDTH), lambda b, c: (b * nch + c, 0)),
        out_shape=jax.ShapeDtypeStruct((B * S, C_WIDTH), F32),
        scratch_shapes=[pltpu.VMEM((P, LANES, LANES), F32)],
        compiler_params=_cparams("parallel", "arbitrary"),
        name="rwkv_state",
    )(qa, bkt, yu, vp, pcb)


def _merge_kernel(oc_ref, os_ref, ow_ref, zag_ref, yb_ref, yr_ref, bo_ref, gg_ref,
                  zga_ref, zgb_ref, zgc_ref, x_ref, wbr_ref, wout_ref, ge_ref, bdm_ref,
                  ln_ref, gain_ref, gt_ref, o_ref):
    sg = jax.nn.sigmoid(zag_ref[...])
    ya = (_dot(sg, ge_ref[0], precision=HIGHEST) * oc_ref[...]
          + _dot(sg, ge_ref[1], precision=HIGHEST) * os_ref[...]
          + _dot(sg, ge_ref[2], precision=HIGHEST) * ow_ref[...])
    yr = yr_ref[...]
    mean = _dot(yr, bdm_ref[...], precision=HIGHEST)
    d = yr - mean
    var = _dot(d * d, bdm_ref[...], precision=HIGHEST)
    yn = d * lax.rsqrt(var + GN_EPS) * ln_ref[0:1, :] + ln_ref[1:2, :]
    yc = (yn + bo_ref[...]) * gg_ref[...]
    merged = (jax.nn.sigmoid(zga_ref[...]) * _dot(ya.astype(BF16), wbr_ref[0:A_Q, :])
              + jax.nn.sigmoid(zgb_ref[...]) * _dot(yb_ref[...].astype(BF16), wbr_ref[A_Q:2 * A_Q, :])
              + jax.nn.sigmoid(zgc_ref[...]) * _dot(yc.astype(BF16), wbr_ref[2 * A_Q:3 * A_Q, :]))
    y = _dot(merged.astype(BF16), wout_ref[...])
    yn2 = y * lax.rsqrt(jnp.mean(y * y, axis=-1, keepdims=True) + EPS) * gain_ref[...]
    o_ref[...] = x_ref[...] + gt_ref[0] * yn2


def _merge(oc, os_, ow, z, yb, yr, bo, gg, x2, wbr, wout, gexp, bdm, ln, gain, gt, S, tm=256):
    N, D = x2.shape
    tpb = S // tm
    W = A_Q
    row = lambda w: pl.BlockSpec((tm, w), lambda i: (i, 0))
    zspec = lambda w, col: pl.BlockSpec((tm, w), lambda i: (i, col // w))
    full = lambda a: pl.BlockSpec(a.shape, lambda i: (0,) * a.ndim)
    return pl.pallas_call(
        _merge_kernel,
        grid=(N // tm,),
        in_specs=[row(W), row(W), row(W), zspec(LANES, Z_AG), row(W), row(W), row(W), row(W),
                  zspec(D, Z_ZG), zspec(D, Z_ZG + D), zspec(D, Z_ZG + 2 * D), row(D),
                  full(wbr), full(wout), full(gexp), full(bdm), full(ln),
                  pl.BlockSpec((1, D), lambda i: (0, 0)),
                  pl.BlockSpec((1, 1, D), lambda i: (i // tpb, 0, 0))],
        out_specs=row(D),
        out_shape=jax.ShapeDtypeStruct((N, D), F32),
        compiler_params=_cparams("parallel"),
        name="merge_out",
    )(oc, os_, ow, z, yb, yr, bo, gg, z, z, z, x2, wbr, wout, gexp, bdm, ln, gain.reshape(1, D), gt)


def _ffn_down_kernel(u_ref, up_ref, cw_ref, cb_ref, wd_ref, x_ref, gain_ref, gt_ref, o_ref, *, tpb):
    i = pl.program_id(0)
    tm = u_ref.shape[0]
    first = (i % tpb) == 0
    fc = 256
    row = lax.broadcasted_iota(jnp.int32, (tm, fc), 0)

    def conv(c0):
        u = u_ref[:, c0:c0 + fc]
        p1 = jnp.where(first, 0.0, up_ref[7:8, c0:c0 + fc])
        p2 = jnp.where(first, 0.0, up_ref[6:7, c0:c0 + fc])
        u1 = jnp.where(row == 0, p1, pltpu.roll(u, 1, axis=0))
        u2 = jnp.where(row == 0, p2, jnp.where(row == 1, p1, pltpu.roll(u, 2, axis=0)))
        return (cw_ref[0:1, c0:c0 + fc] * u2 + cw_ref[1:2, c0:c0 + fc] * u1
                + cw_ref[2:3, c0:c0 + fc] * u + cb_ref[:, c0:c0 + fc])

    f = jnp.zeros((tm, D_MODEL), F32)
    for c in range(D_FF // fc):
        a = _gelu(conv(c * fc)) * conv(D_FF + c * fc)
        f = f + _dot(a.astype(BF16), wd_ref[c * fc:(c + 1) * fc, :])
    fn = f * lax.rsqrt(jnp.mean(f * f, axis=-1, keepdims=True) + EPS) * gain_ref[...]
    o_ref[...] = x_ref[...] + gt_ref[0] * fn


def _ffn_down(u, conv_w, conv_b, wd, x2, gain, gt, S, tm=256):
    N, D = x2.shape
    F2 = u.shape[1]
    tpb = S // tm
    full = lambda a: pl.BlockSpec(a.shape, lambda i: (0,) * a.ndim)
    return pl.pallas_call(
        functools.partial(_ffn_down_kernel, tpb=tpb),
        grid=(N // tm,),
        in_specs=[pl.BlockSpec((tm, F2), lambda i: (i, 0)),
                  pl.BlockSpec((8, F2), lambda i: (jnp.maximum(i * (tm // 8) - 1, 0), 0)),
                  full(conv_w), pl.BlockSpec((1, F2), lambda i: (0, 0)), full(wd),
                  pl.BlockSpec((tm, D), lambda i: (i, 0)),
                  pl.BlockSpec((1, D), lambda i: (0, 0)),
                  pl.BlockSpec((1, 1, D), lambda i: (i // tpb, 0, 0))],
        out_specs=pl.BlockSpec((tm, D), lambda i: (i, 0)),
        out_shape=jax.ShapeDtypeStruct((N, D), F32),
        compiler_params=_cparams("parallel"),
        name="ffn_down",
    )(u, u, conv_w, conv_b.reshape(1, F2), wd, x2, gain.reshape(1, D), gt)


def _t5_bucket_np(dist):
    d = np.maximum(dist, 0)
    max_exact = N_BUCKETS // 2
    large = max_exact + (np.log(np.maximum(d, 1).astype(np.float32) / max_exact)
                         / math.log(MAX_DISTANCE / max_exact) * (N_BUCKETS - max_exact)).astype(np.int32)
    return np.where(d < max_exact, d, np.minimum(large, N_BUCKETS - 1))


def _bias_table(rel_bias, dist, valid):
    tq, ncol = dist.shape
    G, HPG = A_KV_GROUPS, A_HEADS // A_KV_GROUPS
    rb = rel_bias[jnp.asarray(_t5_bucket_np(dist))]
    rb = jnp.where(jnp.asarray(valid)[:, :, None], rb, NEG)
    rb = rb.transpose(2, 0, 1).reshape(G, HPG // 2, 2, tq, ncol)
    return rb.transpose(0, 2, 1, 3, 4).reshape(G, 2, 2 * tq, ncol)


def _nsa_tables(rel_bias, S):
    NC = S // CMP_STRIDE
    i = np.arange(TQ)[:, None]
    c = np.arange(NC)[None, :]
    dc = i - CMP_STRIDE * c + CMP_STRIDE * NC - TQ - (CMP_BLOCK - 1)
    cb = _bias_table(rel_bias, dc, dc >= 0)
    j = np.arange(TQ)[None, :]
    d0, d1 = i - j, TQ + i - j
    far = np.full((TQ, TQ), MAX_DISTANCE)
    slc = jnp.stack([_bias_table(rel_bias, d0, d0 >= 0), _bias_table(rel_bias, d1, d1 >= 0),
                     _bias_table(rel_bias, far, far >= 0)], axis=2)
    d2 = 2 * TQ + i - j
    win = jnp.stack([_bias_table(rel_bias, d, (d >= 0) & (d < WINDOW)) for d in (d0, d1, d2)], axis=2)
    return cb, slc, win


def _pad_cols(w, n):
    return jnp.pad(w, ((0, 0), (0, n - w.shape[1])))


def _prep_w_in(w):
    D = w.shape[0]
    za, zb, zc, zg = jnp.split(w, np.cumsum([A_COLS, B_COLS, C_COLS]).tolist(), axis=1)
    a_parts = jnp.split(za, np.cumsum([A_Q] + [A_KV] * 6).tolist(), axis=1)
    qa, kva, kr = jnp.split(zb, [Q_LORA, Q_LORA + KV_LORA], axis=1)
    half = ROPE_DIM // 2
    kr_rot = jnp.concatenate([-kr[:, half:], kr[:, :half]], axis=1)
    z64 = jnp.zeros((D, NOPE_DIM), w.dtype)
    z32 = jnp.zeros((D, LANES - NOPE_DIM - ROPE_DIM), w.dtype)
    c_r, c_k, c_v, c_wd, c_ad, c_gd = jnp.split(
        zc, np.cumsum([C_WIDTH] * 3 + [DECAY_LORA, AAA_LORA]).tolist(), axis=1)
    cols = [a_parts[0], qa] + a_parts[1:7] + [
        _pad_cols(a_parts[7], LANES), kva,
        jnp.concatenate([z64, kr, z32], axis=1),
        jnp.concatenate([z64, kr_rot, z32], axis=1),
        c_r, c_k, c_v, c_wd, c_ad, c_gd, jnp.zeros((D, Z_ZG - Z_CG - LANES), w.dtype), zg]
    out = jnp.concatenate(cols, axis=1)
    assert out.shape[1] == Z_COLS
    return out.astype(BF16)


def _prep_mla(w_uq, w_ukv):
    dq = NOPE_DIM + ROPE_DIM
    half = ROPE_DIM // 2
    wq = w_uq.reshape(Q_LORA, B_HEADS, dq)
    nope, r1, r2 = wq[..., :NOPE_DIM], wq[..., NOPE_DIM:NOPE_DIM + half], wq[..., NOPE_DIM + half:]
    zq = jnp.zeros((Q_LORA, B_HEADS, LANES - dq), w_uq.dtype)
    wqa = jnp.concatenate([nope, r1, r2, zq], axis=-1).reshape(Q_LORA, B_HEADS * LANES)
    wqb = jnp.concatenate([jnp.zeros_like(nope), -r2, r1, zq], axis=-1).reshape(Q_LORA, B_HEADS * LANES)
    wkv = w_ukv.reshape(KV_LORA, B_HEADS, NOPE_DIM + V_DIM)
    kn, vv = wkv[..., :NOPE_DIM], wkv[..., NOPE_DIM:]
    wk = jnp.concatenate([kn, jnp.zeros_like(kn)], axis=-1).reshape(KV_LORA, B_HEADS * LANES)
    zv = jnp.zeros_like(vv)
    even = (jnp.arange(B_HEADS) % 2 == 0)[None, :, None]
    wv = jnp.concatenate([jnp.where(even, vv, zv), jnp.where(even, zv, vv)], axis=-1)
    wv = wv.reshape(KV_LORA, B_HEADS * LANES)
    return wqa.astype(BF16), wqb.astype(BF16), wk.astype(BF16), wv.astype(BF16)


def _rope_tables(S):
    half = ROPE_DIM // 2
    inv = ROPE_THETA ** (-jnp.arange(half, dtype=F32) / half)
    ang = jnp.arange(S, dtype=F32)[:, None] * inv
    cos2 = jnp.tile(jnp.cos(ang), (1, 2))
    sin2 = jnp.tile(jnp.sin(ang), (1, 2))
    scale = (NOPE_DIM + ROPE_DIM) ** -0.5
    one = jnp.ones((S, NOPE_DIM), F32)
    z64 = jnp.zeros((S, NOPE_DIM), F32)
    z32 = jnp.zeros((S, LANES - NOPE_DIM - ROPE_DIM), F32)
    caq = jnp.concatenate([one, cos2, z32], axis=1) * scale
    cbq = jnp.concatenate([z64, sin2, z32], axis=1) * scale
    cak = jnp.concatenate([z64, cos2, z32], axis=1)
    cbk = jnp.concatenate([z64, sin2, z32], axis=1)
    return caq, cbq, cak, cbk


def _prep_compress(cmp_pos, cmp_w1, cmp_w2):
    Dh = A_HEAD_DIM
    w1 = cmp_w1.reshape(2, CMP_BLOCK, Dh, Dh)
    z = jnp.zeros_like(w1)
    w1bd = jnp.concatenate([jnp.concatenate([w1, z], axis=-1), jnp.concatenate([z, w1], axis=-1)], axis=-2)
    posrow = jnp.broadcast_to(cmp_pos.reshape(2, 1, CMP_BLOCK * Dh), (2, 8, CMP_BLOCK * Dh))
    w1cat = jnp.concatenate([cmp_w1, cmp_w1], axis=-1)
    zz = jnp.zeros((2, Dh, Dh), cmp_w2.dtype)
    blk = lambda a, b, c, d: jnp.concatenate(
        [jnp.concatenate([a, b], axis=-1), jnp.concatenate([c, d], axis=-1)], axis=-2)
    w2v = jnp.stack([jnp.stack([blk(cmp_w2, zz, zz, zz), blk(zz, cmp_w2, zz, zz)], axis=1),
                     jnp.stack([blk(zz, zz, cmp_w2, zz), blk(zz, zz, zz, cmp_w2)], axis=1)], axis=1)
    return w1bd.astype(BF16), posrow, w1cat.astype(BF16), w2v.astype(BF16)


def _gate_expand():
    e = np.zeros((3, LANES, A_Q), np.float32)
    for h in range(A_HEADS):
        for r in range(3):
            e[r, 3 * h + r, h * A_HEAD_DIM:(h + 1) * A_HEAD_DIM] = 1.0
    return jnp.asarray(e)


def _block_diag_ones(scale):
    idx = np.arange(C_WIDTH) // C_HEAD_DIM
    return jnp.asarray((idx[:, None] == idx[None, :]).astype(np.float32) * scale)


def kernel(x, c, rel_bias, ada_w, ada_b, norm_gain, w_in, nsa_cmp_pos, nsa_cmp_w1, nsa_cmp_w2, mla_q_norm, mla_kv_norm, mla_w_uq, mla_w_ukv, rwkv_mu, rwkv_w0, rwkv_a0, rwkv_k_k, rwkv_k_a, rwkv_w2, rwkv_a2, rwkv_g2, rwkv_r_k, rwkv_ln, w_branch, w_out, ffn_up, ffn_conv_w, ffn_conv_b, ffn_down):
    B, S, D = x.shape
    L = ada_w.shape[0]
    assert S % TQ == 0 and S // SLC_BLOCK <= A_HEAD_DIM and S >= 2 * TQ
    mod = _adaln(c, ada_w, ada_b)
    cb, slc_tab, win_tab = _nsa_tables(rel_bias, S)
    caq, cbq, cak, cbk = _rope_tables(S)
    gexp = _gate_expand()
    bd1 = _block_diag_ones(1.0)
    bdm = _block_diag_ones(1.0 / C_HEAD_DIM)
    row = lambda v: v.reshape(1, -1)
    x2 = x.reshape(B * S, D)
    for l in range(L):
        m6 = mod[l].reshape(B, 6, 1, D)
        sh1, sc1, gt1, sh2, sc2, gt2 = (m6[:, i] for i in range(6))
        z = _norm_mod_matmul(x2, norm_gain[l, 0], sc1, sh1, _prep_w_in(w_in[l]), S, tm=512, tn=1024)
        kcp, vcp = _compress(z, *_prep_compress(nsa_cmp_pos[l], nsa_cmp_w1[l], nsa_cmp_w2[l]), B, S)
        oc, sb = _cmpsel(z, kcp, vcp, cb, B, S)
        os_ = _nsa_flash(z, sb, slc_tab, Z_AKS, Z_AVS, B, S, selected=True)
        ow = _nsa_flash(z, None, win_tab, Z_AKW, Z_AVW, B, S, selected=False)
        wqa, wqb, wk, wv = _prep_mla(mla_w_uq[l], mla_w_ukv[l])
        q, k, v = _mla_proj(z, row(mla_q_norm[l]), row(mla_kv_norm[l]), wqa, wqb, wk, wv,
                            caq, cbq, cak, cbk, B, S)
        yb = _mla_flash(q, k, v, B, S)
        mu = rwkv_mu[l]
        o = 3 * C_WIDTH
        mus = [row(mu[:C_WIDTH]), row(mu[C_WIDTH:2 * C_WIDTH]), row(mu[2 * C_WIDTH:o]),
               row(mu[o:o + LANES]), row(mu[o + LANES:])]
        w2p = jnp.concatenate([rwkv_w2[l], jnp.zeros_like(rwkv_a2[l])], axis=0).astype(BF16)
        a2p = jnp.concatenate([jnp.zeros_like(rwkv_w2[l]), rwkv_a2[l]], axis=0).astype(BF16)
        rr, k2, vv, kk, be, ld, gg, bo = _rwkv_prep(
            z, mus, row(rwkv_w0[l]), row(rwkv_a0[l]), row(rwkv_k_k[l]), row(rwkv_k_a[l]),
            row(rwkv_r_k[l]), w2p, a2p, rwkv_g2[l].astype(BF16), bd1, B, S)
        qa_, bkt, yu, vp, pcb = _rwkv_chunk(rr, k2, vv, kk, be, ld, B, S)
        yr = _rwkv_state(qa_, bkt, yu, vp, pcb, B, S)
        x2 = _merge(oc, os_, ow, z, yb, yr, bo, gg, x2, w_branch[l].astype(BF16), w_out[l].astype(BF16),
                    gexp, bdm, rwkv_ln[l], norm_gain[l, 1], gt1, S)
        u = _norm_mod_matmul(x2, norm_gain[l, 2], sc2, sh2, ffn_up[l].astype(BF16), S, tm=512, tn=512)
        x2 = _ffn_down(u, ffn_conv_w[l], ffn_conv_b[l], ffn_down[l].astype(BF16), x2,
                       norm_gain[l, 3], gt2, S)
    return x2.reshape(B, S, D)
```

```python
import functools
import math

import jax
import jax.numpy as jnp
import numpy as np
from jax import lax
from jax.experimental import pallas as pl
from jax.experimental.pallas import tpu as pltpu

F32 = jnp.float32
BF16 = jnp.bfloat16
HIGHEST = lax.Precision.HIGHEST

D_MODEL = 1024
DEPTH = 4
A_HEADS, A_KV_GROUPS, A_HEAD_DIM = 8, 2, 64
CMP_BLOCK, CMP_STRIDE = 32, 16
SLC_BLOCK, SLC_TOPK, N_LOCAL_BLOCKS = 64, 16, 2
WINDOW = 512
FORCE_SCORE = 1e9
B_HEADS, Q_LORA, KV_LORA, NOPE_DIM, ROPE_DIM, V_DIM = 8, 256, 128, 64, 32, 64
ROPE_THETA = 10000.0
C_HEADS, C_HEAD_DIM = 8, 64
C_WIDTH = C_HEADS * C_HEAD_DIM
DECAY_LORA, AAA_LORA, GATE_LORA = 64, 64, 128
GN_EPS = 64e-5
N_BUCKETS, MAX_DISTANCE = 32, 128
D_FF = 2816
EPS = 1e-6
NEG = -1e30
LOG2E = math.log2(math.e)

A_Q = A_HEADS * A_HEAD_DIM
A_KV = A_KV_GROUPS * A_HEAD_DIM
A_GATE = 3 * A_HEADS
A_COLS = A_Q + 6 * A_KV + A_GATE
B_COLS = Q_LORA + KV_LORA + ROPE_DIM
C_COLS = 3 * C_WIDTH + DECAY_LORA + AAA_LORA + GATE_LORA

LANES = 128
VMEM_LIMIT = 48 * 1024 * 1024

Z_AQ = 0
Z_QA = 512
Z_AKC, Z_AVC, Z_AKS, Z_AVS, Z_AKW, Z_AVW = 768, 896, 1024, 1152, 1280, 1408
Z_AG = 1536
Z_KVA = 1664
Z_KRA = 1792
Z_KRB = 1920
Z_CR, Z_CK, Z_CV = 2048, 2560, 3072
Z_CWA = 3584
Z_CG = 3712
Z_ZG = 4096
Z_COLS = 7168

TQ = 256
TM = 512
CH = 64


def _cparams(*sem):
    return pltpu.CompilerParams(dimension_semantics=sem, vmem_limit_bytes=VMEM_LIMIT)


def _gelu(x):
    return 0.5 * x * (1.0 + jnp.tanh(0.7978845608028654 * (x + 0.044715 * (x * x * x))))


def _dot(a, b, **kw):
    return jnp.dot(a, b, preferred_element_type=F32, **kw)


def _dot_t(a, b):
    return lax.dot_general(a, b, (((1,), (1,)), ((), ())), preferred_element_type=F32)


def _adaln_kernel(c_ref, w_ref, b_ref, o_ref):
    c = c_ref[...]
    cond = c * jax.nn.sigmoid(c)
    o_ref[0] = _dot(cond.astype(BF16), w_ref[0].astype(BF16)) + b_ref[0]


def _adaln(c, ada_w, ada_b):
    L, D, N6 = ada_w.shape
    B = c.shape[0]
    tn = 1536
    return pl.pallas_call(
        _adaln_kernel,
        grid=(L, N6 // tn),
        in_specs=[pl.BlockSpec((B, D), lambda l, j: (0, 0)),
                  pl.BlockSpec((1, D, tn), lambda l, j: (l, 0, j)),
                  pl.BlockSpec((1, 1, tn), lambda l, j: (l, 0, j))],
        out_specs=pl.BlockSpec((1, B, tn), lambda l, j: (l, 0, j)),
        out_shape=jax.ShapeDtypeStruct((L, B, N6), F32),
        compiler_params=_cparams("parallel", "parallel"),
        name="adaln",
    )(c, ada_w, ada_b.reshape(L, 1, N6))


def _nmm_kernel(x_ref, g_ref, sc_ref, sh_ref, w_ref, o_ref, h_ref):
    @pl.when(pl.program_id(1) == 0)
    def _():
        x = x_ref[...]
        y = x * lax.rsqrt(jnp.mean(x * x, axis=-1, keepdims=True) + EPS)
        h = (y * g_ref[...]) * (1.0 + sc_ref[0]) + sh_ref[0]
        h_ref[...] = h.astype(BF16)

    o_ref[...] = _dot(h_ref[...], w_ref[...]).astype(o_ref.dtype)


def _norm_mod_matmul(x2, gain, sc, sh, w, S, tm, tn, out_dtype=F32):
    N, D = x2.shape
    NC = w.shape[1]
    tpb = S // tm
    return pl.pallas_call(
        _nmm_kernel,
        grid=(N // tm, NC // tn),
        in_specs=[pl.BlockSpec((tm, D), lambda i, j: (i, 0)),
                  pl.BlockSpec((1, D), lambda i, j: (0, 0)),
                  pl.BlockSpec((1, 1, D), lambda i, j: (i // tpb, 0, 0)),
                  pl.BlockSpec((1, 1, D), lambda i, j: (i // tpb, 0, 0)),
                  pl.BlockSpec((D, tn), lambda i, j: (0, j))],
        out_specs=pl.BlockSpec((tm, tn), lambda i, j: (i, j)),
        out_shape=jax.ShapeDtypeStruct((N, NC), out_dtype),
        scratch_shapes=[pltpu.VMEM((tm, D), BF16)],
        compiler_params=_cparams("parallel", "arbitrary"),
        name="norm_mod_matmul",
    )(x2, gain.reshape(1, D), sc, sh, w)


def _compress_kernel(zk_ref, zv_ref, w1_ref, pos_ref, w1c_ref, w2_ref, kc_ref, vc_ref, *, NC):
    for kind, (z_ref, o_ref) in enumerate(((zk_ref, kc_ref), (zv_ref, vc_ref))):
        pa = jnp.zeros((NC, LANES), F32)
        pb = jnp.zeros((NC, LANES), F32)
        for l in range(CMP_STRIDE):
            xl = z_ref[pl.ds(l, NC, stride=CMP_STRIDE), :].astype(BF16)
            pa = pa + _dot(xl, w1_ref[kind, l])
            pb = pb + _dot(xl, w1_ref[kind, CMP_STRIDE + l])
        posb = _dot(pos_ref[kind].astype(BF16), w1c_ref[kind])[0:1, :]
        h = pa + pltpu.roll(pb, NC - 1, axis=0) + posb
        act = _gelu(h).astype(BF16)
        row = lax.broadcasted_iota(jnp.int32, (NC, LANES), 0)
        for g in range(A_KV_GROUPS):
            for eo in range(2):
                out = _dot(act, w2_ref[kind, g, eo])
                out = jnp.where(row < NC - 1, out, 0.0)
                o_ref[g, eo, pl.ds(0, NC), :] = jnp.zeros((NC, LANES), BF16)
                o_ref[g, eo, pl.ds(NC, NC), :] = out.astype(BF16)


def _compress(z, w1bd, posrow, w1cat, w2v, B, S):
    NC = S // CMP_STRIDE
    out = jax.ShapeDtypeStruct((B, A_KV_GROUPS, 2, 2 * NC, LANES), BF16)
    ospec = pl.BlockSpec((None, A_KV_GROUPS, 2, 2 * NC, LANES), lambda b: (b, 0, 0, 0, 0))
    full = lambda a: pl.BlockSpec(a.shape, lambda b: (0,) * a.ndim)
    return pl.pallas_call(
        functools.partial(_compress_kernel, NC=NC),
        grid=(B,),
        in_specs=[pl.BlockSpec((S, LANES), lambda b: (b, Z_AKC // LANES)),
                  pl.BlockSpec((S, LANES), lambda b: (b, Z_AVC // LANES)),
                  full(w1bd), full(posrow), full(w1cat), full(w2v)],
        out_specs=[ospec, ospec],
        out_shape=[out, out],
        compiler_params=_cparams("parallel"),
        name="nsa_compress",
    )(z, z, w1bd, posrow, w1cat, w2v)


def _stack_pairs(zq, scale):
    q = zq * scale
    return jnp.concatenate([q[:, :LANES], q[:, LANES:]], axis=0).astype(BF16)


def _cmpsel_kernel(zq_ref, kc_ref, vc_ref, cb_ref, oc_ref, sb_ref, *, NC, NS):
    qt = pl.program_id(2)
    r = TQ // CMP_STRIDE
    lhs = _stack_pairs(zq_ref[...], A_HEAD_DIM ** -0.5)
    st = pl.multiple_of(r * (qt + 1), 16)
    col = lax.broadcasted_iota(jnp.int32, (2 * TQ, NC), 1)
    exists = col >= NC - r * (qt + 1)
    acc = jnp.zeros((2 * TQ, LANES), F32)
    psum = jnp.zeros((TQ, NC), F32)
    for eo in range(2):
        kwin = kc_ref[eo, pl.ds(st, NC), :]
        vwin = vc_ref[eo, pl.ds(st, NC), :]
        logits = _dot_t(lhs, kwin) + cb_ref[eo]
        logits = jnp.where(exists, logits, NEG)
        m = jnp.max(logits, axis=-1, keepdims=True)
        p = jnp.where(logits > 0.1 * NEG, jnp.exp(logits - m), 0.0)
        l = jnp.maximum(jnp.sum(p, axis=-1, keepdims=True), 1e-30)
        p = p * (1.0 / l)
        acc = acc + _dot(p.astype(BF16), vwin)
        psum = psum + p[:TQ] + p[TQ:]
    oc_ref[:, :LANES] = acc[:TQ]
    oc_ref[:, LANES:] = acc[TQ:]

    jj = lax.broadcasted_iota(jnp.int32, (LANES, NC), 0)
    nn = lax.broadcasted_iota(jnp.int32, (LANES, NC), 1) + (r * (qt + 1) - NC)
    delta = 4 * jj - nn
    mt = jnp.where((delta == 0) | (delta == 4), 1.0, 0.0) + jnp.where((delta >= 1) & (delta <= 3), 2.0, 0.0)
    mt = jnp.where(jj < NS, mt, 0.0)
    imp_t = lax.dot_general(mt, psum, (((1,), (1,)), ((), ())), preferred_element_type=F32,
                            precision=HIGHEST)
    jb = lax.broadcasted_iota(jnp.int32, (LANES, TQ), 0)
    tpos = qt * TQ + lax.broadcasted_iota(jnp.int32, (LANES, TQ), 1)
    back = (tpos >> 6) - jb
    forced = (jb == 0) | ((back >= 0) & (back < N_LOCAL_BLOCKS))
    score = jnp.where(forced, FORCE_SCORE, jnp.where(back >= 0, imp_t, -1.0))
    score = jnp.where(jb < NS, score, -2.0)
    rank = jnp.zeros((LANES, TQ), F32)
    for jp in range(NS):
        row = score[jp:jp + 1, :]
        beats = (row > score) | ((row == score) & (jb > jp))
        rank = rank + jnp.where(beats, 1.0, 0.0)
    sel = (rank < float(min(SLC_TOPK, NS))) & (back >= 0) & (jb < NS)
    sb_ref[...] = jnp.where(sel, 0.0, jnp.where(jb < NS, NEG, 0.0)).astype(BF16)


def _cmpsel(z, kcp, vcp, cb, B, S):
    NC = S // CMP_STRIDE
    NS = S // SLC_BLOCK
    nq = S // TQ
    kspec = pl.BlockSpec((None, None, 2, 2 * NC, LANES), lambda b, g, q: (b, g, 0, 0, 0))
    return pl.pallas_call(
        functools.partial(_cmpsel_kernel, NC=NC, NS=NS),
        grid=(B, A_KV_GROUPS, nq),
        in_specs=[pl.BlockSpec((TQ, 2 * LANES), lambda b, g, q: (b * nq + q, g)),
                  kspec, kspec,
                  pl.BlockSpec((None, 2, 2 * TQ, NC), lambda b, g, q: (g, 0, 0, 0))],
        out_specs=[pl.BlockSpec((TQ, 2 * LANES), lambda b, g, q: (b * nq + q, g)),
                   pl.BlockSpec((None, None, LANES, TQ), lambda b, g, q: (b, g, 0, q))],
        out_shape=[jax.ShapeDtypeStruct((B * S, A_Q), F32),
                   jax.ShapeDtypeStruct((B, A_KV_GROUPS, LANES, S), BF16)],
        compiler_params=_cparams("parallel", "parallel", "parallel"),
        name="nsa_cmp_select",
    )(z, kcp, vcp, cb)


def _flash_update_t(s, vt, m, l, acc_ref):
    m_new = jnp.maximum(m, jnp.max(s, axis=0, keepdims=True))
    alpha = jnp.exp2(m - m_new)
    p = jnp.exp2(s - m_new)
    acc_ref[...] = alpha * acc_ref[...] + _dot(vt, p.astype(BF16))
    return m_new, alpha * l + jnp.sum(p, axis=0, keepdims=True)


def _split_heads_kv(slab, g):
    lane = lax.broadcasted_iota(jnp.int32, slab.shape, 1)
    own = jnp.where((lane < A_HEAD_DIM) == (g == 0), slab, 0.0)
    other = pltpu.roll(own, A_HEAD_DIM, axis=1)
    is0 = g == 0
    return jnp.where(is0, own, other), jnp.where(is0, other, own)


def _nsa_flash_kernel(*refs, S, selected):
    if selected:
        zq_ref, sbt_ref, zk_ref, zv_ref, tab_ref, o_ref, ke_ref, ko_ref, vt_ref, acc_ref = refs
    else:
        zq_ref, zk_ref, zv_ref, tab_ref, o_ref, ke_ref, ko_ref, vt_ref, acc_ref = refs
    g = pl.program_id(1)
    qt = pl.program_id(2)
    k_refs = (ke_ref, ko_ref)

    @pl.when(qt == 0)
    def _():
        ke, ko = _split_heads_kv(zk_ref[...], g)
        ke_ref[:, :LANES] = ke.astype(BF16)
        ko_ref[:, :LANES] = ko.astype(BF16)
        for c in range(S // TQ):
            vt_ref[c] = zv_ref[c * TQ:(c + 1) * TQ, :].T.astype(BF16)
        if selected:
            rowb = lax.broadcasted_iota(jnp.int32, (S, LANES), 0) >> 6
            lane = lax.broadcasted_iota(jnp.int32, (S, LANES), 1)
            onehot = jnp.where(rowb == lane, 1.0, 0.0).astype(BF16)
            ke_ref[:, LANES:] = onehot
            ko_ref[:, LANES:] = onehot

    zq = zq_ref[...] * (A_HEAD_DIM ** -0.5 * LOG2E)
    rhs = jnp.concatenate([zq[:, :LANES], zq[:, LANES:]], axis=0).T.astype(BF16)
    if selected:
        sbt = sbt_ref[...]
        rhs = jnp.concatenate([rhs, jnp.concatenate([sbt, sbt], axis=1)], axis=0)
    acc_ref[...] = jnp.zeros(acc_ref.shape, F32)
    vrow = pl.multiple_of(g * A_HEAD_DIM, A_HEAD_DIM)

    def chunk(kc, ti, carry):
        ks = pl.multiple_of(kc * TQ, TQ)
        vt = vt_ref[kc, pl.ds(vrow, A_HEAD_DIM), :]
        new = []
        for eo in range(2):
            s = _dot(k_refs[eo][pl.ds(ks, TQ), :], rhs) + tab_ref[eo, ti]
            new.append(_flash_update_t(s, vt, carry[eo][0], carry[eo][1], acc_ref.at[eo]))
        return tuple(new)

    init = (jnp.full((1, 2 * TQ), NEG, F32), jnp.zeros((1, 2 * TQ), F32))
    carry = chunk(qt, 0, (init, init))
    lo = 0 if selected else jnp.maximum(qt - 2, 0)
    carry = lax.fori_loop(lo, qt, lambda kc, cr: chunk(kc, jnp.minimum(qt - kc, 2), cr), carry)
    out_t = jnp.concatenate([acc_ref[0] * (1.0 / carry[0][1]), acc_ref[1] * (1.0 / carry[1][1])], axis=0)
    out = out_t.T
    o_ref[:, :LANES] = out[:TQ]
    o_ref[:, LANES:] = out[TQ:]


def _nsa_flash(z, sb, tab, zk_col, zv_col, B, S, selected):
    nq = S // TQ
    kw = 2 * LANES if selected else LANES
    in_specs = [pl.BlockSpec((TQ, 2 * LANES), lambda b, g, q: (b * nq + q, g))]
    args = [z]
    if selected:
        in_specs.append(pl.BlockSpec((None, None, LANES, TQ), lambda b, g, q: (b, g, 0, q)))
        args.append(sb)
    in_specs += [pl.BlockSpec((S, LANES), lambda b, g, q: (b, zk_col // LANES)),
                 pl.BlockSpec((S, LANES), lambda b, g, q: (b, zv_col // LANES)),
                 pl.BlockSpec((None, 2, 3, TQ, 2 * TQ), lambda b, g, q: (g, 0, 0, 0, 0))]
    args += [z, z, tab]
    return pl.pallas_call(
        functools.partial(_nsa_flash_kernel, S=S, selected=selected),
        grid=(B, A_KV_GROUPS, nq),
        in_specs=in_specs,
        out_specs=pl.BlockSpec((TQ, 2 * LANES), lambda b, g, q: (b * nq + q, g)),
        out_shape=jax.ShapeDtypeStruct((B * S, A_Q), F32),
        scratch_shapes=[pltpu.VMEM((S, kw), BF16), pltpu.VMEM((S, kw), BF16),
                        pltpu.VMEM((S // TQ, LANES, TQ), BF16),
                        pltpu.VMEM((2, A_HEAD_DIM, 2 * TQ), F32)],
        compiler_params=_cparams("parallel", "parallel", "arbitrary"),
        name="nsa_selected" if selected else "nsa_window",
    )(*args)


def _mla_proj_kernel(zqa_ref, zkva_ref, zkra_ref, zkrb_ref, qn_ref, kvn_ref, wqa_ref, wqb_ref,
                     wk_ref, wv_ref, caq_ref, cbq_ref, cak_ref, cbk_ref, q_ref, k_ref, v_ref):
    def rms(x, gain):
        return x * lax.rsqrt(jnp.mean(x * x, axis=-1, keepdims=True) + EPS) * gain

    nq = rms(zqa_ref[...], qn_ref[...]).astype(BF16)
    nkv = rms(zkva_ref[...], kvn_ref[...]).astype(BF16)
    qa = _dot_t(wqa_ref[...], nq)
    qb = _dot_t(wqb_ref[...], nq)
    kn = _dot(nkv, wk_ref[...])
    vt = _dot_t(wv_ref[...], nkv)
    caq, cbq = caq_ref[...], cbq_ref[...]
    kr = zkra_ref[...] * cak_ref[...] + zkrb_ref[...] * cbk_ref[...]
    for h in range(B_HEADS):
        sl = slice(h * LANES, (h + 1) * LANES)
        q_ref[h] = (qa[sl, :] * caq + qb[sl, :] * cbq).astype(BF16)
        k_ref[h] = (kn[:, sl] + kr).astype(BF16)
        v_ref[h] = vt[h * V_DIM:(h + 1) * V_DIM, :].astype(BF16)


def _mla_proj(z, qn, kvn, wqa, wqb, wk, wv, caq, cbq, cak, cbk, B, S):
    tm = TM
    nt = S // tm
    zspec = lambda w, col: pl.BlockSpec((tm, w), lambda b, i: (b * nt + i, col // w))
    full = lambda a: pl.BlockSpec(a.shape, lambda b, i: (0,) * a.ndim)
    tspec = pl.BlockSpec((tm, LANES), lambda b, i: (i, 0))
    tspec_t = pl.BlockSpec((LANES, tm), lambda b, i: (0, i))
    return pl.pallas_call(
        _mla_proj_kernel,
        grid=(B, nt),
        in_specs=[zspec(Q_LORA, Z_QA), zspec(LANES, Z_KVA), zspec(LANES, Z_KRA), zspec(LANES, Z_KRB),
                  full(qn), full(kvn), full(wqa), full(wqb), full(wk), full(wv),
                  tspec_t, tspec_t, tspec, tspec],
        out_specs=[pl.BlockSpec((None, B_HEADS, LANES, tm), lambda b, i: (b, 0, 0, i)),
                   pl.BlockSpec((None, B_HEADS, tm, LANES), lambda b, i: (b, 0, i, 0)),
                   pl.BlockSpec((None, B_HEADS, None, V_DIM, tm), lambda b, i: (b, 0, i, 0, 0))],
        out_shape=[jax.ShapeDtypeStruct((B, B_HEADS, LANES, S), BF16),
                   jax.ShapeDtypeStruct((B, B_HEADS, S, LANES), BF16),
                   jax.ShapeDtypeStruct((B, B_HEADS, nt, V_DIM, tm), BF16)],
        compiler_params=_cparams("parallel", "parallel"),
        name="mla_proj",
    )(z, z, z, z, qn, kvn, wqa, wqb, wk, wv, caq, cbq, cak, cbk)


def _mla_flash_kernel(qt_ref, k_ref, vt_ref, o_ref, acc_ref):
    qi = pl.program_id(2)
    acc_ref[...] = jnp.zeros(acc_ref.shape, F32)
    ri = lax.broadcasted_iota(jnp.int32, (TM, TM), 0)
    ci = lax.broadcasted_iota(jnp.int32, (TM, TM), 1)
    causal = ri <= ci

    def chunk(kc, carry, diag):
        ks = pl.multiple_of(kc * TM, TM)
        new = []
        for hh in range(2):
            s = _dot(k_ref[hh, pl.ds(ks, TM), :], qt_ref[hh])
            if diag:
                s = jnp.where(causal, s, NEG)
            new.append(_flash_update_t(s, vt_ref[hh, kc], carry[hh][0], carry[hh][1], acc_ref.at[hh]))
        return tuple(new)

    init = (jnp.full((1, TM), NEG, F32), jnp.zeros((1, TM), F32))
    carry = chunk(qi, (init, init), True)
    carry = lax.fori_loop(0, qi, lambda kc, cr: chunk(kc, cr, False), carry)
    out_t = jnp.concatenate([acc_ref[0] * (1.0 / carry[0][1]), acc_ref[1] * (1.0 / carry[1][1])], axis=0)
    o_ref[...] = out_t.T


def _mla_flash(qt, k, vt, B, S):
    nq = S // TM
    hp = B_HEADS // 2
    return pl.pallas_call(
        _mla_flash_kernel,
        grid=(B, hp, nq),
        in_specs=[pl.BlockSpec((None, 2, LANES, TM), lambda b, h, i: (b, h, 0, i)),
                  pl.BlockSpec((None, 2, S, LANES), lambda b, h, i: (b, h, 0, 0)),
                  pl.BlockSpec((None, 2, nq, V_DIM, TM), lambda b, h, i: (b, h, 0, 0, 0))],
        out_specs=pl.BlockSpec((TM, LANES), lambda b, h, i: (b * nq + i, h)),
        out_shape=jax.ShapeDtypeStruct((B * S, B_HEADS * V_DIM), F32),
        scratch_shapes=[pltpu.VMEM((2, V_DIM, TM), F32)],
        compiler_params=_cparams("parallel", "parallel", "arbitrary"),
        name="mla_flash",
    )(qt, k, vt)


def _rwkv_prep_kernel(zr_ref, zk_ref, zv_ref, zwa_ref, zg_ref, mu_r, mu_k, mu_v, mu_wa, mu_g,
                      w0_ref, a0_ref, kk_ref, ka_ref, rk_ref, w2_ref, a2_ref, g2_ref, bd_ref,
                      r_o, k_o, v_o, kk_o, be_o, ld_o, g_o, bo_o,
                      c_r, c_k, c_v, c_wa, c_g):
    t = pl.program_id(1)
    tm = zr_ref.shape[0]

    def shifted(z_ref, mu_ref, c_ref):
        x = z_ref[...]
        row = lax.broadcasted_iota(jnp.int32, x.shape, 0)
        prev = jnp.where(t == 0, 0.0, c_ref[0:1, :])
        xs = jnp.where(row == 0, prev, pltpu.roll(x, 1, axis=0))
        c_ref[0:1, :] = x[tm - 1:tm, :]
        return x + (xs - x) * mu_ref[...]

    r = shifted(zr_ref, mu_r, c_r)
    k = shifted(zk_ref, mu_k, c_k)
    v = shifted(zv_ref, mu_v, c_v)
    wa = shifted(zwa_ref, mu_wa, c_wa)
    gd = shifted(zg_ref, mu_g, c_g)
    w = w0_ref[...] + _dot(jnp.tanh(wa).astype(BF16), w2_ref[...])
    ld_o[...] = -jax.nn.sigmoid(w) * math.exp(-0.5)
    a = jax.nn.sigmoid(a0_ref[...] + _dot(wa.astype(BF16), a2_ref[...]))
    g_o[...] = _dot(jax.nn.sigmoid(gd).astype(BF16), g2_ref[...])
    kk = k * kk_ref[...]
    nsq = _dot(kk * kk, bd_ref[...], precision=HIGHEST)
    kk = kk / jnp.maximum(jnp.sqrt(nsq), 1e-12)
    k2 = k * (1.0 + (a - 1.0) * ka_ref[...])
    rks = _dot(r * k2 * rk_ref[...], bd_ref[...], precision=HIGHEST)
    r_o[...] = r
    k_o[...] = k2
    v_o[...] = v
    kk_o[...] = kk
    be_o[...] = kk * a
    bo_o[...] = rks * v


def _rwkv_prep(z, mus, w0, a0, k_k, k_a, r_k, w2p, a2p, g2, bd, B, S, tm=256):
    nt = S // tm
    W = C_WIDTH
    zspec = lambda w, col: pl.BlockSpec((tm, w), lambda b, i: (b * nt + i, col // w))
    full = lambda a: pl.BlockSpec(a.shape, lambda b, i: (0,) * a.ndim)
    ospec = pl.BlockSpec((tm, W), lambda b, i: (b * nt + i, 0))
    oshape = jax.ShapeDtypeStruct((B * S, W), F32)
    consts = list(mus) + [w0, a0, k_k, k_a, r_k, w2p, a2p, g2, bd]
    return pl.pallas_call(
        _rwkv_prep_kernel,
        grid=(B, nt),
        in_specs=[zspec(W, Z_CR), zspec(W, Z_CK), zspec(W, Z_CV), zspec(LANES, Z_CWA), zspec(LANES, Z_CG)]
                 + [full(a) for a in consts],
        out_specs=[ospec] * 8,
        out_shape=[oshape] * 8,
        scratch_shapes=[pltpu.VMEM((8, W), F32)] * 3 + [pltpu.VMEM((8, LANES), F32)] * 2,
        compiler_params=_cparams("parallel", "arbitrary"),
        name="rwkv_prep",
    )(z, z, z, z, z, *consts)


def _rwkv_chunk_kernel(r_ref, k_ref, v_ref, kk_ref, be_ref, ld_ref,
                       qa_o, bkt_o, yu_o, vp_o, pcb_o):
    C = CH
    ld = ld_ref[...]
    ri = lax.broadcasted_iota(jnp.int32, (C, C), 0)
    ci = lax.broadcasted_iota(jnp.int32, (C, C), 1)
    incl = ci <= ri
    strict = ci < ri
    cs = _dot(jnp.where(incl, 1.0, 0.0), ld, precision=HIGHEST)
    cl = cs[C - 1:C, :]
    ex, exn, exx, exc = jnp.exp(cs), jnp.exp(-cs), jnp.exp(cs - ld), jnp.exp(cl - cs)
    kk, be, k2, v = kk_ref[...], be_ref[...], k_ref[...], v_ref[...]
    at = -kk * exx
    rt = r_ref[...] * ex
    bt = be * exn
    kt = k2 * exn
    btc = be * exc
    ktc = k2 * exc
    pc = jnp.exp(cl)
    eye = jnp.where(ci == ri, 1.0, 0.0)
    lane = lax.broadcasted_iota(jnp.int32, (C, LANES), 1)
    heads = [(p, hh) for p in range(C_HEADS // 2) for hh in range(2)]
    sls = [slice(p * LANES, (p + 1) * LANES) for p in range(C_HEADS // 2)]
    lhs2 = [jnp.concatenate([at[:, sl], rt[:, sl]], axis=0).astype(BF16) for sl in sls]
    bm, km, vm, am = [], [], [], []
    for p, hh in heads:
        msk = (lane < C_HEAD_DIM) if hh == 0 else (lane >= C_HEAD_DIM)
        bm.append(jnp.where(msk, bt[:, sls[p]], 0.0).astype(BF16))
        km.append(jnp.where(msk, kt[:, sls[p]], 0.0).astype(BF16))
        vm.append(jnp.where(msk, v[:, sls[p]], 0.0).astype(BF16))
        am.append(jnp.where(msk, at[:, sls[p]], 0.0).astype(BF16))
    g1 = [_dot_t(lhs2[p], bm[i]) for i, (p, hh) in enumerate(heads)]
    g2 = [_dot_t(lhs2[p], km[i]) for i, (p, hh) in enumerate(heads)]
    lab = [jnp.where(strict, g[:C], 0.0) for g in g1]
    mrb = [jnp.where(incl, g[C:], 0.0).astype(BF16) for g in g1]
    lak = [jnp.where(strict, g[:C], 0.0).astype(BF16) for g in g2]
    mrk = [jnp.where(incl, g[C:], 0.0).astype(BF16) for g in g2]
    w2 = [_dot(a, b) for a, b in zip(lak, vm)]
    yk = [_dot(a, b) for a, b in zip(mrk, vm)]
    tinv = [eye + x for x in lab]
    lp = lab
    for _ in range(int(math.log2(C)) - 1):
        lpb = [x.astype(BF16) for x in lp]
        lp = [_dot(x, x) for x in lpb]
        tinv = [t + _dot(t.astype(BF16), x.astype(BF16)) for t, x in zip(tinv, lp)]
    au = [_dot(t.astype(BF16), jnp.concatenate([a, w.astype(BF16)], axis=1)) for t, a, w in zip(tinv, am, w2)]
    qy = [_dot(m, x.astype(BF16)) for m, x in zip(mrb, au)]
    for p in range(C_HEADS // 2):
        sl = sls[p]
        e, o = 2 * p, 2 * p + 1
        ahat = au[e][:, :LANES] + au[o][:, :LANES]
        uhat = au[e][:, LANES:] + au[o][:, LANES:]
        qhat = rt[:, sl] + qy[e][:, :LANES] + qy[o][:, :LANES]
        yi = qy[e][:, LANES:] + qy[o][:, LANES:] + yk[e] + yk[o]
        qa_o[p] = jnp.concatenate([qhat, ahat], axis=0).astype(BF16)
        yu_o[p] = jnp.concatenate([yi, uhat], axis=0)
        bkt_o[p] = jnp.concatenate([btc[:, sl], ktc[:, sl]], axis=0).T.astype(BF16)
        vp_o[p] = v[:, sl].astype(BF16)
        pcb_o[p] = jnp.broadcast_to(pc[:, sl], (LANES, LANES)).T


def _rwkv_chunk(r, k2, v, kk, be, ld, B, S):
    nch = S // CH
    W = C_WIDTH
    P = C_HEADS // 2
    ispec = pl.BlockSpec((CH, W), lambda b, c: (b * nch + c, 0))
    sq = lambda rows: pl.BlockSpec((None, None, P, rows, LANES), lambda b, c: (b, c, 0, 0, 0))
    shp = lambda rows, dt: jax.ShapeDtypeStruct((B, nch, P, rows, LANES), dt)
    return pl.pallas_call(
        _rwkv_chunk_kernel,
        grid=(B, nch),
        in_specs=[ispec] * 6,
        out_specs=[sq(2 * CH), sq(LANES), sq(2 * CH), sq(CH), sq(LANES)],
        out_shape=[shp(2 * CH, BF16), shp(LANES, BF16), shp(2 * CH, F32), shp(CH, BF16), shp(LANES, F32)],
        compiler_params=_cparams("parallel", "parallel"),
        name="rwkv_chunk",
    )(r, k2, v, kk, be, ld)


def _rwkv_state_kernel(qa_ref, bkt_ref, yu_ref, vp_ref, pcb_ref, y_ref, ap_ref):
    @pl.when(pl.program_id(1) == 0)
    def _():
        ap_ref[...] = jnp.zeros(ap_ref.shape, F32)

    ri = lax.broadcasted_iota(jnp.int32, (LANES, LANES), 0)
    ci = lax.broadcasted_iota(jnp.int32, (LANES, LANES), 1)
    same_head = (ri < C_HEAD_DIM) == (ci < C_HEAD_DIM)
    for p in range(C_HEADS // 2):
        a = ap_ref[p]
        x = _dot(qa_ref[p], a.astype(BF16))
        yu = yu_ref[p]
        y_ref[:, p * LANES:(p + 1) * LANES] = x[:CH] + yu[:CH]
        u = x[CH:] + yu[CH:]
        uv = jnp.concatenate([u.astype(BF16), vp_ref[p]], axis=0)
        upd = _dot(bkt_ref[p], uv)
        ap_ref[p] = pcb_ref[p] * a + jnp.where(same_head, upd, 0.0)


def _rwkv_state(qa, bkt, yu, vp, pcb, B, S):
    nch = S // CH
    P = C_HEADS // 2
    sq = lambda rows: pl.BlockSpec((None, None, P, rows, LANES), lambda b, c: (b, c, 0, 0, 0))
    return pl.pallas_call(
        _rwkv_state_kernel,
        grid=(B, nch),
        in_specs=[sq(2 * CH), sq(LANES), sq(2 * CH), sq(CH), sq(LANES)],
        out_specs=pl.BlockSpec((CH, C_WIDTH), lambda b, c: (b * nch + c, 0)),
        out_shape=jax.ShapeDtypeStruct((B * S, C_WIDTH), F32),
        scratch_shapes=[pltpu.VMEM((P, LANES, LANES), F32)],
        compiler_params=_cparams("parallel", "arbitrary"),
        name="rwkv_state",
    )(qa, bkt, yu, vp, pcb)


def _merge_kernel(oc_ref, os_ref, ow_ref, zag_ref, yb_ref, yr_ref, bo_ref, gg_ref,
                  zga_ref, zgb_ref, zgc_ref, x_ref, wbr_ref, wout_ref, ge_ref, bdm_ref,
                  ln_ref, gain_ref, gt_ref, o_ref):
    sg = jax.nn.sigmoid(zag_ref[...])
    ya = (_dot(sg, ge_ref[0], precision=HIGHEST) * oc_ref[...]
          + _dot(sg, ge_ref[1], precision=HIGHEST) * os_ref[...]
          + _dot(sg, ge_ref[2], precision=HIGHEST) * ow_ref[...])
    yr = yr_ref[...]
    mean = _dot(yr, bdm_ref[...], precision=HIGHEST)
    d = yr - mean
    var = _dot(d * d, bdm_ref[...], precision=HIGHEST)
    yn = d * lax.rsqrt(var + GN_EPS) * ln_ref[0:1, :] + ln_ref[1:2, :]
    yc = (yn + bo_ref[...]) * gg_ref[...]
    merged = (jax.nn.sigmoid(zga_ref[...]) * _dot(ya.astype(BF16), wbr_ref[0:A_Q, :])
              + jax.nn.sigmoid(zgb_ref[...]) * _dot(yb_ref[...].astype(BF16), wbr_ref[A_Q:2 * A_Q, :])
              + jax.nn.sigmoid(zgc_ref[...]) * _dot(yc.astype(BF16), wbr_ref[2 * A_Q:3 * A_Q, :]))
    y = _dot(merged.astype(BF16), wout_ref[...])
    yn2 = y * lax.rsqrt(jnp.mean(y * y, axis=-1, keepdims=True) + EPS) * gain_ref[...]
    o_ref[...] = x_ref[...] + gt_ref[0] * yn2


def _merge(oc, os_, ow, z, yb, yr, bo, gg, x2, wbr, wout, gexp, bdm, ln, gain, gt, S, tm=256):
    N, D = x2.shape
    tpb = S // tm
    W = A_Q
    row = lambda w: pl.BlockSpec((tm, w), lambda i: (i, 0))
    zspec = lambda w, col: pl.BlockSpec((tm, w), lambda i: (i, col // w))
    full = lambda a: pl.BlockSpec(a.shape, lambda i: (0,) * a.ndim)
    return pl.pallas_call(
        _merge_kernel,
        grid=(N // tm,),
        in_specs=[row(W), row(W), row(W), zspec(LANES, Z_AG), row(W), row(W), row(W), row(W),
                  zspec(D, Z_ZG), zspec(D, Z_ZG + D), zspec(D, Z_ZG + 2 * D), row(D),
                  full(wbr), full(wout), full(gexp), full(bdm), full(ln),
                  pl.BlockSpec((1, D), lambda i: (0, 0)),
                  pl.BlockSpec((1, 1, D), lambda i: (i // tpb, 0, 0))],
        out_specs=row(D),
        out_shape=jax.ShapeDtypeStruct((N, D), F32),
        compiler_params=_cparams("parallel"),
        name="merge_out",
    )(oc, os_, ow, z, yb, yr, bo, gg, z, z, z, x2, wbr, wout, gexp, bdm, ln, gain.reshape(1, D), gt)


def _ffn_down_kernel(u_ref, up_ref, cw_ref, cb_ref, wd_ref, x_ref, gain_ref, gt_ref, o_ref, *, tpb):
    i = pl.program_id(0)
    tm = u_ref.shape[0]
    first = (i % tpb) == 0
    fc = 256
    row = lax.broadcasted_iota(jnp.int32, (tm, fc), 0)

    def conv(c0):
        u = u_ref[:, c0:c0 + fc]
        p1 = jnp.where(first, 0.0, up_ref[7:8, c0:c0 + fc])
        p2 = jnp.where(first, 0.0, up_ref[6:7, c0:c0 + fc])
        u1 = jnp.where(row == 0, p1, pltpu.roll(u, 1, axis=0))
        u2 = jnp.where(row == 0, p2, jnp.where(row == 1, p1, pltpu.roll(u, 2, axis=0)))
        return (cw_ref[0:1, c0:c0 + fc] * u2 + cw_ref[1:2, c0:c0 + fc] * u1
                + cw_ref[2:3, c0:c0 + fc] * u + cb_ref[:, c0:c0 + fc])

    f = jnp.zeros((tm, D_MODEL), F32)
    for c in range(D_FF // fc):
        a = _gelu(conv(c * fc)) * conv(D_FF + c * fc)
        f = f + _dot(a.astype(BF16), wd_ref[c * fc:(c + 1) * fc, :])
    fn = f * lax.rsqrt(jnp.mean(f * f, axis=-1, keepdims=True) + EPS) * gain_ref[...]
    o_ref[...] = x_ref[...] + gt_ref[0] * fn


def _ffn_down(u, conv_w, conv_b, wd, x2, gain, gt, S, tm=256):
    N, D = x2.shape
    F2 = u.shape[1]
    tpb = S // tm
    full = lambda a: pl.BlockSpec(a.shape, lambda i: (0,) * a.ndim)
    return pl.pallas_call(
        functools.partial(_ffn_down_kernel, tpb=tpb),
        grid=(N // tm,),
        in_specs=[pl.BlockSpec((tm, F2), lambda i: (i, 0)),
                  pl.BlockSpec((8, F2), lambda i: (jnp.maximum(i * (tm // 8) - 1, 0), 0)),
                  full(conv_w), pl.BlockSpec((1, F2), lambda i: (0, 0)), full(wd),
                  pl.BlockSpec((tm, D), lambda i: (i, 0)),
                  pl.BlockSpec((1, D), lambda i: (0, 0)),
                  pl.BlockSpec((1, 1, D), lambda i: (i // tpb, 0, 0))],
        out_specs=pl.BlockSpec((tm, D), lambda i: (i, 0)),
        out_shape=jax.ShapeDtypeStruct((N, D), F32),
        compiler_params=_cparams("parallel"),
        name="ffn_down",
    )(u, u, conv_w, conv_b.reshape(1, F2), wd, x2, gain.reshape(1, D), gt)


def _t5_bucket_np(dist):
    d = np.maximum(dist, 0)
    max_exact = N_BUCKETS // 2
    large = max_exact + (np.log(np.maximum(d, 1).astype(np.float32) / max_exact)
                         / math.log(MAX_DISTANCE / max_exact) * (N_BUCKETS - max_exact)).astype(np.int32)
    return np.where(d < max_exact, d, np.minimum(large, N_BUCKETS - 1))


def _bias_table(rel_bias, dist, valid):
    tq, ncol = dist.shape
    G, HPG = A_KV_GROUPS, A_HEADS // A_KV_GROUPS
    rb = rel_bias[jnp.asarray(_t5_bucket_np(dist))]
    rb = jnp.where(jnp.asarray(valid)[:, :, None], rb, NEG)
    rb = rb.transpose(2, 0, 1).reshape(G, HPG // 2, 2, tq, ncol)
    return rb.transpose(0, 2, 1, 3, 4).reshape(G, 2, 2 * tq, ncol)


def _nsa_tables(rel_bias, S):
    NC = S // CMP_STRIDE
    i = np.arange(TQ)[:, None]
    c = np.arange(NC)[None, :]
    dc = i - CMP_STRIDE * c + CMP_STRIDE * NC - TQ - (CMP_BLOCK - 1)
    cb = _bias_table(rel_bias, dc, dc >= 0)
    j = np.arange(TQ)[None, :]
    d0, d1 = i - j, TQ + i - j
    far = np.full((TQ, TQ), MAX_DISTANCE)
    tr = lambda t: jnp.swapaxes(t, -1, -2) * LOG2E
    slc = jnp.stack([tr(_bias_table(rel_bias, d0, d0 >= 0)), tr(_bias_table(rel_bias, d1, d1 >= 0)),
                     tr(_bias_table(rel_bias, far, far >= 0))], axis=2)
    d2 = 2 * TQ + i - j
    win = jnp.stack([tr(_bias_table(rel_bias, d, (d >= 0) & (d < WINDOW))) for d in (d0, d1, d2)], axis=2)
    return cb, slc, win


def _pad_cols(w, n):
    return jnp.pad(w, ((0, 0), (0, n - w.shape[1])))


def _prep_w_in(w):
    D = w.shape[0]
    za, zb, zc, zg = jnp.split(w, np.cumsum([A_COLS, B_COLS, C_COLS]).tolist(), axis=1)
    a_parts = jnp.split(za, np.cumsum([A_Q] + [A_KV] * 6).tolist(), axis=1)
    qa, kva, kr = jnp.split(zb, [Q_LORA, Q_LORA + KV_LORA], axis=1)
    half = ROPE_DIM // 2
    kr_rot = jnp.concatenate([-kr[:, half:], kr[:, :half]], axis=1)
    z64 = jnp.zeros((D, NOPE_DIM), w.dtype)
    z32 = jnp.zeros((D, LANES - NOPE_DIM - ROPE_DIM), w.dtype)
    c_r, c_k, c_v, c_wd, c_ad, c_gd = jnp.split(
        zc, np.cumsum([C_WIDTH] * 3 + [DECAY_LORA, AAA_LORA]).tolist(), axis=1)
    cols = [a_parts[0], qa] + a_parts[1:7] + [
        _pad_cols(a_parts[7], LANES), kva,
        jnp.concatenate([z64, kr, z32], axis=1),
        jnp.concatenate([z64, kr_rot, z32], axis=1),
        c_r, c_k, c_v, c_wd, c_ad, c_gd, jnp.zeros((D, Z_ZG - Z_CG - LANES), w.dtype), zg]
    out = jnp.concatenate(cols, axis=1)
    assert out.shape[1] == Z_COLS
    return out.astype(BF16)


def _prep_mla(w_uq, w_ukv):
    dq = NOPE_DIM + ROPE_DIM
    half = ROPE_DIM // 2
    wq = w_uq.reshape(Q_LORA, B_HEADS, dq)
    nope, r1, r2 = wq[..., :NOPE_DIM], wq[..., NOPE_DIM:NOPE_DIM + half], wq[..., NOPE_DIM + half:]
    zq = jnp.zeros((Q_LORA, B_HEADS, LANES - dq), w_uq.dtype)
    wqa = jnp.concatenate([nope, r1, r2, zq], axis=-1).reshape(Q_LORA, B_HEADS * LANES)
    wqb = jnp.concatenate([jnp.zeros_like(nope), -r2, r1, zq], axis=-1).reshape(Q_LORA, B_HEADS * LANES)
    wkv = w_ukv.reshape(KV_LORA, B_HEADS, NOPE_DIM + V_DIM)
    kn, vv = wkv[..., :NOPE_DIM], wkv[..., NOPE_DIM:]
    wk = jnp.concatenate([kn, jnp.zeros_like(kn)], axis=-1).reshape(KV_LORA, B_HEADS * LANES)
    wv = vv.reshape(KV_LORA, B_HEADS * V_DIM)
    return wqa.T.astype(BF16), wqb.T.astype(BF16), wk.astype(BF16), wv.T.astype(BF16)


def _rope_tables(S):
    half = ROPE_DIM // 2
    inv = ROPE_THETA ** (-jnp.arange(half, dtype=F32) / half)
    ang = jnp.arange(S, dtype=F32)[:, None] * inv
    cos2 = jnp.tile(jnp.cos(ang), (1, 2))
    sin2 = jnp.tile(jnp.sin(ang), (1, 2))
    scale = (NOPE_DIM + ROPE_DIM) ** -0.5 * LOG2E
    one = jnp.ones((S, NOPE_DIM), F32)
    z64 = jnp.zeros((S, NOPE_DIM), F32)
    z32 = jnp.zeros((S, LANES - NOPE_DIM - ROPE_DIM), F32)
    caq = (jnp.concatenate([one, cos2, z32], axis=1) * scale).T
    cbq = (jnp.concatenate([z64, sin2, z32], axis=1) * scale).T
    cak = jnp.concatenate([z64, cos2, z32], axis=1)
    cbk = jnp.concatenate([z64, sin2, z32], axis=1)
    return caq, cbq, cak, cbk


def _prep_compress(cmp_pos, cmp_w1, cmp_w2):
    Dh = A_HEAD_DIM
    w1 = cmp_w1.reshape(2, CMP_BLOCK, Dh, Dh)
    z = jnp.zeros_like(w1)
    w1bd = jnp.concatenate([jnp.concatenate([w1, z], axis=-1), jnp.concatenate([z, w1], axis=-1)], axis=-2)
    posrow = jnp.broadcast_to(cmp_pos.reshape(2, 1, CMP_BLOCK * Dh), (2, 8, CMP_BLOCK * Dh))
    w1cat = jnp.concatenate([cmp_w1, cmp_w1], axis=-1)
    zz = jnp.zeros((2, Dh, Dh), cmp_w2.dtype)
    blk = lambda a, b, c, d: jnp.concatenate(
        [jnp.concatenate([a, b], axis=-1), jnp.concatenate([c, d], axis=-1)], axis=-2)
    w2v = jnp.stack([jnp.stack([blk(cmp_w2, zz, zz, zz), blk(zz, cmp_w2, zz, zz)], axis=1),
                     jnp.stack([blk(zz, zz, cmp_w2, zz), blk(zz, zz, zz, cmp_w2)], axis=1)], axis=1)
    return w1bd.astype(BF16), posrow, w1cat.astype(BF16), w2v.astype(BF16)


def _gate_expand():
    e = np.zeros((3, LANES, A_Q), np.float32)
    for h in range(A_HEADS):
        for r in range(3):
            e[r, 3 * h + r, h * A_HEAD_DIM:(h + 1) * A_HEAD_DIM] = 1.0
    return jnp.asarray(e)


def _block_diag_ones(scale):
    idx = np.arange(C_WIDTH) // C_HEAD_DIM
    return jnp.asarray((idx[:, None] == idx[None, :]).astype(np.float32) * scale)


def kernel(x, c, rel_bias, ada_w, ada_b, norm_gain, w_in, nsa_cmp_pos, nsa_cmp_w1, nsa_cmp_w2, mla_q_norm, mla_kv_norm, mla_w_uq, mla_w_ukv, rwkv_mu, rwkv_w0, rwkv_a0, rwkv_k_k, rwkv_k_a, rwkv_w2, rwkv_a2, rwkv_g2, rwkv_r_k, rwkv_ln, w_branch, w_out, ffn_up, ffn_conv_w, ffn_conv_b, ffn_down):
    B, S, D = x.shape
    L = ada_w.shape[0]
    assert S % TQ == 0 and S % TM == 0 and S // SLC_BLOCK <= A_HEAD_DIM and S >= 2 * TQ
    mod = _adaln(c, ada_w, ada_b)
    cb, slc_tab, win_tab = _nsa_tables(rel_bias, S)
    caq, cbq, cak, cbk = _rope_tables(S)
    gexp = _gate_expand()
    bd1 = _block_diag_ones(1.0)
    bdm = _block_diag_ones(1.0 / C_HEAD_DIM)
    row = lambda v: v.reshape(1, -1)
    x2 = x.reshape(B * S, D)
    for l in range(L):
        m6 = mod[l].reshape(B, 6, 1, D)
        sh1, sc1, gt1, sh2, sc2, gt2 = (m6[:, i] for i in range(6))
        z = _norm_mod_matmul(x2, norm_gain[l, 0], sc1, sh1, _prep_w_in(w_in[l]), S, tm=512, tn=1024)
        kcp, vcp = _compress(z, *_prep_compress(nsa_cmp_pos[l], nsa_cmp_w1[l], nsa_cmp_w2[l]), B, S)
        oc, sb = _cmpsel(z, kcp, vcp, cb, B, S)
        os_ = _nsa_flash(z, sb, slc_tab, Z_AKS, Z_AVS, B, S, selected=True)
        ow = _nsa_flash(z, None, win_tab, Z_AKW, Z_AVW, B, S, selected=False)
        wqa, wqb, wk, wv = _prep_mla(mla_w_uq[l], mla_w_ukv[l])
        q, k, v = _mla_proj(z, row(mla_q_norm[l]), row(mla_kv_norm[l]), wqa, wqb, wk, wv,
                            caq, cbq, cak, cbk, B, S)
        yb = _mla_flash(q, k, v, B, S)
        mu = rwkv_mu[l]
        o = 3 * C_WIDTH
        mus = [row(mu[:C_WIDTH]), row(mu[C_WIDTH:2 * C_WIDTH]), row(mu[2 * C_WIDTH:o]),
               row(mu[o:o + LANES]), row(mu[o + LANES:])]
        w2p = jnp.concatenate([rwkv_w2[l], jnp.zeros_like(rwkv_a2[l])], axis=0).astype(BF16)
        a2p = jnp.concatenate([jnp.zeros_like(rwkv_w2[l]), rwkv_a2[l]], axis=0).astype(BF16)
        rr, k2, vv, kk, be, ld, gg, bo = _rwkv_prep(
            z, mus, row(rwkv_w0[l]), row(rwkv_a0[l]), row(rwkv_k_k[l]), row(rwkv_k_a[l]),
            row(rwkv_r_k[l]), w2p, a2p, rwkv_g2[l].astype(BF16), bd1, B, S)
        qa_, bkt, yu, vp, pcb = _rwkv_chunk(rr, k2, vv, kk, be, ld, B, S)
        yr = _rwkv_state(qa_, bkt, yu, vp, pcb, B, S)
        x2 = _merge(oc, os_, ow, z, yb, yr, bo, gg, x2, w_branch[l].astype(BF16), w_out[l].astype(BF16),
                    gexp, bdm, rwkv_ln[l], norm_gain[l, 1], gt1, S)
        u = _norm_mod_matmul(x2, norm_gain[l, 2], sc2, sh2, ffn_up[l].astype(BF16), S, tm=512, tn=512)
        x2 = _ffn_down(u, ffn_conv_w[l], ffn_conv_b[l], ffn_down[l].astype(BF16), x2,
                       norm_gain[l, 3], gt2, S)
    return x2.reshape(B, S, D)
```

```python
import functools
import math

import jax
import jax.numpy as jnp
import numpy as np
from jax import lax
from jax.experimental import pallas as pl
from jax.experimental.pallas import tpu as pltpu

F32 = jnp.float32
BF16 = jnp.bfloat16
HIGHEST = lax.Precision.HIGHEST

D_MODEL = 1024
DEPTH = 4
A_HEADS, A_KV_GROUPS, A_HEAD_DIM = 8, 2, 64
CMP_BLOCK, CMP_STRIDE = 32, 16
SLC_BLOCK, SLC_TOPK, N_LOCAL_BLOCKS = 64, 16, 2
WINDOW = 512
FORCE_SCORE = 1e9
B_HEADS, Q_LORA, KV_LORA, NOPE_DIM, ROPE_DIM, V_DIM = 8, 256, 128, 64, 32, 64
ROPE_THETA = 10000.0
C_HEADS, C_HEAD_DIM = 8, 64
C_WIDTH = C_HEADS * C_HEAD_DIM
DECAY_LORA, AAA_LORA, GATE_LORA = 64, 64, 128
GN_EPS = 64e-5
N_BUCKETS, MAX_DISTANCE = 32, 128
D_FF = 2816
EPS = 1e-6
NEG = -1e30
LOG2E = math.log2(math.e)

A_Q = A_HEADS * A_HEAD_DIM
A_KV = A_KV_GROUPS * A_HEAD_DIM
A_GATE = 3 * A_HEADS
A_COLS = A_Q + 6 * A_KV + A_GATE
B_COLS = Q_LORA + KV_LORA + ROPE_DIM
C_COLS = 3 * C_WIDTH + DECAY_LORA + AAA_LORA + GATE_LORA

LANES = 128
PREV_ROWS = 16
VMEM_LIMIT = 48 * 1024 * 1024

Z_AQ = 0
Z_QA = 512
Z_AKC, Z_AVC, Z_AKS, Z_AVS, Z_AKW, Z_AVW = 768, 896, 1024, 1152, 1280, 1408
Z_AG = 1536
Z_KVA = 1664
Z_KRA = 1792
Z_KRB = 1920
Z_CR, Z_CK, Z_CV = 2048, 2560, 3072
Z_CWA = 3584
Z_CG = 3712
Z_ZG = 4096
Z_COLS = 7168

TQ = 256
TM = 512
CH = 64


def _cparams(*sem):
    return pltpu.CompilerParams(dimension_semantics=sem, vmem_limit_bytes=VMEM_LIMIT)


def _gelu(x):
    return 0.5 * x * (1.0 + jnp.tanh(0.7978845608028654 * (x + 0.044715 * (x * x * x))))


def _dot(a, b, **kw):
    return jnp.dot(a, b, preferred_element_type=F32, **kw)


def _split3(x):
    hi = x.astype(BF16)
    r = x - hi.astype(F32)
    mid = r.astype(BF16)
    return hi, mid, (r - mid.astype(F32)).astype(BF16)


def _dot3(x, w):
    return sum(_dot(t, w) for t in _split3(x))


def _dot_t(a, b):
    return lax.dot_general(a, b, (((1,), (1,)), ((), ())), preferred_element_type=F32)


def _adaln_kernel(c_ref, w_ref, b_ref, o_ref):
    c = c_ref[...]
    cond = c * jax.nn.sigmoid(c)
    o_ref[0] = _dot(cond.astype(BF16), w_ref[0].astype(BF16)) + b_ref[0]


def _adaln(c, ada_w, ada_b):
    L, D, N6 = ada_w.shape
    B = c.shape[0]
    tn = 1536
    return pl.pallas_call(
        _adaln_kernel,
        grid=(L, N6 // tn),
        in_specs=[pl.BlockSpec((B, D), lambda l, j: (0, 0)),
                  pl.BlockSpec((1, D, tn), lambda l, j: (l, 0, j)),
                  pl.BlockSpec((1, 1, tn), lambda l, j: (l, 0, j))],
        out_specs=pl.BlockSpec((1, B, tn), lambda l, j: (l, 0, j)),
        out_shape=jax.ShapeDtypeStruct((L, B, N6), F32),
        compiler_params=_cparams("parallel", "parallel"),
        name="adaln",
    )(c, ada_w, ada_b.reshape(L, 1, N6))


def _nmm_kernel(x_ref, g_ref, sc_ref, sh_ref, w_ref, o_ref, h_ref):
    @pl.when(pl.program_id(1) == 0)
    def _():
        x = x_ref[...]
        y = x * lax.rsqrt(jnp.mean(x * x, axis=-1, keepdims=True) + EPS)
        h = (y * g_ref[...]) * (1.0 + sc_ref[0]) + sh_ref[0]
        h_ref[...] = h.astype(BF16)

    o_ref[...] = _dot(h_ref[...], w_ref[...]).astype(o_ref.dtype)


def _norm_mod_matmul(x2, gain, sc, sh, w, S, tn, out_dtype=BF16):
    N, D = x2.shape
    NC = w.shape[1]
    tm = min(1024, S)
    tpb = S // tm
    return pl.pallas_call(
        _nmm_kernel,
        grid=(N // tm, NC // tn),
        in_specs=[pl.BlockSpec((tm, D), lambda i, j: (i, 0)),
                  pl.BlockSpec((1, D), lambda i, j: (0, 0)),
                  pl.BlockSpec((1, 1, D), lambda i, j: (i // tpb, 0, 0)),
                  pl.BlockSpec((1, 1, D), lambda i, j: (i // tpb, 0, 0)),
                  pl.BlockSpec((D, tn), lambda i, j: (0, j))],
        out_specs=pl.BlockSpec((tm, tn), lambda i, j: (i, j)),
        out_shape=jax.ShapeDtypeStruct((N, NC), out_dtype),
        scratch_shapes=[pltpu.VMEM((tm, D), BF16)],
        compiler_params=_cparams("parallel", "arbitrary"),
        name="norm_mod_matmul",
    )(x2, gain.reshape(1, D), sc, sh, w)


def _compress_kernel(zk_ref, zv_ref, w1_ref, pos_ref, w1c_ref, w2_ref, kc_ref, vc_ref, zf_ref, *, NC):
    for kind, (z_ref, o_ref) in enumerate(((zk_ref, kc_ref), (zv_ref, vc_ref))):
        pa = jnp.zeros((NC, LANES), F32)
        pb = jnp.zeros((NC, LANES), F32)
        zf_ref[...] = z_ref[...].astype(F32)
        for l in range(CMP_STRIDE):
            xl = zf_ref[pl.ds(l, NC, stride=CMP_STRIDE), :].astype(BF16)
            pa = pa + _dot(xl, w1_ref[kind, l])
            pb = pb + _dot(xl, w1_ref[kind, CMP_STRIDE + l])
        posb = _dot(pos_ref[kind].astype(BF16), w1c_ref[kind])[0:1, :]
        h = pa + pltpu.roll(pb, NC - 1, axis=0) + posb
        act = _gelu(h).astype(BF16)
        row = lax.broadcasted_iota(jnp.int32, (NC, LANES), 0)
        for g in range(A_KV_GROUPS):
            for eo in range(2):
                out = _dot(act, w2_ref[kind, g, eo])
                out = jnp.where(row < NC - 1, out, 0.0)
                o_ref[g, eo, pl.ds(0, NC), :] = jnp.zeros((NC, LANES), BF16)
                o_ref[g, eo, pl.ds(NC, NC), :] = out.astype(BF16)


def _compress(z, w1bd, posrow, w1cat, w2v, B, S):
    NC = S // CMP_STRIDE
    out = jax.ShapeDtypeStruct((B, A_KV_GROUPS, 2, 2 * NC, LANES), BF16)
    ospec = pl.BlockSpec((None, A_KV_GROUPS, 2, 2 * NC, LANES), lambda b: (b, 0, 0, 0, 0))
    full = lambda a: pl.BlockSpec(a.shape, lambda b: (0,) * a.ndim)
    return pl.pallas_call(
        functools.partial(_compress_kernel, NC=NC),
        grid=(B,),
        in_specs=[pl.BlockSpec((S, LANES), lambda b: (b, Z_AKC // LANES)),
                  pl.BlockSpec((S, LANES), lambda b: (b, Z_AVC // LANES)),
                  full(w1bd), full(posrow), full(w1cat), full(w2v)],
        out_specs=[ospec, ospec],
        out_shape=[out, out],
        scratch_shapes=[pltpu.VMEM((S, LANES), F32)],
        compiler_params=_cparams("parallel"),
        name="nsa_compress",
    )(z, z, w1bd, posrow, w1cat, w2v)


def _stack_pairs(zq, scale):
    q = zq.astype(F32) * scale
    return jnp.concatenate([q[:, :LANES], q[:, LANES:]], axis=0).astype(BF16)


def _cmpsel_kernel(zq_ref, kc_ref, vc_ref, cb_ref, oc_ref, sb_ref, *, NC, NS):
    qt = pl.program_id(2)
    r = TQ // CMP_STRIDE
    lhs = _stack_pairs(zq_ref[...], A_HEAD_DIM ** -0.5)
    st = pl.multiple_of(r * (qt + 1), 16)
    col = lax.broadcasted_iota(jnp.int32, (2 * TQ, NC), 1)
    exists = col >= NC - r * (qt + 1)
    acc = jnp.zeros((2 * TQ, LANES), F32)
    psum = jnp.zeros((TQ, NC), F32)
    for eo in range(2):
        kwin = kc_ref[eo, pl.ds(st, NC), :]
        vwin = vc_ref[eo, pl.ds(st, NC), :]
        logits = _dot_t(lhs, kwin) + cb_ref[eo]
        logits = jnp.where(exists, logits, NEG)
        m = jnp.max(logits, axis=-1, keepdims=True)
        p = jnp.where(logits > 0.1 * NEG, jnp.exp(logits - m), 0.0)
        l = jnp.maximum(jnp.sum(p, axis=-1, keepdims=True), 1e-30)
        p = p * (1.0 / l)
        acc = acc + _dot(p.astype(BF16), vwin)
        psum = psum + p[:TQ] + p[TQ:]
    oc_ref[:, :LANES] = acc[:TQ]
    oc_ref[:, LANES:] = acc[TQ:]

    jj = lax.broadcasted_iota(jnp.int32, (LANES, NC), 0)
    nn = lax.broadcasted_iota(jnp.int32, (LANES, NC), 1) + (r * (qt + 1) - NC)
    delta = 4 * jj - nn
    mt = jnp.where((delta == 0) | (delta == 4), 1.0, 0.0) + jnp.where((delta >= 1) & (delta <= 3), 2.0, 0.0)
    mt = jnp.where(jj < NS, mt, 0.0)
    mtb = mt.astype(BF16)
    imp_t = sum(_dot_t(mtb, t) for t in _split3(psum))
    jb = lax.broadcasted_iota(jnp.int32, (LANES, TQ), 0)
    tpos = qt * TQ + lax.broadcasted_iota(jnp.int32, (LANES, TQ), 1)
    back = (tpos >> 6) - jb
    forced = (jb == 0) | ((back >= 0) & (back < N_LOCAL_BLOCKS))
    score = jnp.where(forced, FORCE_SCORE, jnp.where(back >= 0, imp_t, -1.0))
    score = jnp.where(jb < NS, score, -2.0)
    rank = jnp.zeros((LANES, TQ), F32)
    for jp in range(NS):
        row = score[jp:jp + 1, :]
        beats = (row > score) | ((row == score) & (jb > jp))
        rank = rank + jnp.where(beats, 1.0, 0.0)
    sel = (rank < float(min(SLC_TOPK, NS))) & (back >= 0) & (jb < NS)
    sb_ref[...] = jnp.where(sel, 0.0, jnp.where(jb < NS, NEG, 0.0)).astype(BF16)


def _cmpsel(z, kcp, vcp, cb, B, S):
    NC = S // CMP_STRIDE
    NS = S // SLC_BLOCK
    nq = S // TQ
    kspec = pl.BlockSpec((None, None, 2, 2 * NC, LANES), lambda b, g, q: (b, g, 0, 0, 0))
    return pl.pallas_call(
        functools.partial(_cmpsel_kernel, NC=NC, NS=NS),
        grid=(B, A_KV_GROUPS, nq),
        in_specs=[pl.BlockSpec((TQ, 2 * LANES), lambda b, g, q: (b * nq + q, g)),
                  kspec, kspec,
                  pl.BlockSpec((None, 2, 2 * TQ, NC), lambda b, g, q: (g, 0, 0, 0))],
        out_specs=[pl.BlockSpec((TQ, 2 * LANES), lambda b, g, q: (b * nq + q, g)),
                   pl.BlockSpec((None, None, LANES, TQ), lambda b, g, q: (b, g, 0, q))],
        out_shape=[jax.ShapeDtypeStruct((B * S, A_Q), F32),
                   jax.ShapeDtypeStruct((B, A_KV_GROUPS, LANES, S), BF16)],
        compiler_params=_cparams("parallel", "parallel", "parallel"),
        name="nsa_cmp_select",
    )(z, kcp, vcp, cb)


def _flash_update_t(ss, vts, carry, acc_refs):
    m_new = [jnp.maximum(c[0], jnp.max(s, axis=0, keepdims=True)) for s, c in zip(ss, carry)]
    ps = [jnp.exp2(s - m) for s, m in zip(ss, m_new)]
    alpha = [jnp.exp2(c[0] - m) for c, m in zip(carry, m_new)]
    pv = [_dot(vt, p.astype(BF16)) for vt, p in zip(vts, ps)]
    for a, ref, x in zip(alpha, acc_refs, pv):
        ref[...] = a * ref[...] + x
    return tuple((m, a * c[1] + jnp.sum(p, axis=0, keepdims=True))
                 for m, a, c, p in zip(m_new, alpha, carry, ps))


def _split_heads_kv(slab, g):
    lane = lax.broadcasted_iota(jnp.int32, slab.shape, 1)
    own = jnp.where((lane < A_HEAD_DIM) == (g == 0), slab, 0.0)
    other = pltpu.roll(own, A_HEAD_DIM, axis=1)
    is0 = g == 0
    return jnp.where(is0, own, other), jnp.where(is0, other, own)


def _nsa_flash_kernel(*refs, S, selected):
    if selected:
        zq_ref, sbt_ref, zk_ref, zv_ref, tab_ref, o_ref, ke_ref, ko_ref, vt_ref, acc_ref = refs
    else:
        zq_ref, zk_ref, zv_ref, tab_ref, o_ref, ke_ref, ko_ref, vt_ref, acc_ref = refs
    g = pl.program_id(1)
    qt = pl.program_id(2)
    k_refs = (ke_ref, ko_ref)

    @pl.when(qt == 0)
    def _():
        ke, ko = _split_heads_kv(zk_ref[...].astype(F32), g)
        ke_ref[:, :LANES] = ke.astype(BF16)
        ko_ref[:, :LANES] = ko.astype(BF16)
        for c in range(S // TQ):
            vt_ref[c] = zv_ref[c * TQ:(c + 1) * TQ, :].astype(F32).T.astype(BF16)
        if selected:
            rowb = lax.broadcasted_iota(jnp.int32, (S, LANES), 0) >> 6
            lane = lax.broadcasted_iota(jnp.int32, (S, LANES), 1)
            onehot = jnp.where(rowb == lane, 1.0, 0.0).astype(BF16)
            ke_ref[:, LANES:] = onehot
            ko_ref[:, LANES:] = onehot

    zq = zq_ref[...].astype(F32) * (A_HEAD_DIM ** -0.5 * LOG2E)
    rhs = jnp.concatenate([zq[:, :LANES], zq[:, LANES:]], axis=0).T.astype(BF16)
    if selected:
        sbt = sbt_ref[...]
        rhs = jnp.concatenate([rhs, jnp.concatenate([sbt, sbt], axis=1)], axis=0)
    acc_ref[...] = jnp.zeros(acc_ref.shape, F32)
    vrow = pl.multiple_of(g * A_HEAD_DIM, A_HEAD_DIM)

    def chunk(kc, ti, carry):
        ks = pl.multiple_of(kc * TQ, TQ)
        vt = vt_ref[kc, pl.ds(vrow, A_HEAD_DIM), :]
        ss = [_dot(k_refs[eo][pl.ds(ks, TQ), :], rhs) + tab_ref[eo, ti] for eo in range(2)]
        return _flash_update_t(ss, [vt, vt], carry, [acc_ref.at[0], acc_ref.at[1]])

    init = (jnp.full((1, 2 * TQ), NEG, F32), jnp.zeros((1, 2 * TQ), F32))
    carry = chunk(qt, 0, (init, init))
    lo = 0 if selected else jnp.maximum(qt - 2, 0)
    carry = lax.fori_loop(lo, qt, lambda kc, cr: chunk(kc, jnp.minimum(qt - kc, 2), cr), carry)
    out_t = jnp.concatenate([acc_ref[0] * (1.0 / carry[0][1]), acc_ref[1] * (1.0 / carry[1][1])], axis=0)
    out = out_t.T
    o_ref[:, :LANES] = out[:TQ]
    o_ref[:, LANES:] = out[TQ:]


def _nsa_flash(z, sb, tab, zk_col, zv_col, B, S, selected):
    nq = S // TQ
    kw = 2 * LANES if selected else LANES
    in_specs = [pl.BlockSpec((TQ, 2 * LANES), lambda b, g, q: (b * nq + q, g))]
    args = [z]
    if selected:
        in_specs.append(pl.BlockSpec((None, None, LANES, TQ), lambda b, g, q: (b, g, 0, q)))
        args.append(sb)
    in_specs += [pl.BlockSpec((S, LANES), lambda b, g, q: (b, zk_col // LANES)),
                 pl.BlockSpec((S, LANES), lambda b, g, q: (b, zv_col // LANES)),
                 pl.BlockSpec((None, 2, 3, TQ, 2 * TQ), lambda b, g, q: (g, 0, 0, 0, 0))]
    args += [z, z, tab]
    return pl.pallas_call(
        functools.partial(_nsa_flash_kernel, S=S, selected=selected),
        grid=(B, A_KV_GROUPS, nq),
        in_specs=in_specs,
        out_specs=pl.BlockSpec((TQ, 2 * LANES), lambda b, g, q: (b * nq + q, g)),
        out_shape=jax.ShapeDtypeStruct((B * S, A_Q), F32),
        scratch_shapes=[pltpu.VMEM((S, kw), BF16), pltpu.VMEM((S, kw), BF16),
                        pltpu.VMEM((S // TQ, LANES, TQ), BF16),
                        pltpu.VMEM((2, A_HEAD_DIM, 2 * TQ), F32)],
        compiler_params=_cparams("parallel", "parallel", "arbitrary"),
        name="nsa_selected" if selected else "nsa_window",
    )(*args)


def _mla_proj_kernel(zqa_ref, zkva_ref, zkra_ref, zkrb_ref, qn_ref, kvn_ref, wqa_ref, wqb_ref,
                     wk_ref, wv_ref, caq_ref, cbq_ref, cak_ref, cbk_ref, q_ref, k_ref, v_ref):
    def rms(x, gain):
        return x * lax.rsqrt(jnp.mean(x * x, axis=-1, keepdims=True) + EPS) * gain

    nq = rms(zqa_ref[...].astype(F32), qn_ref[...]).astype(BF16)
    nkv = rms(zkva_ref[...].astype(F32), kvn_ref[...]).astype(BF16)
    qa = _dot_t(wqa_ref[...], nq)
    qb = _dot_t(wqb_ref[...], nq)
    kn = _dot(nkv, wk_ref[...])
    vt = _dot_t(wv_ref[...], nkv)
    caq, cbq = caq_ref[...], cbq_ref[...]
    kr = zkra_ref[...].astype(F32) * cak_ref[...] + zkrb_ref[...].astype(F32) * cbk_ref[...]
    for h in range(B_HEADS):
        sl = slice(h * LANES, (h + 1) * LANES)
        q_ref[h] = (qa[sl, :] * caq + qb[sl, :] * cbq).astype(BF16)
        k_ref[h] = (kn[:, sl] + kr).astype(BF16)
        v_ref[h] = vt[h * V_DIM:(h + 1) * V_DIM, :].astype(BF16)


def _mla_proj(z, qn, kvn, wqa, wqb, wk, wv, caq, cbq, cak, cbk, B, S):
    tm = TM
    nt = S // tm
    zspec = lambda w, col: pl.BlockSpec((tm, w), lambda b, i: (b * nt + i, col // w))
    full = lambda a: pl.BlockSpec(a.shape, lambda b, i: (0,) * a.ndim)
    tspec = pl.BlockSpec((tm, LANES), lambda b, i: (i, 0))
    tspec_t = pl.BlockSpec((LANES, tm), lambda b, i: (0, i))
    return pl.pallas_call(
        _mla_proj_kernel,
        grid=(B, nt),
        in_specs=[zspec(Q_LORA, Z_QA), zspec(LANES, Z_KVA), zspec(LANES, Z_KRA), zspec(LANES, Z_KRB),
                  full(qn), full(kvn), full(wqa), full(wqb), full(wk), full(wv),
                  tspec_t, tspec_t, tspec, tspec],
        out_specs=[pl.BlockSpec((None, B_HEADS, LANES, tm), lambda b, i: (b, 0, 0, i)),
                   pl.BlockSpec((None, B_HEADS, tm, LANES), lambda b, i: (b, 0, i, 0)),
                   pl.BlockSpec((None, B_HEADS, None, V_DIM, tm), lambda b, i: (b, 0, i, 0, 0))],
        out_shape=[jax.ShapeDtypeStruct((B, B_HEADS, LANES, S), BF16),
                   jax.ShapeDtypeStruct((B, B_HEADS, S, LANES), BF16),
                   jax.ShapeDtypeStruct((B, B_HEADS, nt, V_DIM, tm), BF16)],
        compiler_params=_cparams("parallel", "parallel"),
        name="mla_proj",
    )(z, z, z, z, qn, kvn, wqa, wqb, wk, wv, caq, cbq, cak, cbk)


def _mla_flash_kernel(qt_ref, k_ref, vt_ref, o_ref, acc_ref):
    qi = pl.program_id(2)
    acc_ref[...] = jnp.zeros(acc_ref.shape, F32)
    ri = lax.broadcasted_iota(jnp.int32, (TM, TM), 0)
    ci = lax.broadcasted_iota(jnp.int32, (TM, TM), 1)
    causal = ri <= ci

    def chunk(kc, carry, diag):
        ks = pl.multiple_of(kc * TM, TM)
        ss = [_dot(k_ref[hh, pl.ds(ks, TM), :], qt_ref[hh]) for hh in range(2)]
        if diag:
            ss = [jnp.where(causal, s, NEG) for s in ss]
        return _flash_update_t(ss, [vt_ref[0, kc], vt_ref[1, kc]], carry, [acc_ref.at[0], acc_ref.at[1]])

    init = (jnp.full((1, TM), NEG, F32), jnp.zeros((1, TM), F32))
    carry = chunk(qi, (init, init), True)
    carry = lax.fori_loop(0, qi, lambda kc, cr: chunk(kc, cr, False), carry)
    out_t = jnp.concatenate([acc_ref[0] * (1.0 / carry[0][1]), acc_ref[1] * (1.0 / carry[1][1])], axis=0)
    o_ref[...] = out_t.T


def _mla_flash(qt, k, vt, B, S):
    nq = S // TM
    hp = B_HEADS // 2
    return pl.pallas_call(
        _mla_flash_kernel,
        grid=(B, hp, nq),
        in_specs=[pl.BlockSpec((None, 2, LANES, TM), lambda b, h, i: (b, h, 0, i)),
                  pl.BlockSpec((None, 2, S, LANES), lambda b, h, i: (b, h, 0, 0)),
                  pl.BlockSpec((None, 2, nq, V_DIM, TM), lambda b, h, i: (b, h, 0, 0, 0))],
        out_specs=pl.BlockSpec((TM, LANES), lambda b, h, i: (b * nq + i, h)),
        out_shape=jax.ShapeDtypeStruct((B * S, B_HEADS * V_DIM), F32),
        scratch_shapes=[pltpu.VMEM((2, V_DIM, TM), F32)],
        compiler_params=_cparams("parallel", "parallel", "arbitrary"),
        name="mla_flash",
    )(qt, k, vt)


def _rwkv_prep_kernel(zr_ref, zk_ref, zv_ref, zwa_ref, zg_ref, mu_r, mu_k, mu_v, mu_wa, mu_g,
                      w0_ref, a0_ref, kk_ref, ka_ref, rk_ref, w2_ref, a2_ref, g2_ref, bd_ref,
                      r_o, k_o, v_o, kk_o, be_o, ld_o, g_o, bo_o,
                      c_r, c_k, c_v, c_wa, c_g):
    t = pl.program_id(1)
    tm = zr_ref.shape[0]

    def shifted(z_ref, mu_ref, c_ref):
        x = z_ref[...].astype(F32)
        row = lax.broadcasted_iota(jnp.int32, x.shape, 0)
        prev = jnp.where(t == 0, 0.0, c_ref[0:1, :])
        xs = jnp.where(row == 0, prev, pltpu.roll(x, 1, axis=0))
        c_ref[0:1, :] = x[tm - 1:tm, :]
        return x + (xs - x) * mu_ref[...]

    r = shifted(zr_ref, mu_r, c_r)
    k = shifted(zk_ref, mu_k, c_k)
    v = shifted(zv_ref, mu_v, c_v)
    wa = shifted(zwa_ref, mu_wa, c_wa)
    gd = shifted(zg_ref, mu_g, c_g)
    w = w0_ref[...] + _dot(jnp.tanh(wa).astype(BF16), w2_ref[...])
    ld_o[...] = -jax.nn.sigmoid(w) * math.exp(-0.5)
    a = jax.nn.sigmoid(a0_ref[...] + _dot(wa.astype(BF16), a2_ref[...]))
    g_o[...] = _dot(jax.nn.sigmoid(gd).astype(BF16), g2_ref[...])
    kk = k * kk_ref[...]
    nsq = _dot3(kk * kk, bd_ref[...])
    kk = kk / jnp.maximum(jnp.sqrt(nsq), 1e-12)
    k2 = k * (1.0 + (a - 1.0) * ka_ref[...])
    rks = _dot3(r * k2 * rk_ref[...], bd_ref[...])
    r_o[...] = r
    k_o[...] = k2
    v_o[...] = v
    kk_o[...] = kk
    be_o[...] = kk * a
    bo_o[...] = rks * v


def _rwkv_prep(z, mus, w0, a0, k_k, k_a, r_k, w2p, a2p, g2, bd, B, S, tm=256):
    nt = S // tm
    W = C_WIDTH
    zspec = lambda w, col: pl.BlockSpec((tm, w), lambda b, i: (b * nt + i, col // w))
    full = lambda a: pl.BlockSpec(a.shape, lambda b, i: (0,) * a.ndim)
    ospec = pl.BlockSpec((tm, W), lambda b, i: (b * nt + i, 0))
    oshape = jax.ShapeDtypeStruct((B * S, W), F32)
    consts = list(mus) + [w0, a0, k_k, k_a, r_k, w2p, a2p, g2, bd]
    return pl.pallas_call(
        _rwkv_prep_kernel,
        grid=(B, nt),
        in_specs=[zspec(W, Z_CR), zspec(W, Z_CK), zspec(W, Z_CV), zspec(LANES, Z_CWA), zspec(LANES, Z_CG)]
                 + [full(a) for a in consts],
        out_specs=[ospec] * 8,
        out_shape=[oshape] * 8,
        scratch_shapes=[pltpu.VMEM((8, W), F32)] * 3 + [pltpu.VMEM((8, LANES), F32)] * 2,
        compiler_params=_cparams("parallel", "arbitrary"),
        name="rwkv_prep",
    )(z, z, z, z, z, *consts)


def _rwkv_chunk_kernel(r_ref, k_ref, v_ref, kk_ref, be_ref, ld_ref,
                       qa_o, bkt_o, yu_o, vp_o, pcb_o):
    C = CH
    ld = ld_ref[...]
    ri = lax.broadcasted_iota(jnp.int32, (C, C), 0)
    ci = lax.broadcasted_iota(jnp.int32, (C, C), 1)
    incl = ci <= ri
    strict = ci < ri
    tri = jnp.where(incl, 1.0, 0.0).astype(BF16)
    cs = sum(_dot(tri, t) for t in _split3(ld))
    cl = cs[C - 1:C, :]
    ex, exn, exx, exc = jnp.exp(cs), jnp.exp(-cs), jnp.exp(cs - ld), jnp.exp(cl - cs)
    kk, be, k2, v = kk_ref[...], be_ref[...], k_ref[...], v_ref[...]
    at = -kk * exx
    rt = r_ref[...] * ex
    bt = be * exn
    kt = k2 * exn
    btc = be * exc
    ktc = k2 * exc
    pc = jnp.exp(cl)
    eye = jnp.where(ci == ri, 1.0, 0.0)
    lane = lax.broadcasted_iota(jnp.int32, (C, LANES), 1)
    heads = [(p, hh) for p in range(C_HEADS // 2) for hh in range(2)]
    sls = [slice(p * LANES, (p + 1) * LANES) for p in range(C_HEADS // 2)]
    lhs2 = [jnp.concatenate([at[:, sl], rt[:, sl]], axis=0).astype(BF16) for sl in sls]
    bm, km, vm, am = [], [], [], []
    for p, hh in heads:
        msk = (lane < C_HEAD_DIM) if hh == 0 else (lane >= C_HEAD_DIM)
        bm.append(jnp.where(msk, bt[:, sls[p]], 0.0).astype(BF16))
        km.append(jnp.where(msk, kt[:, sls[p]], 0.0).astype(BF16))
        vm.append(jnp.where(msk, v[:, sls[p]], 0.0).astype(BF16))
        am.append(jnp.where(msk, at[:, sls[p]], 0.0).astype(BF16))
    g1 = [_dot_t(lhs2[p], bm[i]) for i, (p, hh) in enumerate(heads)]
    g2 = [_dot_t(lhs2[p], km[i]) for i, (p, hh) in enumerate(heads)]
    lab = [jnp.where(strict, g[:C], 0.0) for g in g1]
    mrb = [jnp.where(incl, g[C:], 0.0).astype(BF16) for g in g1]
    lak = [jnp.where(strict, g[:C], 0.0).astype(BF16) for g in g2]
    mrk = [jnp.where(incl, g[C:], 0.0).astype(BF16) for g in g2]
    w2 = [_dot(a, b) for a, b in zip(lak, vm)]
    yk = [_dot(a, b) for a, b in zip(mrk, vm)]
    tinv = [eye + x for x in lab]
    lp = lab
    for _ in range(int(math.log2(C)) - 1):
        lpb = [x.astype(BF16) for x in lp]
        lp = [_dot(x, x) for x in lpb]
        tinv = [t + _dot(t.astype(BF16), x.astype(BF16)) for t, x in zip(tinv, lp)]
    au = [_dot(t.astype(BF16), jnp.concatenate([a, w.astype(BF16)], axis=1)) for t, a, w in zip(tinv, am, w2)]
    qy = [_dot(m, x.astype(BF16)) for m, x in zip(mrb, au)]
    for p in range(C_HEADS // 2):
        sl = sls[p]
        e, o = 2 * p, 2 * p + 1
        ahat = au[e][:, :LANES] + au[o][:, :LANES]
        uhat = au[e][:, LANES:] + au[o][:, LANES:]
        qhat = rt[:, sl] + qy[e][:, :LANES] + qy[o][:, :LANES]
        yi = qy[e][:, LANES:] + qy[o][:, LANES:] + yk[e] + yk[o]
        qa_o[p] = jnp.concatenate([qhat, ahat], axis=0).astype(BF16)
        yu_o[p] = jnp.concatenate([yi, uhat], axis=0)
        bkt_o[p] = jnp.concatenate([btc[:, sl], ktc[:, sl]], axis=0).T.astype(BF16)
        vp_o[p] = v[:, sl].astype(BF16)
        pcb_o[p] = jnp.broadcast_to(pc[:, sl], (LANES, LANES)).T


def _rwkv_chunk(r, k2, v, kk, be, ld, B, S):
    nch = S // CH
    W = C_WIDTH
    P = C_HEADS // 2
    ispec = pl.BlockSpec((CH, W), lambda b, c: (b * nch + c, 0))
    sq = lambda rows: pl.BlockSpec((None, None, P, rows, LANES), lambda b, c: (b, c, 0, 0, 0))
    shp = lambda rows, dt: jax.ShapeDtypeStruct((B, nch, P, rows, LANES), dt)
    return pl.pallas_call(
        _rwkv_chunk_kernel,
        grid=(B, nch),
        in_specs=[ispec] * 6,
        out_specs=[sq(2 * CH), sq(LANES), sq(2 * CH), sq(CH), sq(LANES)],
        out_shape=[shp(2 * CH, BF16), shp(LANES, BF16), shp(2 * CH, F32), shp(CH, BF16), shp(LANES, F32)],
        compiler_params=_cparams("parallel", "parallel"),
        name="rwkv_chunk",
    )(r, k2, v, kk, be, ld)


def _rwkv_state_kernel(qa_ref, bkt_ref, yu_ref, vp_ref, pcb_ref, y_ref, ap_ref):
    @pl.when(pl.program_id(1) == 0)
    def _():
        ap_ref[...] = jnp.zeros(ap_ref.shape, F32)

    ri = lax.broadcasted_iota(jnp.int32, (LANES, LANES), 0)
    ci = lax.broadcasted_iota(jnp.int32, (LANES, LANES), 1)
    same_head = (ri < C_HEAD_DIM) == (ci < C_HEAD_DIM)
    idx = [(b, p) for b in range(qa_ref.shape[0]) for p in range(C_HEADS // 2)]
    a = [ap_ref[b, p] for b, p in idx]
    x = [_dot(qa_ref[b, p], s.astype(BF16)) for (b, p), s in zip(idx, a)]
    yu = [yu_ref[b, p] for b, p in idx]
    for (b, p), xi, yi in zip(idx, x, yu):
        y_ref[b, :, p * LANES:(p + 1) * LANES] = xi[:CH] + yi[:CH]
    uv = [jnp.concatenate([(xi[CH:] + yi[CH:]).astype(BF16), vp_ref[b, p]], axis=0)
          for (b, p), xi, yi in zip(idx, x, yu)]
    upd = [_dot(bkt_ref[b, p], t) for (b, p), t in zip(idx, uv)]
    for (b, p), s, t in zip(idx, a, upd):
        ap_ref[b, p] = pcb_ref[b, p] * s + jnp.where(same_head, t, 0.0)


def _rwkv_state(qa, bkt, yu, vp, pcb, B, S):
    nch = S // CH
    P = C_HEADS // 2
    nb = 2 if B % 2 == 0 else 1
    sq = lambda rows: pl.BlockSpec((nb, None, P, rows, LANES), lambda b, c: (b, c, 0, 0, 0))
    return pl.pallas_call(
        _rwkv_state_kernel,
        grid=(B // nb, nch),
        in_specs=[sq(2 * CH), sq(LANES), sq(2 * CH), sq(CH), sq(LANES)],
        out_specs=pl.BlockSpec((nb, CH, C_WIDTH), lambda b, c: (b, c, 0)),
        out_shape=jax.ShapeDtypeStruct((B, S, C_WIDTH), F32),
        scratch_shapes=[pltpu.VMEM((nb, P, LANES, LANES), F32)],
        compiler_params=_cparams("parallel", "arbitrary"),
        name="rwkv_state",
    )(qa, bkt, yu, vp, pcb).reshape(B * S, C_WIDTH)


def _merge_kernel(oc_ref, os_ref, ow_ref, zag_ref, yb_ref, yr_ref, bo_ref, gg_ref,
                  zga_ref, zgb_ref, zgc_ref, x_ref, wbr_ref, wout_ref, ge_ref, bdm_ref,
                  ln_ref, gain_ref, gt_ref, o_ref):
    sg = jax.nn.sigmoid(zag_ref[...].astype(F32))
    ya = (_dot3(sg, ge_ref[0]) * oc_ref[...] + _dot3(sg, ge_ref[1]) * os_ref[...]
          + _dot3(sg, ge_ref[2]) * ow_ref[...])
    yr = yr_ref[...]
    mean = _dot3(yr, bdm_ref[...])
    d = yr - mean
    var = _dot3(d * d, bdm_ref[...])
    yn = d * lax.rsqrt(var + GN_EPS) * ln_ref[0:1, :] + ln_ref[1:2, :]
    yc = (yn + bo_ref[...]) * gg_ref[...]
    sig = lambda ref: jax.nn.sigmoid(ref[...].astype(F32))
    merged = (sig(zga_ref) * _dot(ya.astype(BF16), wbr_ref[0:A_Q, :])
              + sig(zgb_ref) * _dot(yb_ref[...].astype(BF16), wbr_ref[A_Q:2 * A_Q, :])
              + sig(zgc_ref) * _dot(yc.astype(BF16), wbr_ref[2 * A_Q:3 * A_Q, :]))
    y = _dot(merged.astype(BF16), wout_ref[...])
    yn2 = y * lax.rsqrt(jnp.mean(y * y, axis=-1, keepdims=True) + EPS) * gain_ref[...]
    o_ref[...] = x_ref[...] + gt_ref[0] * yn2


def _merge(oc, os_, ow, z, yb, yr, bo, gg, x2, wbr, wout, gexp, bdm, ln, gain, gt, S, tm=256):
    N, D = x2.shape
    tpb = S // tm
    W = A_Q
    row = lambda w: pl.BlockSpec((tm, w), lambda i: (i, 0))
    zspec = lambda w, col: pl.BlockSpec((tm, w), lambda i: (i, col // w))
    full = lambda a: pl.BlockSpec(a.shape, lambda i: (0,) * a.ndim)
    return pl.pallas_call(
        _merge_kernel,
        grid=(N // tm,),
        in_specs=[row(W), row(W), row(W), zspec(LANES, Z_AG), row(W), row(W), row(W), row(W),
                  zspec(D, Z_ZG), zspec(D, Z_ZG + D), zspec(D, Z_ZG + 2 * D), row(D),
                  full(wbr), full(wout), full(gexp), full(bdm), full(ln),
                  pl.BlockSpec((1, D), lambda i: (0, 0)),
                  pl.BlockSpec((1, 1, D), lambda i: (i // tpb, 0, 0))],
        out_specs=row(D),
        out_shape=jax.ShapeDtypeStruct((N, D), F32),
        compiler_params=_cparams("parallel"),
        name="merge_out",
    )(oc, os_, ow, z, yb, yr, bo, gg, z, z, z, x2, wbr, wout, gexp, bdm, ln, gain.reshape(1, D), gt)


def _ffn_down_kernel(u_ref, up_ref, cw_ref, cb_ref, wd_ref, x_ref, gain_ref, gt_ref, o_ref, *, tpb):
    i = pl.program_id(0)
    tm = u_ref.shape[0]
    first = (i % tpb) == 0
    fc = 256
    row = lax.broadcasted_iota(jnp.int32, (tm, fc), 0)

    def conv(c0):
        u = u_ref[:, c0:c0 + fc].astype(F32)
        prev = up_ref[:, c0:c0 + fc].astype(F32)
        p1 = jnp.where(first, 0.0, prev[PREV_ROWS - 1:PREV_ROWS])
        p2 = jnp.where(first, 0.0, prev[PREV_ROWS - 2:PREV_ROWS - 1])
        u1 = jnp.where(row == 0, p1, pltpu.roll(u, 1, axis=0))
        u2 = jnp.where(row == 0, p2, jnp.where(row == 1, p1, pltpu.roll(u, 2, axis=0)))
        return (cw_ref[0:1, c0:c0 + fc] * u2 + cw_ref[1:2, c0:c0 + fc] * u1
                + cw_ref[2:3, c0:c0 + fc] * u + cb_ref[:, c0:c0 + fc])

    f = jnp.zeros((tm, D_MODEL), F32)
    for c in range(D_FF // fc):
        a = _gelu(conv(c * fc)) * conv(D_FF + c * fc)
        f = f + _dot(a.astype(BF16), wd_ref[c * fc:(c + 1) * fc, :])
    fn = f * lax.rsqrt(jnp.mean(f * f, axis=-1, keepdims=True) + EPS) * gain_ref[...]
    o_ref[...] = x_ref[...] + gt_ref[0] * fn


def _ffn_down(u, conv_w, conv_b, wd, x2, gain, gt, S, tm=256):
    N, D = x2.shape
    F2 = u.shape[1]
    tpb = S // tm
    full = lambda a: pl.BlockSpec(a.shape, lambda i: (0,) * a.ndim)
    return pl.pallas_call(
        functools.partial(_ffn_down_kernel, tpb=tpb),
        grid=(N // tm,),
        in_specs=[pl.BlockSpec((tm, F2), lambda i: (i, 0)),
                  pl.BlockSpec((PREV_ROWS, F2), lambda i: (jnp.maximum(i * (tm // PREV_ROWS) - 1, 0), 0)),
                  full(conv_w), pl.BlockSpec((1, F2), lambda i: (0, 0)), full(wd),
                  pl.BlockSpec((tm, D), lambda i: (i, 0)),
                  pl.BlockSpec((1, D), lambda i: (0, 0)),
                  pl.BlockSpec((1, 1, D), lambda i: (i // tpb, 0, 0))],
        out_specs=pl.BlockSpec((tm, D), lambda i: (i, 0)),
        out_shape=jax.ShapeDtypeStruct((N, D), F32),
        compiler_params=_cparams("parallel"),
        name="ffn_down",
    )(u, u, conv_w, conv_b.reshape(1, F2), wd, x2, gain.reshape(1, D), gt)


def _t5_bucket_np(dist):
    d = np.maximum(dist, 0)
    max_exact = N_BUCKETS // 2
    large = max_exact + (np.log(np.maximum(d, 1).astype(np.float32) / max_exact)
                         / math.log(MAX_DISTANCE / max_exact) * (N_BUCKETS - max_exact)).astype(np.int32)
    return np.where(d < max_exact, d, np.minimum(large, N_BUCKETS - 1))


def _bias_table(rel_bias, c0, step, ncol, lo, hi):
    G, HPG, H = A_KV_GROUPS, A_HEADS // A_KV_GROUPS, A_HEADS
    span = step * (ncol - 1)
    d = c0 - span + np.arange(TQ + span)
    onehot = np.eye(N_BUCKETS, dtype=np.float32)[_t5_bucket_np(d)]
    u = jnp.dot(jnp.asarray(onehot), rel_bias, precision=HIGHEST)
    u = jnp.where(jnp.asarray((d >= lo) & (d < hi))[:, None], u, NEG).T
    L = u.shape[1]
    hank = jnp.tile(u, (1, TQ + 1))[:, :TQ * (L + 1)].reshape(H, TQ, L + 1)
    rb = hank[:, :, 0:span + 1:step][:, :, ::-1]
    rb = rb.reshape(G, HPG // 2, 2, TQ, ncol)
    return rb.transpose(0, 2, 1, 3, 4).reshape(G, 2, 2 * TQ, ncol)


def _nsa_tables(rel_bias, S):
    NC = S // CMP_STRIDE
    big = 1 << 30
    cb = _bias_table(rel_bias, CMP_STRIDE * NC - TQ - (CMP_BLOCK - 1), CMP_STRIDE, NC, 0, big)
    tab = lambda c0, hi: jnp.swapaxes(_bias_table(rel_bias, c0, 1, TQ, 0, hi), -1, -2) * LOG2E
    far = MAX_DISTANCE + TQ
    slc = jnp.stack([tab(0, big), tab(TQ, big), tab(far, big)], axis=2)
    win = jnp.stack([tab(0, WINDOW), tab(TQ, WINDOW), tab(2 * TQ, WINDOW)], axis=2)
    return cb, slc, win


def _pad_cols(w, n):
    return jnp.pad(w, ((0, 0), (0, n - w.shape[1])))


def _prep_w_in(w):
    D = w.shape[0]
    za, zb, zc, zg = jnp.split(w, np.cumsum([A_COLS, B_COLS, C_COLS]).tolist(), axis=1)
    a_parts = jnp.split(za, np.cumsum([A_Q] + [A_KV] * 6).tolist(), axis=1)
    qa, kva, kr = jnp.split(zb, [Q_LORA, Q_LORA + KV_LORA], axis=1)
    half = ROPE_DIM // 2
    kr_rot = jnp.concatenate([-kr[:, half:], kr[:, :half]], axis=1)
    z64 = jnp.zeros((D, NOPE_DIM), w.dtype)
    z32 = jnp.zeros((D, LANES - NOPE_DIM - ROPE_DIM), w.dtype)
    c_r, c_k, c_v, c_wd, c_ad, c_gd = jnp.split(
        zc, np.cumsum([C_WIDTH] * 3 + [DECAY_LORA, AAA_LORA]).tolist(), axis=1)
    cols = [a_parts[0], qa] + a_parts[1:7] + [
        _pad_cols(a_parts[7], LANES), kva,
        jnp.concatenate([z64, kr, z32], axis=1),
        jnp.concatenate([z64, kr_rot, z32], axis=1),
        c_r, c_k, c_v, c_wd, c_ad, c_gd, jnp.zeros((D, Z_ZG - Z_CG - LANES), w.dtype), zg]
    out = jnp.concatenate(cols, axis=1)
    assert out.shape[1] == Z_COLS
    return out.astype(BF16)


def _prep_mla(w_uq, w_ukv):
    dq = NOPE_DIM + ROPE_DIM
    half = ROPE_DIM // 2
    wq = w_uq.reshape(Q_LORA, B_HEADS, dq)
    nope, r1, r2 = wq[..., :NOPE_DIM], wq[..., NOPE_DIM:NOPE_DIM + half], wq[..., NOPE_DIM + half:]
    zq = jnp.zeros((Q_LORA, B_HEADS, LANES - dq), w_uq.dtype)
    wqa = jnp.concatenate([nope, r1, r2, zq], axis=-1).reshape(Q_LORA, B_HEADS * LANES)
    wqb = jnp.concatenate([jnp.zeros_like(nope), -r2, r1, zq], axis=-1).reshape(Q_LORA, B_HEADS * LANES)
    wkv = w_ukv.reshape(KV_LORA, B_HEADS, NOPE_DIM + V_DIM)
    kn, vv = wkv[..., :NOPE_DIM], wkv[..., NOPE_DIM:]
    wk = jnp.concatenate([kn, jnp.zeros_like(kn)], axis=-1).reshape(KV_LORA, B_HEADS * LANES)
    wv = vv.reshape(KV_LORA, B_HEADS * V_DIM)
    return wqa.T.astype(BF16), wqb.T.astype(BF16), wk.astype(BF16), wv.T.astype(BF16)


def _rope_tables(S):
    half = ROPE_DIM // 2
    inv = ROPE_THETA ** (-jnp.arange(half, dtype=F32) / half)
    ang = jnp.arange(S, dtype=F32)[:, None] * inv
    cos2 = jnp.tile(jnp.cos(ang), (1, 2))
    sin2 = jnp.tile(jnp.sin(ang), (1, 2))
    scale = (NOPE_DIM + ROPE_DIM) ** -0.5 * LOG2E
    one = jnp.ones((S, NOPE_DIM), F32)
    z64 = jnp.zeros((S, NOPE_DIM), F32)
    z32 = jnp.zeros((S, LANES - NOPE_DIM - ROPE_DIM), F32)
    caq = (jnp.concatenate([one, cos2, z32], axis=1) * scale).T
    cbq = (jnp.concatenate([z64, sin2, z32], axis=1) * scale).T
    cak = jnp.concatenate([z64, cos2, z32], axis=1)
    cbk = jnp.concatenate([z64, sin2, z32], axis=1)
    return caq, cbq, cak, cbk


def _prep_compress(cmp_pos, cmp_w1, cmp_w2):
    Dh = A_HEAD_DIM
    w1 = cmp_w1.reshape(2, CMP_BLOCK, Dh, Dh)
    z = jnp.zeros_like(w1)
    w1bd = jnp.concatenate([jnp.concatenate([w1, z], axis=-1), jnp.concatenate([z, w1], axis=-1)], axis=-2)
    posrow = jnp.broadcast_to(cmp_pos.reshape(2, 1, CMP_BLOCK * Dh), (2, 8, CMP_BLOCK * Dh))
    w1cat = jnp.concatenate([cmp_w1, cmp_w1], axis=-1)
    zz = jnp.zeros((2, Dh, Dh), cmp_w2.dtype)
    blk = lambda a, b, c, d: jnp.concatenate(
        [jnp.concatenate([a, b], axis=-1), jnp.concatenate([c, d], axis=-1)], axis=-2)
    w2v = jnp.stack([jnp.stack([blk(cmp_w2, zz, zz, zz), blk(zz, cmp_w2, zz, zz)], axis=1),
                     jnp.stack([blk(zz, zz, cmp_w2, zz), blk(zz, zz, zz, cmp_w2)], axis=1)], axis=1)
    return w1bd.astype(BF16), posrow, w1cat.astype(BF16), w2v.astype(BF16)


def _gate_expand():
    e = np.zeros((3, LANES, A_Q), np.float32)
    for h in range(A_HEADS):
        for r in range(3):
            e[r, 3 * h + r, h * A_HEAD_DIM:(h + 1) * A_HEAD_DIM] = 1.0
    return jnp.asarray(e, dtype=BF16)


def _block_diag_ones(scale):
    idx = np.arange(C_WIDTH) // C_HEAD_DIM
    return jnp.asarray((idx[:, None] == idx[None, :]).astype(np.float32) * scale, dtype=BF16)


def kernel(x, c, rel_bias, ada_w, ada_b, norm_gain, w_in, nsa_cmp_pos, nsa_cmp_w1, nsa_cmp_w2, mla_q_norm, mla_kv_norm, mla_w_uq, mla_w_ukv, rwkv_mu, rwkv_w0, rwkv_a0, rwkv_k_k, rwkv_k_a, rwkv_w2, rwkv_a2, rwkv_g2, rwkv_r_k, rwkv_ln, w_branch, w_out, ffn_up, ffn_conv_w, ffn_conv_b, ffn_down):
    B, S, D = x.shape
    L = ada_w.shape[0]
    assert S % TQ == 0 and S % TM == 0 and S // SLC_BLOCK <= A_HEAD_DIM and S >= 2 * TQ
    mod = _adaln(c, ada_w, ada_b)
    cb, slc_tab, win_tab = _nsa_tables(rel_bias, S)
    caq, cbq, cak, cbk = _rope_tables(S)
    gexp = _gate_expand()
    bd1 = _block_diag_ones(1.0)
    bdm = _block_diag_ones(1.0 / C_HEAD_DIM)
    row = lambda v: v.reshape(1, -1)
    x2 = x.reshape(B * S, D)
    for l in range(L):
        m6 = mod[l].reshape(B, 6, 1, D)
        sh1, sc1, gt1, sh2, sc2, gt2 = (m6[:, i] for i in range(6))
        z = _norm_mod_matmul(x2, norm_gain[l, 0], sc1, sh1, _prep_w_in(w_in[l]), S, tn=1024)
        kcp, vcp = _compress(z, *_prep_compress(nsa_cmp_pos[l], nsa_cmp_w1[l], nsa_cmp_w2[l]), B, S)
        oc, sb = _cmpsel(z, kcp, vcp, cb, B, S)
        os_ = _nsa_flash(z, sb, slc_tab, Z_AKS, Z_AVS, B, S, selected=True)
        ow = _nsa_flash(z, None, win_tab, Z_AKW, Z_AVW, B, S, selected=False)
        wqa, wqb, wk, wv = _prep_mla(mla_w_uq[l], mla_w_ukv[l])
        q, k, v = _mla_proj(z, row(mla_q_norm[l]), row(mla_kv_norm[l]), wqa, wqb, wk, wv,
                            caq, cbq, cak, cbk, B, S)
        yb = _mla_flash(q, k, v, B, S)
        mu = rwkv_mu[l]
        o = 3 * C_WIDTH
        mus = [row(mu[:C_WIDTH]), row(mu[C_WIDTH:2 * C_WIDTH]), row(mu[2 * C_WIDTH:o]),
               row(mu[o:o + LANES]), row(mu[o + LANES:])]
        w2p = jnp.concatenate([rwkv_w2[l], jnp.zeros_like(rwkv_a2[l])], axis=0).astype(BF16)
        a2p = jnp.concatenate([jnp.zeros_like(rwkv_w2[l]), rwkv_a2[l]], axis=0).astype(BF16)
        rr, k2, vv, kk, be, ld, gg, bo = _rwkv_prep(
            z, mus, row(rwkv_w0[l]), row(rwkv_a0[l]), row(rwkv_k_k[l]), row(rwkv_k_a[l]),
            row(rwkv_r_k[l]), w2p, a2p, rwkv_g2[l].astype(BF16), bd1, B, S)
        qa_, bkt, yu, vp, pcb = _rwkv_chunk(rr, k2, vv, kk, be, ld, B, S)
        yr = _rwkv_state(qa_, bkt, yu, vp, pcb, B, S)
        x2 = _merge(oc, os_, ow, z, yb, yr, bo, gg, x2, w_branch[l].astype(BF16), w_out[l].astype(BF16),
                    gexp, bdm, rwkv_ln[l], norm_gain[l, 1], gt1, S)
        u = _norm_mod_matmul(x2, norm_gain[l, 2], sc2, sh2, ffn_up[l].astype(BF16), S, tn=2 * D_FF // 4)
        x2 = _ffn_down(u, ffn_conv_w[l], ffn_conv_b[l], ffn_down[l].astype(BF16), x2,
                       norm_gain[l, 3], gt2, S)
    return x2.reshape(B, S, D)
```

```python
import functools
import math

import jax
import jax.numpy as jnp
import numpy as np
from jax import lax
from jax.experimental import pallas as pl
from jax.experimental.pallas import tpu as pltpu

F32 = jnp.float32
BF16 = jnp.bfloat16

D_MODEL = 1024
DEPTH = 4
A_HEADS, A_KV_GROUPS, A_HEAD_DIM = 8, 2, 64
CMP_BLOCK, CMP_STRIDE = 32, 16
SLC_BLOCK, SLC_TOPK, N_LOCAL_BLOCKS = 64, 16, 2
WINDOW = 512
FORCE_SCORE = 1e9
B_HEADS, Q_LORA, KV_LORA, NOPE_DIM, ROPE_DIM, V_DIM = 8, 256, 128, 64, 32, 64
ROPE_THETA = 10000.0
C_HEADS, C_HEAD_DIM = 8, 64
C_WIDTH = C_HEADS * C_HEAD_DIM
DECAY_LORA, AAA_LORA, GATE_LORA = 64, 64, 128
GN_EPS = 64e-5
N_BUCKETS, MAX_DISTANCE = 32, 128
D_FF = 2816
EPS = 1e-6
NEG = -1e30
LOG2E = math.log2(math.e)

A_Q = A_HEADS * A_HEAD_DIM
A_KV = A_KV_GROUPS * A_HEAD_DIM
A_GATE = 3 * A_HEADS
A_COLS = A_Q + 6 * A_KV + A_GATE
B_COLS = Q_LORA + KV_LORA + ROPE_DIM
C_COLS = 3 * C_WIDTH + DECAY_LORA + AAA_LORA + GATE_LORA

LANES = 128
PREV_ROWS = 16
VMEM_LIMIT = 48 * 1024 * 1024

Z_AQ = 0
Z_QA = 512
Z_AKC, Z_AVC, Z_AKS, Z_AVS, Z_AKW, Z_AVW = 768, 896, 1024, 1152, 1280, 1408
Z_AG = 1536
Z_KVA = 1664
Z_KRA = 1792
Z_KRB = 1920
Z_CR, Z_CK, Z_CV = 2048, 2560, 3072
Z_CWA = 3584
Z_CG = 3712
Z_ZG = 4096
Z_COLS = 7168

TQ = 256
TM = 512
CH = 64
RW_SUB = 2


def _cparams(*sem):
    return pltpu.CompilerParams(dimension_semantics=sem, vmem_limit_bytes=VMEM_LIMIT)


def _gelu(x):
    return 0.5 * x * (1.0 + jnp.tanh(0.7978845608028654 * (x + 0.044715 * (x * x * x))))


def _dot(a, b, **kw):
    return jnp.dot(a, b, preferred_element_type=F32, **kw)


def _split3(x):
    hi = x.astype(BF16)
    r = x - hi.astype(F32)
    mid = r.astype(BF16)
    return hi, mid, (r - mid.astype(F32)).astype(BF16)


def _dot3(x, w):
    return sum(_dot(t, w) for t in _split3(x))


def _dot_t(a, b):
    return lax.dot_general(a, b, (((1,), (1,)), ((), ())), preferred_element_type=F32)


def _adaln_kernel(c_ref, w_ref, b_ref, o_ref):
    c = c_ref[...]
    cond = c * jax.nn.sigmoid(c)
    o_ref[0] = _dot(cond.astype(BF16), w_ref[0].astype(BF16)) + b_ref[0]


def _adaln(c, ada_w, ada_b):
    L, D, N6 = ada_w.shape
    B = c.shape[0]
    tn = 1536
    return pl.pallas_call(
        _adaln_kernel,
        grid=(L, N6 // tn),
        in_specs=[pl.BlockSpec((B, D), lambda l, j: (0, 0)),
                  pl.BlockSpec((1, D, tn), lambda l, j: (l, 0, j)),
                  pl.BlockSpec((1, 1, tn), lambda l, j: (l, 0, j))],
        out_specs=pl.BlockSpec((1, B, tn), lambda l, j: (l, 0, j)),
        out_shape=jax.ShapeDtypeStruct((L, B, N6), F32),
        compiler_params=_cparams("parallel", "parallel"),
        name="adaln",
    )(c, ada_w, ada_b.reshape(L, 1, N6))


def _nmm_kernel(x_ref, g_ref, sc_ref, sh_ref, w_ref, o_ref, h_ref):
    @pl.when(pl.program_id(1) == 0)
    def _():
        x = x_ref[...]
        y = x * lax.rsqrt(jnp.mean(x * x, axis=-1, keepdims=True) + EPS)
        h = (y * g_ref[...]) * (1.0 + sc_ref[0]) + sh_ref[0]
        h_ref[...] = h.astype(BF16)

    o_ref[...] = _dot(h_ref[...], w_ref[...]).astype(o_ref.dtype)


def _norm_mod_matmul(x2, gain, sc, sh, w, S, tn, out_dtype=BF16):
    N, D = x2.shape
    NC = w.shape[1]
    tm = min(1024, S)
    tpb = S // tm
    return pl.pallas_call(
        _nmm_kernel,
        grid=(N // tm, NC // tn),
        in_specs=[pl.BlockSpec((tm, D), lambda i, j: (i, 0)),
                  pl.BlockSpec((1, D), lambda i, j: (0, 0)),
                  pl.BlockSpec((1, 1, D), lambda i, j: (i // tpb, 0, 0)),
                  pl.BlockSpec((1, 1, D), lambda i, j: (i // tpb, 0, 0)),
                  pl.BlockSpec((D, tn), lambda i, j: (0, j))],
        out_specs=pl.BlockSpec((tm, tn), lambda i, j: (i, j)),
        out_shape=jax.ShapeDtypeStruct((N, NC), out_dtype),
        scratch_shapes=[pltpu.VMEM((tm, D), BF16)],
        compiler_params=_cparams("parallel", "arbitrary"),
        name="norm_mod_matmul",
    )(x2, gain.reshape(1, D), sc, sh, w)


def _compress_kernel(zk_ref, zv_ref, w1_ref, pos_ref, w1c_ref, w2_ref, kc_ref, vc_ref, zf_ref, *, NC):
    for kind, (z_ref, o_ref) in enumerate(((zk_ref, kc_ref), (zv_ref, vc_ref))):
        pa = jnp.zeros((NC, LANES), F32)
        pb = jnp.zeros((NC, LANES), F32)
        zf_ref[...] = z_ref[...].astype(F32)
        for l in range(CMP_STRIDE):
            xl = zf_ref[pl.ds(l, NC, stride=CMP_STRIDE), :].astype(BF16)
            pa = pa + _dot(xl, w1_ref[kind, l])
            pb = pb + _dot(xl, w1_ref[kind, CMP_STRIDE + l])
        posb = _dot(pos_ref[kind].astype(BF16), w1c_ref[kind])[0:1, :]
        h = pa + pltpu.roll(pb, NC - 1, axis=0) + posb
        act = _gelu(h).astype(BF16)
        row = lax.broadcasted_iota(jnp.int32, (NC, LANES), 0)
        for g in range(A_KV_GROUPS):
            for eo in range(2):
                out = _dot(act, w2_ref[kind, g, eo])
                out = jnp.where(row < NC - 1, out, 0.0)
                o_ref[g, eo, pl.ds(0, NC), :] = jnp.zeros((NC, LANES), BF16)
                o_ref[g, eo, pl.ds(NC, NC), :] = out.astype(BF16)


def _compress(z, w1bd, posrow, w1cat, w2v, B, S):
    NC = S // CMP_STRIDE
    out = jax.ShapeDtypeStruct((B, A_KV_GROUPS, 2, 2 * NC, LANES), BF16)
    ospec = pl.BlockSpec((None, A_KV_GROUPS, 2, 2 * NC, LANES), lambda b: (b, 0, 0, 0, 0))
    full = lambda a: pl.BlockSpec(a.shape, lambda b: (0,) * a.ndim)
    return pl.pallas_call(
        functools.partial(_compress_kernel, NC=NC),
        grid=(B,),
        in_specs=[pl.BlockSpec((S, LANES), lambda b: (b, Z_AKC // LANES)),
                  pl.BlockSpec((S, LANES), lambda b: (b, Z_AVC // LANES)),
                  full(w1bd), full(posrow), full(w1cat), full(w2v)],
        out_specs=[ospec, ospec],
        out_shape=[out, out],
        scratch_shapes=[pltpu.VMEM((S, LANES), F32)],
        compiler_params=_cparams("parallel"),
        name="nsa_compress",
    )(z, z, w1bd, posrow, w1cat, w2v)


def _stack_pairs(zq, scale):
    q = zq.astype(F32) * scale
    return jnp.concatenate([q[:, :LANES], q[:, LANES:]], axis=0).astype(BF16)


def _cmpsel_kernel(zq_ref, kc_ref, vc_ref, cb_ref, oc_ref, sb_ref, *, NC, NS):
    qt = pl.program_id(2)
    r = TQ // CMP_STRIDE
    lhs = _stack_pairs(zq_ref[...], A_HEAD_DIM ** -0.5)
    st = pl.multiple_of(r * (qt + 1), 16)
    col = lax.broadcasted_iota(jnp.int32, (2 * TQ, NC), 1)
    exists = col >= NC - r * (qt + 1)
    acc = jnp.zeros((2 * TQ, LANES), F32)
    psum = jnp.zeros((TQ, NC), F32)
    for eo in range(2):
        kwin = kc_ref[eo, pl.ds(st, NC), :]
        vwin = vc_ref[eo, pl.ds(st, NC), :]
        logits = _dot_t(lhs, kwin) + cb_ref[eo]
        logits = jnp.where(exists, logits, NEG)
        m = jnp.max(logits, axis=-1, keepdims=True)
        p = jnp.where(logits > 0.1 * NEG, jnp.exp(logits - m), 0.0)
        l = jnp.maximum(jnp.sum(p, axis=-1, keepdims=True), 1e-30)
        p = p * (1.0 / l)
        acc = acc + _dot(p.astype(BF16), vwin)
        psum = psum + p[:TQ] + p[TQ:]
    oc_ref[:, :LANES] = acc[:TQ]
    oc_ref[:, LANES:] = acc[TQ:]

    jj = lax.broadcasted_iota(jnp.int32, (LANES, NC), 0)
    nn = lax.broadcasted_iota(jnp.int32, (LANES, NC), 1) + (r * (qt + 1) - NC)
    delta = 4 * jj - nn
    mt = jnp.where((delta == 0) | (delta == 4), 1.0, 0.0) + jnp.where((delta >= 1) & (delta <= 3), 2.0, 0.0)
    mt = jnp.where(jj < NS, mt, 0.0)
    mtb = mt.astype(BF16)
    imp_t = sum(_dot_t(mtb, t) for t in _split3(psum))
    jb = lax.broadcasted_iota(jnp.int32, (LANES, TQ), 0)
    tpos = qt * TQ + lax.broadcasted_iota(jnp.int32, (LANES, TQ), 1)
    back = (tpos >> 6) - jb
    forced = (jb == 0) | ((back >= 0) & (back < N_LOCAL_BLOCKS))
    score = jnp.where(forced, FORCE_SCORE, jnp.where(back >= 0, imp_t, -1.0))
    score = jnp.where(jb < NS, score, -2.0)
    rank = jnp.zeros((LANES, TQ), F32)
    for jp in range(NS):
        row = score[jp:jp + 1, :]
        beats = (row > score) | ((row == score) & (jb > jp))
        rank = rank + jnp.where(beats, 1.0, 0.0)
    sel = (rank < float(min(SLC_TOPK, NS))) & (back >= 0) & (jb < NS)
    sb_ref[...] = jnp.where(sel, 0.0, jnp.where(jb < NS, NEG, 0.0)).astype(BF16)


def _cmpsel(z, kcp, vcp, cb, B, S):
    NC = S // CMP_STRIDE
    NS = S // SLC_BLOCK
    nq = S // TQ
    kspec = pl.BlockSpec((None, None, 2, 2 * NC, LANES), lambda b, g, q: (b, g, 0, 0, 0))
    return pl.pallas_call(
        functools.partial(_cmpsel_kernel, NC=NC, NS=NS),
        grid=(B, A_KV_GROUPS, nq),
        in_specs=[pl.BlockSpec((TQ, 2 * LANES), lambda b, g, q: (b * nq + q, g)),
                  kspec, kspec,
                  pl.BlockSpec((None, 2, 2 * TQ, NC), lambda b, g, q: (g, 0, 0, 0))],
        out_specs=[pl.BlockSpec((TQ, 2 * LANES), lambda b, g, q: (b * nq + q, g)),
                   pl.BlockSpec((None, None, LANES, TQ), lambda b, g, q: (b, g, 0, q))],
        out_shape=[jax.ShapeDtypeStruct((B * S, A_Q), F32),
                   jax.ShapeDtypeStruct((B, A_KV_GROUPS, LANES, S), BF16)],
        compiler_params=_cparams("parallel", "parallel", "parallel"),
        name="nsa_cmp_select",
    )(z, kcp, vcp, cb)


def _flash_update_t(s, vt, state, acc_ref):
    m, l = state
    m_new = jnp.maximum(m, jnp.max(s, axis=0, keepdims=True))
    p = jnp.exp2(s - m_new)
    alpha = jnp.exp2(m - m_new)
    acc_ref[...] = alpha * acc_ref[...] + _dot(vt, p.astype(BF16))
    return m_new, alpha * l + jnp.sum(p, axis=0, keepdims=True)


def _flash_pipeline(qk, softmax, lo, hi, state, s_ref):
    s_ref[0] = qk(0, lo)

    def body(kc, st):
        s_ref[1] = qk(1, kc)
        st0 = softmax(0, s_ref[0], kc, st[0], False)
        s_ref[0] = qk(0, kc + 1)
        st1 = softmax(1, s_ref[1], kc, st[1], False)
        return st0, st1

    st = lax.fori_loop(lo, hi, body, state)
    s_ref[1] = qk(1, hi)
    st0 = softmax(0, s_ref[0], hi, st[0], True)
    st1 = softmax(1, s_ref[1], hi, st[1], True)
    return st0, st1


def _split_heads_kv(slab, g):
    lane = lax.broadcasted_iota(jnp.int32, slab.shape, 1)
    own = jnp.where((lane < A_HEAD_DIM) == (g == 0), slab, 0.0)
    other = pltpu.roll(own, A_HEAD_DIM, axis=1)
    is0 = g == 0
    return jnp.where(is0, own, other), jnp.where(is0, other, own)


def _nsa_flash_kernel(*refs, S, selected):
    if selected:
        zq_ref, sbt_ref, zk_ref, zv_ref, tab_ref, o_ref, ke_ref, ko_ref, vt_ref, acc_ref, s_ref = refs
    else:
        zq_ref, zk_ref, zv_ref, tab_ref, o_ref, ke_ref, ko_ref, vt_ref, acc_ref, s_ref = refs
    g = pl.program_id(1)
    qt = pl.program_id(2)
    k_refs = (ke_ref, ko_ref)

    @pl.when(qt == 0)
    def _():
        ke, ko = _split_heads_kv(zk_ref[...].astype(F32), g)
        ke_ref[:, :LANES] = ke.astype(BF16)
        ko_ref[:, :LANES] = ko.astype(BF16)
        for c in range(S // TQ):
            vt_ref[c] = zv_ref[c * TQ:(c + 1) * TQ, :].astype(F32).T.astype(BF16)
        if selected:
            rowb = lax.broadcasted_iota(jnp.int32, (S, LANES), 0) >> 6
            lane = lax.broadcasted_iota(jnp.int32, (S, LANES), 1)
            onehot = jnp.where(rowb == lane, 1.0, 0.0).astype(BF16)
            ke_ref[:, LANES:] = onehot
            ko_ref[:, LANES:] = onehot

    zq = zq_ref[...].astype(F32) * (A_HEAD_DIM ** -0.5 * LOG2E)
    rhs = jnp.concatenate([zq[:, :LANES], zq[:, LANES:]], axis=0).T.astype(BF16)
    if selected:
        sbt = sbt_ref[...]
        rhs = jnp.concatenate([rhs, jnp.concatenate([sbt, sbt], axis=1)], axis=0)
    acc_ref[...] = jnp.zeros(acc_ref.shape, F32)
    vrow = pl.multiple_of(g * A_HEAD_DIM, A_HEAD_DIM)

    def qk(eo, kc):
        ks = pl.multiple_of(kc * TQ, TQ)
        return _dot(k_refs[eo][pl.ds(ks, TQ), :], rhs)

    def softmax(eo, s, kc, state, diag):
        ti = 0 if diag else jnp.minimum(qt - kc, 2)
        vt = vt_ref[kc, pl.ds(vrow, A_HEAD_DIM), :]
        return _flash_update_t(s + tab_ref[eo, ti], vt, state, acc_ref.at[eo])

    init = (jnp.full((1, 2 * TQ), 0.5 * NEG, F32), jnp.zeros((1, 2 * TQ), F32))
    lo = 0 if selected else jnp.maximum(qt - 2, 0)
    carry = _flash_pipeline(qk, softmax, lo, qt, (init, init), s_ref)
    out_t = jnp.concatenate([acc_ref[0] * (1.0 / carry[0][1]), acc_ref[1] * (1.0 / carry[1][1])], axis=0)
    out = out_t.T
    o_ref[:, :LANES] = out[:TQ]
    o_ref[:, LANES:] = out[TQ:]


def _nsa_flash(z, sb, tab, zk_col, zv_col, B, S, selected):
    nq = S // TQ
    kw = 2 * LANES if selected else LANES
    in_specs = [pl.BlockSpec((TQ, 2 * LANES), lambda b, g, q: (b * nq + q, g))]
    args = [z]
    if selected:
        in_specs.append(pl.BlockSpec((None, None, LANES, TQ), lambda b, g, q: (b, g, 0, q)))
        args.append(sb)
    in_specs += [pl.BlockSpec((S, LANES), lambda b, g, q: (b, zk_col // LANES)),
                 pl.BlockSpec((S, LANES), lambda b, g, q: (b, zv_col // LANES)),
                 pl.BlockSpec((None, 2, 3, TQ, 2 * TQ), lambda b, g, q: (g, 0, 0, 0, 0))]
    args += [z, z, tab]
    return pl.pallas_call(
        functools.partial(_nsa_flash_kernel, S=S, selected=selected),
        grid=(B, A_KV_GROUPS, nq),
        in_specs=in_specs,
        out_specs=pl.BlockSpec((TQ, 2 * LANES), lambda b, g, q: (b * nq + q, g)),
        out_shape=jax.ShapeDtypeStruct((B * S, A_Q), F32),
        scratch_shapes=[pltpu.VMEM((S, kw), BF16), pltpu.VMEM((S, kw), BF16),
                        pltpu.VMEM((S // TQ, LANES, TQ), BF16),
                        pltpu.VMEM((2, A_HEAD_DIM, 2 * TQ), F32),
                        pltpu.VMEM((2, TQ, 2 * TQ), F32)],
        compiler_params=_cparams("parallel", "parallel", "arbitrary"),
        name="nsa_selected" if selected else "nsa_window",
    )(*args)


def _mla_proj_kernel(zqa_ref, zkva_ref, zkra_ref, zkrb_ref, qn_ref, kvn_ref, wqa_ref, wqb_ref,
                     wk_ref, wv_ref, caq_ref, cbq_ref, cak_ref, cbk_ref, q_ref, k_ref, v_ref):
    def rms(x, gain):
        return x * lax.rsqrt(jnp.mean(x * x, axis=-1, keepdims=True) + EPS) * gain

    nq = rms(zqa_ref[...].astype(F32), qn_ref[...]).astype(BF16)
    nkv = rms(zkva_ref[...].astype(F32), kvn_ref[...]).astype(BF16)
    qa = _dot_t(wqa_ref[...], nq)
    qb = _dot_t(wqb_ref[...], nq)
    kn = _dot(nkv, wk_ref[...])
    vt = _dot_t(wv_ref[...], nkv)
    caq, cbq = caq_ref[...], cbq_ref[...]
    kr = zkra_ref[...].astype(F32) * cak_ref[...] + zkrb_ref[...].astype(F32) * cbk_ref[...]
    for h in range(B_HEADS):
        sl = slice(h * LANES, (h + 1) * LANES)
        q_ref[h] = (qa[sl, :] * caq + qb[sl, :] * cbq).astype(BF16)
        k_ref[h] = (kn[:, sl] + kr).astype(BF16)
        v_ref[h] = vt[h * V_DIM:(h + 1) * V_DIM, :].astype(BF16)


def _mla_proj(z, qn, kvn, wqa, wqb, wk, wv, caq, cbq, cak, cbk, B, S):
    tm = TM
    nt = S // tm
    zspec = lambda w, col: pl.BlockSpec((tm, w), lambda b, i: (b * nt + i, col // w))
    full = lambda a: pl.BlockSpec(a.shape, lambda b, i: (0,) * a.ndim)
    tspec = pl.BlockSpec((tm, LANES), lambda b, i: (i, 0))
    tspec_t = pl.BlockSpec((LANES, tm), lambda b, i: (0, i))
    return pl.pallas_call(
        _mla_proj_kernel,
        grid=(B, nt),
        in_specs=[zspec(Q_LORA, Z_QA), zspec(LANES, Z_KVA), zspec(LANES, Z_KRA), zspec(LANES, Z_KRB),
                  full(qn), full(kvn), full(wqa), full(wqb), full(wk), full(wv),
                  tspec_t, tspec_t, tspec, tspec],
        out_specs=[pl.BlockSpec((None, B_HEADS, LANES, tm), lambda b, i: (b, 0, 0, i)),
                   pl.BlockSpec((None, B_HEADS, tm, LANES), lambda b, i: (b, 0, i, 0)),
                   pl.BlockSpec((None, B_HEADS, None, V_DIM, tm), lambda b, i: (b, 0, i, 0, 0))],
        out_shape=[jax.ShapeDtypeStruct((B, B_HEADS, LANES, S), BF16),
                   jax.ShapeDtypeStruct((B, B_HEADS, S, LANES), BF16),
                   jax.ShapeDtypeStruct((B, B_HEADS, nt, V_DIM, tm), BF16)],
        compiler_params=_cparams("parallel", "parallel"),
        name="mla_proj",
    )(z, z, z, z, qn, kvn, wqa, wqb, wk, wv, caq, cbq, cak, cbk)


def _mla_flash_kernel(qt_ref, k_ref, vt_ref, o_ref, acc_ref, s_ref):
    qi = pl.program_id(2)
    acc_ref[...] = jnp.zeros(acc_ref.shape, F32)
    ri = lax.broadcasted_iota(jnp.int32, (TM, TM), 0)
    ci = lax.broadcasted_iota(jnp.int32, (TM, TM), 1)
    causal = ri <= ci

    def qk(hh, kc):
        ks = pl.multiple_of(kc * TM, TM)
        return _dot(k_ref[hh, pl.ds(ks, TM), :], qt_ref[hh])

    def softmax(hh, s, kc, state, diag):
        if diag:
            s = jnp.where(causal, s, NEG)
        return _flash_update_t(s, vt_ref[hh, kc], state, acc_ref.at[hh])

    init = (jnp.full((1, TM), 0.5 * NEG, F32), jnp.zeros((1, TM), F32))
    carry = _flash_pipeline(qk, softmax, 0, qi, (init, init), s_ref)
    out_t = jnp.concatenate([acc_ref[0] * (1.0 / carry[0][1]), acc_ref[1] * (1.0 / carry[1][1])], axis=0)
    o_ref[...] = out_t.T


def _mla_flash(qt, k, vt, B, S):
    nq = S // TM
    hp = B_HEADS // 2
    return pl.pallas_call(
        _mla_flash_kernel,
        grid=(B, hp, nq),
        in_specs=[pl.BlockSpec((None, 2, LANES, TM), lambda b, h, i: (b, h, 0, i)),
                  pl.BlockSpec((None, 2, S, LANES), lambda b, h, i: (b, h, 0, 0)),
                  pl.BlockSpec((None, 2, nq, V_DIM, TM), lambda b, h, i: (b, h, 0, 0, 0))],
        out_specs=pl.BlockSpec((TM, LANES), lambda b, h, i: (b * nq + i, h)),
        out_shape=jax.ShapeDtypeStruct((B * S, B_HEADS * V_DIM), F32),
        scratch_shapes=[pltpu.VMEM((2, V_DIM, TM), F32), pltpu.VMEM((2, TM, TM), F32)],
        compiler_params=_cparams("parallel", "parallel", "arbitrary"),
        name="mla_flash",
    )(qt, k, vt)


def _rwkv_prep_kernel(zr_ref, zk_ref, zv_ref, zwa_ref, zg_ref, mu_r, mu_k, mu_v, mu_wa, mu_g,
                      w0_ref, a0_ref, kk_ref, ka_ref, rk_ref, w2_ref, a2_ref, g2_ref, bd_ref,
                      r_o, k_o, v_o, kk_o, be_o, ld_o, g_o, bo_o,
                      c_r, c_k, c_v, c_wa, c_g):
    t = pl.program_id(1)
    tm = zr_ref.shape[0]

    def shifted(z_ref, mu_ref, c_ref):
        x = z_ref[...].astype(F32)
        row = lax.broadcasted_iota(jnp.int32, x.shape, 0)
        prev = jnp.where(t == 0, 0.0, c_ref[0:1, :])
        xs = jnp.where(row == 0, prev, pltpu.roll(x, 1, axis=0))
        c_ref[0:1, :] = x[tm - 1:tm, :]
        return x + (xs - x) * mu_ref[...]

    r = shifted(zr_ref, mu_r, c_r)
    k = shifted(zk_ref, mu_k, c_k)
    v = shifted(zv_ref, mu_v, c_v)
    wa = shifted(zwa_ref, mu_wa, c_wa)
    gd = shifted(zg_ref, mu_g, c_g)
    w = w0_ref[...] + _dot(jnp.tanh(wa).astype(BF16), w2_ref[...])
    ld_o[...] = -jax.nn.sigmoid(w) * math.exp(-0.5)
    a = jax.nn.sigmoid(a0_ref[...] + _dot(wa.astype(BF16), a2_ref[...]))
    g_o[...] = _dot(jax.nn.sigmoid(gd).astype(BF16), g2_ref[...])
    kk = k * kk_ref[...]
    nsq = _dot3(kk * kk, bd_ref[...])
    kk = kk / jnp.maximum(jnp.sqrt(nsq), 1e-12)
    k2 = k * (1.0 + (a - 1.0) * ka_ref[...])
    rks = _dot3(r * k2 * rk_ref[...], bd_ref[...])
    r_o[...] = r
    k_o[...] = k2
    v_o[...] = v
    kk_o[...] = kk
    be_o[...] = kk * a
    bo_o[...] = rks * v


def _rwkv_prep(z, mus, w0, a0, k_k, k_a, r_k, w2p, a2p, g2, bd, B, S, tm=256):
    nt = S // tm
    W = C_WIDTH
    zspec = lambda w, col: pl.BlockSpec((tm, w), lambda b, i: (b * nt + i, col // w))
    full = lambda a: pl.BlockSpec(a.shape, lambda b, i: (0,) * a.ndim)
    ospec = pl.BlockSpec((tm, W), lambda b, i: (b * nt + i, 0))
    oshape = jax.ShapeDtypeStruct((B * S, W), F32)
    consts = list(mus) + [w0, a0, k_k, k_a, r_k, w2p, a2p, g2, bd]
    return pl.pallas_call(
        _rwkv_prep_kernel,
        grid=(B, nt),
        in_specs=[zspec(W, Z_CR), zspec(W, Z_CK), zspec(W, Z_CV), zspec(LANES, Z_CWA), zspec(LANES, Z_CG)]
                 + [full(a) for a in consts],
        out_specs=[ospec] * 8,
        out_shape=[oshape] * 8,
        scratch_shapes=[pltpu.VMEM((8, W), F32)] * 3 + [pltpu.VMEM((8, LANES), F32)] * 2,
        compiler_params=_cparams("parallel", "arbitrary"),
        name="rwkv_prep",
    )(z, z, z, z, z, *consts)


def _rwkv_chunk_kernel(r_ref, k_ref, v_ref, kk_ref, be_ref, ld_ref,
                       qa_o, bkt_o, yu_o, vp_o, pcb_o):
    C = CH
    R = ld_ref.shape[0]
    ld = ld_ref[...]
    rr = lax.broadcasted_iota(jnp.int32, (R, R), 0)
    cc = lax.broadcasted_iota(jnp.int32, (R, R), 1)
    sh = int(math.log2(C))
    tri = jnp.where((cc <= rr) & ((cc >> sh) == (rr >> sh)), 1.0, 0.0).astype(BF16)
    cs = sum(_dot(tri, t) for t in _split3(ld))
    cl = jnp.concatenate([jnp.broadcast_to(cs[(s + 1) * C - 1:(s + 1) * C, :], (C, C_WIDTH))
                          for s in range(R // C)], axis=0)
    ex, exn, exx, exc = jnp.exp(cs), jnp.exp(-cs), jnp.exp(cs - ld), jnp.exp(cl - cs)
    kk, be, k2, v = kk_ref[...], be_ref[...], k_ref[...], v_ref[...]
    at = -kk * exx
    rt = r_ref[...] * ex
    bt = be * exn
    kt = k2 * exn
    btc = be * exc
    ktc = k2 * exc
    pc = jnp.exp(cl)
    ri = lax.broadcasted_iota(jnp.int32, (C, C), 0)
    ci = lax.broadcasted_iota(jnp.int32, (C, C), 1)
    incl = ci <= ri
    strict = ci < ri
    eye = jnp.where(ci == ri, 1.0, 0.0)
    lane = lax.broadcasted_iota(jnp.int32, (C, LANES), 1)
    P = C_HEADS // 2
    pairs = [(s, p) for s in range(R // C) for p in range(P)]
    heads = [(i, hh) for i in range(len(pairs)) for hh in range(2)]
    blk = lambda x, i: x[pairs[i][0] * C:(pairs[i][0] + 1) * C, pairs[i][1] * LANES:(pairs[i][1] + 1) * LANES]
    lhs2 = [jnp.concatenate([blk(at, i), blk(rt, i)], axis=0).astype(BF16) for i in range(len(pairs))]
    bm, km, vm, am = [], [], [], []
    for i, hh in heads:
        msk = (lane < C_HEAD_DIM) if hh == 0 else (lane >= C_HEAD_DIM)
        bm.append(jnp.where(msk, blk(bt, i), 0.0).astype(BF16))
        km.append(jnp.where(msk, blk(kt, i), 0.0).astype(BF16))
        vm.append(jnp.where(msk, blk(v, i), 0.0).astype(BF16))
        am.append(jnp.where(msk, blk(at, i), 0.0).astype(BF16))
    g1 = [_dot_t(lhs2[i], bm[j]) for j, (i, hh) in enumerate(heads)]
    g2 = [_dot_t(lhs2[i], km[j]) for j, (i, hh) in enumerate(heads)]
    lab = [jnp.where(strict, g[:C], 0.0) for g in g1]
    mrb = [jnp.where(incl, g[C:], 0.0).astype(BF16) for g in g1]
    lak = [jnp.where(strict, g[:C], 0.0).astype(BF16) for g in g2]
    mrk = [jnp.where(incl, g[C:], 0.0).astype(BF16) for g in g2]
    w2 = [_dot(a, b) for a, b in zip(lak, vm)]
    yk = [_dot(a, b) for a, b in zip(mrk, vm)]
    tinv = [eye + x for x in lab]
    lp = lab
    for _ in range(int(math.log2(C)) - 1):
        lpb = [x.astype(BF16) for x in lp]
        lp = [_dot(x, x) for x in lpb]
        tinv = [t + _dot(t.astype(BF16), x.astype(BF16)) for t, x in zip(tinv, lp)]
    au = [_dot(t.astype(BF16), jnp.concatenate([a, w.astype(BF16)], axis=1)) for t, a, w in zip(tinv, am, w2)]
    qy = [_dot(m, x.astype(BF16)) for m, x in zip(mrb, au)]
    for i, (s, p) in enumerate(pairs):
        e, o = 2 * i, 2 * i + 1
        ahat = au[e][:, :LANES] + au[o][:, :LANES]
        uhat = au[e][:, LANES:] + au[o][:, LANES:]
        qhat = blk(rt, i) + qy[e][:, :LANES] + qy[o][:, :LANES]
        yi = qy[e][:, LANES:] + qy[o][:, LANES:] + yk[e] + yk[o]
        qa_o[s, p] = jnp.concatenate([qhat, ahat], axis=0).astype(BF16)
        yu_o[s, p] = jnp.concatenate([yi, uhat], axis=0)
        bkt_o[s, p] = jnp.concatenate([blk(btc, i), blk(ktc, i)], axis=0).T.astype(BF16)
        vp_o[s, p] = blk(v, i).astype(BF16)
        pcb_o[s, p] = jnp.concatenate([blk(pc, i), blk(pc, i)], axis=0).T


def _rwkv_chunk(r, k2, v, kk, be, ld, B, S):
    nch = S // CH
    W = C_WIDTH
    P = C_HEADS // 2
    ns = RW_SUB
    ispec = pl.BlockSpec((ns * CH, W), lambda b, c: (b * (nch // ns) + c, 0))
    sq = lambda rows: pl.BlockSpec((None, ns, P, rows, LANES), lambda b, c: (b, c, 0, 0, 0))
    shp = lambda rows, dt: jax.ShapeDtypeStruct((B, nch, P, rows, LANES), dt)
    return pl.pallas_call(
        _rwkv_chunk_kernel,
        grid=(B, nch // ns),
        in_specs=[ispec] * 6,
        out_specs=[sq(2 * CH), sq(LANES), sq(2 * CH), sq(CH), sq(LANES)],
        out_shape=[shp(2 * CH, BF16), shp(LANES, BF16), shp(2 * CH, F32), shp(CH, BF16), shp(LANES, F32)],
        compiler_params=_cparams("parallel", "parallel"),
        name="rwkv_chunk",
    )(r, k2, v, kk, be, ld)


def _rwkv_state_kernel(qa_ref, bkt_ref, yu_ref, vp_ref, pcb_ref, y_ref, ap_ref):
    @pl.when(pl.program_id(1) == 0)
    def _():
        ap_ref[...] = jnp.zeros(ap_ref.shape, F32)

    ri = lax.broadcasted_iota(jnp.int32, (LANES, LANES), 0)
    ci = lax.broadcasted_iota(jnp.int32, (LANES, LANES), 1)
    same_head = (ri < C_HEAD_DIM) == (ci < C_HEAD_DIM)
    idx = [(b, p) for b in range(qa_ref.shape[0]) for p in range(C_HEADS // 2)]
    a = [ap_ref[b, p] for b, p in idx]
    x = [_dot(qa_ref[b, p], s.astype(BF16)) for (b, p), s in zip(idx, a)]
    yu = [yu_ref[b, p] for b, p in idx]
    for (b, p), xi, yi in zip(idx, x, yu):
        y_ref[b, :, p * LANES:(p + 1) * LANES] = xi[:CH] + yi[:CH]
    uv = [jnp.concatenate([(xi[CH:] + yi[CH:]).astype(BF16), vp_ref[b, p]], axis=0)
          for (b, p), xi, yi in zip(idx, x, yu)]
    upd = [_dot(bkt_ref[b, p], t) for (b, p), t in zip(idx, uv)]
    for (b, p), s, t in zip(idx, a, upd):
        ap_ref[b, p] = pcb_ref[b, p] * s + jnp.where(same_head, t, 0.0)


def _rwkv_state(qa, bkt, yu, vp, pcb, B, S):
    nch = S // CH
    P = C_HEADS // 2
    nb = 2 if B % 2 == 0 else 1
    sq = lambda rows: pl.BlockSpec((nb, None, P, rows, LANES), lambda b, c: (b, c, 0, 0, 0))
    return pl.pallas_call(
        _rwkv_state_kernel,
        grid=(B // nb, nch),
        in_specs=[sq(2 * CH), sq(LANES), sq(2 * CH), sq(CH), sq(LANES)],
        out_specs=pl.BlockSpec((nb, CH, C_WIDTH), lambda b, c: (b, c, 0)),
        out_shape=jax.ShapeDtypeStruct((B, S, C_WIDTH), F32),
        scratch_shapes=[pltpu.VMEM((nb, P, LANES, LANES), F32)],
        compiler_params=_cparams("parallel", "arbitrary"),
        name="rwkv_state",
    )(qa, bkt, yu, vp, pcb).reshape(B * S, C_WIDTH)


def _merge_kernel(oc_ref, os_ref, ow_ref, zag_ref, yb_ref, yr_ref, bo_ref, gg_ref,
                  zga_ref, zgb_ref, zgc_ref, x_ref, wbr_ref, wout_ref, ge_ref, bdm_ref,
                  ln_ref, gain_ref, gt_ref, o_ref):
    sg = jax.nn.sigmoid(zag_ref[...].astype(F32))
    ya = (_dot3(sg, ge_ref[0]) * oc_ref[...] + _dot3(sg, ge_ref[1]) * os_ref[...]
          + _dot3(sg, ge_ref[2]) * ow_ref[...])
    yr = yr_ref[...]
    mean = _dot3(yr, bdm_ref[...])
    d = yr - mean
    var = _dot3(d * d, bdm_ref[...])
    yn = d * lax.rsqrt(var + GN_EPS) * ln_ref[0:1, :] + ln_ref[1:2, :]
    yc = (yn + bo_ref[...]) * gg_ref[...]
    sig = lambda ref: jax.nn.sigmoid(ref[...].astype(F32))
    merged = (sig(zga_ref) * _dot(ya.astype(BF16), wbr_ref[0:A_Q, :])
              + sig(zgb_ref) * _dot(yb_ref[...].astype(BF16), wbr_ref[A_Q:2 * A_Q, :])
              + sig(zgc_ref) * _dot(yc.astype(BF16), wbr_ref[2 * A_Q:3 * A_Q, :]))
    y = _dot(merged.astype(BF16), wout_ref[...])
    yn2 = y * lax.rsqrt(jnp.mean(y * y, axis=-1, keepdims=True) + EPS) * gain_ref[...]
    o_ref[...] = x_ref[...] + gt_ref[0] * yn2


def _merge(oc, os_, ow, z, yb, yr, bo, gg, x2, wbr, wout, gexp, bdm, ln, gain, gt, S, tm=256):
    N, D = x2.shape
    tpb = S // tm
    W = A_Q
    row = lambda w: pl.BlockSpec((tm, w), lambda i: (i, 0))
    zspec = lambda w, col: pl.BlockSpec((tm, w), lambda i: (i, col // w))
    full = lambda a: pl.BlockSpec(a.shape, lambda i: (0,) * a.ndim)
    return pl.pallas_call(
        _merge_kernel,
        grid=(N // tm,),
        in_specs=[row(W), row(W), row(W), zspec(LANES, Z_AG), row(W), row(W), row(W), row(W),
                  zspec(D, Z_ZG), zspec(D, Z_ZG + D), zspec(D, Z_ZG + 2 * D), row(D),
                  full(wbr), full(wout), full(gexp), full(bdm), full(ln),
                  pl.BlockSpec((1, D), lambda i: (0, 0)),
                  pl.BlockSpec((1, 1, D), lambda i: (i // tpb, 0, 0))],
        out_specs=row(D),
        out_shape=jax.ShapeDtypeStruct((N, D), F32),
        compiler_params=_cparams("parallel"),
        name="merge_out",
    )(oc, os_, ow, z, yb, yr, bo, gg, z, z, z, x2, wbr, wout, gexp, bdm, ln, gain.reshape(1, D), gt)


def _ffn_down_kernel(u_ref, up_ref, cw_ref, cb_ref, wd_ref, x_ref, gain_ref, gt_ref, o_ref, *, tpb):
    i = pl.program_id(0)
    tm = u_ref.shape[0]
    first = (i % tpb) == 0
    fc = 256
    row = lax.broadcasted_iota(jnp.int32, (tm, fc), 0)

    def conv(c0):
        u = u_ref[:, c0:c0 + fc].astype(F32)
        prev = up_ref[:, c0:c0 + fc].astype(F32)
        p1 = jnp.where(first, 0.0, prev[PREV_ROWS - 1:PREV_ROWS])
        p2 = jnp.where(first, 0.0, prev[PREV_ROWS - 2:PREV_ROWS - 1])
        u1 = jnp.where(row == 0, p1, pltpu.roll(u, 1, axis=0))
        u2 = jnp.where(row == 0, p2, jnp.where(row == 1, p1, pltpu.roll(u, 2, axis=0)))
        return (cw_ref[0:1, c0:c0 + fc] * u2 + cw_ref[1:2, c0:c0 + fc] * u1
                + cw_ref[2:3, c0:c0 + fc] * u + cb_ref[:, c0:c0 + fc])

    f = jnp.zeros((tm, D_MODEL), F32)
    for c in range(D_FF // fc):
        a = _gelu(conv(c * fc)) * conv(D_FF + c * fc)
        f = f + _dot(a.astype(BF16), wd_ref[c * fc:(c + 1) * fc, :])
    fn = f * lax.rsqrt(jnp.mean(f * f, axis=-1, keepdims=True) + EPS) * gain_ref[...]
    o_ref[...] = x_ref[...] + gt_ref[0] * fn


def _ffn_down(u, conv_w, conv_b, wd, x2, gain, gt, S, tm=256):
    N, D = x2.shape
    F2 = u.shape[1]
    tpb = S // tm
    full = lambda a: pl.BlockSpec(a.shape, lambda i: (0,) * a.ndim)
    return pl.pallas_call(
        functools.partial(_ffn_down_kernel, tpb=tpb),
        grid=(N // tm,),
        in_specs=[pl.BlockSpec((tm, F2), lambda i: (i, 0)),
                  pl.BlockSpec((PREV_ROWS, F2), lambda i: (jnp.maximum(i * (tm // PREV_ROWS) - 1, 0), 0)),
                  full(conv_w), pl.BlockSpec((1, F2), lambda i: (0, 0)), full(wd),
                  pl.BlockSpec((tm, D), lambda i: (i, 0)),
                  pl.BlockSpec((1, D), lambda i: (0, 0)),
                  pl.BlockSpec((1, 1, D), lambda i: (i // tpb, 0, 0))],
        out_specs=pl.BlockSpec((tm, D), lambda i: (i, 0)),
        out_shape=jax.ShapeDtypeStruct((N, D), F32),
        compiler_params=_cparams("parallel"),
        name="ffn_down",
    )(u, u, conv_w, conv_b.reshape(1, F2), wd, x2, gain.reshape(1, D), gt)


def _t5_bucket_np(dist):
    d = np.maximum(dist, 0)
    max_exact = N_BUCKETS // 2
    large = max_exact + (np.log(np.maximum(d, 1).astype(np.float32) / max_exact)
                         / math.log(MAX_DISTANCE / max_exact) * (N_BUCKETS - max_exact)).astype(np.int32)
    return np.where(d < max_exact, d, np.minimum(large, N_BUCKETS - 1))


def _bias_table_kernel(scale_ref, rb_ref, code_ref, o_ref):
    code = code_ref[0]
    sc = scale_ref[pl.program_id(0)]
    for h in range(A_HEADS):
        acc = jnp.full(code.shape, NEG, F32)
        for b in range(N_BUCKETS):
            acc = jnp.where(code == b, rb_ref[b, h], acc)
        o_ref[0, h] = acc * sc


def _bias_tables(rel_bias, codes, scales):
    nt, R, C = codes.shape
    smem = pl.BlockSpec(memory_space=pltpu.SMEM)
    return pl.pallas_call(
        _bias_table_kernel,
        grid=(nt,),
        in_specs=[smem, smem, pl.BlockSpec((1, R, C), lambda t: (t, 0, 0))],
        out_specs=pl.BlockSpec((1, A_HEADS, R, C), lambda t: (t, 0, 0, 0)),
        out_shape=jax.ShapeDtypeStruct((nt, A_HEADS, R, C), F32),
        compiler_params=_cparams("parallel"),
        name="bias_tables",
    )(jnp.asarray(scales, F32), rel_bias, jnp.asarray(codes))


def _codes(dist, lo, hi):
    return np.where((dist >= lo) & (dist < hi), _t5_bucket_np(dist), N_BUCKETS).astype(np.int32)


def _nsa_tables(rel_bias, S):
    NC = S // CMP_STRIDE
    G, HPG = A_KV_GROUPS, A_HEADS // A_KV_GROUPS
    big = 1 << 30
    i = np.arange(TQ)[:, None]
    dc = i - CMP_STRIDE * np.arange(NC)[None, :] + CMP_STRIDE * NC - TQ - (CMP_BLOCK - 1)
    cb = _bias_tables(rel_bias, _codes(dc, 0, big)[None], [1.0])[0]
    cb = cb.reshape(G, HPG // 2, 2, TQ, NC).transpose(0, 2, 1, 3, 4).reshape(G, 2, 2 * TQ, NC)
    dt = np.arange(TQ)[None, :] - np.arange(TQ)[:, None]
    far = np.full((TQ, TQ), MAX_DISTANCE)
    codes = np.stack([_codes(dt, 0, WINDOW), _codes(TQ + dt, 0, WINDOW), _codes(far, 0, big),
                      _codes(2 * TQ + dt, 0, WINDOW)])
    t = _bias_tables(rel_bias, codes, [LOG2E] * 4)
    t = t.reshape(4, G, HPG // 2, 2, TQ, TQ).transpose(1, 3, 0, 4, 2, 5).reshape(G, 2, 4, TQ, 2 * TQ)
    return cb, t[:, :, 0:3], jnp.concatenate([t[:, :, 0:2], t[:, :, 3:4]], axis=2)


def _pad_cols(w, n):
    return jnp.pad(w, ((0, 0), (0, n - w.shape[1])))


def _prep_w_in(w):
    D = w.shape[0]
    za, zb, zc, zg = jnp.split(w, np.cumsum([A_COLS, B_COLS, C_COLS]).tolist(), axis=1)
    a_parts = jnp.split(za, np.cumsum([A_Q] + [A_KV] * 6).tolist(), axis=1)
    qa, kva, kr = jnp.split(zb, [Q_LORA, Q_LORA + KV_LORA], axis=1)
    half = ROPE_DIM // 2
    kr_rot = jnp.concatenate([-kr[:, half:], kr[:, :half]], axis=1)
    z64 = jnp.zeros((D, NOPE_DIM), w.dtype)
    z32 = jnp.zeros((D, LANES - NOPE_DIM - ROPE_DIM), w.dtype)
    c_r, c_k, c_v, c_wd, c_ad, c_gd = jnp.split(
        zc, np.cumsum([C_WIDTH] * 3 + [DECAY_LORA, AAA_LORA]).tolist(), axis=1)
    cols = [a_parts[0], qa] + a_parts[1:7] + [
        _pad_cols(a_parts[7], LANES), kva,
        jnp.concatenate([z64, kr, z32], axis=1),
        jnp.concatenate([z64, kr_rot, z32], axis=1),
        c_r, c_k, c_v, c_wd, c_ad, c_gd, jnp.zeros((D, Z_ZG - Z_CG - LANES), w.dtype), zg]
    out = jnp.concatenate(cols, axis=1)
    assert out.shape[1] == Z_COLS
    return out.astype(BF16)


def _prep_mla(w_uq, w_ukv):
    dq = NOPE_DIM + ROPE_DIM
    half = ROPE_DIM // 2
    wq = w_uq.reshape(Q_LORA, B_HEADS, dq)
    nope, r1, r2 = wq[..., :NOPE_DIM], wq[..., NOPE_DIM:NOPE_DIM + half], wq[..., NOPE_DIM + half:]
    zq = jnp.zeros((Q_LORA, B_HEADS, LANES - dq), w_uq.dtype)
    wqa = jnp.concatenate([nope, r1, r2, zq], axis=-1).reshape(Q_LORA, B_HEADS * LANES)
    wqb = jnp.concatenate([jnp.zeros_like(nope), -r2, r1, zq], axis=-1).reshape(Q_LORA, B_HEADS * LANES)
    wkv = w_ukv.reshape(KV_LORA, B_HEADS, NOPE_DIM + V_DIM)
    kn, vv = wkv[..., :NOPE_DIM], wkv[..., NOPE_DIM:]
    wk = jnp.concatenate([kn, jnp.zeros_like(kn)], axis=-1).reshape(KV_LORA, B_HEADS * LANES)
    wv = vv.reshape(KV_LORA, B_HEADS * V_DIM)
    return wqa.T.astype(BF16), wqb.T.astype(BF16), wk.astype(BF16), wv.T.astype(BF16)


def _rope_tables(S):
    half = ROPE_DIM // 2
    inv = ROPE_THETA ** (-jnp.arange(half, dtype=F32) / half)
    ang = jnp.arange(S, dtype=F32)[:, None] * inv
    cos2 = jnp.tile(jnp.cos(ang), (1, 2))
    sin2 = jnp.tile(jnp.sin(ang), (1, 2))
    scale = (NOPE_DIM + ROPE_DIM) ** -0.5 * LOG2E
    one = jnp.ones((S, NOPE_DIM), F32)
    z64 = jnp.zeros((S, NOPE_DIM), F32)
    z32 = jnp.zeros((S, LANES - NOPE_DIM - ROPE_DIM), F32)
    caq = (jnp.concatenate([one, cos2, z32], axis=1) * scale).T
    cbq = (jnp.concatenate([z64, sin2, z32], axis=1) * scale).T
    cak = jnp.concatenate([z64, cos2, z32], axis=1)
    cbk = jnp.concatenate([z64, sin2, z32], axis=1)
    return caq, cbq, cak, cbk


def _prep_compress(cmp_pos, cmp_w1, cmp_w2):
    Dh = A_HEAD_DIM
    w1 = cmp_w1.reshape(2, CMP_BLOCK, Dh, Dh)
    z = jnp.zeros_like(w1)
    w1bd = jnp.concatenate([jnp.concatenate([w1, z], axis=-1), jnp.concatenate([z, w1], axis=-1)], axis=-2)
    posrow = jnp.broadcast_to(cmp_pos.reshape(2, 1, CMP_BLOCK * Dh), (2, 8, CMP_BLOCK * Dh))
    w1cat = jnp.concatenate([cmp_w1, cmp_w1], axis=-1)
    zz = jnp.zeros((2, Dh, Dh), cmp_w2.dtype)
    blk = lambda a, b, c, d: jnp.concatenate(
        [jnp.concatenate([a, b], axis=-1), jnp.concatenate([c, d], axis=-1)], axis=-2)
    w2v = jnp.stack([jnp.stack([blk(cmp_w2, zz, zz, zz), blk(zz, cmp_w2, zz, zz)], axis=1),
                     jnp.stack([blk(zz, zz, cmp_w2, zz), blk(zz, zz, zz, cmp_w2)], axis=1)], axis=1)
    return w1bd.astype(BF16), posrow, w1cat.astype(BF16), w2v.astype(BF16)


def _gate_expand():
    e = np.zeros((3, LANES, A_Q), np.float32)
    for h in range(A_HEADS):
        for r in range(3):
            e[r, 3 * h + r, h * A_HEAD_DIM:(h + 1) * A_HEAD_DIM] = 1.0
    return jnp.asarray(e, dtype=BF16)


def _block_diag_ones(scale):
    idx = np.arange(C_WIDTH) // C_HEAD_DIM
    return jnp.asarray((idx[:, None] == idx[None, :]).astype(np.float32) * scale, dtype=BF16)


def kernel(x, c, rel_bias, ada_w, ada_b, norm_gain, w_in, nsa_cmp_pos, nsa_cmp_w1, nsa_cmp_w2, mla_q_norm, mla_kv_norm, mla_w_uq, mla_w_ukv, rwkv_mu, rwkv_w0, rwkv_a0, rwkv_k_k, rwkv_k_a, rwkv_w2, rwkv_a2, rwkv_g2, rwkv_r_k, rwkv_ln, w_branch, w_out, ffn_up, ffn_conv_w, ffn_conv_b, ffn_down):
    B, S, D = x.shape
    L = ada_w.shape[0]
    assert S % TQ == 0 and S % TM == 0 and S // SLC_BLOCK <= A_HEAD_DIM and S >= 2 * TQ
    mod = _adaln(c, ada_w, ada_b)
    cb, slc_tab, win_tab = _nsa_tables(rel_bias, S)
    caq, cbq, cak, cbk = _rope_tables(S)
    gexp = _gate_expand()
    bd1 = _block_diag_ones(1.0)
    bdm = _block_diag_ones(1.0 / C_HEAD_DIM)
    row = lambda v: v.reshape(1, -1)
    x2 = x.reshape(B * S, D)
    for l in range(L):
        m6 = mod[l].reshape(B, 6, 1, D)
        sh1, sc1, gt1, sh2, sc2, gt2 = (m6[:, i] for i in range(6))
        z = _norm_mod_matmul(x2, norm_gain[l, 0], sc1, sh1, _prep_w_in(w_in[l]), S, tn=1024)
        kcp, vcp = _compress(z, *_prep_compress(nsa_cmp_pos[l], nsa_cmp_w1[l], nsa_cmp_w2[l]), B, S)
        oc, sb = _cmpsel(z, kcp, vcp, cb, B, S)
        os_ = _nsa_flash(z, sb, slc_tab, Z_AKS, Z_AVS, B, S, selected=True)
        ow = _nsa_flash(z, None, win_tab, Z_AKW, Z_AVW, B, S, selected=False)
        wqa, wqb, wk, wv = _prep_mla(mla_w_uq[l], mla_w_ukv[l])
        q, k, v = _mla_proj(z, row(mla_q_norm[l]), row(mla_kv_norm[l]), wqa, wqb, wk, wv,
                            caq, cbq, cak, cbk, B, S)
        yb = _mla_flash(q, k, v, B, S)
        mu = rwkv_mu[l]
        o = 3 * C_WIDTH
        mus = [row(mu[:C_WIDTH]), row(mu[C_WIDTH:2 * C_WIDTH]), row(mu[2 * C_WIDTH:o]),
               row(mu[o:o + LANES]), row(mu[o + LANES:])]
        w2p = jnp.concatenate([rwkv_w2[l], jnp.zeros_like(rwkv_a2[l])], axis=0).astype(BF16)
        a2p = jnp.concatenate([jnp.zeros_like(rwkv_w2[l]), rwkv_a2[l]], axis=0).astype(BF16)
        rr, k2, vv, kk, be, ld, gg, bo = _rwkv_prep(
            z, mus, row(rwkv_w0[l]), row(rwkv_a0[l]), row(rwkv_k_k[l]), row(rwkv_k_a[l]),
            row(rwkv_r_k[l]), w2p, a2p, rwkv_g2[l].astype(BF16), bd1, B, S)
        qa_, bkt, yu, vp, pcb = _rwkv_chunk(rr, k2, vv, kk, be, ld, B, S)
        yr = _rwkv_state(qa_, bkt, yu, vp, pcb, B, S)
        x2 = _merge(oc, os_, ow, z, yb, yr, bo, gg, x2, w_branch[l].astype(BF16), w_out[l].astype(BF16),
                    gexp, bdm, rwkv_ln[l], norm_gain[l, 1], gt1, S)
        u = _norm_mod_matmul(x2, norm_gain[l, 2], sc2, sh2, ffn_up[l].astype(BF16), S, tn=2 * D_FF // 4)
        x2 = _ffn_down(u, ffn_conv_w[l], ffn_conv_b[l], ffn_down[l].astype(BF16), x2,
                       norm_gain[l, 3], gt2, S)
    return x2.reshape(B, S, D)
```

```python
import functools
import math

import jax
import jax.numpy as jnp
import numpy as np
from jax import lax
from jax.experimental import pallas as pl
from jax.experimental.pallas import tpu as pltpu

F32 = jnp.float32
BF16 = jnp.bfloat16

D_MODEL = 1024
DEPTH = 4
A_HEADS, A_KV_GROUPS, A_HEAD_DIM = 8, 2, 64
CMP_BLOCK, CMP_STRIDE = 32, 16
SLC_BLOCK, SLC_TOPK, N_LOCAL_BLOCKS = 64, 16, 2
WINDOW = 512
FORCE_SCORE = 1e9
B_HEADS, Q_LORA, KV_LORA, NOPE_DIM, ROPE_DIM, V_DIM = 8, 256, 128, 64, 32, 64
ROPE_THETA = 10000.0
C_HEADS, C_HEAD_DIM = 8, 64
C_WIDTH = C_HEADS * C_HEAD_DIM
DECAY_LORA, AAA_LORA, GATE_LORA = 64, 64, 128
GN_EPS = 64e-5
N_BUCKETS, MAX_DISTANCE = 32, 128
D_FF = 2816
EPS = 1e-6
NEG = -1e30
LOG2E = math.log2(math.e)

A_Q = A_HEADS * A_HEAD_DIM
A_KV = A_KV_GROUPS * A_HEAD_DIM
A_GATE = 3 * A_HEADS
A_COLS = A_Q + 6 * A_KV + A_GATE
B_COLS = Q_LORA + KV_LORA + ROPE_DIM
C_COLS = 3 * C_WIDTH + DECAY_LORA + AAA_LORA + GATE_LORA

LANES = 128
FFN_SUB = 256
VMEM_LIMIT = 48 * 1024 * 1024

Z_AQ = 0
Z_QA = 512
Z_AKC, Z_AVC, Z_AKS, Z_AVS, Z_AKW, Z_AVW = 768, 896, 1024, 1152, 1280, 1408
Z_AG = 1536
Z_KVA = 1664
Z_KRA = 1792
Z_KRB = 1920
Z_CR, Z_CK, Z_CV = 2048, 2560, 3072
Z_CWA = 3584
Z_CG = 3712
Z_ZG = 4096
Z_COLS = 7168

TQ = 256
TM = 512
CH = 64
RW_SUB = 2


def _cparams(*sem):
    return pltpu.CompilerParams(dimension_semantics=sem, vmem_limit_bytes=VMEM_LIMIT)


def _gelu(x):
    return 0.5 * x * (1.0 + jnp.tanh(0.7978845608028654 * (x + 0.044715 * (x * x * x))))


def _dot(a, b, **kw):
    return jnp.dot(a, b, preferred_element_type=F32, **kw)


def _split3(x):
    hi = x.astype(BF16)
    r = x - hi.astype(F32)
    mid = r.astype(BF16)
    return hi, mid, (r - mid.astype(F32)).astype(BF16)


def _dot3(x, w):
    return sum(_dot(t, w) for t in _split3(x))


def _dot2(x, w):
    hi, mid, _ = _split3(x)
    return _dot(hi, w) + _dot(mid, w)


def _dot_t(a, b):
    return lax.dot_general(a, b, (((1,), (1,)), ((), ())), preferred_element_type=F32)


def _adaln_kernel(c_ref, w_ref, b_ref, o_ref):
    c = c_ref[...]
    cond = c * jax.nn.sigmoid(c)
    o_ref[0] = _dot(cond.astype(BF16), w_ref[0].astype(BF16)) + b_ref[0]


def _adaln(c, ada_w, ada_b):
    L, D, N6 = ada_w.shape
    B = c.shape[0]
    tn = 1536
    return pl.pallas_call(
        _adaln_kernel,
        grid=(L, N6 // tn),
        in_specs=[pl.BlockSpec((B, D), lambda l, j: (0, 0)),
                  pl.BlockSpec((1, D, tn), lambda l, j: (l, 0, j)),
                  pl.BlockSpec((1, 1, tn), lambda l, j: (l, 0, j))],
        out_specs=pl.BlockSpec((1, B, tn), lambda l, j: (l, 0, j)),
        out_shape=jax.ShapeDtypeStruct((L, B, N6), F32),
        compiler_params=_cparams("parallel", "parallel"),
        name="adaln",
    )(c, ada_w, ada_b.reshape(L, 1, N6))


def _nmm_kernel(x_ref, g_ref, sc_ref, sh_ref, w_ref, o_ref, h_ref):
    @pl.when(pl.program_id(1) == 0)
    def _():
        x = x_ref[...]
        y = x * lax.rsqrt(jnp.mean(x * x, axis=-1, keepdims=True) + EPS)
        h = (y * g_ref[...]) * (1.0 + sc_ref[0]) + sh_ref[0]
        h_ref[...] = h.astype(BF16)

    o_ref[...] = _dot(h_ref[...], w_ref[...]).astype(o_ref.dtype)


def _norm_mod_matmul(x2, gain, sc, sh, w, S, tn, out_dtype=BF16):
    N, D = x2.shape
    NC = w.shape[1]
    tm = min(1024, S)
    tpb = S // tm
    return pl.pallas_call(
        _nmm_kernel,
        grid=(N // tm, NC // tn),
        in_specs=[pl.BlockSpec((tm, D), lambda i, j: (i, 0)),
                  pl.BlockSpec((1, D), lambda i, j: (0, 0)),
                  pl.BlockSpec((1, 1, D), lambda i, j: (i // tpb, 0, 0)),
                  pl.BlockSpec((1, 1, D), lambda i, j: (i // tpb, 0, 0)),
                  pl.BlockSpec((D, tn), lambda i, j: (0, j))],
        out_specs=pl.BlockSpec((tm, tn), lambda i, j: (i, j)),
        out_shape=jax.ShapeDtypeStruct((N, NC), out_dtype),
        scratch_shapes=[pltpu.VMEM((tm, D), BF16)],
        compiler_params=_cparams("parallel", "arbitrary"),
        name="norm_mod_matmul",
    )(x2, gain.reshape(1, D), sc, sh, w)


def _compress_kernel(zk_ref, zv_ref, w1_ref, pos_ref, w1c_ref, w2_ref, kc_ref, vc_ref, zf_ref, *, NC):
    for kind, (z_ref, o_ref) in enumerate(((zk_ref, kc_ref), (zv_ref, vc_ref))):
        pa = jnp.zeros((NC, LANES), F32)
        pb = jnp.zeros((NC, LANES), F32)
        zf_ref[...] = z_ref[...].astype(F32)
        for l in range(CMP_STRIDE):
            xl = zf_ref[pl.ds(l, NC, stride=CMP_STRIDE), :].astype(BF16)
            pa = pa + _dot(xl, w1_ref[kind, l])
            pb = pb + _dot(xl, w1_ref[kind, CMP_STRIDE + l])
        posb = _dot(pos_ref[kind].astype(BF16), w1c_ref[kind])[0:1, :]
        h = pa + pltpu.roll(pb, NC - 1, axis=0) + posb
        act = _gelu(h).astype(BF16)
        row = lax.broadcasted_iota(jnp.int32, (NC, LANES), 0)
        for g in range(A_KV_GROUPS):
            for eo in range(2):
                out = _dot(act, w2_ref[kind, g, eo])
                out = jnp.where(row < NC - 1, out, 0.0)
                o_ref[g, eo, pl.ds(0, NC), :] = jnp.zeros((NC, LANES), BF16)
                o_ref[g, eo, pl.ds(NC, NC), :] = out.astype(BF16)


def _compress(z, w1bd, posrow, w1cat, w2v, B, S):
    NC = S // CMP_STRIDE
    out = jax.ShapeDtypeStruct((B, A_KV_GROUPS, 2, 2 * NC, LANES), BF16)
    ospec = pl.BlockSpec((None, A_KV_GROUPS, 2, 2 * NC, LANES), lambda b: (b, 0, 0, 0, 0))
    full = lambda a: pl.BlockSpec(a.shape, lambda b: (0,) * a.ndim)
    return pl.pallas_call(
        functools.partial(_compress_kernel, NC=NC),
        grid=(B,),
        in_specs=[pl.BlockSpec((S, LANES), lambda b: (b, Z_AKC // LANES)),
                  pl.BlockSpec((S, LANES), lambda b: (b, Z_AVC // LANES)),
                  full(w1bd), full(posrow), full(w1cat), full(w2v)],
        out_specs=[ospec, ospec],
        out_shape=[out, out],
        scratch_shapes=[pltpu.VMEM((S, LANES), F32)],
        compiler_params=_cparams("parallel"),
        name="nsa_compress",
    )(z, z, w1bd, posrow, w1cat, w2v)


def _stack_pairs(zq, scale):
    q = zq.astype(F32) * scale
    return jnp.concatenate([q[:, :LANES], q[:, LANES:]], axis=0).astype(BF16)


def _cmpsel_kernel(zq_ref, kc_ref, vc_ref, cb_ref, oc_ref, sb_ref, *, NC, NS):
    qt = pl.program_id(2)
    r = TQ // CMP_STRIDE
    lhs = _stack_pairs(zq_ref[...], A_HEAD_DIM ** -0.5)
    st = pl.multiple_of(r * (qt + 1), 16)
    col = lax.broadcasted_iota(jnp.int32, (2 * TQ, NC), 1)
    exists = col >= NC - r * (qt + 1)
    acc = jnp.zeros((2 * TQ, LANES), F32)
    psum = jnp.zeros((TQ, NC), F32)
    for eo in range(2):
        kwin = kc_ref[eo, pl.ds(st, NC), :]
        vwin = vc_ref[eo, pl.ds(st, NC), :]
        logits = _dot_t(lhs, kwin) + cb_ref[eo]
        logits = jnp.where(exists, logits, NEG)
        m = jnp.max(logits, axis=-1, keepdims=True)
        p = jnp.where(logits > 0.1 * NEG, jnp.exp(logits - m), 0.0)
        l = jnp.maximum(jnp.sum(p, axis=-1, keepdims=True), 1e-30)
        p = p * (1.0 / l)
        acc = acc + _dot(p.astype(BF16), vwin)
        psum = psum + p[:TQ] + p[TQ:]
    oc_ref[:, :LANES] = acc[:TQ]
    oc_ref[:, LANES:] = acc[TQ:]

    jj = lax.broadcasted_iota(jnp.int32, (LANES, NC), 0)
    nn = lax.broadcasted_iota(jnp.int32, (LANES, NC), 1) + (r * (qt + 1) - NC)
    delta = 4 * jj - nn
    mt = jnp.where((delta == 0) | (delta == 4), 1.0, 0.0) + jnp.where((delta >= 1) & (delta <= 3), 2.0, 0.0)
    mt = jnp.where(jj < NS, mt, 0.0)
    mtb = mt.astype(BF16)
    imp_t = sum(_dot_t(mtb, t) for t in _split3(psum))
    jb = lax.broadcasted_iota(jnp.int32, (LANES, TQ), 0)
    tpos = qt * TQ + lax.broadcasted_iota(jnp.int32, (LANES, TQ), 1)
    back = (tpos >> 6) - jb
    forced = (jb == 0) | ((back >= 0) & (back < N_LOCAL_BLOCKS))
    score = jnp.where(forced, FORCE_SCORE, jnp.where(back >= 0, imp_t, -1.0))
    score = jnp.where(jb < NS, score, -2.0)
    rank = jnp.zeros((LANES, TQ), F32)
    for jp in range(NS):
        row = score[jp:jp + 1, :]
        beats = (row > score) | ((row == score) & (jb > jp))
        rank = rank + jnp.where(beats, 1.0, 0.0)
    sel = (rank < float(min(SLC_TOPK, NS))) & (back >= 0) & (jb < NS)
    sb_ref[...] = jnp.where(sel, 0.0, jnp.where(jb < NS, NEG, 0.0)).astype(BF16)


def _cmpsel(z, kcp, vcp, cb, B, S):
    NC = S // CMP_STRIDE
    NS = S // SLC_BLOCK
    nq = S // TQ
    kspec = pl.BlockSpec((None, None, 2, 2 * NC, LANES), lambda b, g, q: (b, g, 0, 0, 0))
    return pl.pallas_call(
        functools.partial(_cmpsel_kernel, NC=NC, NS=NS),
        grid=(B, A_KV_GROUPS, nq),
        in_specs=[pl.BlockSpec((TQ, 2 * LANES), lambda b, g, q: (b * nq + q, g)),
                  kspec, kspec,
                  pl.BlockSpec((None, 2, 2 * TQ, NC), lambda b, g, q: (g, 0, 0, 0))],
        out_specs=[pl.BlockSpec((TQ, 2 * LANES), lambda b, g, q: (b * nq + q, g)),
                   pl.BlockSpec((None, None, LANES, TQ), lambda b, g, q: (b, g, 0, q))],
        out_shape=[jax.ShapeDtypeStruct((B * S, A_Q), F32),
                   jax.ShapeDtypeStruct((B, A_KV_GROUPS, LANES, S), BF16)],
        compiler_params=_cparams("parallel", "parallel", "parallel"),
        name="nsa_cmp_select",
    )(z, kcp, vcp, cb)


def _flash_update_t(s, vt, state, acc_ref):
    m, l = state
    m_new = jnp.maximum(m, jnp.max(s, axis=0, keepdims=True))
    p = jnp.exp2(s - m_new)
    alpha = jnp.exp2(m - m_new)
    acc_ref[...] = alpha * acc_ref[...] + _dot(vt, p.astype(BF16))
    return m_new, alpha * l + jnp.sum(p, axis=0, keepdims=True)


def _flash_pipeline(qk, softmax, lo, hi, state, s_ref):
    s_ref[0] = qk(0, lo)

    def body(kc, st):
        s_ref[1] = qk(1, kc)
        st0 = softmax(0, s_ref[0], kc, st[0], False)
        s_ref[0] = qk(0, kc + 1)
        st1 = softmax(1, s_ref[1], kc, st[1], False)
        return st0, st1

    st = lax.fori_loop(lo, hi, body, state)
    s_ref[1] = qk(1, hi)
    st0 = softmax(0, s_ref[0], hi, st[0], True)
    st1 = softmax(1, s_ref[1], hi, st[1], True)
    return st0, st1


def _split_heads_kv(slab, g):
    lane = lax.broadcasted_iota(jnp.int32, slab.shape, 1)
    own = jnp.where((lane < A_HEAD_DIM) == (g == 0), slab, 0.0)
    other = pltpu.roll(own, A_HEAD_DIM, axis=1)
    is0 = g == 0
    return jnp.where(is0, own, other), jnp.where(is0, other, own)


def _nsa_flash_kernel(*refs, S, selected):
    if selected:
        zq_ref, sbt_ref, zk_ref, zv_ref, tab_ref, o_ref, ke_ref, ko_ref, vt_ref, acc_ref, s_ref = refs
    else:
        zq_ref, zk_ref, zv_ref, tab_ref, o_ref, ke_ref, ko_ref, vt_ref, acc_ref, s_ref = refs
    g = pl.program_id(1)
    qt = pl.program_id(2)
    k_refs = (ke_ref, ko_ref)

    @pl.when(qt == 0)
    def _():
        ke, ko = _split_heads_kv(zk_ref[...].astype(F32), g)
        ke_ref[:, :LANES] = ke.astype(BF16)
        ko_ref[:, :LANES] = ko.astype(BF16)
        for c in range(S // TQ):
            vt_ref[c] = zv_ref[c * TQ:(c + 1) * TQ, :].astype(F32).T.astype(BF16)
        if selected:
            rowb = lax.broadcasted_iota(jnp.int32, (S, LANES), 0) >> 6
            lane = lax.broadcasted_iota(jnp.int32, (S, LANES), 1)
            onehot = jnp.where(rowb == lane, 1.0, 0.0).astype(BF16)
            ke_ref[:, LANES:] = onehot
            ko_ref[:, LANES:] = onehot

    zq = zq_ref[...].astype(F32) * (A_HEAD_DIM ** -0.5 * LOG2E)
    rhs = jnp.concatenate([zq[:, :LANES], zq[:, LANES:]], axis=0).T.astype(BF16)
    if selected:
        sbt = sbt_ref[...]
        rhs = jnp.concatenate([rhs, jnp.concatenate([sbt, sbt], axis=1)], axis=0)
    acc_ref[...] = jnp.zeros(acc_ref.shape, F32)
    vrow = pl.multiple_of(g * A_HEAD_DIM, A_HEAD_DIM)

    def qk(eo, kc):
        ks = pl.multiple_of(kc * TQ, TQ)
        return _dot(k_refs[eo][pl.ds(ks, TQ), :], rhs)

    def softmax(eo, s, kc, state, diag):
        ti = 0 if diag else jnp.minimum(qt - kc, 2)
        vt = vt_ref[kc, pl.ds(vrow, A_HEAD_DIM), :]
        return _flash_update_t(s + tab_ref[eo, ti], vt, state, acc_ref.at[eo])

    init = (jnp.full((1, 2 * TQ), 0.5 * NEG, F32), jnp.zeros((1, 2 * TQ), F32))
    lo = 0 if selected else jnp.maximum(qt - 2, 0)
    carry = _flash_pipeline(qk, softmax, lo, qt, (init, init), s_ref)
    out_t = jnp.concatenate([acc_ref[0] * (1.0 / carry[0][1]), acc_ref[1] * (1.0 / carry[1][1])], axis=0)
    out = out_t.T
    o_ref[:, :LANES] = out[:TQ]
    o_ref[:, LANES:] = out[TQ:]


def _nsa_flash(z, sb, tab, zk_col, zv_col, B, S, selected):
    nq = S // TQ
    kw = 2 * LANES if selected else LANES
    in_specs = [pl.BlockSpec((TQ, 2 * LANES), lambda b, g, q: (b * nq + q, g))]
    args = [z]
    if selected:
        in_specs.append(pl.BlockSpec((None, None, LANES, TQ), lambda b, g, q: (b, g, 0, q)))
        args.append(sb)
    in_specs += [pl.BlockSpec((S, LANES), lambda b, g, q: (b, zk_col // LANES)),
                 pl.BlockSpec((S, LANES), lambda b, g, q: (b, zv_col // LANES)),
                 pl.BlockSpec((None, 2, 3, TQ, 2 * TQ), lambda b, g, q: (g, 0, 0, 0, 0))]
    args += [z, z, tab]
    return pl.pallas_call(
        functools.partial(_nsa_flash_kernel, S=S, selected=selected),
        grid=(B, A_KV_GROUPS, nq),
        in_specs=in_specs,
        out_specs=pl.BlockSpec((TQ, 2 * LANES), lambda b, g, q: (b * nq + q, g)),
        out_shape=jax.ShapeDtypeStruct((B * S, A_Q), F32),
        scratch_shapes=[pltpu.VMEM((S, kw), BF16), pltpu.VMEM((S, kw), BF16),
                        pltpu.VMEM((S // TQ, LANES, TQ), BF16),
                        pltpu.VMEM((2, A_HEAD_DIM, 2 * TQ), F32),
                        pltpu.VMEM((2, TQ, 2 * TQ), F32)],
        compiler_params=_cparams("parallel", "parallel", "arbitrary"),
        name="nsa_selected" if selected else "nsa_window",
    )(*args)


def _mla_proj_kernel(zqa_ref, zkva_ref, zkra_ref, zkrb_ref, qn_ref, kvn_ref, wqa_ref, wqb_ref,
                     wk_ref, wv_ref, caq_ref, cbq_ref, cak_ref, cbk_ref, q_ref, k_ref, v_ref):
    def rms(x, gain):
        return x * lax.rsqrt(jnp.mean(x * x, axis=-1, keepdims=True) + EPS) * gain

    nq = rms(zqa_ref[...].astype(F32), qn_ref[...]).astype(BF16)
    nkv = rms(zkva_ref[...].astype(F32), kvn_ref[...]).astype(BF16)
    qa = _dot_t(wqa_ref[...], nq)
    qb = _dot_t(wqb_ref[...], nq)
    kn = _dot(nkv, wk_ref[...])
    vt = _dot_t(wv_ref[...], nkv)
    caq, cbq = caq_ref[...], cbq_ref[...]
    kr = zkra_ref[...].astype(F32) * cak_ref[...] + zkrb_ref[...].astype(F32) * cbk_ref[...]
    for h in range(B_HEADS):
        sl = slice(h * LANES, (h + 1) * LANES)
        q_ref[h] = (qa[sl, :] * caq + qb[sl, :] * cbq).astype(BF16)
        k_ref[h] = (kn[:, sl] + kr).astype(BF16)
        v_ref[h] = vt[h * V_DIM:(h + 1) * V_DIM, :].astype(BF16)


def _mla_proj(z, qn, kvn, wqa, wqb, wk, wv, caq, cbq, cak, cbk, B, S):
    tm = TM
    nt = S // tm
    zspec = lambda w, col: pl.BlockSpec((tm, w), lambda b, i: (b * nt + i, col // w))
    full = lambda a: pl.BlockSpec(a.shape, lambda b, i: (0,) * a.ndim)
    tspec = pl.BlockSpec((tm, LANES), lambda b, i: (i, 0))
    tspec_t = pl.BlockSpec((LANES, tm), lambda b, i: (0, i))
    return pl.pallas_call(
        _mla_proj_kernel,
        grid=(B, nt),
        in_specs=[zspec(Q_LORA, Z_QA), zspec(LANES, Z_KVA), zspec(LANES, Z_KRA), zspec(LANES, Z_KRB),
                  full(qn), full(kvn), full(wqa), full(wqb), full(wk), full(wv),
                  tspec_t, tspec_t, tspec, tspec],
        out_specs=[pl.BlockSpec((None, B_HEADS, LANES, tm), lambda b, i: (b, 0, 0, i)),
                   pl.BlockSpec((None, B_HEADS, tm, LANES), lambda b, i: (b, 0, i, 0)),
                   pl.BlockSpec((None, B_HEADS, None, V_DIM, tm), lambda b, i: (b, 0, i, 0, 0))],
        out_shape=[jax.ShapeDtypeStruct((B, B_HEADS, LANES, S), BF16),
                   jax.ShapeDtypeStruct((B, B_HEADS, S, LANES), BF16),
                   jax.ShapeDtypeStruct((B, B_HEADS, nt, V_DIM, tm), BF16)],
        compiler_params=_cparams("parallel", "parallel"),
        name="mla_proj",
    )(z, z, z, z, qn, kvn, wqa, wqb, wk, wv, caq, cbq, cak, cbk)


def _mla_flash_kernel(qt_ref, k_ref, vt_ref, o_ref, acc_ref, s_ref):
    qi = pl.program_id(2)
    acc_ref[...] = jnp.zeros(acc_ref.shape, F32)
    ri = lax.broadcasted_iota(jnp.int32, (TM, TM), 0)
    ci = lax.broadcasted_iota(jnp.int32, (TM, TM), 1)
    causal = ri <= ci

    def qk(hh, kc):
        ks = pl.multiple_of(kc * TM, TM)
        return _dot(k_ref[hh, pl.ds(ks, TM), :], qt_ref[hh])

    def softmax(hh, s, kc, state, diag):
        if diag:
            s = jnp.where(causal, s, NEG)
        return _flash_update_t(s, vt_ref[hh, kc], state, acc_ref.at[hh])

    init = (jnp.full((1, TM), 0.5 * NEG, F32), jnp.zeros((1, TM), F32))
    carry = _flash_pipeline(qk, softmax, 0, qi, (init, init), s_ref)
    out_t = jnp.concatenate([acc_ref[0] * (1.0 / carry[0][1]), acc_ref[1] * (1.0 / carry[1][1])], axis=0)
    o_ref[...] = out_t.T


def _mla_flash(qt, k, vt, B, S):
    nq = S // TM
    hp = B_HEADS // 2
    return pl.pallas_call(
        _mla_flash_kernel,
        grid=(B, hp, nq),
        in_specs=[pl.BlockSpec((None, 2, LANES, TM), lambda b, h, i: (b, h, 0, i)),
                  pl.BlockSpec((None, 2, S, LANES), lambda b, h, i: (b, h, 0, 0)),
                  pl.BlockSpec((None, 2, nq, V_DIM, TM), lambda b, h, i: (b, h, 0, 0, 0))],
        out_specs=pl.BlockSpec((TM, LANES), lambda b, h, i: (b * nq + i, h)),
        out_shape=jax.ShapeDtypeStruct((B * S, B_HEADS * V_DIM), F32),
        scratch_shapes=[pltpu.VMEM((2, V_DIM, TM), F32), pltpu.VMEM((2, TM, TM), F32)],
        compiler_params=_cparams("parallel", "parallel", "arbitrary"),
        name="mla_flash",
    )(qt, k, vt)


def _rwkv_prep_kernel(zr_ref, zk_ref, zv_ref, zwa_ref, zg_ref, mu_r, mu_k, mu_v, mu_wa, mu_g,
                      w0_ref, a0_ref, kk_ref, ka_ref, rk_ref, w2_ref, a2_ref, g2_ref, bd_ref,
                      r_o, k_o, v_o, kk_o, be_o, ld_o, g_o, bo_o,
                      c_r, c_k, c_v, c_wa, c_g):
    t = pl.program_id(1)
    tm = zr_ref.shape[0]

    def shifted(z_ref, mu_ref, c_ref):
        x = z_ref[...].astype(F32)
        row = lax.broadcasted_iota(jnp.int32, x.shape, 0)
        prev = jnp.where(t == 0, 0.0, c_ref[0:1, :])
        xs = jnp.where(row == 0, prev, pltpu.roll(x, 1, axis=0))
        c_ref[0:1, :] = x[tm - 1:tm, :]
        return x + (xs - x) * mu_ref[...]

    r = shifted(zr_ref, mu_r, c_r)
    k = shifted(zk_ref, mu_k, c_k)
    v = shifted(zv_ref, mu_v, c_v)
    wa = shifted(zwa_ref, mu_wa, c_wa)
    gd = shifted(zg_ref, mu_g, c_g)
    w = w0_ref[...] + _dot(jnp.tanh(wa).astype(BF16), w2_ref[...])
    ld_o[...] = -jax.nn.sigmoid(w) * math.exp(-0.5)
    a = jax.nn.sigmoid(a0_ref[...] + _dot(wa.astype(BF16), a2_ref[...]))
    g_o[...] = _dot(jax.nn.sigmoid(gd).astype(BF16), g2_ref[...])
    kk = k * kk_ref[...]
    nsq = _dot3(kk * kk, bd_ref[...])
    kk = kk / jnp.maximum(jnp.sqrt(nsq), 1e-12)
    k2 = k * (1.0 + (a - 1.0) * ka_ref[...])
    rks = _dot3(r * k2 * rk_ref[...], bd_ref[...])
    r_o[...] = r
    k_o[...] = k2
    v_o[...] = v
    kk_o[...] = kk
    be_o[...] = kk * a
    bo_o[...] = rks * v


def _rwkv_prep(z, mus, w0, a0, k_k, k_a, r_k, w2p, a2p, g2, bd, B, S, tm=256):
    nt = S // tm
    W = C_WIDTH
    zspec = lambda w, col: pl.BlockSpec((tm, w), lambda b, i: (b * nt + i, col // w))
    full = lambda a: pl.BlockSpec(a.shape, lambda b, i: (0,) * a.ndim)
    ospec = pl.BlockSpec((tm, W), lambda b, i: (b * nt + i, 0))
    oshape = jax.ShapeDtypeStruct((B * S, W), F32)
    consts = list(mus) + [w0, a0, k_k, k_a, r_k, w2p, a2p, g2, bd]
    return pl.pallas_call(
        _rwkv_prep_kernel,
        grid=(B, nt),
        in_specs=[zspec(W, Z_CR), zspec(W, Z_CK), zspec(W, Z_CV), zspec(LANES, Z_CWA), zspec(LANES, Z_CG)]
                 + [full(a) for a in consts],
        out_specs=[ospec] * 8,
        out_shape=[oshape] * 8,
        scratch_shapes=[pltpu.VMEM((8, W), F32)] * 3 + [pltpu.VMEM((8, LANES), F32)] * 2,
        compiler_params=_cparams("parallel", "arbitrary"),
        name="rwkv_prep",
    )(z, z, z, z, z, *consts)


def _rwkv_chunk_kernel(r_ref, k_ref, v_ref, kk_ref, be_ref, ld_ref,
                       qa_o, bkt_o, yu_o, vp_o, pcb_o):
    C = CH
    R = ld_ref.shape[0]
    ld = ld_ref[...]
    rr = lax.broadcasted_iota(jnp.int32, (R, R), 0)
    cc = lax.broadcasted_iota(jnp.int32, (R, R), 1)
    sh = int(math.log2(C))
    tri = jnp.where((cc <= rr) & ((cc >> sh) == (rr >> sh)), 1.0, 0.0).astype(BF16)
    cs = sum(_dot(tri, t) for t in _split3(ld))
    cl = jnp.concatenate([jnp.broadcast_to(cs[(s + 1) * C - 1:(s + 1) * C, :], (C, C_WIDTH))
                          for s in range(R // C)], axis=0)
    ex, exn, exx, exc = jnp.exp(cs), jnp.exp(-cs), jnp.exp(cs - ld), jnp.exp(cl - cs)
    kk, be, k2, v = kk_ref[...], be_ref[...], k_ref[...], v_ref[...]
    at = -kk * exx
    rt = r_ref[...] * ex
    bt = be * exn
    kt = k2 * exn
    btc = be * exc
    ktc = k2 * exc
    pc = jnp.exp(cl)
    ri = lax.broadcasted_iota(jnp.int32, (C, C), 0)
    ci = lax.broadcasted_iota(jnp.int32, (C, C), 1)
    incl = ci <= ri
    strict = ci < ri
    eye = jnp.where(ci == ri, 1.0, 0.0)
    lane = lax.broadcasted_iota(jnp.int32, (C, LANES), 1)
    P = C_HEADS // 2
    pairs = [(s, p) for s in range(R // C) for p in range(P)]
    heads = [(i, hh) for i in range(len(pairs)) for hh in range(2)]
    blk = lambda x, i: x[pairs[i][0] * C:(pairs[i][0] + 1) * C, pairs[i][1] * LANES:(pairs[i][1] + 1) * LANES]
    lhs2 = [jnp.concatenate([blk(at, i), blk(rt, i)], axis=0).astype(BF16) for i in range(len(pairs))]
    bm, km, vm, am = [], [], [], []
    for i, hh in heads:
        msk = (lane < C_HEAD_DIM) if hh == 0 else (lane >= C_HEAD_DIM)
        bm.append(jnp.where(msk, blk(bt, i), 0.0).astype(BF16))
        km.append(jnp.where(msk, blk(kt, i), 0.0).astype(BF16))
        vm.append(jnp.where(msk, blk(v, i), 0.0).astype(BF16))
        am.append(jnp.where(msk, blk(at, i), 0.0).astype(BF16))
    g1 = [_dot_t(lhs2[i], bm[j]) for j, (i, hh) in enumerate(heads)]
    g2 = [_dot_t(lhs2[i], km[j]) for j, (i, hh) in enumerate(heads)]
    lab = [jnp.where(strict, g[:C], 0.0) for g in g1]
    mrb = [jnp.where(incl, g[C:], 0.0).astype(BF16) for g in g1]
    lak = [jnp.where(strict, g[:C], 0.0).astype(BF16) for g in g2]
    mrk = [jnp.where(incl, g[C:], 0.0).astype(BF16) for g in g2]
    w2 = [_dot(a, b) for a, b in zip(lak, vm)]
    yk = [_dot(a, b) for a, b in zip(mrk, vm)]
    tinv = [eye + x for x in lab]
    lp = lab
    for _ in range(int(math.log2(C)) - 1):
        lpb = [x.astype(BF16) for x in lp]
        lp = [_dot(x, x) for x in lpb]
        tinv = [t + _dot(t.astype(BF16), x.astype(BF16)) for t, x in zip(tinv, lp)]
    au = [_dot(t.astype(BF16), jnp.concatenate([a, w.astype(BF16)], axis=1)) for t, a, w in zip(tinv, am, w2)]
    qy = [_dot(m, x.astype(BF16)) for m, x in zip(mrb, au)]
    for i, (s, p) in enumerate(pairs):
        e, o = 2 * i, 2 * i + 1
        ahat = au[e][:, :LANES] + au[o][:, :LANES]
        uhat = au[e][:, LANES:] + au[o][:, LANES:]
        qhat = blk(rt, i) + qy[e][:, :LANES] + qy[o][:, :LANES]
        yi = qy[e][:, LANES:] + qy[o][:, LANES:] + yk[e] + yk[o]
        qa_o[s, p] = jnp.concatenate([qhat, ahat], axis=0).astype(BF16)
        yu_o[s, p] = jnp.concatenate([yi, uhat], axis=0)
        bkt_o[s, p] = jnp.concatenate([blk(btc, i), blk(ktc, i)], axis=0).T.astype(BF16)
        vp_o[s, p] = blk(v, i).astype(BF16)
        pcb_o[s, p] = jnp.concatenate([blk(pc, i), blk(pc, i)], axis=0).T


def _rwkv_chunk(r, k2, v, kk, be, ld, B, S):
    nch = S // CH
    W = C_WIDTH
    P = C_HEADS // 2
    ns = RW_SUB
    ispec = pl.BlockSpec((ns * CH, W), lambda b, c: (b * (nch // ns) + c, 0))
    sq = lambda rows: pl.BlockSpec((None, ns, P, rows, LANES), lambda b, c: (b, c, 0, 0, 0))
    shp = lambda rows, dt: jax.ShapeDtypeStruct((B, nch, P, rows, LANES), dt)
    return pl.pallas_call(
        _rwkv_chunk_kernel,
        grid=(B, nch // ns),
        in_specs=[ispec] * 6,
        out_specs=[sq(2 * CH), sq(LANES), sq(2 * CH), sq(CH), sq(LANES)],
        out_shape=[shp(2 * CH, BF16), shp(LANES, BF16), shp(2 * CH, F32), shp(CH, BF16), shp(LANES, F32)],
        compiler_params=_cparams("parallel", "parallel"),
        name="rwkv_chunk",
    )(r, k2, v, kk, be, ld)


def _rwkv_state_kernel(qa_ref, bkt_ref, yu_ref, vp_ref, pcb_ref, y_ref, ap_ref):
    @pl.when(pl.program_id(1) == 0)
    def _():
        ap_ref[...] = jnp.zeros(ap_ref.shape, F32)

    ri = lax.broadcasted_iota(jnp.int32, (LANES, LANES), 0)
    ci = lax.broadcasted_iota(jnp.int32, (LANES, LANES), 1)
    same_head = (ri < C_HEAD_DIM) == (ci < C_HEAD_DIM)
    idx = [(b, p) for b in range(qa_ref.shape[0]) for p in range(C_HEADS // 2)]
    a = [ap_ref[b, p] for b, p in idx]
    for c in range(qa_ref.shape[1]):
        x = [_dot(qa_ref[b, c, p], s.astype(BF16)) for (b, p), s in zip(idx, a)]
        yu = [yu_ref[b, c, p] for b, p in idx]
        for (b, p), xi, yi in zip(idx, x, yu):
            y_ref[b, c * CH:(c + 1) * CH, p * LANES:(p + 1) * LANES] = xi[:CH] + yi[:CH]
        uv = [jnp.concatenate([(xi[CH:] + yi[CH:]).astype(BF16), vp_ref[b, c, p]], axis=0)
              for (b, p), xi, yi in zip(idx, x, yu)]
        upd = [_dot(bkt_ref[b, c, p], t) for (b, p), t in zip(idx, uv)]
        a = [pcb_ref[b, c, p] * s + jnp.where(same_head, t, 0.0) for (b, p), s, t in zip(idx, a, upd)]
    for (b, p), s in zip(idx, a):
        ap_ref[b, p] = s


def _rwkv_state(qa, bkt, yu, vp, pcb, B, S):
    nch = S // CH
    P = C_HEADS // 2
    nb = 2 if B % 2 == 0 else 1
    nc = 4 if nch % 4 == 0 else 1
    sq = lambda rows: pl.BlockSpec((nb, nc, P, rows, LANES), lambda b, c: (b, c, 0, 0, 0))
    return pl.pallas_call(
        _rwkv_state_kernel,
        grid=(B // nb, nch // nc),
        in_specs=[sq(2 * CH), sq(LANES), sq(2 * CH), sq(CH), sq(LANES)],
        out_specs=pl.BlockSpec((nb, nc * CH, C_WIDTH), lambda b, c: (b, c, 0)),
        out_shape=jax.ShapeDtypeStruct((B, S, C_WIDTH), F32),
        scratch_shapes=[pltpu.VMEM((nb, P, LANES, LANES), F32)],
        compiler_params=_cparams("parallel", "arbitrary"),
        name="rwkv_state",
    )(qa, bkt, yu, vp, pcb).reshape(B * S, C_WIDTH)


def _merge_kernel(oc_ref, os_ref, ow_ref, zag_ref, yb_ref, yr_ref, bo_ref, gg_ref,
                  zga_ref, zgb_ref, zgc_ref, x_ref, wbr_ref, wout_ref, ge_ref, bdm_ref,
                  ln_ref, gain_ref, gt_ref, o_ref):
    sg = jax.nn.sigmoid(zag_ref[...].astype(F32))
    ya = (_dot2(sg, ge_ref[0]) * oc_ref[...] + _dot2(sg, ge_ref[1]) * os_ref[...]
          + _dot2(sg, ge_ref[2]) * ow_ref[...])
    yr = yr_ref[...]
    mean = _dot3(yr, bdm_ref[...])
    d = yr - mean
    var = _dot2(d * d, bdm_ref[...])
    yn = d * lax.rsqrt(var + GN_EPS) * ln_ref[0:1, :] + ln_ref[1:2, :]
    yc = (yn + bo_ref[...]) * gg_ref[...]
    sig = lambda ref: jax.nn.sigmoid(ref[...].astype(F32))
    merged = (sig(zga_ref) * _dot(ya.astype(BF16), wbr_ref[0:A_Q, :])
              + sig(zgb_ref) * _dot(yb_ref[...].astype(BF16), wbr_ref[A_Q:2 * A_Q, :])
              + sig(zgc_ref) * _dot(yc.astype(BF16), wbr_ref[2 * A_Q:3 * A_Q, :]))
    y = _dot(merged.astype(BF16), wout_ref[...])
    yn2 = y * lax.rsqrt(jnp.mean(y * y, axis=-1, keepdims=True) + EPS) * gain_ref[...]
    o_ref[...] = x_ref[...] + gt_ref[0] * yn2


def _merge(oc, os_, ow, z, yb, yr, bo, gg, x2, wbr, wout, gexp, bdm, ln, gain, gt, S, tm=256):
    N, D = x2.shape
    tpb = S // tm
    W = A_Q
    row = lambda w: pl.BlockSpec((tm, w), lambda i: (i, 0))
    zspec = lambda w, col: pl.BlockSpec((tm, w), lambda i: (i, col // w))
    full = lambda a: pl.BlockSpec(a.shape, lambda i: (0,) * a.ndim)
    return pl.pallas_call(
        _merge_kernel,
        grid=(N // tm,),
        in_specs=[row(W), row(W), row(W), zspec(LANES, Z_AG), row(W), row(W), row(W), row(W),
                  zspec(D, Z_ZG), zspec(D, Z_ZG + D), zspec(D, Z_ZG + 2 * D), row(D),
                  full(wbr), full(wout), full(gexp), full(bdm), full(ln),
                  pl.BlockSpec((1, D), lambda i: (0, 0)),
                  pl.BlockSpec((1, 1, D), lambda i: (i // tpb, 0, 0))],
        out_specs=row(D),
        out_shape=jax.ShapeDtypeStruct((N, D), F32),
        compiler_params=_cparams("parallel"),
        name="merge_out",
    )(oc, os_, ow, z, yb, yr, bo, gg, z, z, z, x2, wbr, wout, gexp, bdm, ln, gain.reshape(1, D), gt)


def _ffn_up_kernel(x_ref, g_ref, sc_ref, sh_ref, wg_ref, wv_ref, cw_ref, cb_ref, o_ref,
                   cg_ref, cv_ref, *, tpb):
    i = pl.program_id(0)
    tm = x_ref.shape[0]
    fc = FFN_SUB
    first = (i % tpb) == 0
    x = x_ref[...]
    y = x * lax.rsqrt(jnp.mean(x * x, axis=-1, keepdims=True) + EPS)
    h = ((y * g_ref[...]) * (1.0 + sc_ref[0]) + sh_ref[0]).astype(BF16)
    row = lax.broadcasted_iota(jnp.int32, (8, fc), 0)

    def up(c):
        sl = slice(c * fc, (c + 1) * fc)
        return _dot(h, wg_ref[:, sl]), _dot(h, wv_ref[:, sl])

    def conv(u, c_ref, c, off):
        sl = slice(c * fc, (c + 1) * fc)
        wl = slice(off + c * fc, off + (c + 1) * fc)
        prev = c_ref[:, sl]
        p1 = jnp.where(first, 0.0, prev[7:8])
        p2 = jnp.where(first, 0.0, prev[6:7])
        c_ref[:, sl] = u[tm - 8:tm, :]
        r1 = pltpu.roll(u, 1, axis=0)
        r2 = pltpu.roll(u, 2, axis=0)
        t1 = jnp.where(row == 0, p1, r1[0:8])
        t2 = jnp.where(row == 0, p2, jnp.where(row == 1, p1, r2[0:8]))
        u1 = jnp.concatenate([t1, r1[8:]], axis=0)
        u2 = jnp.concatenate([t2, r2[8:]], axis=0)
        return cw_ref[0:1, wl] * u2 + cw_ref[1:2, wl] * u1 + cw_ref[2:3, wl] * u + cb_ref[:, wl]

    nsub = D_FF // fc
    cur = up(0)
    for c in range(nsub):
        nxt = up(c + 1) if c + 1 < nsub else None
        a = _gelu(conv(cur[0], cg_ref, c, 0)) * conv(cur[1], cv_ref, c, D_FF)
        o_ref[:, c * fc:(c + 1) * fc] = a.astype(BF16)
        cur = nxt


def _ffn_up(x2, gain, sc, sh, w_up, conv_w, conv_b, S, tm=256):
    N, D = x2.shape
    tpb = S // tm
    full = lambda a: pl.BlockSpec(a.shape, lambda i: (0,) * a.ndim)
    return pl.pallas_call(
        functools.partial(_ffn_up_kernel, tpb=tpb),
        grid=(N // tm,),
        in_specs=[pl.BlockSpec((tm, D), lambda i: (i, 0)),
                  pl.BlockSpec((1, D), lambda i: (0, 0)),
                  pl.BlockSpec((1, 1, D), lambda i: (i // tpb, 0, 0)),
                  pl.BlockSpec((1, 1, D), lambda i: (i // tpb, 0, 0)),
                  pl.BlockSpec((D, D_FF), lambda i: (0, 0)),
                  pl.BlockSpec((D, D_FF), lambda i: (0, 1)),
                  full(conv_w), pl.BlockSpec((1, 2 * D_FF), lambda i: (0, 0))],
        out_specs=pl.BlockSpec((tm, D_FF), lambda i: (i, 0)),
        out_shape=jax.ShapeDtypeStruct((N, D_FF), BF16),
        scratch_shapes=[pltpu.VMEM((8, D_FF), F32), pltpu.VMEM((8, D_FF), F32)],
        compiler_params=_cparams("arbitrary"),
        name="ffn_up",
    )(x2, gain.reshape(1, D), sc, sh, w_up, w_up, conv_w, conv_b.reshape(1, 2 * D_FF))


def _ffn_down_kernel(a_ref, wd_ref, x_ref, gain_ref, gt_ref, o_ref):
    f = _dot(a_ref[...], wd_ref[...])
    fn = f * lax.rsqrt(jnp.mean(f * f, axis=-1, keepdims=True) + EPS) * gain_ref[...]
    o_ref[...] = x_ref[...] + gt_ref[0] * fn


def _ffn_down(a, wd, x2, gain, gt, S, tm=512):
    N, D = x2.shape
    tpb = S // tm
    return pl.pallas_call(
        _ffn_down_kernel,
        grid=(N // tm,),
        in_specs=[pl.BlockSpec((tm, D_FF), lambda i: (i, 0)),
                  pl.BlockSpec((D_FF, D), lambda i: (0, 0)),
                  pl.BlockSpec((tm, D), lambda i: (i, 0)),
                  pl.BlockSpec((1, D), lambda i: (0, 0)),
                  pl.BlockSpec((1, 1, D), lambda i: (i // tpb, 0, 0))],
        out_specs=pl.BlockSpec((tm, D), lambda i: (i, 0)),
        out_shape=jax.ShapeDtypeStruct((N, D), F32),
        compiler_params=_cparams("parallel"),
        name="ffn_down",
    )(a, wd, x2, gain.reshape(1, D), gt)


def _t5_bucket_np(dist):
    d = np.maximum(dist, 0)
    max_exact = N_BUCKETS // 2
    large = max_exact + (np.log(np.maximum(d, 1).astype(np.float32) / max_exact)
                         / math.log(MAX_DISTANCE / max_exact) * (N_BUCKETS - max_exact)).astype(np.int32)
    return np.where(d < max_exact, d, np.minimum(large, N_BUCKETS - 1))


def _bias_table_kernel(scale_ref, rb_ref, code_ref, o_ref):
    code = code_ref[0]
    sc = scale_ref[pl.program_id(0)]
    for h in range(A_HEADS):
        acc = jnp.full(code.shape, NEG, F32)
        for b in range(N_BUCKETS):
            acc = jnp.where(code == b, rb_ref[b, h], acc)
        o_ref[0, h] = acc * sc


def _bias_tables(rel_bias, codes, scales):
    nt, R, C = codes.shape
    smem = pl.BlockSpec(memory_space=pltpu.SMEM)
    return pl.pallas_call(
        _bias_table_kernel,
        grid=(nt,),
        in_specs=[smem, smem, pl.BlockSpec((1, R, C), lambda t: (t, 0, 0))],
        out_specs=pl.BlockSpec((1, A_HEADS, R, C), lambda t: (t, 0, 0, 0)),
        out_shape=jax.ShapeDtypeStruct((nt, A_HEADS, R, C), F32),
        compiler_params=_cparams("parallel"),
        name="bias_tables",
    )(jnp.asarray(scales, F32), rel_bias, jnp.asarray(codes))


def _codes(dist, lo, hi):
    return np.where((dist >= lo) & (dist < hi), _t5_bucket_np(dist), N_BUCKETS).astype(np.int32)


def _nsa_tables(rel_bias, S):
    NC = S // CMP_STRIDE
    G, HPG = A_KV_GROUPS, A_HEADS // A_KV_GROUPS
    big = 1 << 30
    i = np.arange(TQ)[:, None]
    dc = i - CMP_STRIDE * np.arange(NC)[None, :] + CMP_STRIDE * NC - TQ - (CMP_BLOCK - 1)
    cb = _bias_tables(rel_bias, _codes(dc, 0, big)[None], [1.0])[0]
    cb = cb.reshape(G, HPG // 2, 2, TQ, NC).transpose(0, 2, 1, 3, 4).reshape(G, 2, 2 * TQ, NC)
    dt = np.arange(TQ)[None, :] - np.arange(TQ)[:, None]
    far = np.full((TQ, TQ), MAX_DISTANCE)
    codes = np.stack([_codes(dt, 0, WINDOW), _codes(TQ + dt, 0, WINDOW), _codes(far, 0, big),
                      _codes(2 * TQ + dt, 0, WINDOW)])
    t = _bias_tables(rel_bias, codes, [LOG2E] * 4)
    t = t.reshape(4, G, HPG // 2, 2, TQ, TQ).transpose(1, 3, 0, 4, 2, 5).reshape(G, 2, 4, TQ, 2 * TQ)
    return cb, t[:, :, 0:3], jnp.concatenate([t[:, :, 0:2], t[:, :, 3:4]], axis=2)


def _pad_cols(w, n):
    return jnp.pad(w, ((0, 0), (0, n - w.shape[1])))


def _prep_w_in(w):
    D = w.shape[0]
    za, zb, zc, zg = jnp.split(w, np.cumsum([A_COLS, B_COLS, C_COLS]).tolist(), axis=1)
    a_parts = jnp.split(za, np.cumsum([A_Q] + [A_KV] * 6).tolist(), axis=1)
    qa, kva, kr = jnp.split(zb, [Q_LORA, Q_LORA + KV_LORA], axis=1)
    half = ROPE_DIM // 2
    kr_rot = jnp.concatenate([-kr[:, half:], kr[:, :half]], axis=1)
    z64 = jnp.zeros((D, NOPE_DIM), w.dtype)
    z32 = jnp.zeros((D, LANES - NOPE_DIM - ROPE_DIM), w.dtype)
    c_r, c_k, c_v, c_wd, c_ad, c_gd = jnp.split(
        zc, np.cumsum([C_WIDTH] * 3 + [DECAY_LORA, AAA_LORA]).tolist(), axis=1)
    cols = [a_parts[0], qa] + a_parts[1:7] + [
        _pad_cols(a_parts[7], LANES), kva,
        jnp.concatenate([z64, kr, z32], axis=1),
        jnp.concatenate([z64, kr_rot, z32], axis=1),
        c_r, c_k, c_v, c_wd, c_ad, c_gd, jnp.zeros((D, Z_ZG - Z_CG - LANES), w.dtype), zg]
    out = jnp.concatenate(cols, axis=1)
    assert out.shape[1] == Z_COLS
    return out.astype(BF16)


def _prep_mla(w_uq, w_ukv):
    dq = NOPE_DIM + ROPE_DIM
    half = ROPE_DIM // 2
    wq = w_uq.reshape(Q_LORA, B_HEADS, dq)
    nope, r1, r2 = wq[..., :NOPE_DIM], wq[..., NOPE_DIM:NOPE_DIM + half], wq[..., NOPE_DIM + half:]
    zq = jnp.zeros((Q_LORA, B_HEADS, LANES - dq), w_uq.dtype)
    wqa = jnp.concatenate([nope, r1, r2, zq], axis=-1).reshape(Q_LORA, B_HEADS * LANES)
    wqb = jnp.concatenate([jnp.zeros_like(nope), -r2, r1, zq], axis=-1).reshape(Q_LORA, B_HEADS * LANES)
    wkv = w_ukv.reshape(KV_LORA, B_HEADS, NOPE_DIM + V_DIM)
    kn, vv = wkv[..., :NOPE_DIM], wkv[..., NOPE_DIM:]
    wk = jnp.concatenate([kn, jnp.zeros_like(kn)], axis=-1).reshape(KV_LORA, B_HEADS * LANES)
    wv = vv.reshape(KV_LORA, B_HEADS * V_DIM)
    return wqa.T.astype(BF16), wqb.T.astype(BF16), wk.astype(BF16), wv.T.astype(BF16)


def _rope_tables(S):
    half = ROPE_DIM // 2
    inv = ROPE_THETA ** (-jnp.arange(half, dtype=F32) / half)
    ang = jnp.arange(S, dtype=F32)[:, None] * inv
    cos2 = jnp.tile(jnp.cos(ang), (1, 2))
    sin2 = jnp.tile(jnp.sin(ang), (1, 2))
    scale = (NOPE_DIM + ROPE_DIM) ** -0.5 * LOG2E
    one = jnp.ones((S, NOPE_DIM), F32)
    z64 = jnp.zeros((S, NOPE_DIM), F32)
    z32 = jnp.zeros((S, LANES - NOPE_DIM - ROPE_DIM), F32)
    caq = (jnp.concatenate([one, cos2, z32], axis=1) * scale).T
    cbq = (jnp.concatenate([z64, sin2, z32], axis=1) * scale).T
    cak = jnp.concatenate([z64, cos2, z32], axis=1)
    cbk = jnp.concatenate([z64, sin2, z32], axis=1)
    return caq, cbq, cak, cbk


def _prep_compress(cmp_pos, cmp_w1, cmp_w2):
    Dh = A_HEAD_DIM
    w1 = cmp_w1.reshape(2, CMP_BLOCK, Dh, Dh)
    z = jnp.zeros_like(w1)
    w1bd = jnp.concatenate([jnp.concatenate([w1, z], axis=-1), jnp.concatenate([z, w1], axis=-1)], axis=-2)
    posrow = jnp.broadcast_to(cmp_pos.reshape(2, 1, CMP_BLOCK * Dh), (2, 8, CMP_BLOCK * Dh))
    w1cat = jnp.concatenate([cmp_w1, cmp_w1], axis=-1)
    zz = jnp.zeros((2, Dh, Dh), cmp_w2.dtype)
    blk = lambda a, b, c, d: jnp.concatenate(
        [jnp.concatenate([a, b], axis=-1), jnp.concatenate([c, d], axis=-1)], axis=-2)
    w2v = jnp.stack([jnp.stack([blk(cmp_w2, zz, zz, zz), blk(zz, cmp_w2, zz, zz)], axis=1),
                     jnp.stack([blk(zz, zz, cmp_w2, zz), blk(zz, zz, zz, cmp_w2)], axis=1)], axis=1)
    return w1bd.astype(BF16), posrow, w1cat.astype(BF16), w2v.astype(BF16)


def _gate_expand():
    e = np.zeros((3, LANES, A_Q), np.float32)
    for h in range(A_HEADS):
        for r in range(3):
            e[r, 3 * h + r, h * A_HEAD_DIM:(h + 1) * A_HEAD_DIM] = 1.0
    return jnp.asarray(e, dtype=BF16)


def _block_diag_ones(scale):
    idx = np.arange(C_WIDTH) // C_HEAD_DIM
    return jnp.asarray((idx[:, None] == idx[None, :]).astype(np.float32) * scale, dtype=BF16)


def kernel(x, c, rel_bias, ada_w, ada_b, norm_gain, w_in, nsa_cmp_pos, nsa_cmp_w1, nsa_cmp_w2, mla_q_norm, mla_kv_norm, mla_w_uq, mla_w_ukv, rwkv_mu, rwkv_w0, rwkv_a0, rwkv_k_k, rwkv_k_a, rwkv_w2, rwkv_a2, rwkv_g2, rwkv_r_k, rwkv_ln, w_branch, w_out, ffn_up, ffn_conv_w, ffn_conv_b, ffn_down):
    B, S, D = x.shape
    L = ada_w.shape[0]
    assert S % TQ == 0 and S % TM == 0 and S // SLC_BLOCK <= A_HEAD_DIM and S >= 2 * TQ
    mod = _adaln(c, ada_w, ada_b)
    cb, slc_tab, win_tab = _nsa_tables(rel_bias, S)
    caq, cbq, cak, cbk = _rope_tables(S)
    gexp = _gate_expand()
    bd1 = _block_diag_ones(1.0)
    bdm = _block_diag_ones(1.0 / C_HEAD_DIM)
    row = lambda v: v.reshape(1, -1)
    x2 = x.reshape(B * S, D)
    for l in range(L):
        m6 = mod[l].reshape(B, 6, 1, D)
        sh1, sc1, gt1, sh2, sc2, gt2 = (m6[:, i] for i in range(6))
        z = _norm_mod_matmul(x2, norm_gain[l, 0], sc1, sh1, _prep_w_in(w_in[l]), S, tn=1024)
        kcp, vcp = _compress(z, *_prep_compress(nsa_cmp_pos[l], nsa_cmp_w1[l], nsa_cmp_w2[l]), B, S)
        oc, sb = _cmpsel(z, kcp, vcp, cb, B, S)
        os_ = _nsa_flash(z, sb, slc_tab, Z_AKS, Z_AVS, B, S, selected=True)
        ow = _nsa_flash(z, None, win_tab, Z_AKW, Z_AVW, B, S, selected=False)
        wqa, wqb, wk, wv = _prep_mla(mla_w_uq[l], mla_w_ukv[l])
        q, k, v = _mla_proj(z, row(mla_q_norm[l]), row(mla_kv_norm[l]), wqa, wqb, wk, wv,
                            caq, cbq, cak, cbk, B, S)
        yb = _mla_flash(q, k, v, B, S)
        mu = rwkv_mu[l]
        o = 3 * C_WIDTH
        mus = [row(mu[:C_WIDTH]), row(mu[C_WIDTH:2 * C_WIDTH]), row(mu[2 * C_WIDTH:o]),
               row(mu[o:o + LANES]), row(mu[o + LANES:])]
        w2p = jnp.concatenate([rwkv_w2[l], jnp.zeros_like(rwkv_a2[l])], axis=0).astype(BF16)
        a2p = jnp.concatenate([jnp.zeros_like(rwkv_w2[l]), rwkv_a2[l]], axis=0).astype(BF16)
        rr, k2, vv, kk, be, ld, gg, bo = _rwkv_prep(
            z, mus, row(rwkv_w0[l]), row(rwkv_a0[l]), row(rwkv_k_k[l]), row(rwkv_k_a[l]),
            row(rwkv_r_k[l]), w2p, a2p, rwkv_g2[l].astype(BF16), bd1, B, S)
        qa_, bkt, yu, vp, pcb = _rwkv_chunk(rr, k2, vv, kk, be, ld, B, S)
        yr = _rwkv_state(qa_, bkt, yu, vp, pcb, B, S)
        x2 = _merge(oc, os_, ow, z, yb, yr, bo, gg, x2, w_branch[l].astype(BF16), w_out[l].astype(BF16),
                    gexp, bdm, rwkv_ln[l], norm_gain[l, 1], gt1, S)
        a = _ffn_up(x2, norm_gain[l, 2], sc2, sh2, ffn_up[l].astype(BF16), ffn_conv_w[l], ffn_conv_b[l], S)
        x2 = _ffn_down(a, ffn_down[l].astype(BF16), x2, norm_gain[l, 3], gt2, S)
    return x2.reshape(B, S, D)
```

```python
import functools
import math

import jax
import jax.numpy as jnp
import numpy as np
from jax import lax
from jax.experimental import pallas as pl
from jax.experimental.pallas import tpu as pltpu

F32 = jnp.float32
BF16 = jnp.bfloat16

D_MODEL = 1024
DEPTH = 4
A_HEADS, A_KV_GROUPS, A_HEAD_DIM = 8, 2, 64
CMP_BLOCK, CMP_STRIDE = 32, 16
SLC_BLOCK, SLC_TOPK, N_LOCAL_BLOCKS = 64, 16, 2
WINDOW = 512
FORCE_SCORE = 1e9
B_HEADS, Q_LORA, KV_LORA, NOPE_DIM, ROPE_DIM, V_DIM = 8, 256, 128, 64, 32, 64
ROPE_THETA = 10000.0
C_HEADS, C_HEAD_DIM = 8, 64
C_WIDTH = C_HEADS * C_HEAD_DIM
DECAY_LORA, AAA_LORA, GATE_LORA = 64, 64, 128
GN_EPS = 64e-5
N_BUCKETS, MAX_DISTANCE = 32, 128
D_FF = 2816
EPS = 1e-6
NEG = -1e30
LOG2E = math.log2(math.e)

A_Q = A_HEADS * A_HEAD_DIM
A_KV = A_KV_GROUPS * A_HEAD_DIM
A_GATE = 3 * A_HEADS
A_COLS = A_Q + 6 * A_KV + A_GATE
B_COLS = Q_LORA + KV_LORA + ROPE_DIM
C_COLS = 3 * C_WIDTH + DECAY_LORA + AAA_LORA + GATE_LORA

LANES = 128
FFN_SUB = 256
VMEM_LIMIT = 48 * 1024 * 1024

Z_AQ = 0
Z_QA = 512
Z_AKC, Z_AVC, Z_AKS, Z_AVS, Z_AKW, Z_AVW = 768, 896, 1024, 1152, 1280, 1408
Z_AG = 1536
Z_KVA = 1664
Z_KRA = 1792
Z_KRB = 1920
Z_CR, Z_CK, Z_CV = 2048, 2560, 3072
Z_CWA = 3584
Z_CG = 3712
Z_ZG = 4096
Z_COLS = 7168

TQ = 256
TM = 512
CH = 64
RW_SUB = 2


def _cparams(*sem):
    return pltpu.CompilerParams(dimension_semantics=sem, vmem_limit_bytes=VMEM_LIMIT)


def _gelu(x):
    return 0.5 * x * (1.0 + jnp.tanh(0.7978845608028654 * (x + 0.044715 * (x * x * x))))


def _dot(a, b, **kw):
    return jnp.dot(a, b, preferred_element_type=F32, **kw)


def _split3(x):
    hi = x.astype(BF16)
    r = x - hi.astype(F32)
    mid = r.astype(BF16)
    return hi, mid, (r - mid.astype(F32)).astype(BF16)


def _dot3(x, w):
    return sum(_dot(t, w) for t in _split3(x))


def _dot2(x, w):
    hi, mid, _ = _split3(x)
    return _dot(hi, w) + _dot(mid, w)


def _dot_t(a, b):
    return lax.dot_general(a, b, (((1,), (1,)), ((), ())), preferred_element_type=F32)


def _adaln_kernel(c_ref, w_ref, b_ref, o_ref):
    c = c_ref[...]
    cond = c * jax.nn.sigmoid(c)
    o_ref[0] = _dot(cond.astype(BF16), w_ref[0].astype(BF16)) + b_ref[0]


def _adaln(c, ada_w, ada_b):
    L, D, N6 = ada_w.shape
    B = c.shape[0]
    tn = 1536
    return pl.pallas_call(
        _adaln_kernel,
        grid=(L, N6 // tn),
        in_specs=[pl.BlockSpec((B, D), lambda l, j: (0, 0)),
                  pl.BlockSpec((1, D, tn), lambda l, j: (l, 0, j)),
                  pl.BlockSpec((1, 1, tn), lambda l, j: (l, 0, j))],
        out_specs=pl.BlockSpec((1, B, tn), lambda l, j: (l, 0, j)),
        out_shape=jax.ShapeDtypeStruct((L, B, N6), F32),
        compiler_params=_cparams("parallel", "parallel"),
        name="adaln",
    )(c, ada_w, ada_b.reshape(L, 1, N6))


def _nmm_kernel(x_ref, g_ref, sc_ref, sh_ref, w_ref, o_ref, h_ref):
    @pl.when(pl.program_id(1) == 0)
    def _():
        x = x_ref[...]
        y = x * lax.rsqrt(jnp.mean(x * x, axis=-1, keepdims=True) + EPS)
        h = (y * g_ref[...]) * (1.0 + sc_ref[0]) + sh_ref[0]
        h_ref[...] = h.astype(BF16)

    o_ref[...] = _dot(h_ref[...], w_ref[...]).astype(o_ref.dtype)


def _norm_mod_matmul(x2, gain, sc, sh, w, S, tn, out_dtype=BF16):
    N, D = x2.shape
    NC = w.shape[1]
    tm = min(1024, S)
    tpb = S // tm
    return pl.pallas_call(
        _nmm_kernel,
        grid=(N // tm, NC // tn),
        in_specs=[pl.BlockSpec((tm, D), lambda i, j: (i, 0)),
                  pl.BlockSpec((1, D), lambda i, j: (0, 0)),
                  pl.BlockSpec((1, 1, D), lambda i, j: (i // tpb, 0, 0)),
                  pl.BlockSpec((1, 1, D), lambda i, j: (i // tpb, 0, 0)),
                  pl.BlockSpec((D, tn), lambda i, j: (0, j))],
        out_specs=pl.BlockSpec((tm, tn), lambda i, j: (i, j)),
        out_shape=jax.ShapeDtypeStruct((N, NC), out_dtype),
        scratch_shapes=[pltpu.VMEM((tm, D), BF16)],
        compiler_params=_cparams("parallel", "arbitrary"),
        name="norm_mod_matmul",
    )(x2, gain.reshape(1, D), sc, sh, w)


def _compress_kernel(zk_ref, zv_ref, w1_ref, pos_ref, w1c_ref, w2_ref, kc_ref, vc_ref, zf_ref, *, NC):
    for kind, (z_ref, o_ref) in enumerate(((zk_ref, kc_ref), (zv_ref, vc_ref))):
        pa = jnp.zeros((NC, LANES), F32)
        pb = jnp.zeros((NC, LANES), F32)
        zf_ref[...] = z_ref[...].astype(F32)
        for l in range(CMP_STRIDE):
            xl = zf_ref[pl.ds(l, NC, stride=CMP_STRIDE), :].astype(BF16)
            pa = pa + _dot(xl, w1_ref[kind, l])
            pb = pb + _dot(xl, w1_ref[kind, CMP_STRIDE + l])
        posb = _dot(pos_ref[kind].astype(BF16), w1c_ref[kind])[0:1, :]
        h = pa + pltpu.roll(pb, NC - 1, axis=0) + posb
        act = _gelu(h).astype(BF16)
        row = lax.broadcasted_iota(jnp.int32, (NC, LANES), 0)
        for g in range(A_KV_GROUPS):
            for eo in range(2):
                out = _dot(act, w2_ref[kind, g, eo])
                out = jnp.where(row < NC - 1, out, 0.0)
                o_ref[g, eo, pl.ds(0, NC), :] = jnp.zeros((NC, LANES), BF16)
                o_ref[g, eo, pl.ds(NC, NC), :] = out.astype(BF16)


def _compress(z, w1bd, posrow, w1cat, w2v, B, S):
    NC = S // CMP_STRIDE
    out = jax.ShapeDtypeStruct((B, A_KV_GROUPS, 2, 2 * NC, LANES), BF16)
    ospec = pl.BlockSpec((None, A_KV_GROUPS, 2, 2 * NC, LANES), lambda b: (b, 0, 0, 0, 0))
    full = lambda a: pl.BlockSpec(a.shape, lambda b: (0,) * a.ndim)
    return pl.pallas_call(
        functools.partial(_compress_kernel, NC=NC),
        grid=(B,),
        in_specs=[pl.BlockSpec((S, LANES), lambda b: (b, Z_AKC // LANES)),
                  pl.BlockSpec((S, LANES), lambda b: (b, Z_AVC // LANES)),
                  full(w1bd), full(posrow), full(w1cat), full(w2v)],
        out_specs=[ospec, ospec],
        out_shape=[out, out],
        scratch_shapes=[pltpu.VMEM((S, LANES), F32)],
        compiler_params=_cparams("parallel"),
        name="nsa_compress",
    )(z, z, w1bd, posrow, w1cat, w2v)


def _stack_pairs(zq, scale):
    q = zq.astype(F32) * scale
    return jnp.concatenate([q[:, :LANES], q[:, LANES:]], axis=0).astype(BF16)


def _cmpsel_kernel(zq_ref, kc_ref, vc_ref, cb_ref, oc_ref, sb_ref, *, NC, NS):
    qt = pl.program_id(2)
    r = TQ // CMP_STRIDE
    lhs = _stack_pairs(zq_ref[...], A_HEAD_DIM ** -0.5)
    st = pl.multiple_of(r * (qt + 1), 16)
    col = lax.broadcasted_iota(jnp.int32, (2 * TQ, NC), 1)
    exists = col >= NC - r * (qt + 1)
    acc = jnp.zeros((2 * TQ, LANES), F32)
    psum = jnp.zeros((TQ, NC), F32)
    for eo in range(2):
        kwin = kc_ref[eo, pl.ds(st, NC), :]
        vwin = vc_ref[eo, pl.ds(st, NC), :]
        logits = _dot_t(lhs, kwin) + cb_ref[eo]
        logits = jnp.where(exists, logits, NEG)
        m = jnp.max(logits, axis=-1, keepdims=True)
        p = jnp.where(logits > 0.1 * NEG, jnp.exp(logits - m), 0.0)
        l = jnp.maximum(jnp.sum(p, axis=-1, keepdims=True), 1e-30)
        p = p * (1.0 / l)
        acc = acc + _dot(p.astype(BF16), vwin)
        psum = psum + p[:TQ] + p[TQ:]
    oc_ref[:, :LANES] = acc[:TQ]
    oc_ref[:, LANES:] = acc[TQ:]

    NR = A_HEAD_DIM
    jj = lax.broadcasted_iota(jnp.int32, (NR, NC), 0)
    nn = lax.broadcasted_iota(jnp.int32, (NR, NC), 1) + (r * (qt + 1) - NC)
    delta = 4 * jj - nn
    mt = jnp.where((delta == 0) | (delta == 4), 1.0, 0.0) + jnp.where((delta >= 1) & (delta <= 3), 2.0, 0.0)
    mt = jnp.where(jj < NS, mt, 0.0)
    mtb = mt.astype(BF16)
    imp_t = sum(_dot_t(mtb, t) for t in _split3(psum))
    jb = lax.broadcasted_iota(jnp.int32, (NR, TQ), 0)
    tpos = qt * TQ + lax.broadcasted_iota(jnp.int32, (NR, TQ), 1)
    back = (tpos >> 6) - jb
    forced = (jb == 0) | ((back >= 0) & (back < N_LOCAL_BLOCKS))
    score = jnp.where(forced, FORCE_SCORE, jnp.where(back >= 0, imp_t, -1.0))
    score = jnp.where(jb < NS, score, -2.0)
    ngrp = -(-NS // 8)
    grp = [score[8 * m:8 * m + 8, :] for m in range(ngrp)]
    rank = [jnp.zeros((8, TQ), F32) for _ in range(ngrp)]
    j8 = lax.broadcasted_iota(jnp.int32, (8, TQ), 0)
    for jp in range(NS):
        row = score[jp:jp + 1, :]
        for m in range(ngrp):
            if m < jp // 8:
                beats = row > grp[m]
            elif m > jp // 8:
                beats = row >= grp[m]
            else:
                beats = (row > grp[m]) | ((row == grp[m]) & (j8 > jp - 8 * m))
            rank[m] = rank[m] + jnp.where(beats, 1.0, 0.0)
    rank = jnp.concatenate(rank + [jnp.zeros((NR - 8 * ngrp, TQ), F32)] * (NR > 8 * ngrp), axis=0)
    sel = (rank < float(min(SLC_TOPK, NS))) & (back >= 0) & (jb < NS)
    sb = jnp.where(sel, 0.0, jnp.where(jb < NS, NEG, 0.0))
    sb_ref[...] = jnp.concatenate([sb, jnp.zeros((LANES - NR, TQ), F32)], axis=0).astype(BF16)


def _cmpsel(z, kcp, vcp, cb, B, S):
    NC = S // CMP_STRIDE
    NS = S // SLC_BLOCK
    nq = S // TQ
    kspec = pl.BlockSpec((None, None, 2, 2 * NC, LANES), lambda b, g, q: (b, g, 0, 0, 0))
    return pl.pallas_call(
        functools.partial(_cmpsel_kernel, NC=NC, NS=NS),
        grid=(B, A_KV_GROUPS, nq),
        in_specs=[pl.BlockSpec((TQ, 2 * LANES), lambda b, g, q: (b * nq + q, g)),
                  kspec, kspec,
                  pl.BlockSpec((None, 2, 2 * TQ, NC), lambda b, g, q: (g, 0, 0, 0))],
        out_specs=[pl.BlockSpec((TQ, 2 * LANES), lambda b, g, q: (b * nq + q, g)),
                   pl.BlockSpec((None, None, LANES, TQ), lambda b, g, q: (b, g, 0, q))],
        out_shape=[jax.ShapeDtypeStruct((B * S, A_Q), F32),
                   jax.ShapeDtypeStruct((B, A_KV_GROUPS, LANES, S), BF16)],
        compiler_params=_cparams("parallel", "parallel", "parallel"),
        name="nsa_cmp_select",
    )(z, kcp, vcp, cb)


def _flash_update_t(s, vt, state, acc_ref):
    m, l = state
    m_new = jnp.maximum(m, jnp.max(s, axis=0, keepdims=True))
    p = jnp.exp2(s - m_new)
    alpha = jnp.exp2(m - m_new)
    acc_ref[...] = alpha * acc_ref[...] + _dot(vt, p.astype(BF16))
    return m_new, alpha * l + jnp.sum(p, axis=0, keepdims=True)


def _flash_pipeline(qk, softmax, lo, hi, state, s_ref):
    s_ref[0] = qk(0, lo)

    def step(kc, st):
        s_ref[1] = qk(1, kc)
        st0 = softmax(0, s_ref[0], kc, st[0], False)
        s_ref[0] = qk(0, kc + 1)
        st1 = softmax(1, s_ref[1], kc, st[1], False)
        return st0, st1

    npair = (hi - lo) // 2
    st = lax.fori_loop(0, npair, lambda i, st: step(lo + 2 * i + 1, step(lo + 2 * i, st)), state)
    st = lax.fori_loop(lo + 2 * npair, hi, step, st)
    s_ref[1] = qk(1, hi)
    st0 = softmax(0, s_ref[0], hi, st[0], True)
    st1 = softmax(1, s_ref[1], hi, st[1], True)
    return st0, st1


def _split_heads_kv(slab, g):
    lane = lax.broadcasted_iota(jnp.int32, slab.shape, 1)
    own = jnp.where((lane < A_HEAD_DIM) == (g == 0), slab, 0.0)
    other = pltpu.roll(own, A_HEAD_DIM, axis=1)
    is0 = g == 0
    return jnp.where(is0, own, other), jnp.where(is0, other, own)


def _nsa_flash_kernel(*refs, S, selected):
    if selected:
        zq_ref, sbt_ref, zk_ref, zv_ref, tab_ref, o_ref, ke_ref, ko_ref, vt_ref, acc_ref, s_ref = refs
    else:
        zq_ref, zk_ref, zv_ref, tab_ref, o_ref, ke_ref, ko_ref, vt_ref, acc_ref, s_ref = refs
    g = pl.program_id(1)
    qt = pl.program_id(2)
    k_refs = (ke_ref, ko_ref)

    @pl.when(qt == 0)
    def _():
        ke, ko = _split_heads_kv(zk_ref[...].astype(F32), g)
        ke_ref[:, :LANES] = ke.astype(BF16)
        ko_ref[:, :LANES] = ko.astype(BF16)
        for c in range(S // TQ):
            vt_ref[c] = zv_ref[c * TQ:(c + 1) * TQ, :].astype(F32).T.astype(BF16)
        if selected:
            rowb = lax.broadcasted_iota(jnp.int32, (S, LANES), 0) >> 6
            lane = lax.broadcasted_iota(jnp.int32, (S, LANES), 1)
            onehot = jnp.where(rowb == lane, 1.0, 0.0).astype(BF16)
            ke_ref[:, LANES:] = onehot
            ko_ref[:, LANES:] = onehot

    zq = zq_ref[...].astype(F32) * (A_HEAD_DIM ** -0.5 * LOG2E)
    rhs = jnp.concatenate([zq[:, :LANES], zq[:, LANES:]], axis=0).T.astype(BF16)
    if selected:
        sbt = sbt_ref[...]
        rhs = jnp.concatenate([rhs, jnp.concatenate([sbt, sbt], axis=1)], axis=0)
    acc_ref[...] = jnp.zeros(acc_ref.shape, F32)
    vrow = pl.multiple_of(g * A_HEAD_DIM, A_HEAD_DIM)

    def qk(eo, kc):
        ks = pl.multiple_of(kc * TQ, TQ)
        return _dot(k_refs[eo][pl.ds(ks, TQ), :], rhs)

    def softmax(eo, s, kc, state, diag):
        ti = 0 if diag else jnp.minimum(qt - kc, 2)
        vt = vt_ref[kc, pl.ds(vrow, A_HEAD_DIM), :]
        return _flash_update_t(s + tab_ref[eo, ti], vt, state, acc_ref.at[eo])

    init = (jnp.full((1, 2 * TQ), 0.5 * NEG, F32), jnp.zeros((1, 2 * TQ), F32))
    lo = 0 if selected else jnp.maximum(qt - 2, 0)
    carry = _flash_pipeline(qk, softmax, lo, qt, (init, init), s_ref)
    out_t = jnp.concatenate([acc_ref[0] * (1.0 / carry[0][1]), acc_ref[1] * (1.0 / carry[1][1])], axis=0)
    out = out_t.T
    o_ref[:, :LANES] = out[:TQ]
    o_ref[:, LANES:] = out[TQ:]


def _nsa_flash(z, sb, tab, zk_col, zv_col, B, S, selected):
    nq = S // TQ
    kw = 2 * LANES if selected else LANES
    in_specs = [pl.BlockSpec((TQ, 2 * LANES), lambda b, g, q: (b * nq + q, g))]
    args = [z]
    if selected:
        in_specs.append(pl.BlockSpec((None, None, LANES, TQ), lambda b, g, q: (b, g, 0, q)))
        args.append(sb)
    in_specs += [pl.BlockSpec((S, LANES), lambda b, g, q: (b, zk_col // LANES)),
                 pl.BlockSpec((S, LANES), lambda b, g, q: (b, zv_col // LANES)),
                 pl.BlockSpec((None, 2, 3, TQ, 2 * TQ), lambda b, g, q: (g, 0, 0, 0, 0))]
    args += [z, z, tab]
    return pl.pallas_call(
        functools.partial(_nsa_flash_kernel, S=S, selected=selected),
        grid=(B, A_KV_GROUPS, nq),
        in_specs=in_specs,
        out_specs=pl.BlockSpec((TQ, 2 * LANES), lambda b, g, q: (b * nq + q, g)),
        out_shape=jax.ShapeDtypeStruct((B * S, A_Q), F32),
        scratch_shapes=[pltpu.VMEM((S, kw), BF16), pltpu.VMEM((S, kw), BF16),
                        pltpu.VMEM((S // TQ, LANES, TQ), BF16),
                        pltpu.VMEM((2, A_HEAD_DIM, 2 * TQ), F32),
                        pltpu.VMEM((2, TQ, 2 * TQ), F32)],
        compiler_params=_cparams("parallel", "parallel", "arbitrary"),
        name="nsa_selected" if selected else "nsa_window",
    )(*args)


def _mla_proj_kernel(zqa_ref, zkva_ref, zkra_ref, zkrb_ref, qn_ref, kvn_ref, wqa_ref, wqb_ref,
                     wk_ref, wv_ref, caq_ref, cbq_ref, cak_ref, cbk_ref, q_ref, k_ref, v_ref):
    def rms(x, gain):
        return x * lax.rsqrt(jnp.mean(x * x, axis=-1, keepdims=True) + EPS) * gain

    nq = rms(zqa_ref[...].astype(F32), qn_ref[...]).astype(BF16)
    nkv = rms(zkva_ref[...].astype(F32), kvn_ref[...]).astype(BF16)
    qa = _dot_t(wqa_ref[...], nq)
    qb = _dot_t(wqb_ref[...], nq)
    kn = _dot(nkv, wk_ref[...])
    vt = _dot_t(wv_ref[...], nkv)
    caq, cbq = caq_ref[...], cbq_ref[...]
    kr = zkra_ref[...].astype(F32) * cak_ref[...] + zkrb_ref[...].astype(F32) * cbk_ref[...]
    for h in range(B_HEADS):
        sl = slice(h * LANES, (h + 1) * LANES)
        q_ref[h] = (qa[sl, :] * caq + qb[sl, :] * cbq).astype(BF16)
        k_ref[h] = (kn[:, sl] + kr).astype(BF16)
        v_ref[h] = vt[h * V_DIM:(h + 1) * V_DIM, :].astype(BF16)


def _mla_proj(z, qn, kvn, wqa, wqb, wk, wv, caq, cbq, cak, cbk, B, S):
    tm = TM
    nt = S // tm
    zspec = lambda w, col: pl.BlockSpec((tm, w), lambda b, i: (b * nt + i, col // w))
    full = lambda a: pl.BlockSpec(a.shape, lambda b, i: (0,) * a.ndim)
    tspec = pl.BlockSpec((tm, LANES), lambda b, i: (i, 0))
    tspec_t = pl.BlockSpec((LANES, tm), lambda b, i: (0, i))
    return pl.pallas_call(
        _mla_proj_kernel,
        grid=(B, nt),
        in_specs=[zspec(Q_LORA, Z_QA), zspec(LANES, Z_KVA), zspec(LANES, Z_KRA), zspec(LANES, Z_KRB),
                  full(qn), full(kvn), full(wqa), full(wqb), full(wk), full(wv),
                  tspec_t, tspec_t, tspec, tspec],
        out_specs=[pl.BlockSpec((None, B_HEADS, LANES, tm), lambda b, i: (b, 0, 0, i)),
                   pl.BlockSpec((None, B_HEADS, tm, LANES), lambda b, i: (b, 0, i, 0)),
                   pl.BlockSpec((None, B_HEADS, None, V_DIM, tm), lambda b, i: (b, 0, i, 0, 0))],
        out_shape=[jax.ShapeDtypeStruct((B, B_HEADS, LANES, S), BF16),
                   jax.ShapeDtypeStruct((B, B_HEADS, S, LANES), BF16),
                   jax.ShapeDtypeStruct((B, B_HEADS, nt, V_DIM, tm), BF16)],
        compiler_params=_cparams("parallel", "parallel"),
        name="mla_proj",
    )(z, z, z, z, qn, kvn, wqa, wqb, wk, wv, caq, cbq, cak, cbk)


def _mla_flash_kernel(qt_ref, k_ref, vt_ref, o_ref, acc_ref, s_ref):
    qi = pl.program_id(2)
    acc_ref[...] = jnp.zeros(acc_ref.shape, F32)
    ri = lax.broadcasted_iota(jnp.int32, (TM, TM), 0)
    ci = lax.broadcasted_iota(jnp.int32, (TM, TM), 1)
    causal = ri <= ci

    def qk(hh, kc):
        ks = pl.multiple_of(kc * TM, TM)
        return _dot(k_ref[hh, pl.ds(ks, TM), :], qt_ref[hh])

    def softmax(hh, s, kc, state, diag):
        if diag:
            s = jnp.where(causal, s, NEG)
        return _flash_update_t(s, vt_ref[hh, kc], state, acc_ref.at[hh])

    init = (jnp.full((1, TM), 0.5 * NEG, F32), jnp.zeros((1, TM), F32))
    carry = _flash_pipeline(qk, softmax, 0, qi, (init, init), s_ref)
    out_t = jnp.concatenate([acc_ref[0] * (1.0 / carry[0][1]), acc_ref[1] * (1.0 / carry[1][1])], axis=0)
    o_ref[...] = out_t.T


def _mla_flash(qt, k, vt, B, S):
    nq = S // TM
    hp = B_HEADS // 2
    return pl.pallas_call(
        _mla_flash_kernel,
        grid=(B, hp, nq),
        in_specs=[pl.BlockSpec((None, 2, LANES, TM), lambda b, h, i: (b, h, 0, i)),
                  pl.BlockSpec((None, 2, S, LANES), lambda b, h, i: (b, h, 0, 0)),
                  pl.BlockSpec((None, 2, nq, V_DIM, TM), lambda b, h, i: (b, h, 0, 0, 0))],
        out_specs=pl.BlockSpec((TM, LANES), lambda b, h, i: (b * nq + i, h)),
        out_shape=jax.ShapeDtypeStruct((B * S, B_HEADS * V_DIM), F32),
        scratch_shapes=[pltpu.VMEM((2, V_DIM, TM), F32), pltpu.VMEM((2, TM, TM), F32)],
        compiler_params=_cparams("parallel", "parallel", "arbitrary"),
        name="mla_flash",
    )(qt, k, vt)


def _rwkv_prep_kernel(zr_ref, zk_ref, zv_ref, zwa_ref, zg_ref, mu_r, mu_k, mu_v, mu_wa, mu_g,
                      w0_ref, a0_ref, kk_ref, ka_ref, rk_ref, w2_ref, a2_ref, g2_ref, bd_ref,
                      r_o, k_o, v_o, kk_o, be_o, ld_o, g_o, bo_o,
                      c_r, c_k, c_v, c_wa, c_g):
    t = pl.program_id(1)
    tm = zr_ref.shape[0]

    def shifted(z_ref, mu_ref, c_ref):
        x = z_ref[...].astype(F32)
        row = lax.broadcasted_iota(jnp.int32, x.shape, 0)
        prev = jnp.where(t == 0, 0.0, c_ref[0:1, :])
        xs = jnp.where(row == 0, prev, pltpu.roll(x, 1, axis=0))
        c_ref[0:1, :] = x[tm - 1:tm, :]
        return x + (xs - x) * mu_ref[...]

    r = shifted(zr_ref, mu_r, c_r)
    k = shifted(zk_ref, mu_k, c_k)
    v = shifted(zv_ref, mu_v, c_v)
    wa = shifted(zwa_ref, mu_wa, c_wa)
    gd = shifted(zg_ref, mu_g, c_g)
    w = w0_ref[...] + _dot(jnp.tanh(wa).astype(BF16), w2_ref[...])
    ld_o[...] = -jax.nn.sigmoid(w) * math.exp(-0.5)
    a = jax.nn.sigmoid(a0_ref[...] + _dot(wa.astype(BF16), a2_ref[...]))
    g_o[...] = _dot(jax.nn.sigmoid(gd).astype(BF16), g2_ref[...])
    kk = k * kk_ref[...]
    nsq = _dot3(kk * kk, bd_ref[...])
    kk = kk / jnp.maximum(jnp.sqrt(nsq), 1e-12)
    k2 = k * (1.0 + (a - 1.0) * ka_ref[...])
    rks = _dot3(r * k2 * rk_ref[...], bd_ref[...])
    r_o[...] = r
    k_o[...] = k2
    v_o[...] = v
    kk_o[...] = kk
    be_o[...] = kk * a
    bo_o[...] = rks * v


def _rwkv_prep(z, mus, w0, a0, k_k, k_a, r_k, w2p, a2p, g2, bd, B, S, tm=256):
    nt = S // tm
    W = C_WIDTH
    zspec = lambda w, col: pl.BlockSpec((tm, w), lambda b, i: (b * nt + i, col // w))
    full = lambda a: pl.BlockSpec(a.shape, lambda b, i: (0,) * a.ndim)
    ospec = pl.BlockSpec((tm, W), lambda b, i: (b * nt + i, 0))
    oshape = jax.ShapeDtypeStruct((B * S, W), F32)
    consts = list(mus) + [w0, a0, k_k, k_a, r_k, w2p, a2p, g2, bd]
    return pl.pallas_call(
        _rwkv_prep_kernel,
        grid=(B, nt),
        in_specs=[zspec(W, Z_CR), zspec(W, Z_CK), zspec(W, Z_CV), zspec(LANES, Z_CWA), zspec(LANES, Z_CG)]
                 + [full(a) for a in consts],
        out_specs=[ospec] * 8,
        out_shape=[oshape] * 8,
        scratch_shapes=[pltpu.VMEM((8, W), F32)] * 3 + [pltpu.VMEM((8, LANES), F32)] * 2,
        compiler_params=_cparams("parallel", "arbitrary"),
        name="rwkv_prep",
    )(z, z, z, z, z, *consts)


def _rwkv_chunk_kernel(r_ref, k_ref, v_ref, kk_ref, be_ref, ld_ref,
                       qa_o, bkt_o, yu_o, vp_o, pcb_o):
    C = CH
    R = ld_ref.shape[0]
    ld = ld_ref[...]
    rr = lax.broadcasted_iota(jnp.int32, (R, R), 0)
    cc = lax.broadcasted_iota(jnp.int32, (R, R), 1)
    sh = int(math.log2(C))
    tri = jnp.where((cc <= rr) & ((cc >> sh) == (rr >> sh)), 1.0, 0.0).astype(BF16)
    cs = sum(_dot(tri, t) for t in _split3(ld))
    cl = jnp.concatenate([jnp.broadcast_to(cs[(s + 1) * C - 1:(s + 1) * C, :], (C, C_WIDTH))
                          for s in range(R // C)], axis=0)
    ex, exn, exx, exc = jnp.exp(cs), jnp.exp(-cs), jnp.exp(cs - ld), jnp.exp(cl - cs)
    kk, be, k2, v = kk_ref[...], be_ref[...], k_ref[...], v_ref[...]
    at = -kk * exx
    rt = r_ref[...] * ex
    bt = be * exn
    kt = k2 * exn
    btc = be * exc
    ktc = k2 * exc
    pc = jnp.exp(cl)
    ri = lax.broadcasted_iota(jnp.int32, (C, C), 0)
    ci = lax.broadcasted_iota(jnp.int32, (C, C), 1)
    incl = ci <= ri
    strict = ci < ri
    eye = jnp.where(ci == ri, 1.0, 0.0)
    lane = lax.broadcasted_iota(jnp.int32, (C, LANES), 1)
    P = C_HEADS // 2
    pairs = [(s, p) for s in range(R // C) for p in range(P)]
    heads = [(i, hh) for i in range(len(pairs)) for hh in range(2)]
    blk = lambda x, i: x[pairs[i][0] * C:(pairs[i][0] + 1) * C, pairs[i][1] * LANES:(pairs[i][1] + 1) * LANES]
    lhs2 = [jnp.concatenate([blk(at, i), blk(rt, i)], axis=0).astype(BF16) for i in range(len(pairs))]
    bm, km, vm, am = [], [], [], []
    for i, hh in heads:
        msk = (lane < C_HEAD_DIM) if hh == 0 else (lane >= C_HEAD_DIM)
        bm.append(jnp.where(msk, blk(bt, i), 0.0).astype(BF16))
        km.append(jnp.where(msk, blk(kt, i), 0.0).astype(BF16))
        vm.append(jnp.where(msk, blk(v, i), 0.0).astype(BF16))
        am.append(jnp.where(msk, blk(at, i), 0.0).astype(BF16))
    g1 = [_dot_t(lhs2[i], bm[j]) for j, (i, hh) in enumerate(heads)]
    g2 = [_dot_t(lhs2[i], km[j]) for j, (i, hh) in enumerate(heads)]
    lab = [jnp.where(strict, g[:C], 0.0) for g in g1]
    mrb = [jnp.where(incl, g[C:], 0.0).astype(BF16) for g in g1]
    lak = [jnp.where(strict, g[:C], 0.0).astype(BF16) for g in g2]
    mrk = [jnp.where(incl, g[C:], 0.0).astype(BF16) for g in g2]
    w2 = [_dot(a, b) for a, b in zip(lak, vm)]
    yk = [_dot(a, b) for a, b in zip(mrk, vm)]
    tinv = [eye + x for x in lab]
    lp = lab
    for _ in range(int(math.log2(C)) - 1):
        lpb = [x.astype(BF16) for x in lp]
        lp = [_dot(x, x) for x in lpb]
        tinv = [t + _dot(t.astype(BF16), x.astype(BF16)) for t, x in zip(tinv, lp)]
    au = [_dot(t.astype(BF16), jnp.concatenate([a, w.astype(BF16)], axis=1)) for t, a, w in zip(tinv, am, w2)]
    qy = [_dot(m, x.astype(BF16)) for m, x in zip(mrb, au)]
    for i, (s, p) in enumerate(pairs):
        e, o = 2 * i, 2 * i + 1
        ahat = au[e][:, :LANES] + au[o][:, :LANES]
        uhat = au[e][:, LANES:] + au[o][:, LANES:]
        qhat = blk(rt, i) + qy[e][:, :LANES] + qy[o][:, :LANES]
        yi = qy[e][:, LANES:] + qy[o][:, LANES:] + yk[e] + yk[o]
        qa_o[s, p] = jnp.concatenate([qhat, ahat], axis=0).astype(BF16)
        yu_o[s, p] = jnp.concatenate([yi, uhat], axis=0)
        bkt_o[s, p] = jnp.concatenate([blk(btc, i), blk(ktc, i)], axis=0).T.astype(BF16)
        vp_o[s, p] = blk(v, i).astype(BF16)
        pcb_o[s, p] = jnp.concatenate([blk(pc, i), blk(pc, i)], axis=0).T


def _rwkv_chunk(r, k2, v, kk, be, ld, B, S):
    nch = S // CH
    W = C_WIDTH
    P = C_HEADS // 2
    ns = RW_SUB
    ispec = pl.BlockSpec((ns * CH, W), lambda b, c: (b * (nch // ns) + c, 0))
    sq = lambda rows: pl.BlockSpec((None, ns, P, rows, LANES), lambda b, c: (b, c, 0, 0, 0))
    shp = lambda rows, dt: jax.ShapeDtypeStruct((B, nch, P, rows, LANES), dt)
    return pl.pallas_call(
        _rwkv_chunk_kernel,
        grid=(B, nch // ns),
        in_specs=[ispec] * 6,
        out_specs=[sq(2 * CH), sq(LANES), sq(2 * CH), sq(CH), sq(LANES)],
        out_shape=[shp(2 * CH, BF16), shp(LANES, BF16), shp(2 * CH, F32), shp(CH, BF16), shp(LANES, F32)],
        compiler_params=_cparams("parallel", "parallel"),
        name="rwkv_chunk",
    )(r, k2, v, kk, be, ld)


def _rwkv_state_kernel(qa_ref, bkt_ref, yu_ref, vp_ref, pcb_ref, y_ref, ap_ref):
    @pl.when(pl.program_id(1) == 0)
    def _():
        ap_ref[...] = jnp.zeros(ap_ref.shape, F32)

    ri = lax.broadcasted_iota(jnp.int32, (LANES, LANES), 0)
    ci = lax.broadcasted_iota(jnp.int32, (LANES, LANES), 1)
    same_head = (ri < C_HEAD_DIM) == (ci < C_HEAD_DIM)
    idx = [(b, p) for b in range(qa_ref.shape[0]) for p in range(C_HEADS // 2)]
    a = [ap_ref[b, p] for b, p in idx]
    for c in range(qa_ref.shape[1]):
        x = [_dot(qa_ref[b, c, p], s.astype(BF16)) for (b, p), s in zip(idx, a)]
        yu = [yu_ref[b, c, p] for b, p in idx]
        for (b, p), xi, yi in zip(idx, x, yu):
            y_ref[b, c * CH:(c + 1) * CH, p * LANES:(p + 1) * LANES] = xi[:CH] + yi[:CH]
        uv = [jnp.concatenate([(xi[CH:] + yi[CH:]).astype(BF16), vp_ref[b, c, p]], axis=0)
              for (b, p), xi, yi in zip(idx, x, yu)]
        upd = [_dot(bkt_ref[b, c, p], t) for (b, p), t in zip(idx, uv)]
        a = [pcb_ref[b, c, p] * s + jnp.where(same_head, t, 0.0) for (b, p), s, t in zip(idx, a, upd)]
    for (b, p), s in zip(idx, a):
        ap_ref[b, p] = s


def _rwkv_state(qa, bkt, yu, vp, pcb, B, S):
    nch = S // CH
    P = C_HEADS // 2
    nb = 2 if B % 2 == 0 else 1
    nc = 4 if nch % 4 == 0 else 1
    sq = lambda rows: pl.BlockSpec((nb, nc, P, rows, LANES), lambda b, c: (b, c, 0, 0, 0))
    return pl.pallas_call(
        _rwkv_state_kernel,
        grid=(B // nb, nch // nc),
        in_specs=[sq(2 * CH), sq(LANES), sq(2 * CH), sq(CH), sq(LANES)],
        out_specs=pl.BlockSpec((nb, nc * CH, C_WIDTH), lambda b, c: (b, c, 0)),
        out_shape=jax.ShapeDtypeStruct((B, S, C_WIDTH), F32),
        scratch_shapes=[pltpu.VMEM((nb, P, LANES, LANES), F32)],
        compiler_params=_cparams("parallel", "arbitrary"),
        name="rwkv_state",
    )(qa, bkt, yu, vp, pcb).reshape(B * S, C_WIDTH)


def _merge_kernel(oc_ref, os_ref, ow_ref, zag_ref, yb_ref, yr_ref, bo_ref, gg_ref,
                  zga_ref, zgb_ref, zgc_ref, x_ref, wbr_ref, wout_ref, ge_ref, bdm_ref,
                  ln_ref, gain_ref, gt_ref, o_ref):
    sg = jax.nn.sigmoid(zag_ref[...].astype(F32))
    ya = (_dot2(sg, ge_ref[0]) * oc_ref[...] + _dot2(sg, ge_ref[1]) * os_ref[...]
          + _dot2(sg, ge_ref[2]) * ow_ref[...])
    yr = yr_ref[...]
    mean = _dot3(yr, bdm_ref[...])
    d = yr - mean
    var = _dot2(d * d, bdm_ref[...])
    yn = d * lax.rsqrt(var + GN_EPS) * ln_ref[0:1, :] + ln_ref[1:2, :]
    yc = (yn + bo_ref[...]) * gg_ref[...]
    sig = lambda ref: jax.nn.sigmoid(ref[...].astype(F32))
    merged = (sig(zga_ref) * _dot(ya.astype(BF16), wbr_ref[0:A_Q, :])
              + sig(zgb_ref) * _dot(yb_ref[...].astype(BF16), wbr_ref[A_Q:2 * A_Q, :])
              + sig(zgc_ref) * _dot(yc.astype(BF16), wbr_ref[2 * A_Q:3 * A_Q, :]))
    y = _dot(merged.astype(BF16), wout_ref[...])
    yn2 = y * lax.rsqrt(jnp.mean(y * y, axis=-1, keepdims=True) + EPS) * gain_ref[...]
    o_ref[...] = x_ref[...] + gt_ref[0] * yn2


def _merge(oc, os_, ow, z, yb, yr, bo, gg, x2, wbr, wout, gexp, bdm, ln, gain, gt, S, tm=256):
    N, D = x2.shape
    tpb = S // tm
    W = A_Q
    row = lambda w: pl.BlockSpec((tm, w), lambda i: (i, 0))
    zspec = lambda w, col: pl.BlockSpec((tm, w), lambda i: (i, col // w))
    full = lambda a: pl.BlockSpec(a.shape, lambda i: (0,) * a.ndim)
    return pl.pallas_call(
        _merge_kernel,
        grid=(N // tm,),
        in_specs=[row(W), row(W), row(W), zspec(LANES, Z_AG), row(W), row(W), row(W), row(W),
                  zspec(D, Z_ZG), zspec(D, Z_ZG + D), zspec(D, Z_ZG + 2 * D), row(D),
                  full(wbr), full(wout), full(gexp), full(bdm), full(ln),
                  pl.BlockSpec((1, D), lambda i: (0, 0)),
                  pl.BlockSpec((1, 1, D), lambda i: (i // tpb, 0, 0))],
        out_specs=row(D),
        out_shape=jax.ShapeDtypeStruct((N, D), F32),
        compiler_params=_cparams("parallel"),
        name="merge_out",
    )(oc, os_, ow, z, yb, yr, bo, gg, z, z, z, x2, wbr, wout, gexp, bdm, ln, gain.reshape(1, D), gt)


def _ffn_up_kernel(x_ref, g_ref, sc_ref, sh_ref, wg_ref, wv_ref, cw_ref, cb_ref, o_ref,
                   cg_ref, cv_ref, *, tpb):
    i = pl.program_id(0)
    tm = x_ref.shape[0]
    fc = FFN_SUB
    first = (i % tpb) == 0
    x = x_ref[...]
    y = x * lax.rsqrt(jnp.mean(x * x, axis=-1, keepdims=True) + EPS)
    h = ((y * g_ref[...]) * (1.0 + sc_ref[0]) + sh_ref[0]).astype(BF16)
    row = lax.broadcasted_iota(jnp.int32, (8, fc), 0)

    def up(c):
        sl = slice(c * fc, (c + 1) * fc)
        return _dot(h, wg_ref[:, sl]), _dot(h, wv_ref[:, sl])

    def conv(u, c_ref, c, off):
        sl = slice(c * fc, (c + 1) * fc)
        wl = slice(off + c * fc, off + (c + 1) * fc)
        prev = c_ref[:, sl]
        p1 = jnp.where(first, 0.0, prev[7:8])
        p2 = jnp.where(first, 0.0, prev[6:7])
        c_ref[:, sl] = u[tm - 8:tm, :]
        r1 = pltpu.roll(u, 1, axis=0)
        r2 = pltpu.roll(u, 2, axis=0)
        t1 = jnp.where(row == 0, p1, r1[0:8])
        t2 = jnp.where(row == 0, p2, jnp.where(row == 1, p1, r2[0:8]))
        u1 = jnp.concatenate([t1, r1[8:]], axis=0)
        u2 = jnp.concatenate([t2, r2[8:]], axis=0)
        return cw_ref[0:1, wl] * u2 + cw_ref[1:2, wl] * u1 + cw_ref[2:3, wl] * u + cb_ref[:, wl]

    nsub = D_FF // fc
    cur = up(0)
    for c in range(nsub):
        nxt = up(c + 1) if c + 1 < nsub else None
        a = _gelu(conv(cur[0], cg_ref, c, 0)) * conv(cur[1], cv_ref, c, D_FF)
        o_ref[:, c * fc:(c + 1) * fc] = a.astype(BF16)
        cur = nxt


def _ffn_up(x2, gain, sc, sh, w_up, conv_w, conv_b, S, tm=256):
    N, D = x2.shape
    tpb = S // tm
    full = lambda a: pl.BlockSpec(a.shape, lambda i: (0,) * a.ndim)
    return pl.pallas_call(
        functools.partial(_ffn_up_kernel, tpb=tpb),
        grid=(N // tm,),
        in_specs=[pl.BlockSpec((tm, D), lambda i: (i, 0)),
                  pl.BlockSpec((1, D), lambda i: (0, 0)),
                  pl.BlockSpec((1, 1, D), lambda i: (i // tpb, 0, 0)),
                  pl.BlockSpec((1, 1, D), lambda i: (i // tpb, 0, 0)),
                  pl.BlockSpec((D, D_FF), lambda i: (0, 0)),
                  pl.BlockSpec((D, D_FF), lambda i: (0, 1)),
                  full(conv_w), pl.BlockSpec((1, 2 * D_FF), lambda i: (0, 0))],
        out_specs=pl.BlockSpec((tm, D_FF), lambda i: (i, 0)),
        out_shape=jax.ShapeDtypeStruct((N, D_FF), BF16),
        scratch_shapes=[pltpu.VMEM((8, D_FF), F32), pltpu.VMEM((8, D_FF), F32)],
        compiler_params=_cparams("arbitrary"),
        name="ffn_up",
    )(x2, gain.reshape(1, D), sc, sh, w_up, w_up, conv_w, conv_b.reshape(1, 2 * D_FF))


def _ffn_down_kernel(a_ref, wd_ref, x_ref, gain_ref, gt_ref, o_ref):
    f = _dot(a_ref[...], wd_ref[...])
    fn = f * lax.rsqrt(jnp.mean(f * f, axis=-1, keepdims=True) + EPS) * gain_ref[...]
    o_ref[...] = x_ref[...] + gt_ref[0] * fn


def _ffn_down(a, wd, x2, gain, gt, S, tm=512):
    N, D = x2.shape
    tpb = S // tm
    return pl.pallas_call(
        _ffn_down_kernel,
        grid=(N // tm,),
        in_specs=[pl.BlockSpec((tm, D_FF), lambda i: (i, 0)),
                  pl.BlockSpec((D_FF, D), lambda i: (0, 0)),
                  pl.BlockSpec((tm, D), lambda i: (i, 0)),
                  pl.BlockSpec((1, D), lambda i: (0, 0)),
                  pl.BlockSpec((1, 1, D), lambda i: (i // tpb, 0, 0))],
        out_specs=pl.BlockSpec((tm, D), lambda i: (i, 0)),
        out_shape=jax.ShapeDtypeStruct((N, D), F32),
        compiler_params=_cparams("parallel"),
        name="ffn_down",
    )(a, wd, x2, gain.reshape(1, D), gt)


def _t5_bucket_np(dist):
    d = np.maximum(dist, 0)
    max_exact = N_BUCKETS // 2
    large = max_exact + (np.log(np.maximum(d, 1).astype(np.float32) / max_exact)
                         / math.log(MAX_DISTANCE / max_exact) * (N_BUCKETS - max_exact)).astype(np.int32)
    return np.where(d < max_exact, d, np.minimum(large, N_BUCKETS - 1))


def _bias_table_kernel(scale_ref, rb_ref, code_ref, o_ref):
    code = code_ref[0]
    sc = scale_ref[pl.program_id(0)]
    for h in range(A_HEADS):
        acc = jnp.full(code.shape, NEG, F32)
        for b in range(N_BUCKETS):
            acc = jnp.where(code == b, rb_ref[b, h], acc)
        o_ref[0, h] = acc * sc


def _bias_tables(rel_bias, codes, scales):
    nt, R, C = codes.shape
    smem = pl.BlockSpec(memory_space=pltpu.SMEM)
    return pl.pallas_call(
        _bias_table_kernel,
        grid=(nt,),
        in_specs=[smem, smem, pl.BlockSpec((1, R, C), lambda t: (t, 0, 0))],
        out_specs=pl.BlockSpec((1, A_HEADS, R, C), lambda t: (t, 0, 0, 0)),
        out_shape=jax.ShapeDtypeStruct((nt, A_HEADS, R, C), F32),
        compiler_params=_cparams("parallel"),
        name="bias_tables",
    )(jnp.asarray(scales, F32), rel_bias, jnp.asarray(codes))


def _codes(dist, lo, hi):
    return np.where((dist >= lo) & (dist < hi), _t5_bucket_np(dist), N_BUCKETS).astype(np.int32)


def _nsa_tables(rel_bias, S):
    NC = S // CMP_STRIDE
    G, HPG = A_KV_GROUPS, A_HEADS // A_KV_GROUPS
    big = 1 << 30
    i = np.arange(TQ)[:, None]
    dc = i - CMP_STRIDE * np.arange(NC)[None, :] + CMP_STRIDE * NC - TQ - (CMP_BLOCK - 1)
    cb = _bias_tables(rel_bias, _codes(dc, 0, big)[None], [1.0])[0]
    cb = cb.reshape(G, HPG // 2, 2, TQ, NC).transpose(0, 2, 1, 3, 4).reshape(G, 2, 2 * TQ, NC)
    dt = np.arange(TQ)[None, :] - np.arange(TQ)[:, None]
    far = np.full((TQ, TQ), MAX_DISTANCE)
    codes = np.stack([_codes(dt, 0, WINDOW), _codes(TQ + dt, 0, WINDOW), _codes(far, 0, big),
                      _codes(2 * TQ + dt, 0, WINDOW)])
    t = _bias_tables(rel_bias, codes, [LOG2E] * 4)
    t = t.reshape(4, G, HPG // 2, 2, TQ, TQ).transpose(1, 3, 0, 4, 2, 5).reshape(G, 2, 4, TQ, 2 * TQ)
    return cb, t[:, :, 0:3], jnp.concatenate([t[:, :, 0:2], t[:, :, 3:4]], axis=2)


def _pad_cols(w, n):
    return jnp.pad(w, ((0, 0), (0, n - w.shape[1])))


def _prep_w_in(w):
    D = w.shape[0]
    za, zb, zc, zg = jnp.split(w, np.cumsum([A_COLS, B_COLS, C_COLS]).tolist(), axis=1)
    a_parts = jnp.split(za, np.cumsum([A_Q] + [A_KV] * 6).tolist(), axis=1)
    qa, kva, kr = jnp.split(zb, [Q_LORA, Q_LORA + KV_LORA], axis=1)
    half = ROPE_DIM // 2
    kr_rot = jnp.concatenate([-kr[:, half:], kr[:, :half]], axis=1)
    z64 = jnp.zeros((D, NOPE_DIM), w.dtype)
    z32 = jnp.zeros((D, LANES - NOPE_DIM - ROPE_DIM), w.dtype)
    c_r, c_k, c_v, c_wd, c_ad, c_gd = jnp.split(
        zc, np.cumsum([C_WIDTH] * 3 + [DECAY_LORA, AAA_LORA]).tolist(), axis=1)
    cols = [a_parts[0], qa] + a_parts[1:7] + [
        _pad_cols(a_parts[7], LANES), kva,
        jnp.concatenate([z64, kr, z32], axis=1),
        jnp.concatenate([z64, kr_rot, z32], axis=1),
        c_r, c_k, c_v, c_wd, c_ad, c_gd, jnp.zeros((D, Z_ZG - Z_CG - LANES), w.dtype), zg]
    out = jnp.concatenate(cols, axis=1)
    assert out.shape[1] == Z_COLS
    return out.astype(BF16)


def _prep_mla(w_uq, w_ukv):
    dq = NOPE_DIM + ROPE_DIM
    half = ROPE_DIM // 2
    wq = w_uq.reshape(Q_LORA, B_HEADS, dq)
    nope, r1, r2 = wq[..., :NOPE_DIM], wq[..., NOPE_DIM:NOPE_DIM + half], wq[..., NOPE_DIM + half:]
    zq = jnp.zeros((Q_LORA, B_HEADS, LANES - dq), w_uq.dtype)
    wqa = jnp.concatenate([nope, r1, r2, zq], axis=-1).reshape(Q_LORA, B_HEADS * LANES)
    wqb = jnp.concatenate([jnp.zeros_like(nope), -r2, r1, zq], axis=-1).reshape(Q_LORA, B_HEADS * LANES)
    wkv = w_ukv.reshape(KV_LORA, B_HEADS, NOPE_DIM + V_DIM)
    kn, vv = wkv[..., :NOPE_DIM], wkv[..., NOPE_DIM:]
    wk = jnp.concatenate([kn, jnp.zeros_like(kn)], axis=-1).reshape(KV_LORA, B_HEADS * LANES)
    wv = vv.reshape(KV_LORA, B_HEADS * V_DIM)
    return wqa.T.astype(BF16), wqb.T.astype(BF16), wk.astype(BF16), wv.T.astype(BF16)


def _rope_tables(S):
    half = ROPE_DIM // 2
    inv = ROPE_THETA ** (-jnp.arange(half, dtype=F32) / half)
    ang = jnp.arange(S, dtype=F32)[:, None] * inv
    cos2 = jnp.tile(jnp.cos(ang), (1, 2))
    sin2 = jnp.tile(jnp.sin(ang), (1, 2))
    scale = (NOPE_DIM + ROPE_DIM) ** -0.5 * LOG2E
    one = jnp.ones((S, NOPE_DIM), F32)
    z64 = jnp.zeros((S, NOPE_DIM), F32)
    z32 = jnp.zeros((S, LANES - NOPE_DIM - ROPE_DIM), F32)
    caq = (jnp.concatenate([one, cos2, z32], axis=1) * scale).T
    cbq = (jnp.concatenate([z64, sin2, z32], axis=1) * scale).T
    cak = jnp.concatenate([z64, cos2, z32], axis=1)
    cbk = jnp.concatenate([z64, sin2, z32], axis=1)
    return caq, cbq, cak, cbk


def _prep_compress(cmp_pos, cmp_w1, cmp_w2):
    Dh = A_HEAD_DIM
    w1 = cmp_w1.reshape(2, CMP_BLOCK, Dh, Dh)
    z = jnp.zeros_like(w1)
    w1bd = jnp.concatenate([jnp.concatenate([w1, z], axis=-1), jnp.concatenate([z, w1], axis=-1)], axis=-2)
    posrow = jnp.broadcast_to(cmp_pos.reshape(2, 1, CMP_BLOCK * Dh), (2, 8, CMP_BLOCK * Dh))
    w1cat = jnp.concatenate([cmp_w1, cmp_w1], axis=-1)
    zz = jnp.zeros((2, Dh, Dh), cmp_w2.dtype)
    blk = lambda a, b, c, d: jnp.concatenate(
        [jnp.concatenate([a, b], axis=-1), jnp.concatenate([c, d], axis=-1)], axis=-2)
    w2v = jnp.stack([jnp.stack([blk(cmp_w2, zz, zz, zz), blk(zz, cmp_w2, zz, zz)], axis=1),
                     jnp.stack([blk(zz, zz, cmp_w2, zz), blk(zz, zz, zz, cmp_w2)], axis=1)], axis=1)
    return w1bd.astype(BF16), posrow, w1cat.astype(BF16), w2v.astype(BF16)


def _gate_expand():
    e = np.zeros((3, LANES, A_Q), np.float32)
    for h in range(A_HEADS):
        for r in range(3):
            e[r, 3 * h + r, h * A_HEAD_DIM:(h + 1) * A_HEAD_DIM] = 1.0
    return jnp.asarray(e, dtype=BF16)


def _block_diag_ones(scale):
    idx = np.arange(C_WIDTH) // C_HEAD_DIM
    return jnp.asarray((idx[:, None] == idx[None, :]).astype(np.float32) * scale, dtype=BF16)


def kernel(x, c, rel_bias, ada_w, ada_b, norm_gain, w_in, nsa_cmp_pos, nsa_cmp_w1, nsa_cmp_w2, mla_q_norm, mla_kv_norm, mla_w_uq, mla_w_ukv, rwkv_mu, rwkv_w0, rwkv_a0, rwkv_k_k, rwkv_k_a, rwkv_w2, rwkv_a2, rwkv_g2, rwkv_r_k, rwkv_ln, w_branch, w_out, ffn_up, ffn_conv_w, ffn_conv_b, ffn_down):
    B, S, D = x.shape
    L = ada_w.shape[0]
    assert S % TQ == 0 and S % TM == 0 and S // SLC_BLOCK <= A_HEAD_DIM and S >= 2 * TQ
    mod = _adaln(c, ada_w, ada_b)
    cb, slc_tab, win_tab = _nsa_tables(rel_bias, S)
    caq, cbq, cak, cbk = _rope_tables(S)
    gexp = _gate_expand()
    bd1 = _block_diag_ones(1.0)
    bdm = _block_diag_ones(1.0 / C_HEAD_DIM)
    row = lambda v: v.reshape(1, -1)
    x2 = x.reshape(B * S, D)
    for l in range(L):
        m6 = mod[l].reshape(B, 6, 1, D)
        sh1, sc1, gt1, sh2, sc2, gt2 = (m6[:, i] for i in range(6))
        z = _norm_mod_matmul(x2, norm_gain[l, 0], sc1, sh1, _prep_w_in(w_in[l]), S, tn=1024)
        kcp, vcp = _compress(z, *_prep_compress(nsa_cmp_pos[l], nsa_cmp_w1[l], nsa_cmp_w2[l]), B, S)
        oc, sb = _cmpsel(z, kcp, vcp, cb, B, S)
        os_ = _nsa_flash(z, sb, slc_tab, Z_AKS, Z_AVS, B, S, selected=True)
        ow = _nsa_flash(z, None, win_tab, Z_AKW, Z_AVW, B, S, selected=False)
        wqa, wqb, wk, wv = _prep_mla(mla_w_uq[l], mla_w_ukv[l])
        q, k, v = _mla_proj(z, row(mla_q_norm[l]), row(mla_kv_norm[l]), wqa, wqb, wk, wv,
                            caq, cbq, cak, cbk, B, S)
        yb = _mla_flash(q, k, v, B, S)
        mu = rwkv_mu[l]
        o = 3 * C_WIDTH
        mus = [row(mu[:C_WIDTH]), row(mu[C_WIDTH:2 * C_WIDTH]), row(mu[2 * C_WIDTH:o]),
               row(mu[o:o + LANES]), row(mu[o + LANES:])]
        w2p = jnp.concatenate([rwkv_w2[l], jnp.zeros_like(rwkv_a2[l])], axis=0).astype(BF16)
        a2p = jnp.concatenate([jnp.zeros_like(rwkv_w2[l]), rwkv_a2[l]], axis=0).astype(BF16)
        rr, k2, vv, kk, be, ld, gg, bo = _rwkv_prep(
            z, mus, row(rwkv_w0[l]), row(rwkv_a0[l]), row(rwkv_k_k[l]), row(rwkv_k_a[l]),
            row(rwkv_r_k[l]), w2p, a2p, rwkv_g2[l].astype(BF16), bd1, B, S)
        qa_, bkt, yu, vp, pcb = _rwkv_chunk(rr, k2, vv, kk, be, ld, B, S)
        yr = _rwkv_state(qa_, bkt, yu, vp, pcb, B, S)
        x2 = _merge(oc, os_, ow, z, yb, yr, bo, gg, x2, w_branch[l].astype(BF16), w_out[l].astype(BF16),
                    gexp, bdm, rwkv_ln[l], norm_gain[l, 1], gt1, S)
        a = _ffn_up(x2, norm_gain[l, 2], sc2, sh2, ffn_up[l].astype(BF16), ffn_conv_w[l], ffn_conv_b[l], S)
        x2 = _ffn_down(a, ffn_down[l].astype(BF16), x2, norm_gain[l, 3], gt2, S)
    return x2.reshape(B, S, D)
```

```python
import functools
import math

import jax
import jax.numpy as jnp
import numpy as np
from jax import lax
from jax.experimental import pallas as pl
from jax.experimental.pallas import tpu as pltpu

F32 = jnp.float32
BF16 = jnp.bfloat16

D_MODEL = 1024
DEPTH = 4
A_HEADS, A_KV_GROUPS, A_HEAD_DIM = 8, 2, 64
CMP_BLOCK, CMP_STRIDE = 32, 16
SLC_BLOCK, SLC_TOPK, N_LOCAL_BLOCKS = 64, 16, 2
WINDOW = 512
FORCE_SCORE = 1e9
B_HEADS, Q_LORA, KV_LORA, NOPE_DIM, ROPE_DIM, V_DIM = 8, 256, 128, 64, 32, 64
ROPE_THETA = 10000.0
C_HEADS, C_HEAD_DIM = 8, 64
C_WIDTH = C_HEADS * C_HEAD_DIM
DECAY_LORA, AAA_LORA, GATE_LORA = 64, 64, 128
GN_EPS = 64e-5
N_BUCKETS, MAX_DISTANCE = 32, 128
D_FF = 2816
EPS = 1e-6
NEG = -1e30
LOG2E = math.log2(math.e)

A_Q = A_HEADS * A_HEAD_DIM
A_KV = A_KV_GROUPS * A_HEAD_DIM
A_GATE = 3 * A_HEADS
A_COLS = A_Q + 6 * A_KV + A_GATE
B_COLS = Q_LORA + KV_LORA + ROPE_DIM
C_COLS = 3 * C_WIDTH + DECAY_LORA + AAA_LORA + GATE_LORA

LANES = 128
FFN_SUB = 256
VMEM_LIMIT = 48 * 1024 * 1024

Z_AQ = 0
Z_QA = 512
Z_AKC, Z_AVC, Z_AKS, Z_AVS, Z_AKW, Z_AVW = 768, 896, 1024, 1152, 1280, 1408
Z_AG = 1536
Z_KVA = 1664
Z_KRA = 1792
Z_KRB = 1920
Z_CR, Z_CK, Z_CV = 2048, 2560, 3072
Z_CWA = 3584
Z_CG = 3712
Z_ZG = 4096
Z_COLS = 7168

TQ = 256
TM = 512
CH = 64
RW_SUB = 4


def _cparams(*sem):
    return pltpu.CompilerParams(dimension_semantics=sem, vmem_limit_bytes=VMEM_LIMIT)


def _gelu(x):
    return 0.5 * x * (1.0 + jnp.tanh(0.7978845608028654 * (x + 0.044715 * (x * x * x))))


def _dot(a, b, **kw):
    return jnp.dot(a, b, preferred_element_type=F32, **kw)


def _split3(x):
    hi = x.astype(BF16)
    r = x - hi.astype(F32)
    mid = r.astype(BF16)
    return hi, mid, (r - mid.astype(F32)).astype(BF16)


def _dot3(x, w):
    return sum(_dot(t, w) for t in _split3(x))


def _dot2(x, w):
    hi, mid, _ = _split3(x)
    return _dot(hi, w) + _dot(mid, w)


def _dot_t(a, b):
    return lax.dot_general(a, b, (((1,), (1,)), ((), ())), preferred_element_type=F32)


def _adaln_kernel(c_ref, w_ref, b_ref, o_ref):
    c = c_ref[...]
    cond = c * jax.nn.sigmoid(c)
    o_ref[0] = _dot(cond.astype(BF16), w_ref[0].astype(BF16)) + b_ref[0]


def _adaln(c, ada_w, ada_b):
    L, D, N6 = ada_w.shape
    B = c.shape[0]
    tn = 1536
    return pl.pallas_call(
        _adaln_kernel,
        grid=(L, N6 // tn),
        in_specs=[pl.BlockSpec((B, D), lambda l, j: (0, 0)),
                  pl.BlockSpec((1, D, tn), lambda l, j: (l, 0, j)),
                  pl.BlockSpec((1, 1, tn), lambda l, j: (l, 0, j))],
        out_specs=pl.BlockSpec((1, B, tn), lambda l, j: (l, 0, j)),
        out_shape=jax.ShapeDtypeStruct((L, B, N6), F32),
        compiler_params=_cparams("parallel", "parallel"),
        name="adaln",
    )(c, ada_w, ada_b.reshape(L, 1, N6))


def _nmm_kernel(x_ref, g_ref, sc_ref, sh_ref, w_ref, o_ref, h_ref):
    @pl.when(pl.program_id(1) == 0)
    def _():
        x = x_ref[...]
        y = x * lax.rsqrt(jnp.mean(x * x, axis=-1, keepdims=True) + EPS)
        h = (y * g_ref[...]) * (1.0 + sc_ref[0]) + sh_ref[0]
        h_ref[...] = h.astype(BF16)

    o_ref[...] = _dot(h_ref[...], w_ref[...]).astype(o_ref.dtype)


def _norm_mod_matmul(x2, gain, sc, sh, w, S, tn, out_dtype=BF16):
    N, D = x2.shape
    NC = w.shape[1]
    tm = min(1024, S)
    tpb = S // tm
    return pl.pallas_call(
        _nmm_kernel,
        grid=(N // tm, NC // tn),
        in_specs=[pl.BlockSpec((tm, D), lambda i, j: (i, 0)),
                  pl.BlockSpec((1, D), lambda i, j: (0, 0)),
                  pl.BlockSpec((1, 1, D), lambda i, j: (i // tpb, 0, 0)),
                  pl.BlockSpec((1, 1, D), lambda i, j: (i // tpb, 0, 0)),
                  pl.BlockSpec((D, tn), lambda i, j: (0, j))],
        out_specs=pl.BlockSpec((tm, tn), lambda i, j: (i, j)),
        out_shape=jax.ShapeDtypeStruct((N, NC), out_dtype),
        scratch_shapes=[pltpu.VMEM((tm, D), BF16)],
        compiler_params=_cparams("parallel", "arbitrary"),
        name="norm_mod_matmul",
    )(x2, gain.reshape(1, D), sc, sh, w)


def _compress_kernel(zk_ref, zv_ref, w1_ref, pos_ref, w1c_ref, w2_ref, kc_ref, vc_ref, zf_ref, *, NC):
    for kind, (z_ref, o_ref) in enumerate(((zk_ref, kc_ref), (zv_ref, vc_ref))):
        pa = jnp.zeros((NC, LANES), F32)
        pb = jnp.zeros((NC, LANES), F32)
        zf_ref[...] = z_ref[...].astype(F32)
        for l in range(CMP_STRIDE):
            xl = zf_ref[pl.ds(l, NC, stride=CMP_STRIDE), :].astype(BF16)
            pa = pa + _dot(xl, w1_ref[kind, l])
            pb = pb + _dot(xl, w1_ref[kind, CMP_STRIDE + l])
        posb = _dot(pos_ref[kind].astype(BF16), w1c_ref[kind])[0:1, :]
        h = pa + pltpu.roll(pb, NC - 1, axis=0) + posb
        act = _gelu(h).astype(BF16)
        row = lax.broadcasted_iota(jnp.int32, (NC, LANES), 0)
        for g in range(A_KV_GROUPS):
            for eo in range(2):
                out = _dot(act, w2_ref[kind, g, eo])
                out = jnp.where(row < NC - 1, out, 0.0)
                o_ref[g, eo, pl.ds(0, NC), :] = jnp.zeros((NC, LANES), BF16)
                o_ref[g, eo, pl.ds(NC, NC), :] = out.astype(BF16)


def _compress(z, w1bd, posrow, w1cat, w2v, B, S):
    NC = S // CMP_STRIDE
    out = jax.ShapeDtypeStruct((B, A_KV_GROUPS, 2, 2 * NC, LANES), BF16)
    ospec = pl.BlockSpec((None, A_KV_GROUPS, 2, 2 * NC, LANES), lambda b: (b, 0, 0, 0, 0))
    full = lambda a: pl.BlockSpec(a.shape, lambda b: (0,) * a.ndim)
    return pl.pallas_call(
        functools.partial(_compress_kernel, NC=NC),
        grid=(B,),
        in_specs=[pl.BlockSpec((S, LANES), lambda b: (b, Z_AKC // LANES)),
                  pl.BlockSpec((S, LANES), lambda b: (b, Z_AVC // LANES)),
                  full(w1bd), full(posrow), full(w1cat), full(w2v)],
        out_specs=[ospec, ospec],
        out_shape=[out, out],
        scratch_shapes=[pltpu.VMEM((S, LANES), F32)],
        compiler_params=_cparams("parallel"),
        name="nsa_compress",
    )(z, z, w1bd, posrow, w1cat, w2v)


def _stack_pairs(zq, scale):
    q = zq.astype(F32) * scale
    return jnp.concatenate([q[:, :LANES], q[:, LANES:]], axis=0).astype(BF16)


def _cmpsel_kernel(zq_ref, kc_ref, vc_ref, cb_ref, oc_ref, sb_ref, *, NC, NS):
    qt = pl.program_id(2)
    r = TQ // CMP_STRIDE
    lhs = _stack_pairs(zq_ref[...], A_HEAD_DIM ** -0.5)
    st = pl.multiple_of(r * (qt + 1), 16)
    col = lax.broadcasted_iota(jnp.int32, (2 * TQ, NC), 1)
    exists = col >= NC - r * (qt + 1)
    acc = jnp.zeros((2 * TQ, LANES), F32)
    psum = jnp.zeros((TQ, NC), F32)
    for eo in range(2):
        kwin = kc_ref[eo, pl.ds(st, NC), :]
        vwin = vc_ref[eo, pl.ds(st, NC), :]
        logits = _dot_t(lhs, kwin) + cb_ref[eo]
        logits = jnp.where(exists, logits, NEG)
        m = jnp.max(logits, axis=-1, keepdims=True)
        p = jnp.where(logits > 0.1 * NEG, jnp.exp(logits - m), 0.0)
        l = jnp.maximum(jnp.sum(p, axis=-1, keepdims=True), 1e-30)
        p = p * (1.0 / l)
        acc = acc + _dot(p.astype(BF16), vwin)
        psum = psum + p[:TQ] + p[TQ:]
    oc_ref[:, :LANES] = acc[:TQ]
    oc_ref[:, LANES:] = acc[TQ:]

    NR = A_HEAD_DIM
    jj = lax.broadcasted_iota(jnp.int32, (NR, NC), 0)
    nn = lax.broadcasted_iota(jnp.int32, (NR, NC), 1) + (r * (qt + 1) - NC)
    delta = 4 * jj - nn
    mt = jnp.where((delta == 0) | (delta == 4), 1.0, 0.0) + jnp.where((delta >= 1) & (delta <= 3), 2.0, 0.0)
    mt = jnp.where(jj < NS, mt, 0.0)
    mtb = mt.astype(BF16)
    imp_t = sum(_dot_t(mtb, t) for t in _split3(psum))
    jb = lax.broadcasted_iota(jnp.int32, (NR, TQ), 0)
    tpos = qt * TQ + lax.broadcasted_iota(jnp.int32, (NR, TQ), 1)
    back = (tpos >> 6) - jb
    forced = (jb == 0) | ((back >= 0) & (back < N_LOCAL_BLOCKS))
    score = jnp.where(forced, FORCE_SCORE, jnp.where(back >= 0, imp_t, -1.0))
    score = jnp.where(jb < NS, score, -2.0)
    ngrp = -(-NS // 8)
    grp = [score[8 * m:8 * m + 8, :] for m in range(ngrp)]
    rank = [jnp.zeros((8, TQ), F32) for _ in range(ngrp)]
    j8 = lax.broadcasted_iota(jnp.int32, (8, TQ), 0)
    for jp in range(NS):
        row = score[jp:jp + 1, :]
        for m in range(ngrp):
            if m < jp // 8:
                beats = row > grp[m]
            elif m > jp // 8:
                beats = row >= grp[m]
            else:
                beats = (row > grp[m]) | ((row == grp[m]) & (j8 > jp - 8 * m))
            rank[m] = rank[m] + jnp.where(beats, 1.0, 0.0)
    rank = jnp.concatenate(rank + [jnp.zeros((NR - 8 * ngrp, TQ), F32)] * (NR > 8 * ngrp), axis=0)
    sel = (rank < float(min(SLC_TOPK, NS))) & (back >= 0) & (jb < NS)
    sb = jnp.where(sel, 0.0, jnp.where(jb < NS, NEG, 0.0))
    sb_ref[...] = jnp.concatenate([sb, jnp.zeros((LANES - NR, TQ), F32)], axis=0).astype(BF16)


def _cmpsel(z, kcp, vcp, cb, B, S):
    NC = S // CMP_STRIDE
    NS = S // SLC_BLOCK
    nq = S // TQ
    kspec = pl.BlockSpec((None, None, 2, 2 * NC, LANES), lambda b, g, q: (b, g, 0, 0, 0))
    return pl.pallas_call(
        functools.partial(_cmpsel_kernel, NC=NC, NS=NS),
        grid=(B, A_KV_GROUPS, nq),
        in_specs=[pl.BlockSpec((TQ, 2 * LANES), lambda b, g, q: (b * nq + q, g)),
                  kspec, kspec,
                  pl.BlockSpec((None, 2, 2 * TQ, NC), lambda b, g, q: (g, 0, 0, 0))],
        out_specs=[pl.BlockSpec((TQ, 2 * LANES), lambda b, g, q: (b * nq + q, g)),
                   pl.BlockSpec((None, None, LANES, TQ), lambda b, g, q: (b, g, 0, q))],
        out_shape=[jax.ShapeDtypeStruct((B * S, A_Q), F32),
                   jax.ShapeDtypeStruct((B, A_KV_GROUPS, LANES, S), BF16)],
        compiler_params=_cparams("parallel", "parallel", "parallel"),
        name="nsa_cmp_select",
    )(z, kcp, vcp, cb)


def _flash_update_t(s, vt, state, acc_ref):
    m, l = state
    m_new = jnp.maximum(m, jnp.max(s, axis=0, keepdims=True))
    p = jnp.exp2(s - m_new)
    alpha = jnp.exp2(m - m_new)
    acc_ref[...] = alpha * acc_ref[...] + _dot(vt, p.astype(BF16))
    return m_new, alpha * l + jnp.sum(p, axis=0, keepdims=True)


def _flash_pipeline(qk, softmax, lo, hi, state, s_ref, diag_in_loop):
    s_ref[0] = qk(0, lo)

    def step(kc, st):
        s_ref[1] = qk(1, kc)
        st0 = softmax(0, s_ref[0], kc, st[0], False)
        s_ref[0] = qk(0, jnp.minimum(kc + 1, hi))
        st1 = softmax(1, s_ref[1], kc, st[1], False)
        return st0, st1

    end = hi + 1 if diag_in_loop else hi
    npair = (end - lo) // 2
    st = lax.fori_loop(0, npair, lambda i, st: step(lo + 2 * i + 1, step(lo + 2 * i, st)), state)
    st = lax.fori_loop(lo + 2 * npair, end, step, st)
    if diag_in_loop:
        return st
    s_ref[1] = qk(1, hi)
    st0 = softmax(0, s_ref[0], hi, st[0], True)
    st1 = softmax(1, s_ref[1], hi, st[1], True)
    return st0, st1


def _split_heads_kv(slab, g):
    lane = lax.broadcasted_iota(jnp.int32, slab.shape, 1)
    own = jnp.where((lane < A_HEAD_DIM) == (g == 0), slab, 0.0)
    other = pltpu.roll(own, A_HEAD_DIM, axis=1)
    is0 = g == 0
    return jnp.where(is0, own, other), jnp.where(is0, other, own)


def _nsa_flash_kernel(*refs, S, selected):
    if selected:
        zq_ref, sbt_ref, zk_ref, zv_ref, tab_ref, o_ref, ke_ref, ko_ref, vt_ref, acc_ref, s_ref = refs
    else:
        zq_ref, zk_ref, zv_ref, tab_ref, o_ref, ke_ref, ko_ref, vt_ref, acc_ref, s_ref = refs
    g = pl.program_id(1)
    qt = pl.program_id(2)
    k_refs = (ke_ref, ko_ref)

    @pl.when(qt == 0)
    def _():
        ke, ko = _split_heads_kv(zk_ref[...].astype(F32), g)
        ke_ref[:, :LANES] = ke.astype(BF16)
        ko_ref[:, :LANES] = ko.astype(BF16)
        for c in range(S // TQ):
            vt_ref[c] = zv_ref[c * TQ:(c + 1) * TQ, :].astype(F32).T.astype(BF16)
        if selected:
            rowb = lax.broadcasted_iota(jnp.int32, (S, LANES), 0) >> 6
            lane = lax.broadcasted_iota(jnp.int32, (S, LANES), 1)
            onehot = jnp.where(rowb == lane, 1.0, 0.0).astype(BF16)
            ke_ref[:, LANES:] = onehot
            ko_ref[:, LANES:] = onehot

    zq = zq_ref[...].astype(F32) * (A_HEAD_DIM ** -0.5 * LOG2E)
    rhs = jnp.concatenate([zq[:, :LANES], zq[:, LANES:]], axis=0).T.astype(BF16)
    if selected:
        sbt = sbt_ref[...]
        rhs = jnp.concatenate([rhs, jnp.concatenate([sbt, sbt], axis=1)], axis=0)
    acc_ref[...] = jnp.zeros(acc_ref.shape, F32)
    vrow = pl.multiple_of(g * A_HEAD_DIM, A_HEAD_DIM)

    def qk(eo, kc):
        ks = pl.multiple_of(kc * TQ, TQ)
        return _dot(k_refs[eo][pl.ds(ks, TQ), :], rhs)

    def softmax(eo, s, kc, state, diag):
        ti = jnp.minimum(qt - kc, 2)
        vt = vt_ref[kc, pl.ds(vrow, A_HEAD_DIM), :]
        return _flash_update_t(s + tab_ref[eo, ti], vt, state, acc_ref.at[eo])

    init = (jnp.full((1, 2 * TQ), 0.5 * NEG, F32), jnp.zeros((1, 2 * TQ), F32))
    lo = 0 if selected else jnp.maximum(qt - 2, 0)
    carry = _flash_pipeline(qk, softmax, lo, qt, (init, init), s_ref, True)
    out_t = jnp.concatenate([acc_ref[0] * (1.0 / carry[0][1]), acc_ref[1] * (1.0 / carry[1][1])], axis=0)
    out = out_t.T
    o_ref[:, :LANES] = out[:TQ]
    o_ref[:, LANES:] = out[TQ:]


def _nsa_flash(z, sb, tab, zk_col, zv_col, B, S, selected):
    nq = S // TQ
    kw = 2 * LANES if selected else LANES
    in_specs = [pl.BlockSpec((TQ, 2 * LANES), lambda b, g, q: (b * nq + q, g))]
    args = [z]
    if selected:
        in_specs.append(pl.BlockSpec((None, None, LANES, TQ), lambda b, g, q: (b, g, 0, q)))
        args.append(sb)
    in_specs += [pl.BlockSpec((S, LANES), lambda b, g, q: (b, zk_col // LANES)),
                 pl.BlockSpec((S, LANES), lambda b, g, q: (b, zv_col // LANES)),
                 pl.BlockSpec((None, 2, 3, TQ, 2 * TQ), lambda b, g, q: (g, 0, 0, 0, 0))]
    args += [z, z, tab]
    return pl.pallas_call(
        functools.partial(_nsa_flash_kernel, S=S, selected=selected),
        grid=(B, A_KV_GROUPS, nq),
        in_specs=in_specs,
        out_specs=pl.BlockSpec((TQ, 2 * LANES), lambda b, g, q: (b * nq + q, g)),
        out_shape=jax.ShapeDtypeStruct((B * S, A_Q), F32),
        scratch_shapes=[pltpu.VMEM((S, kw), BF16), pltpu.VMEM((S, kw), BF16),
                        pltpu.VMEM((S // TQ, LANES, TQ), BF16),
                        pltpu.VMEM((2, A_HEAD_DIM, 2 * TQ), F32),
                        pltpu.VMEM((2, TQ, 2 * TQ), F32)],
        compiler_params=_cparams("parallel", "parallel", "arbitrary"),
        name="nsa_selected" if selected else "nsa_window",
    )(*args)


def _mla_proj_kernel(zqa_ref, zkva_ref, zkra_ref, zkrb_ref, qn_ref, kvn_ref, wqa_ref, wqb_ref,
                     wk_ref, wv_ref, caq_ref, cbq_ref, cak_ref, cbk_ref, q_ref, k_ref, v_ref):
    def rms(x, gain):
        return x * lax.rsqrt(jnp.mean(x * x, axis=-1, keepdims=True) + EPS) * gain

    nq = rms(zqa_ref[...].astype(F32), qn_ref[...]).astype(BF16)
    nkv = rms(zkva_ref[...].astype(F32), kvn_ref[...]).astype(BF16)
    qa = _dot_t(wqa_ref[...], nq)
    qb = _dot_t(wqb_ref[...], nq)
    kn = _dot(nkv, wk_ref[...])
    vt = _dot_t(wv_ref[...], nkv)
    caq, cbq = caq_ref[...], cbq_ref[...]
    kr = zkra_ref[...].astype(F32) * cak_ref[...] + zkrb_ref[...].astype(F32) * cbk_ref[...]
    for h in range(B_HEADS):
        sl = slice(h * LANES, (h + 1) * LANES)
        q_ref[h] = (qa[sl, :] * caq + qb[sl, :] * cbq).astype(BF16)
        k_ref[h] = (kn[:, sl] + kr).astype(BF16)
        v_ref[h] = vt[h * V_DIM:(h + 1) * V_DIM, :].astype(BF16)


def _mla_proj(z, qn, kvn, wqa, wqb, wk, wv, caq, cbq, cak, cbk, B, S):
    tm = TM
    nt = S // tm
    zspec = lambda w, col: pl.BlockSpec((tm, w), lambda b, i: (b * nt + i, col // w))
    full = lambda a: pl.BlockSpec(a.shape, lambda b, i: (0,) * a.ndim)
    tspec = pl.BlockSpec((tm, LANES), lambda b, i: (i, 0))
    tspec_t = pl.BlockSpec((LANES, tm), lambda b, i: (0, i))
    return pl.pallas_call(
        _mla_proj_kernel,
        grid=(B, nt),
        in_specs=[zspec(Q_LORA, Z_QA), zspec(LANES, Z_KVA), zspec(LANES, Z_KRA), zspec(LANES, Z_KRB),
                  full(qn), full(kvn), full(wqa), full(wqb), full(wk), full(wv),
                  tspec_t, tspec_t, tspec, tspec],
        out_specs=[pl.BlockSpec((None, B_HEADS, LANES, tm), lambda b, i: (b, 0, 0, i)),
                   pl.BlockSpec((None, B_HEADS, tm, LANES), lambda b, i: (b, 0, i, 0)),
                   pl.BlockSpec((None, B_HEADS, None, V_DIM, tm), lambda b, i: (b, 0, i, 0, 0))],
        out_shape=[jax.ShapeDtypeStruct((B, B_HEADS, LANES, S), BF16),
                   jax.ShapeDtypeStruct((B, B_HEADS, S, LANES), BF16),
                   jax.ShapeDtypeStruct((B, B_HEADS, nt, V_DIM, tm), BF16)],
        compiler_params=_cparams("parallel", "parallel"),
        name="mla_proj",
    )(z, z, z, z, qn, kvn, wqa, wqb, wk, wv, caq, cbq, cak, cbk)


def _mla_flash_kernel(qt_ref, k_ref, vt_ref, o_ref, acc_ref, s_ref):
    qi = pl.program_id(2)
    acc_ref[...] = jnp.zeros(acc_ref.shape, F32)
    ri = lax.broadcasted_iota(jnp.int32, (TM, TM), 0)
    ci = lax.broadcasted_iota(jnp.int32, (TM, TM), 1)
    causal = ri <= ci

    def qk(hh, kc):
        ks = pl.multiple_of(kc * TM, TM)
        return _dot(k_ref[hh, pl.ds(ks, TM), :], qt_ref[hh])

    def softmax(hh, s, kc, state, diag):
        if diag:
            s = jnp.where(causal, s, NEG)
        return _flash_update_t(s, vt_ref[hh, kc], state, acc_ref.at[hh])

    init = (jnp.full((1, TM), 0.5 * NEG, F32), jnp.zeros((1, TM), F32))
    carry = _flash_pipeline(qk, softmax, 0, qi, (init, init), s_ref, False)
    out_t = jnp.concatenate([acc_ref[0] * (1.0 / carry[0][1]), acc_ref[1] * (1.0 / carry[1][1])], axis=0)
    o_ref[...] = out_t.T


def _mla_flash(qt, k, vt, B, S):
    nq = S // TM
    hp = B_HEADS // 2
    return pl.pallas_call(
        _mla_flash_kernel,
        grid=(B, hp, nq),
        in_specs=[pl.BlockSpec((None, 2, LANES, TM), lambda b, h, i: (b, h, 0, i)),
                  pl.BlockSpec((None, 2, S, LANES), lambda b, h, i: (b, h, 0, 0)),
                  pl.BlockSpec((None, 2, nq, V_DIM, TM), lambda b, h, i: (b, h, 0, 0, 0))],
        out_specs=pl.BlockSpec((TM, LANES), lambda b, h, i: (b * nq + i, h)),
        out_shape=jax.ShapeDtypeStruct((B * S, B_HEADS * V_DIM), F32),
        scratch_shapes=[pltpu.VMEM((2, V_DIM, TM), F32), pltpu.VMEM((2, TM, TM), F32)],
        compiler_params=_cparams("parallel", "parallel", "arbitrary"),
        name="mla_flash",
    )(qt, k, vt)


def _rwkv_prep_kernel(zr_ref, zk_ref, zv_ref, zwa_ref, zg_ref, mu_r, mu_k, mu_v, mu_wa, mu_g,
                      w0_ref, a0_ref, kk_ref, ka_ref, rk_ref, w2_ref, a2_ref, g2_ref, bd_ref,
                      r_o, k_o, v_o, kk_o, be_o, ld_o, g_o, bo_o,
                      c_r, c_k, c_v, c_wa, c_g):
    t = pl.program_id(1)
    tm = zr_ref.shape[0]

    def shifted(z_ref, mu_ref, c_ref):
        x = z_ref[...].astype(F32)
        row = lax.broadcasted_iota(jnp.int32, x.shape, 0)
        prev = jnp.where(t == 0, 0.0, c_ref[0:1, :])
        xs = jnp.where(row == 0, prev, pltpu.roll(x, 1, axis=0))
        c_ref[0:1, :] = x[tm - 1:tm, :]
        return x + (xs - x) * mu_ref[...]

    r = shifted(zr_ref, mu_r, c_r)
    k = shifted(zk_ref, mu_k, c_k)
    v = shifted(zv_ref, mu_v, c_v)
    wa = shifted(zwa_ref, mu_wa, c_wa)
    gd = shifted(zg_ref, mu_g, c_g)
    w = w0_ref[...] + _dot(jnp.tanh(wa).astype(BF16), w2_ref[...])
    ld_o[...] = -jax.nn.sigmoid(w) * math.exp(-0.5)
    a = jax.nn.sigmoid(a0_ref[...] + _dot(wa.astype(BF16), a2_ref[...]))
    g_o[...] = _dot(jax.nn.sigmoid(gd).astype(BF16), g2_ref[...])
    kk = k * kk_ref[...]
    nsq = _dot3(kk * kk, bd_ref[...])
    kk = kk / jnp.maximum(jnp.sqrt(nsq), 1e-12)
    k2 = k * (1.0 + (a - 1.0) * ka_ref[...])
    rks = _dot3(r * k2 * rk_ref[...], bd_ref[...])
    r_o[...] = r
    k_o[...] = k2
    v_o[...] = v
    kk_o[...] = kk
    be_o[...] = kk * a
    bo_o[...] = rks * v


def _rwkv_prep(z, mus, w0, a0, k_k, k_a, r_k, w2p, a2p, g2, bd, B, S, tm=256):
    nt = S // tm
    W = C_WIDTH
    zspec = lambda w, col: pl.BlockSpec((tm, w), lambda b, i: (b * nt + i, col // w))
    full = lambda a: pl.BlockSpec(a.shape, lambda b, i: (0,) * a.ndim)
    ospec = pl.BlockSpec((tm, W), lambda b, i: (b * nt + i, 0))
    oshape = jax.ShapeDtypeStruct((B * S, W), F32)
    consts = list(mus) + [w0, a0, k_k, k_a, r_k, w2p, a2p, g2, bd]
    return pl.pallas_call(
        _rwkv_prep_kernel,
        grid=(B, nt),
        in_specs=[zspec(W, Z_CR), zspec(W, Z_CK), zspec(W, Z_CV), zspec(LANES, Z_CWA), zspec(LANES, Z_CG)]
                 + [full(a) for a in consts],
        out_specs=[ospec] * 8,
        out_shape=[oshape] * 8,
        scratch_shapes=[pltpu.VMEM((8, W), F32)] * 3 + [pltpu.VMEM((8, LANES), F32)] * 2,
        compiler_params=_cparams("parallel", "arbitrary"),
        name="rwkv_prep",
    )(z, z, z, z, z, *consts)


def _rwkv_chunk_kernel(r_ref, k_ref, v_ref, kk_ref, be_ref, ld_ref,
                       qa_o, bkt_o, yu_o, vp_o, pcb_o):
    C = CH
    R = ld_ref.shape[0]
    ld = ld_ref[...]
    rr = lax.broadcasted_iota(jnp.int32, (R, R), 0)
    cc = lax.broadcasted_iota(jnp.int32, (R, R), 1)
    sh = int(math.log2(C))
    tri = jnp.where((cc <= rr) & ((cc >> sh) == (rr >> sh)), 1.0, 0.0).astype(BF16)
    cs = sum(_dot(tri, t) for t in _split3(ld))
    cl = jnp.concatenate([jnp.broadcast_to(cs[(s + 1) * C - 1:(s + 1) * C, :], (C, C_WIDTH))
                          for s in range(R // C)], axis=0)
    ex, exn, exx, exc = jnp.exp(cs), jnp.exp(-cs), jnp.exp(cs - ld), jnp.exp(cl - cs)
    kk, be, k2, v = kk_ref[...], be_ref[...], k_ref[...], v_ref[...]
    at = -kk * exx
    rt = r_ref[...] * ex
    bt = be * exn
    kt = k2 * exn
    btc = be * exc
    ktc = k2 * exc
    pc = jnp.exp(cl)
    ri = lax.broadcasted_iota(jnp.int32, (C, C), 0)
    ci = lax.broadcasted_iota(jnp.int32, (C, C), 1)
    incl = ci <= ri
    strict = ci < ri
    eye = jnp.where(ci == ri, 1.0, 0.0)
    lane = lax.broadcasted_iota(jnp.int32, (C, LANES), 1)
    P = C_HEADS // 2
    pairs = [(s, p) for s in range(R // C) for p in range(P)]
    heads = [(i, hh) for i in range(len(pairs)) for hh in range(2)]
    blk = lambda x, i: x[pairs[i][0] * C:(pairs[i][0] + 1) * C, pairs[i][1] * LANES:(pairs[i][1] + 1) * LANES]
    lhs2 = [jnp.concatenate([blk(at, i), blk(rt, i)], axis=0).astype(BF16) for i in range(len(pairs))]
    bm, km, vm, am = [], [], [], []
    for i, hh in heads:
        msk = (lane < C_HEAD_DIM) if hh == 0 else (lane >= C_HEAD_DIM)
        bm.append(jnp.where(msk, blk(bt, i), 0.0).astype(BF16))
        km.append(jnp.where(msk, blk(kt, i), 0.0).astype(BF16))
        vm.append(jnp.where(msk, blk(v, i), 0.0).astype(BF16))
        am.append(jnp.where(msk, blk(at, i), 0.0).astype(BF16))
    g1 = [_dot_t(lhs2[i], bm[j]) for j, (i, hh) in enumerate(heads)]
    g2 = [_dot_t(lhs2[i], km[j]) for j, (i, hh) in enumerate(heads)]
    lab = [jnp.where(strict, g[:C], 0.0) for g in g1]
    mrb = [jnp.where(incl, g[C:], 0.0).astype(BF16) for g in g1]
    lak = [jnp.where(strict, g[:C], 0.0).astype(BF16) for g in g2]
    mrk = [jnp.where(incl, g[C:], 0.0).astype(BF16) for g in g2]
    w2 = [_dot(a, b) for a, b in zip(lak, vm)]
    yk = [_dot(a, b) for a, b in zip(mrk, vm)]
    tinv = [eye + x for x in lab]
    lp = lab
    for _ in range(int(math.log2(C)) - 1):
        lpb = [x.astype(BF16) for x in lp]
        lp = [_dot(x, x) for x in lpb]
        tinv = [t + _dot(t.astype(BF16), x.astype(BF16)) for t, x in zip(tinv, lp)]
    au = [_dot(t.astype(BF16), jnp.concatenate([a, w.astype(BF16)], axis=1)) for t, a, w in zip(tinv, am, w2)]
    qy = [_dot(m, x.astype(BF16)) for m, x in zip(mrb, au)]
    for i, (s, p) in enumerate(pairs):
        e, o = 2 * i, 2 * i + 1
        ahat = au[e][:, :LANES] + au[o][:, :LANES]
        uhat = au[e][:, LANES:] + au[o][:, LANES:]
        qhat = blk(rt, i) + qy[e][:, :LANES] + qy[o][:, :LANES]
        yi = qy[e][:, LANES:] + qy[o][:, LANES:] + yk[e] + yk[o]
        qa_o[s, p] = jnp.concatenate([qhat, ahat], axis=0).astype(BF16)
        yu_o[s, p] = jnp.concatenate([yi, uhat], axis=0)
        bkt_o[s, p] = jnp.concatenate([blk(btc, i), blk(ktc, i)], axis=0).T.astype(BF16)
        vp_o[s, p] = blk(v, i).astype(BF16)
        pcb_o[s, p] = jnp.concatenate([blk(pc, i), blk(pc, i)], axis=0).T


def _rwkv_chunk(r, k2, v, kk, be, ld, B, S):
    nch = S // CH
    W = C_WIDTH
    P = C_HEADS // 2
    ns = RW_SUB
    ispec = pl.BlockSpec((ns * CH, W), lambda b, c: (b * (nch // ns) + c, 0))
    sq = lambda rows: pl.BlockSpec((None, ns, P, rows, LANES), lambda b, c: (b, c, 0, 0, 0))
    shp = lambda rows, dt: jax.ShapeDtypeStruct((B, nch, P, rows, LANES), dt)
    return pl.pallas_call(
        _rwkv_chunk_kernel,
        grid=(B, nch // ns),
        in_specs=[ispec] * 6,
        out_specs=[sq(2 * CH), sq(LANES), sq(2 * CH), sq(CH), sq(LANES)],
        out_shape=[shp(2 * CH, BF16), shp(LANES, BF16), shp(2 * CH, F32), shp(CH, BF16), shp(LANES, F32)],
        compiler_params=_cparams("parallel", "parallel"),
        name="rwkv_chunk",
    )(r, k2, v, kk, be, ld)


def _rwkv_state_kernel(qa_ref, bkt_ref, yu_ref, vp_ref, pcb_ref, y_ref, ap_ref):
    @pl.when(pl.program_id(1) == 0)
    def _():
        ap_ref[...] = jnp.zeros(ap_ref.shape, F32)

    ri = lax.broadcasted_iota(jnp.int32, (LANES, LANES), 0)
    ci = lax.broadcasted_iota(jnp.int32, (LANES, LANES), 1)
    same_head = (ri < C_HEAD_DIM) == (ci < C_HEAD_DIM)
    idx = [(b, p) for b in range(qa_ref.shape[0]) for p in range(C_HEADS // 2)]
    a = [ap_ref[b, p] for b, p in idx]
    for c in range(qa_ref.shape[1]):
        x = [_dot(qa_ref[b, c, p], s.astype(BF16)) for (b, p), s in zip(idx, a)]
        yu = [yu_ref[b, c, p] for b, p in idx]
        for (b, p), xi, yi in zip(idx, x, yu):
            y_ref[b, c * CH:(c + 1) * CH, p * LANES:(p + 1) * LANES] = xi[:CH] + yi[:CH]
        uv = [jnp.concatenate([(xi[CH:] + yi[CH:]).astype(BF16), vp_ref[b, c, p]], axis=0)
              for (b, p), xi, yi in zip(idx, x, yu)]
        upd = [_dot(bkt_ref[b, c, p], t) for (b, p), t in zip(idx, uv)]
        a = [pcb_ref[b, c, p] * s + jnp.where(same_head, t, 0.0) for (b, p), s, t in zip(idx, a, upd)]
    for (b, p), s in zip(idx, a):
        ap_ref[b, p] = s


def _rwkv_state(qa, bkt, yu, vp, pcb, B, S):
    nch = S // CH
    P = C_HEADS // 2
    nb = 2 if B % 2 == 0 else 1
    nc = 4 if nch % 4 == 0 else 1
    sq = lambda rows: pl.BlockSpec((nb, nc, P, rows, LANES), lambda b, c: (b, c, 0, 0, 0))
    return pl.pallas_call(
        _rwkv_state_kernel,
        grid=(B // nb, nch // nc),
        in_specs=[sq(2 * CH), sq(LANES), sq(2 * CH), sq(CH), sq(LANES)],
        out_specs=pl.BlockSpec((nb, nc * CH, C_WIDTH), lambda b, c: (b, c, 0)),
        out_shape=jax.ShapeDtypeStruct((B, S, C_WIDTH), F32),
        scratch_shapes=[pltpu.VMEM((nb, P, LANES, LANES), F32)],
        compiler_params=_cparams("parallel", "arbitrary"),
        name="rwkv_state",
    )(qa, bkt, yu, vp, pcb).reshape(B * S, C_WIDTH)


def _merge_kernel(oc_ref, os_ref, ow_ref, zag_ref, yb_ref, yr_ref, bo_ref, gg_ref,
                  zga_ref, zgb_ref, zgc_ref, x_ref, wbr_ref, wout_ref, ge_ref, bdm_ref,
                  ln_ref, gain_ref, gt_ref, o_ref):
    sg = jax.nn.sigmoid(zag_ref[...].astype(F32))
    ya = (_dot2(sg, ge_ref[0]) * oc_ref[...] + _dot2(sg, ge_ref[1]) * os_ref[...]
          + _dot2(sg, ge_ref[2]) * ow_ref[...])
    yr = yr_ref[...]
    mean = _dot3(yr, bdm_ref[...])
    d = yr - mean
    var = _dot2(d * d, bdm_ref[...])
    yn = d * lax.rsqrt(var + GN_EPS) * ln_ref[0:1, :] + ln_ref[1:2, :]
    yc = (yn + bo_ref[...]) * gg_ref[...]
    sig = lambda ref: jax.nn.sigmoid(ref[...].astype(F32))
    merged = (sig(zga_ref) * _dot(ya.astype(BF16), wbr_ref[0:A_Q, :])
              + sig(zgb_ref) * _dot(yb_ref[...].astype(BF16), wbr_ref[A_Q:2 * A_Q, :])
              + sig(zgc_ref) * _dot(yc.astype(BF16), wbr_ref[2 * A_Q:3 * A_Q, :]))
    y = _dot(merged.astype(BF16), wout_ref[...])
    yn2 = y * lax.rsqrt(jnp.mean(y * y, axis=-1, keepdims=True) + EPS) * gain_ref[...]
    o_ref[...] = x_ref[...] + gt_ref[0] * yn2


def _merge(oc, os_, ow, z, yb, yr, bo, gg, x2, wbr, wout, gexp, bdm, ln, gain, gt, S, tm=256):
    N, D = x2.shape
    tpb = S // tm
    W = A_Q
    row = lambda w: pl.BlockSpec((tm, w), lambda i: (i, 0))
    zspec = lambda w, col: pl.BlockSpec((tm, w), lambda i: (i, col // w))
    full = lambda a: pl.BlockSpec(a.shape, lambda i: (0,) * a.ndim)
    return pl.pallas_call(
        _merge_kernel,
        grid=(N // tm,),
        in_specs=[row(W), row(W), row(W), zspec(LANES, Z_AG), row(W), row(W), row(W), row(W),
                  zspec(D, Z_ZG), zspec(D, Z_ZG + D), zspec(D, Z_ZG + 2 * D), row(D),
                  full(wbr), full(wout), full(gexp), full(bdm), full(ln),
                  pl.BlockSpec((1, D), lambda i: (0, 0)),
                  pl.BlockSpec((1, 1, D), lambda i: (i // tpb, 0, 0))],
        out_specs=row(D),
        out_shape=jax.ShapeDtypeStruct((N, D), F32),
        compiler_params=_cparams("parallel"),
        name="merge_out",
    )(oc, os_, ow, z, yb, yr, bo, gg, z, z, z, x2, wbr, wout, gexp, bdm, ln, gain.reshape(1, D), gt)


def _ffn_up_kernel(x_ref, g_ref, sc_ref, sh_ref, wg_ref, wv_ref, cw_ref, cb_ref, o_ref,
                   cg_ref, cv_ref, *, tpb):
    i = pl.program_id(0)
    tm = x_ref.shape[0]
    fc = FFN_SUB
    first = (i % tpb) == 0
    x = x_ref[...]
    y = x * lax.rsqrt(jnp.mean(x * x, axis=-1, keepdims=True) + EPS)
    h = ((y * g_ref[...]) * (1.0 + sc_ref[0]) + sh_ref[0]).astype(BF16)
    row = lax.broadcasted_iota(jnp.int32, (8, fc), 0)

    def up(c):
        sl = slice(c * fc, (c + 1) * fc)
        return _dot(h, wg_ref[:, sl]), _dot(h, wv_ref[:, sl])

    def conv(u, c_ref, c, off):
        sl = slice(c * fc, (c + 1) * fc)
        wl = slice(off + c * fc, off + (c + 1) * fc)
        prev = c_ref[:, sl]
        p1 = jnp.where(first, 0.0, prev[7:8])
        p2 = jnp.where(first, 0.0, prev[6:7])
        c_ref[:, sl] = u[tm - 8:tm, :]
        r1 = pltpu.roll(u, 1, axis=0)
        r2 = pltpu.roll(u, 2, axis=0)
        t1 = jnp.where(row == 0, p1, r1[0:8])
        t2 = jnp.where(row == 0, p2, jnp.where(row == 1, p1, r2[0:8]))
        u1 = jnp.concatenate([t1, r1[8:]], axis=0)
        u2 = jnp.concatenate([t2, r2[8:]], axis=0)
        return cw_ref[0:1, wl] * u2 + cw_ref[1:2, wl] * u1 + cw_ref[2:3, wl] * u + cb_ref[:, wl]

    nsub = D_FF // fc
    cur = up(0)
    for c in range(nsub):
        nxt = up(c + 1) if c + 1 < nsub else None
        a = _gelu(conv(cur[0], cg_ref, c, 0)) * conv(cur[1], cv_ref, c, D_FF)
        o_ref[:, c * fc:(c + 1) * fc] = a.astype(BF16)
        cur = nxt


def _ffn_up(x2, gain, sc, sh, w_up, conv_w, conv_b, S, tm=256):
    N, D = x2.shape
    tpb = S // tm
    full = lambda a: pl.BlockSpec(a.shape, lambda i: (0,) * a.ndim)
    return pl.pallas_call(
        functools.partial(_ffn_up_kernel, tpb=tpb),
        grid=(N // tm,),
        in_specs=[pl.BlockSpec((tm, D), lambda i: (i, 0)),
                  pl.BlockSpec((1, D), lambda i: (0, 0)),
                  pl.BlockSpec((1, 1, D), lambda i: (i // tpb, 0, 0)),
                  pl.BlockSpec((1, 1, D), lambda i: (i // tpb, 0, 0)),
                  pl.BlockSpec((D, D_FF), lambda i: (0, 0)),
                  pl.BlockSpec((D, D_FF), lambda i: (0, 1)),
                  full(conv_w), pl.BlockSpec((1, 2 * D_FF), lambda i: (0, 0))],
        out_specs=pl.BlockSpec((tm, D_FF), lambda i: (i, 0)),
        out_shape=jax.ShapeDtypeStruct((N, D_FF), BF16),
        scratch_shapes=[pltpu.VMEM((8, D_FF), F32), pltpu.VMEM((8, D_FF), F32)],
        compiler_params=_cparams("arbitrary"),
        name="ffn_up",
    )(x2, gain.reshape(1, D), sc, sh, w_up, w_up, conv_w, conv_b.reshape(1, 2 * D_FF))


def _ffn_down_kernel(a_ref, wd_ref, x_ref, gain_ref, gt_ref, o_ref):
    f = _dot(a_ref[...], wd_ref[...])
    fn = f * lax.rsqrt(jnp.mean(f * f, axis=-1, keepdims=True) + EPS) * gain_ref[...]
    o_ref[...] = x_ref[...] + gt_ref[0] * fn


def _ffn_down(a, wd, x2, gain, gt, S, tm=512):
    N, D = x2.shape
    tpb = S // tm
    return pl.pallas_call(
        _ffn_down_kernel,
        grid=(N // tm,),
        in_specs=[pl.BlockSpec((tm, D_FF), lambda i: (i, 0)),
                  pl.BlockSpec((D_FF, D), lambda i: (0, 0)),
                  pl.BlockSpec((tm, D), lambda i: (i, 0)),
                  pl.BlockSpec((1, D), lambda i: (0, 0)),
                  pl.BlockSpec((1, 1, D), lambda i: (i // tpb, 0, 0))],
        out_specs=pl.BlockSpec((tm, D), lambda i: (i, 0)),
        out_shape=jax.ShapeDtypeStruct((N, D), F32),
        compiler_params=_cparams("parallel"),
        name="ffn_down",
    )(a, wd, x2, gain.reshape(1, D), gt)


def _t5_bucket_np(dist):
    d = np.maximum(dist, 0)
    max_exact = N_BUCKETS // 2
    large = max_exact + (np.log(np.maximum(d, 1).astype(np.float32) / max_exact)
                         / math.log(MAX_DISTANCE / max_exact) * (N_BUCKETS - max_exact)).astype(np.int32)
    return np.where(d < max_exact, d, np.minimum(large, N_BUCKETS - 1))


def _bias_table_kernel(scale_ref, rb_ref, code_ref, o_ref):
    code = code_ref[0]
    sc = scale_ref[pl.program_id(0)]
    for h in range(A_HEADS):
        acc = jnp.full(code.shape, NEG, F32)
        for b in range(N_BUCKETS):
            acc = jnp.where(code == b, rb_ref[b, h], acc)
        o_ref[0, h] = acc * sc


def _bias_tables(rel_bias, codes, scales):
    nt, R, C = codes.shape
    smem = pl.BlockSpec(memory_space=pltpu.SMEM)
    return pl.pallas_call(
        _bias_table_kernel,
        grid=(nt,),
        in_specs=[smem, smem, pl.BlockSpec((1, R, C), lambda t: (t, 0, 0))],
        out_specs=pl.BlockSpec((1, A_HEADS, R, C), lambda t: (t, 0, 0, 0)),
        out_shape=jax.ShapeDtypeStruct((nt, A_HEADS, R, C), F32),
        compiler_params=_cparams("parallel"),
        name="bias_tables",
    )(jnp.asarray(scales, F32), rel_bias, jnp.asarray(codes))


def _codes(dist, lo, hi):
    return np.where((dist >= lo) & (dist < hi), _t5_bucket_np(dist), N_BUCKETS).astype(np.int32)


def _nsa_tables(rel_bias, S):
    NC = S // CMP_STRIDE
    G, HPG = A_KV_GROUPS, A_HEADS // A_KV_GROUPS
    big = 1 << 30
    i = np.arange(TQ)[:, None]
    dc = i - CMP_STRIDE * np.arange(NC)[None, :] + CMP_STRIDE * NC - TQ - (CMP_BLOCK - 1)
    cb = _bias_tables(rel_bias, _codes(dc, 0, big)[None], [1.0])[0]
    cb = cb.reshape(G, HPG // 2, 2, TQ, NC).transpose(0, 2, 1, 3, 4).reshape(G, 2, 2 * TQ, NC)
    dt = np.arange(TQ)[None, :] - np.arange(TQ)[:, None]
    far = np.full((TQ, TQ), MAX_DISTANCE)
    codes = np.stack([_codes(dt, 0, WINDOW), _codes(TQ + dt, 0, WINDOW), _codes(far, 0, big),
                      _codes(2 * TQ + dt, 0, WINDOW)])
    t = _bias_tables(rel_bias, codes, [LOG2E] * 4)
    t = t.reshape(4, G, HPG // 2, 2, TQ, TQ).transpose(1, 3, 0, 4, 2, 5).reshape(G, 2, 4, TQ, 2 * TQ)
    return cb, t[:, :, 0:3], jnp.concatenate([t[:, :, 0:2], t[:, :, 3:4]], axis=2)


def _pad_cols(w, n):
    return jnp.pad(w, ((0, 0), (0, n - w.shape[1])))


def _prep_w_in(w):
    D = w.shape[0]
    za, zb, zc, zg = jnp.split(w, np.cumsum([A_COLS, B_COLS, C_COLS]).tolist(), axis=1)
    a_parts = jnp.split(za, np.cumsum([A_Q] + [A_KV] * 6).tolist(), axis=1)
    qa, kva, kr = jnp.split(zb, [Q_LORA, Q_LORA + KV_LORA], axis=1)
    half = ROPE_DIM // 2
    kr_rot = jnp.concatenate([-kr[:, half:], kr[:, :half]], axis=1)
    z64 = jnp.zeros((D, NOPE_DIM), w.dtype)
    z32 = jnp.zeros((D, LANES - NOPE_DIM - ROPE_DIM), w.dtype)
    c_r, c_k, c_v, c_wd, c_ad, c_gd = jnp.split(
        zc, np.cumsum([C_WIDTH] * 3 + [DECAY_LORA, AAA_LORA]).tolist(), axis=1)
    cols = [a_parts[0], qa] + a_parts[1:7] + [
        _pad_cols(a_parts[7], LANES), kva,
        jnp.concatenate([z64, kr, z32], axis=1),
        jnp.concatenate([z64, kr_rot, z32], axis=1),
        c_r, c_k, c_v, c_wd, c_ad, c_gd, jnp.zeros((D, Z_ZG - Z_CG - LANES), w.dtype), zg]
    out = jnp.concatenate(cols, axis=1)
    assert out.shape[1] == Z_COLS
    return out.astype(BF16)


def _prep_mla(w_uq, w_ukv):
    dq = NOPE_DIM + ROPE_DIM
    half = ROPE_DIM // 2
    wq = w_uq.reshape(Q_LORA, B_HEADS, dq)
    nope, r1, r2 = wq[..., :NOPE_DIM], wq[..., NOPE_DIM:NOPE_DIM + half], wq[..., NOPE_DIM + half:]
    zq = jnp.zeros((Q_LORA, B_HEADS, LANES - dq), w_uq.dtype)
    wqa = jnp.concatenate([nope, r1, r2, zq], axis=-1).reshape(Q_LORA, B_HEADS * LANES)
    wqb = jnp.concatenate([jnp.zeros_like(nope), -r2, r1, zq], axis=-1).reshape(Q_LORA, B_HEADS * LANES)
    wkv = w_ukv.reshape(KV_LORA, B_HEADS, NOPE_DIM + V_DIM)
    kn, vv = wkv[..., :NOPE_DIM], wkv[..., NOPE_DIM:]
    wk = jnp.concatenate([kn, jnp.zeros_like(kn)], axis=-1).reshape(KV_LORA, B_HEADS * LANES)
    wv = vv.reshape(KV_LORA, B_HEADS * V_DIM)
    return wqa.T.astype(BF16), wqb.T.astype(BF16), wk.astype(BF16), wv.T.astype(BF16)


def _rope_tables(S):
    half = ROPE_DIM // 2
    inv = ROPE_THETA ** (-jnp.arange(half, dtype=F32) / half)
    ang = jnp.arange(S, dtype=F32)[:, None] * inv
    cos2 = jnp.tile(jnp.cos(ang), (1, 2))
    sin2 = jnp.tile(jnp.sin(ang), (1, 2))
    scale = (NOPE_DIM + ROPE_DIM) ** -0.5 * LOG2E
    one = jnp.ones((S, NOPE_DIM), F32)
    z64 = jnp.zeros((S, NOPE_DIM), F32)
    z32 = jnp.zeros((S, LANES - NOPE_DIM - ROPE_DIM), F32)
    caq = (jnp.concatenate([one, cos2, z32], axis=1) * scale).T
    cbq = (jnp.concatenate([z64, sin2, z32], axis=1) * scale).T
    cak = jnp.concatenate([z64, cos2, z32], axis=1)
    cbk = jnp.concatenate([z64, sin2, z32], axis=1)
    return caq, cbq, cak, cbk


def _prep_compress(cmp_pos, cmp_w1, cmp_w2):
    Dh = A_HEAD_DIM
    w1 = cmp_w1.reshape(2, CMP_BLOCK, Dh, Dh)
    z = jnp.zeros_like(w1)
    w1bd = jnp.concatenate([jnp.concatenate([w1, z], axis=-1), jnp.concatenate([z, w1], axis=-1)], axis=-2)
    posrow = jnp.broadcast_to(cmp_pos.reshape(2, 1, CMP_BLOCK * Dh), (2, 8, CMP_BLOCK * Dh))
    w1cat = jnp.concatenate([cmp_w1, cmp_w1], axis=-1)
    zz = jnp.zeros((2, Dh, Dh), cmp_w2.dtype)
    blk = lambda a, b, c, d: jnp.concatenate(
        [jnp.concatenate([a, b], axis=-1), jnp.concatenate([c, d], axis=-1)], axis=-2)
    w2v = jnp.stack([jnp.stack([blk(cmp_w2, zz, zz, zz), blk(zz, cmp_w2, zz, zz)], axis=1),
                     jnp.stack([blk(zz, zz, cmp_w2, zz), blk(zz, zz, zz, cmp_w2)], axis=1)], axis=1)
    return w1bd.astype(BF16), posrow, w1cat.astype(BF16), w2v.astype(BF16)


def _gate_expand():
    e = np.zeros((3, LANES, A_Q), np.float32)
    for h in range(A_HEADS):
        for r in range(3):
            e[r, 3 * h + r, h * A_HEAD_DIM:(h + 1) * A_HEAD_DIM] = 1.0
    return jnp.asarray(e, dtype=BF16)


def _block_diag_ones(scale):
    idx = np.arange(C_WIDTH) // C_HEAD_DIM
    return jnp.asarray((idx[:, None] == idx[None, :]).astype(np.float32) * scale, dtype=BF16)


def kernel(x, c, rel_bias, ada_w, ada_b, norm_gain, w_in, nsa_cmp_pos, nsa_cmp_w1, nsa_cmp_w2, mla_q_norm, mla_kv_norm, mla_w_uq, mla_w_ukv, rwkv_mu, rwkv_w0, rwkv_a0, rwkv_k_k, rwkv_k_a, rwkv_w2, rwkv_a2, rwkv_g2, rwkv_r_k, rwkv_ln, w_branch, w_out, ffn_up, ffn_conv_w, ffn_conv_b, ffn_down):
    B, S, D = x.shape
    L = ada_w.shape[0]
    assert S % TQ == 0 and S % TM == 0 and S // SLC_BLOCK <= A_HEAD_DIM and S >= 2 * TQ
    mod = _adaln(c, ada_w, ada_b)
    cb, slc_tab, win_tab = _nsa_tables(rel_bias, S)
    caq, cbq, cak, cbk = _rope_tables(S)
    gexp = _gate_expand()
    bd1 = _block_diag_ones(1.0)
    bdm = _block_diag_ones(1.0 / C_HEAD_DIM)
    row = lambda v: v.reshape(1, -1)
    x2 = x.reshape(B * S, D)
    for l in range(L):
        m6 = mod[l].reshape(B, 6, 1, D)
        sh1, sc1, gt1, sh2, sc2, gt2 = (m6[:, i] for i in range(6))
        z = _norm_mod_matmul(x2, norm_gain[l, 0], sc1, sh1, _prep_w_in(w_in[l]), S, tn=Z_COLS // 4)
        kcp, vcp = _compress(z, *_prep_compress(nsa_cmp_pos[l], nsa_cmp_w1[l], nsa_cmp_w2[l]), B, S)
        oc, sb = _cmpsel(z, kcp, vcp, cb, B, S)
        os_ = _nsa_flash(z, sb, slc_tab, Z_AKS, Z_AVS, B, S, selected=True)
        ow = _nsa_flash(z, None, win_tab, Z_AKW, Z_AVW, B, S, selected=False)
        wqa, wqb, wk, wv = _prep_mla(mla_w_uq[l], mla_w_ukv[l])
        q, k, v = _mla_proj(z, row(mla_q_norm[l]), row(mla_kv_norm[l]), wqa, wqb, wk, wv,
                            caq, cbq, cak, cbk, B, S)
        yb = _mla_flash(q, k, v, B, S)
        mu = rwkv_mu[l]
        o = 3 * C_WIDTH
        mus = [row(mu[:C_WIDTH]), row(mu[C_WIDTH:2 * C_WIDTH]), row(mu[2 * C_WIDTH:o]),
               row(mu[o:o + LANES]), row(mu[o + LANES:])]
        w2p = jnp.concatenate([rwkv_w2[l], jnp.zeros_like(rwkv_a2[l])], axis=0).astype(BF16)
        a2p = jnp.concatenate([jnp.zeros_like(rwkv_w2[l]), rwkv_a2[l]], axis=0).astype(BF16)
        rr, k2, vv, kk, be, ld, gg, bo = _rwkv_prep(
            z, mus, row(rwkv_w0[l]), row(rwkv_a0[l]), row(rwkv_k_k[l]), row(rwkv_k_a[l]),
            row(rwkv_r_k[l]), w2p, a2p, rwkv_g2[l].astype(BF16), bd1, B, S)
        qa_, bkt, yu, vp, pcb = _rwkv_chunk(rr, k2, vv, kk, be, ld, B, S)
        yr = _rwkv_state(qa_, bkt, yu, vp, pcb, B, S)
        x2 = _merge(oc, os_, ow, z, yb, yr, bo, gg, x2, w_branch[l].astype(BF16), w_out[l].astype(BF16),
                    gexp, bdm, rwkv_ln[l], norm_gain[l, 1], gt1, S)
        a = _ffn_up(x2, norm_gain[l, 2], sc2, sh2, ffn_up[l].astype(BF16), ffn_conv_w[l], ffn_conv_b[l], S)
        x2 = _ffn_down(a, ffn_down[l].astype(BF16), x2, norm_gain[l, 3], gt2, S)
    return x2.reshape(B, S, D)
```

```python
import functools
import math

import jax
import jax.numpy as jnp
import numpy as np
from jax import lax
from jax.experimental import pallas as pl
from jax.experimental.pallas import tpu as pltpu

F32 = jnp.float32
BF16 = jnp.bfloat16

D_MODEL = 1024
DEPTH = 4
A_HEADS, A_KV_GROUPS, A_HEAD_DIM = 8, 2, 64
CMP_BLOCK, CMP_STRIDE = 32, 16
SLC_BLOCK, SLC_TOPK, N_LOCAL_BLOCKS = 64, 16, 2
WINDOW = 512
FORCE_SCORE = 1e9
B_HEADS, Q_LORA, KV_LORA, NOPE_DIM, ROPE_DIM, V_DIM = 8, 256, 128, 64, 32, 64
ROPE_THETA = 10000.0
C_HEADS, C_HEAD_DIM = 8, 64
C_WIDTH = C_HEADS * C_HEAD_DIM
DECAY_LORA, AAA_LORA, GATE_LORA = 64, 64, 128
GN_EPS = 64e-5
N_BUCKETS, MAX_DISTANCE = 32, 128
D_FF = 2816
EPS = 1e-6
NEG = -1e30
LOG2E = math.log2(math.e)

A_Q = A_HEADS * A_HEAD_DIM
A_KV = A_KV_GROUPS * A_HEAD_DIM
A_GATE = 3 * A_HEADS
A_COLS = A_Q + 6 * A_KV + A_GATE
B_COLS = Q_LORA + KV_LORA + ROPE_DIM
C_COLS = 3 * C_WIDTH + DECAY_LORA + AAA_LORA + GATE_LORA

LANES = 128
FFN_SUB = 256
VMEM_LIMIT = 48 * 1024 * 1024

Z_AQ = 0
Z_QA = 512
Z_AKC, Z_AVC, Z_AKS, Z_AVS, Z_AKW, Z_AVW = 768, 896, 1024, 1152, 1280, 1408
Z_AG = 1536
Z_KVA = 1664
Z_KRA = 1792
Z_KRB = 1920
Z_CR, Z_CK, Z_CV = 2048, 2560, 3072
Z_CWA = 3584
Z_CG = 3712
Z_ZG = 4096
Z_COLS = 7168

TQ = 256
TM = 512
CH = 64
RW_SUB = 4


def _cparams(*sem):
    return pltpu.CompilerParams(dimension_semantics=sem, vmem_limit_bytes=VMEM_LIMIT)


def _gelu(x):
    return 0.5 * x * (1.0 + jnp.tanh(0.7978845608028654 * (x + 0.044715 * (x * x * x))))


def _dot(a, b, **kw):
    return jnp.dot(a, b, preferred_element_type=F32, **kw)


def _split3(x):
    hi = x.astype(BF16)
    r = x - hi.astype(F32)
    mid = r.astype(BF16)
    return hi, mid, (r - mid.astype(F32)).astype(BF16)


def _dot3(x, w):
    return sum(_dot(t, w) for t in _split3(x))


def _dot2(x, w):
    hi, mid, _ = _split3(x)
    return _dot(hi, w) + _dot(mid, w)


def _dot_t(a, b):
    return lax.dot_general(a, b, (((1,), (1,)), ((), ())), preferred_element_type=F32)


def _adaln_kernel(c_ref, w_ref, b_ref, o_ref):
    c = c_ref[...]
    cond = c * jax.nn.sigmoid(c)
    o_ref[0] = _dot(cond.astype(BF16), w_ref[0].astype(BF16)) + b_ref[0]


def _adaln(c, ada_w, ada_b):
    L, D, N6 = ada_w.shape
    B = c.shape[0]
    tn = 1536
    return pl.pallas_call(
        _adaln_kernel,
        grid=(L, N6 // tn),
        in_specs=[pl.BlockSpec((B, D), lambda l, j: (0, 0)),
                  pl.BlockSpec((1, D, tn), lambda l, j: (l, 0, j)),
                  pl.BlockSpec((1, 1, tn), lambda l, j: (l, 0, j))],
        out_specs=pl.BlockSpec((1, B, tn), lambda l, j: (l, 0, j)),
        out_shape=jax.ShapeDtypeStruct((L, B, N6), F32),
        compiler_params=_cparams("parallel", "parallel"),
        name="adaln",
    )(c, ada_w, ada_b.reshape(L, 1, N6))


def _nmm_kernel(x_ref, g_ref, sc_ref, sh_ref, w_ref, o_ref, h_ref):
    @pl.when(pl.program_id(1) == 0)
    def _():
        x = x_ref[...]
        y = x * lax.rsqrt(jnp.mean(x * x, axis=-1, keepdims=True) + EPS)
        h = (y * g_ref[...]) * (1.0 + sc_ref[0]) + sh_ref[0]
        h_ref[...] = h.astype(BF16)

    o_ref[...] = _dot(h_ref[...], w_ref[...]).astype(o_ref.dtype)


def _norm_mod_matmul(x2, gain, sc, sh, w, S, tn, out_dtype=BF16):
    N, D = x2.shape
    NC = w.shape[1]
    tm = min(1024, S)
    tpb = S // tm
    return pl.pallas_call(
        _nmm_kernel,
        grid=(N // tm, NC // tn),
        in_specs=[pl.BlockSpec((tm, D), lambda i, j: (i, 0)),
                  pl.BlockSpec((1, D), lambda i, j: (0, 0)),
                  pl.BlockSpec((1, 1, D), lambda i, j: (i // tpb, 0, 0)),
                  pl.BlockSpec((1, 1, D), lambda i, j: (i // tpb, 0, 0)),
                  pl.BlockSpec((D, tn), lambda i, j: (0, j))],
        out_specs=pl.BlockSpec((tm, tn), lambda i, j: (i, j)),
        out_shape=jax.ShapeDtypeStruct((N, NC), out_dtype),
        scratch_shapes=[pltpu.VMEM((tm, D), BF16)],
        compiler_params=_cparams("parallel", "arbitrary"),
        name="norm_mod_matmul",
    )(x2, gain.reshape(1, D), sc, sh, w)


def _compress_kernel(zk_ref, zv_ref, w1_ref, pos_ref, w1c_ref, w2_ref, kc_ref, vc_ref, zf_ref, *, NC):
    for kind, (z_ref, o_ref) in enumerate(((zk_ref, kc_ref), (zv_ref, vc_ref))):
        pa = jnp.zeros((NC, LANES), F32)
        pb = jnp.zeros((NC, LANES), F32)
        zf_ref[...] = z_ref[...].astype(F32)
        for l in range(CMP_STRIDE):
            xl = zf_ref[pl.ds(l, NC, stride=CMP_STRIDE), :].astype(BF16)
            pa = pa + _dot(xl, w1_ref[kind, l])
            pb = pb + _dot(xl, w1_ref[kind, CMP_STRIDE + l])
        posb = _dot(pos_ref[kind].astype(BF16), w1c_ref[kind])[0:1, :]
        h = pa + pltpu.roll(pb, NC - 1, axis=0) + posb
        act = _gelu(h).astype(BF16)
        row = lax.broadcasted_iota(jnp.int32, (NC, LANES), 0)
        for g in range(A_KV_GROUPS):
            for eo in range(2):
                out = _dot(act, w2_ref[kind, g, eo])
                out = jnp.where(row < NC - 1, out, 0.0)
                o_ref[g, eo, pl.ds(0, NC), :] = jnp.zeros((NC, LANES), BF16)
                o_ref[g, eo, pl.ds(NC, NC), :] = out.astype(BF16)


def _compress(z, w1bd, posrow, w1cat, w2v, B, S):
    NC = S // CMP_STRIDE
    out = jax.ShapeDtypeStruct((B, A_KV_GROUPS, 2, 2 * NC, LANES), BF16)
    ospec = pl.BlockSpec((None, A_KV_GROUPS, 2, 2 * NC, LANES), lambda b: (b, 0, 0, 0, 0))
    full = lambda a: pl.BlockSpec(a.shape, lambda b: (0,) * a.ndim)
    return pl.pallas_call(
        functools.partial(_compress_kernel, NC=NC),
        grid=(B,),
        in_specs=[pl.BlockSpec((S, LANES), lambda b: (b, Z_AKC // LANES)),
                  pl.BlockSpec((S, LANES), lambda b: (b, Z_AVC // LANES)),
                  full(w1bd), full(posrow), full(w1cat), full(w2v)],
        out_specs=[ospec, ospec],
        out_shape=[out, out],
        scratch_shapes=[pltpu.VMEM((S, LANES), F32)],
        compiler_params=_cparams("parallel"),
        name="nsa_compress",
    )(z, z, w1bd, posrow, w1cat, w2v)


def _stack_pairs(zq, scale):
    q = zq.astype(F32) * scale
    return jnp.concatenate([q[:, :LANES], q[:, LANES:]], axis=0).astype(BF16)


def _cmpsel_kernel(zq_ref, kc_ref, vc_ref, cb_ref, oc_ref, sb_ref, *, NC, NS):
    qt = pl.program_id(2)
    r = TQ // CMP_STRIDE
    lhs = _stack_pairs(zq_ref[...], A_HEAD_DIM ** -0.5)
    st = pl.multiple_of(r * (qt + 1), 16)
    col = lax.broadcasted_iota(jnp.int32, (2 * TQ, NC), 1)
    exists = col >= NC - r * (qt + 1)
    acc = jnp.zeros((2 * TQ, LANES), F32)
    psum = jnp.zeros((TQ, NC), F32)
    for eo in range(2):
        kwin = kc_ref[eo, pl.ds(st, NC), :]
        vwin = vc_ref[eo, pl.ds(st, NC), :]
        logits = _dot_t(lhs, kwin) + cb_ref[eo]
        logits = jnp.where(exists, logits, NEG)
        m = jnp.max(logits, axis=-1, keepdims=True)
        p = jnp.where(logits > 0.1 * NEG, jnp.exp(logits - m), 0.0)
        l = jnp.maximum(jnp.sum(p, axis=-1, keepdims=True), 1e-30)
        p = p * (1.0 / l)
        acc = acc + _dot(p.astype(BF16), vwin)
        psum = psum + p[:TQ] + p[TQ:]
    oc_ref[:, :LANES] = acc[:TQ]
    oc_ref[:, LANES:] = acc[TQ:]

    NR = A_HEAD_DIM
    jj = lax.broadcasted_iota(jnp.int32, (NR, NC), 0)
    nn = lax.broadcasted_iota(jnp.int32, (NR, NC), 1) + (r * (qt + 1) - NC)
    delta = 4 * jj - nn
    mt = jnp.where((delta == 0) | (delta == 4), 1.0, 0.0) + jnp.where((delta >= 1) & (delta <= 3), 2.0, 0.0)
    mt = jnp.where(jj < NS, mt, 0.0)
    mtb = mt.astype(BF16)
    imp_t = sum(_dot_t(mtb, t) for t in _split3(psum))
    jb = lax.broadcasted_iota(jnp.int32, (NR, TQ), 0)
    tpos = qt * TQ + lax.broadcasted_iota(jnp.int32, (NR, TQ), 1)
    back = (tpos >> 6) - jb
    forced = (jb == 0) | ((back >= 0) & (back < N_LOCAL_BLOCKS))
    score = jnp.where(forced, FORCE_SCORE, jnp.where(back >= 0, imp_t, -1.0))
    score = jnp.where(jb < NS, score, -2.0)
    ngrp = -(-NS // 8)
    grp = [score[8 * m:8 * m + 8, :] for m in range(ngrp)]
    rank = [jnp.zeros((8, TQ), F32) for _ in range(ngrp)]
    j8 = lax.broadcasted_iota(jnp.int32, (8, TQ), 0)
    for jp in range(NS):
        row = score[jp:jp + 1, :]
        for m in range(ngrp):
            if m < jp // 8:
                beats = row > grp[m]
            elif m > jp // 8:
                beats = row >= grp[m]
            else:
                beats = (row > grp[m]) | ((row == grp[m]) & (j8 > jp - 8 * m))
            rank[m] = rank[m] + jnp.where(beats, 1.0, 0.0)
    rank = jnp.concatenate(rank + [jnp.zeros((NR - 8 * ngrp, TQ), F32)] * (NR > 8 * ngrp), axis=0)
    sel = (rank < float(min(SLC_TOPK, NS))) & (back >= 0) & (jb < NS)
    sb = jnp.where(sel, 0.0, jnp.where(jb < NS, NEG, 0.0))
    sb_ref[...] = jnp.concatenate([sb, jnp.zeros((LANES - NR, TQ), F32)], axis=0).astype(BF16)


def _cmpsel(z, kcp, vcp, cb, B, S):
    NC = S // CMP_STRIDE
    NS = S // SLC_BLOCK
    nq = S // TQ
    kspec = pl.BlockSpec((None, None, 2, 2 * NC, LANES), lambda b, g, q: (b, g, 0, 0, 0))
    return pl.pallas_call(
        functools.partial(_cmpsel_kernel, NC=NC, NS=NS),
        grid=(B, A_KV_GROUPS, nq),
        in_specs=[pl.BlockSpec((TQ, 2 * LANES), lambda b, g, q: (b * nq + q, g)),
                  kspec, kspec,
                  pl.BlockSpec((None, 2, 2 * TQ, NC), lambda b, g, q: (g, 0, 0, 0))],
        out_specs=[pl.BlockSpec((TQ, 2 * LANES), lambda b, g, q: (b * nq + q, g)),
                   pl.BlockSpec((None, None, LANES, TQ), lambda b, g, q: (b, g, 0, q))],
        out_shape=[jax.ShapeDtypeStruct((B * S, A_Q), F32),
                   jax.ShapeDtypeStruct((B, A_KV_GROUPS, LANES, S), BF16)],
        compiler_params=_cparams("parallel", "parallel", "parallel"),
        name="nsa_cmp_select",
    )(z, kcp, vcp, cb)


def _flash_update_t(s, vt, state, acc_ref):
    m, l = state
    m_new = jnp.maximum(m, jnp.max(s, axis=0, keepdims=True))
    p = jnp.exp2(s - m_new)
    alpha = jnp.exp2(m - m_new)
    acc_ref[...] = alpha * acc_ref[...] + _dot(vt, p.astype(BF16))
    return m_new, alpha * l + jnp.sum(p, axis=0, keepdims=True)


def _flash_pipeline(qk, softmax, lo, hi, state, s_ref, diag_in_loop):
    s_ref[0] = qk(0, lo)

    def step(kc, st):
        s_ref[1] = qk(1, kc)
        st0 = softmax(0, s_ref[0], kc, st[0], False)
        s_ref[0] = qk(0, jnp.minimum(kc + 1, hi))
        st1 = softmax(1, s_ref[1], kc, st[1], False)
        return st0, st1

    end = hi + 1 if diag_in_loop else hi
    npair = (end - lo) // 2
    st = lax.fori_loop(0, npair, lambda i, st: step(lo + 2 * i + 1, step(lo + 2 * i, st)), state)
    st = lax.fori_loop(lo + 2 * npair, end, step, st)
    if diag_in_loop:
        return st
    s_ref[1] = qk(1, hi)
    st0 = softmax(0, s_ref[0], hi, st[0], True)
    st1 = softmax(1, s_ref[1], hi, st[1], True)
    return st0, st1


def _split_heads_kv(slab, g):
    lane = lax.broadcasted_iota(jnp.int32, slab.shape, 1)
    own = jnp.where((lane < A_HEAD_DIM) == (g == 0), slab, 0.0)
    other = pltpu.roll(own, A_HEAD_DIM, axis=1)
    is0 = g == 0
    return jnp.where(is0, own, other), jnp.where(is0, other, own)


def _nsa_flash_kernel(zq_ref, sbt_ref, zks_ref, zvs_ref, zkw_ref, zvw_ref, tab_ref, os_ref, ow_ref,
                      kse_ref, kso_ref, kwe_ref, kwo_ref, vts_ref, vtw_ref, acc_ref, s_ref, *, S):
    g = pl.program_id(1)
    qt = pl.program_id(2)

    @pl.when(qt == 0)
    def _():
        rowb = lax.broadcasted_iota(jnp.int32, (S, LANES), 0) >> 6
        lane = lax.broadcasted_iota(jnp.int32, (S, LANES), 1)
        onehot = jnp.where(rowb == lane, 1.0, 0.0).astype(BF16)
        for zk_ref, zv_ref, ke_ref, ko_ref, vt_ref in ((zks_ref, zvs_ref, kse_ref, kso_ref, vts_ref),
                                                      (zkw_ref, zvw_ref, kwe_ref, kwo_ref, vtw_ref)):
            ke, ko = _split_heads_kv(zk_ref[...].astype(F32), g)
            ke_ref[:, :LANES] = ke.astype(BF16)
            ko_ref[:, :LANES] = ko.astype(BF16)
            for c in range(S // TQ):
                vt_ref[c] = zv_ref[c * TQ:(c + 1) * TQ, :].astype(F32).T.astype(BF16)
        kse_ref[:, LANES:] = onehot
        kso_ref[:, LANES:] = onehot

    zq = zq_ref[...].astype(F32) * (A_HEAD_DIM ** -0.5 * LOG2E)
    rhs_q = jnp.concatenate([zq[:, :LANES], zq[:, LANES:]], axis=0).T.astype(BF16)
    sbt = sbt_ref[...]
    rhs_sel = jnp.concatenate([rhs_q, jnp.concatenate([sbt, sbt], axis=1)], axis=0)
    vrow = pl.multiple_of(g * A_HEAD_DIM, A_HEAD_DIM)
    init = (jnp.full((1, 2 * TQ), 0.5 * NEG, F32), jnp.zeros((1, 2 * TQ), F32))

    def branch(k_refs, vt_ref, rhs, lo, table_of, o_ref):
        acc_ref[...] = jnp.zeros(acc_ref.shape, F32)

        def qk(eo, kc):
            ks = pl.multiple_of(kc * TQ, TQ)
            return _dot(k_refs[eo][pl.ds(ks, TQ), :], rhs)

        def softmax(eo, s, kc, state, diag):
            vt = vt_ref[kc, pl.ds(vrow, A_HEAD_DIM), :]
            return _flash_update_t(s + tab_ref[eo, table_of(qt - kc)], vt, state, acc_ref.at[eo])

        carry = _flash_pipeline(qk, softmax, lo, qt, (init, init), s_ref, True)
        out_t = jnp.concatenate([acc_ref[0] * (1.0 / carry[0][1]), acc_ref[1] * (1.0 / carry[1][1])], axis=0)
        out = out_t.T
        o_ref[:, :LANES] = out[:TQ]
        o_ref[:, LANES:] = out[TQ:]

    branch((kse_ref, kso_ref), vts_ref, rhs_sel, 0, lambda d: jnp.minimum(d, 2), os_ref)
    branch((kwe_ref, kwo_ref), vtw_ref, rhs_q, jnp.maximum(qt - 2, 0),
           lambda d: jnp.where(d == 2, 3, d), ow_ref)


def _nsa_flash(z, sb, tab, B, S):
    nq = S // TQ
    zslab = lambda col: pl.BlockSpec((S, LANES), lambda b, g, q: (b, col // LANES))
    ospec = pl.BlockSpec((TQ, 2 * LANES), lambda b, g, q: (b * nq + q, g))
    oshape = jax.ShapeDtypeStruct((B * S, A_Q), F32)
    return pl.pallas_call(
        functools.partial(_nsa_flash_kernel, S=S),
        grid=(B, A_KV_GROUPS, nq),
        in_specs=[pl.BlockSpec((TQ, 2 * LANES), lambda b, g, q: (b * nq + q, g)),
                  pl.BlockSpec((None, None, LANES, TQ), lambda b, g, q: (b, g, 0, q)),
                  zslab(Z_AKS), zslab(Z_AVS), zslab(Z_AKW), zslab(Z_AVW),
                  pl.BlockSpec((None, 2, 4, TQ, 2 * TQ), lambda b, g, q: (g, 0, 0, 0, 0))],
        out_specs=[ospec, ospec],
        out_shape=[oshape, oshape],
        scratch_shapes=[pltpu.VMEM((S, 2 * LANES), BF16), pltpu.VMEM((S, 2 * LANES), BF16),
                        pltpu.VMEM((S, LANES), BF16), pltpu.VMEM((S, LANES), BF16),
                        pltpu.VMEM((S // TQ, LANES, TQ), BF16), pltpu.VMEM((S // TQ, LANES, TQ), BF16),
                        pltpu.VMEM((2, A_HEAD_DIM, 2 * TQ), F32),
                        pltpu.VMEM((2, TQ, 2 * TQ), F32)],
        compiler_params=_cparams("parallel", "parallel", "arbitrary"),
        name="nsa_selected_window",
    )(z, sb, z, z, z, z, tab)


def _mla_proj_kernel(zqa_ref, zkva_ref, zkra_ref, zkrb_ref, qn_ref, kvn_ref, wqa_ref, wqb_ref,
                     wk_ref, wv_ref, caq_ref, cbq_ref, cak_ref, cbk_ref, q_ref, k_ref, v_ref):
    def rms(x, gain):
        return x * lax.rsqrt(jnp.mean(x * x, axis=-1, keepdims=True) + EPS) * gain

    nq = rms(zqa_ref[...].astype(F32), qn_ref[...]).astype(BF16)
    nkv = rms(zkva_ref[...].astype(F32), kvn_ref[...]).astype(BF16)
    qa = _dot_t(wqa_ref[...], nq)
    qb = _dot_t(wqb_ref[...], nq)
    kn = _dot(nkv, wk_ref[...])
    vt = _dot_t(wv_ref[...], nkv)
    caq, cbq = caq_ref[...], cbq_ref[...]
    kr = zkra_ref[...].astype(F32) * cak_ref[...] + zkrb_ref[...].astype(F32) * cbk_ref[...]
    for h in range(B_HEADS):
        sl = slice(h * LANES, (h + 1) * LANES)
        q_ref[h] = (qa[sl, :] * caq + qb[sl, :] * cbq).astype(BF16)
        k_ref[h] = (kn[:, sl] + kr).astype(BF16)
        v_ref[h] = vt[h * V_DIM:(h + 1) * V_DIM, :].astype(BF16)


def _mla_proj(z, qn, kvn, wqa, wqb, wk, wv, caq, cbq, cak, cbk, B, S):
    tm = TM
    nt = S // tm
    zspec = lambda w, col: pl.BlockSpec((tm, w), lambda b, i: (b * nt + i, col // w))
    full = lambda a: pl.BlockSpec(a.shape, lambda b, i: (0,) * a.ndim)
    tspec = pl.BlockSpec((tm, LANES), lambda b, i: (i, 0))
    tspec_t = pl.BlockSpec((LANES, tm), lambda b, i: (0, i))
    return pl.pallas_call(
        _mla_proj_kernel,
        grid=(B, nt),
        in_specs=[zspec(Q_LORA, Z_QA), zspec(LANES, Z_KVA), zspec(LANES, Z_KRA), zspec(LANES, Z_KRB),
                  full(qn), full(kvn), full(wqa), full(wqb), full(wk), full(wv),
                  tspec_t, tspec_t, tspec, tspec],
        out_specs=[pl.BlockSpec((None, B_HEADS, LANES, tm), lambda b, i: (b, 0, 0, i)),
                   pl.BlockSpec((None, B_HEADS, tm, LANES), lambda b, i: (b, 0, i, 0)),
                   pl.BlockSpec((None, B_HEADS, None, V_DIM, tm), lambda b, i: (b, 0, i, 0, 0))],
        out_shape=[jax.ShapeDtypeStruct((B, B_HEADS, LANES, S), BF16),
                   jax.ShapeDtypeStruct((B, B_HEADS, S, LANES), BF16),
                   jax.ShapeDtypeStruct((B, B_HEADS, nt, V_DIM, tm), BF16)],
        compiler_params=_cparams("parallel", "parallel"),
        name="mla_proj",
    )(z, z, z, z, qn, kvn, wqa, wqb, wk, wv, caq, cbq, cak, cbk)


def _mla_flash_kernel(qt_ref, k_ref, vt_ref, o_ref, acc_ref, s_ref):
    qi = pl.program_id(2)
    acc_ref[...] = jnp.zeros(acc_ref.shape, F32)
    ri = lax.broadcasted_iota(jnp.int32, (TM, TM), 0)
    ci = lax.broadcasted_iota(jnp.int32, (TM, TM), 1)
    causal = ri <= ci

    def qk(hh, kc):
        ks = pl.multiple_of(kc * TM, TM)
        return _dot(k_ref[hh, pl.ds(ks, TM), :], qt_ref[hh])

    def softmax(hh, s, kc, state, diag):
        if diag:
            s = jnp.where(causal, s, NEG)
        return _flash_update_t(s, vt_ref[hh, kc], state, acc_ref.at[hh])

    init = (jnp.full((1, TM), 0.5 * NEG, F32), jnp.zeros((1, TM), F32))
    carry = _flash_pipeline(qk, softmax, 0, qi, (init, init), s_ref, False)
    out_t = jnp.concatenate([acc_ref[0] * (1.0 / carry[0][1]), acc_ref[1] * (1.0 / carry[1][1])], axis=0)
    o_ref[...] = out_t.T


def _mla_flash(qt, k, vt, B, S):
    nq = S // TM
    hp = B_HEADS // 2
    return pl.pallas_call(
        _mla_flash_kernel,
        grid=(B, hp, nq),
        in_specs=[pl.BlockSpec((None, 2, LANES, TM), lambda b, h, i: (b, h, 0, i)),
                  pl.BlockSpec((None, 2, S, LANES), lambda b, h, i: (b, h, 0, 0)),
                  pl.BlockSpec((None, 2, nq, V_DIM, TM), lambda b, h, i: (b, h, 0, 0, 0))],
        out_specs=pl.BlockSpec((TM, LANES), lambda b, h, i: (b * nq + i, h)),
        out_shape=jax.ShapeDtypeStruct((B * S, B_HEADS * V_DIM), F32),
        scratch_shapes=[pltpu.VMEM((2, V_DIM, TM), F32), pltpu.VMEM((2, TM, TM), F32)],
        compiler_params=_cparams("parallel", "parallel", "arbitrary"),
        name="mla_flash",
    )(qt, k, vt)


def _rwkv_prep_kernel(zr_ref, zk_ref, zv_ref, zwa_ref, zg_ref, mu_r, mu_k, mu_v, mu_wa, mu_g,
                      w0_ref, a0_ref, kk_ref, ka_ref, rk_ref, w2_ref, a2_ref, g2_ref, bd_ref,
                      r_o, k_o, v_o, kk_o, be_o, ld_o, g_o, bo_o,
                      c_r, c_k, c_v, c_wa, c_g):
    t = pl.program_id(1)
    tm = zr_ref.shape[0]

    def shifted(z_ref, mu_ref, c_ref):
        x = z_ref[...].astype(F32)
        row = lax.broadcasted_iota(jnp.int32, x.shape, 0)
        prev = jnp.where(t == 0, 0.0, c_ref[0:1, :])
        xs = jnp.where(row == 0, prev, pltpu.roll(x, 1, axis=0))
        c_ref[0:1, :] = x[tm - 1:tm, :]
        return x + (xs - x) * mu_ref[...]

    r = shifted(zr_ref, mu_r, c_r)
    k = shifted(zk_ref, mu_k, c_k)
    v = shifted(zv_ref, mu_v, c_v)
    wa = shifted(zwa_ref, mu_wa, c_wa)
    gd = shifted(zg_ref, mu_g, c_g)
    w = w0_ref[...] + _dot(jnp.tanh(wa).astype(BF16), w2_ref[...])
    ld_o[...] = -jax.nn.sigmoid(w) * math.exp(-0.5)
    a = jax.nn.sigmoid(a0_ref[...] + _dot(wa.astype(BF16), a2_ref[...]))
    g_o[...] = _dot(jax.nn.sigmoid(gd).astype(BF16), g2_ref[...])
    kk = k * kk_ref[...]
    nsq = _dot3(kk * kk, bd_ref[...])
    kk = kk / jnp.maximum(jnp.sqrt(nsq), 1e-12)
    k2 = k * (1.0 + (a - 1.0) * ka_ref[...])
    rks = _dot3(r * k2 * rk_ref[...], bd_ref[...])
    r_o[...] = r
    k_o[...] = k2
    v_o[...] = v
    kk_o[...] = kk
    be_o[...] = kk * a
    bo_o[...] = rks * v


def _rwkv_prep(z, mus, w0, a0, k_k, k_a, r_k, w2p, a2p, g2, bd, B, S, tm=256):
    nt = S // tm
    W = C_WIDTH
    zspec = lambda w, col: pl.BlockSpec((tm, w), lambda b, i: (b * nt + i, col // w))
    full = lambda a: pl.BlockSpec(a.shape, lambda b, i: (0,) * a.ndim)
    ospec = pl.BlockSpec((tm, W), lambda b, i: (b * nt + i, 0))
    oshape = jax.ShapeDtypeStruct((B * S, W), F32)
    consts = list(mus) + [w0, a0, k_k, k_a, r_k, w2p, a2p, g2, bd]
    return pl.pallas_call(
        _rwkv_prep_kernel,
        grid=(B, nt),
        in_specs=[zspec(W, Z_CR), zspec(W, Z_CK), zspec(W, Z_CV), zspec(LANES, Z_CWA), zspec(LANES, Z_CG)]
                 + [full(a) for a in consts],
        out_specs=[ospec] * 8,
        out_shape=[oshape] * 8,
        scratch_shapes=[pltpu.VMEM((8, W), F32)] * 3 + [pltpu.VMEM((8, LANES), F32)] * 2,
        compiler_params=_cparams("parallel", "arbitrary"),
        name="rwkv_prep",
    )(z, z, z, z, z, *consts)


def _rwkv_chunk_kernel(r_ref, k_ref, v_ref, kk_ref, be_ref, ld_ref,
                       qa_o, bkt_o, yu_o, vp_o, pcb_o):
    C = CH
    R = ld_ref.shape[0]
    ld = ld_ref[...]
    rr = lax.broadcasted_iota(jnp.int32, (R, R), 0)
    cc = lax.broadcasted_iota(jnp.int32, (R, R), 1)
    sh = int(math.log2(C))
    tri = jnp.where((cc <= rr) & ((cc >> sh) == (rr >> sh)), 1.0, 0.0).astype(BF16)
    cs = sum(_dot(tri, t) for t in _split3(ld))
    cl = jnp.concatenate([jnp.broadcast_to(cs[(s + 1) * C - 1:(s + 1) * C, :], (C, C_WIDTH))
                          for s in range(R // C)], axis=0)
    ex, exn, exx, exc = jnp.exp(cs), jnp.exp(-cs), jnp.exp(cs - ld), jnp.exp(cl - cs)
    kk, be, k2, v = kk_ref[...], be_ref[...], k_ref[...], v_ref[...]
    at = -kk * exx
    rt = r_ref[...] * ex
    bt = be * exn
    kt = k2 * exn
    btc = be * exc
    ktc = k2 * exc
    pc = jnp.exp(cl)
    ri = lax.broadcasted_iota(jnp.int32, (C, C), 0)
    ci = lax.broadcasted_iota(jnp.int32, (C, C), 1)
    incl = ci <= ri
    strict = ci < ri
    eye = jnp.where(ci == ri, 1.0, 0.0)
    lane = lax.broadcasted_iota(jnp.int32, (C, LANES), 1)
    P = C_HEADS // 2
    pairs = [(s, p) for s in range(R // C) for p in range(P)]
    heads = [(i, hh) for i in range(len(pairs)) for hh in range(2)]
    blk = lambda x, i: x[pairs[i][0] * C:(pairs[i][0] + 1) * C, pairs[i][1] * LANES:(pairs[i][1] + 1) * LANES]
    lhs2 = [jnp.concatenate([blk(at, i), blk(rt, i)], axis=0).astype(BF16) for i in range(len(pairs))]
    bm, km, vm, am = [], [], [], []
    for i, hh in heads:
        msk = (lane < C_HEAD_DIM) if hh == 0 else (lane >= C_HEAD_DIM)
        bm.append(jnp.where(msk, blk(bt, i), 0.0).astype(BF16))
        km.append(jnp.where(msk, blk(kt, i), 0.0).astype(BF16))
        vm.append(jnp.where(msk, blk(v, i), 0.0).astype(BF16))
        am.append(jnp.where(msk, blk(at, i), 0.0).astype(BF16))
    g1 = [_dot_t(lhs2[i], bm[j]) for j, (i, hh) in enumerate(heads)]
    g2 = [_dot_t(lhs2[i], km[j]) for j, (i, hh) in enumerate(heads)]
    lab = [jnp.where(strict, g[:C], 0.0) for g in g1]
    mrb = [jnp.where(incl, g[C:], 0.0).astype(BF16) for g in g1]
    lak = [jnp.where(strict, g[:C], 0.0).astype(BF16) for g in g2]
    mrk = [jnp.where(incl, g[C:], 0.0).astype(BF16) for g in g2]
    w2 = [_dot(a, b) for a, b in zip(lak, vm)]
    yk = [_dot(a, b) for a, b in zip(mrk, vm)]
    tinv = [eye + x for x in lab]
    lp = lab
    for _ in range(int(math.log2(C)) - 1):
        lpb = [x.astype(BF16) for x in lp]
        lp = [_dot(x, x) for x in lpb]
        tinv = [t + _dot(t.astype(BF16), x.astype(BF16)) for t, x in zip(tinv, lp)]
    au = [_dot(t.astype(BF16), jnp.concatenate([a, w.astype(BF16)], axis=1)) for t, a, w in zip(tinv, am, w2)]
    qy = [_dot(m, x.astype(BF16)) for m, x in zip(mrb, au)]
    for i, (s, p) in enumerate(pairs):
        e, o = 2 * i, 2 * i + 1
        ahat = au[e][:, :LANES] + au[o][:, :LANES]
        uhat = au[e][:, LANES:] + au[o][:, LANES:]
        qhat = blk(rt, i) + qy[e][:, :LANES] + qy[o][:, :LANES]
        yi = qy[e][:, LANES:] + qy[o][:, LANES:] + yk[e] + yk[o]
        qa_o[s, p] = jnp.concatenate([qhat, ahat], axis=0).astype(BF16)
        yu_o[s, p] = jnp.concatenate([yi, uhat], axis=0)
        bkt_o[s, p] = jnp.concatenate([blk(btc, i), blk(ktc, i)], axis=0).T.astype(BF16)
        vp_o[s, p] = blk(v, i).astype(BF16)
        pcb_o[s, p] = jnp.concatenate([blk(pc, i), blk(pc, i)], axis=0).T


def _rwkv_chunk(r, k2, v, kk, be, ld, B, S):
    nch = S // CH
    W = C_WIDTH
    P = C_HEADS // 2
    ns = RW_SUB
    ispec = pl.BlockSpec((ns * CH, W), lambda b, c: (b * (nch // ns) + c, 0))
    sq = lambda rows: pl.BlockSpec((None, ns, P, rows, LANES), lambda b, c: (b, c, 0, 0, 0))
    shp = lambda rows, dt: jax.ShapeDtypeStruct((B, nch, P, rows, LANES), dt)
    return pl.pallas_call(
        _rwkv_chunk_kernel,
        grid=(B, nch // ns),
        in_specs=[ispec] * 6,
        out_specs=[sq(2 * CH), sq(LANES), sq(2 * CH), sq(CH), sq(LANES)],
        out_shape=[shp(2 * CH, BF16), shp(LANES, BF16), shp(2 * CH, F32), shp(CH, BF16), shp(LANES, F32)],
        compiler_params=_cparams("parallel", "parallel"),
        name="rwkv_chunk",
    )(r, k2, v, kk, be, ld)


def _rwkv_state_kernel(qa_ref, bkt_ref, yu_ref, vp_ref, pcb_ref, y_ref, ap_ref):
    @pl.when(pl.program_id(1) == 0)
    def _():
        ap_ref[...] = jnp.zeros(ap_ref.shape, F32)

    ri = lax.broadcasted_iota(jnp.int32, (LANES, LANES), 0)
    ci = lax.broadcasted_iota(jnp.int32, (LANES, LANES), 1)
    same_head = (ri < C_HEAD_DIM) == (ci < C_HEAD_DIM)
    idx = [(b, p) for b in range(qa_ref.shape[0]) for p in range(C_HEADS // 2)]
    a = [ap_ref[b, p] for b, p in idx]
    for c in range(qa_ref.shape[1]):
        x = [_dot(qa_ref[b, c, p], s.astype(BF16)) for (b, p), s in zip(idx, a)]
        yu = [yu_ref[b, c, p] for b, p in idx]
        for (b, p), xi, yi in zip(idx, x, yu):
            y_ref[b, c * CH:(c + 1) * CH, p * LANES:(p + 1) * LANES] = xi[:CH] + yi[:CH]
        uv = [jnp.concatenate([(xi[CH:] + yi[CH:]).astype(BF16), vp_ref[b, c, p]], axis=0)
              for (b, p), xi, yi in zip(idx, x, yu)]
        upd = [_dot(bkt_ref[b, c, p], t) for (b, p), t in zip(idx, uv)]
        a = [pcb_ref[b, c, p] * s + jnp.where(same_head, t, 0.0) for (b, p), s, t in zip(idx, a, upd)]
    for (b, p), s in zip(idx, a):
        ap_ref[b, p] = s


def _rwkv_state(qa, bkt, yu, vp, pcb, B, S):
    nch = S // CH
    P = C_HEADS // 2
    nb = 2 if B % 2 == 0 else 1
    nc = 4 if nch % 4 == 0 else 1
    sq = lambda rows: pl.BlockSpec((nb, nc, P, rows, LANES), lambda b, c: (b, c, 0, 0, 0))
    return pl.pallas_call(
        _rwkv_state_kernel,
        grid=(B // nb, nch // nc),
        in_specs=[sq(2 * CH), sq(LANES), sq(2 * CH), sq(CH), sq(LANES)],
        out_specs=pl.BlockSpec((nb, nc * CH, C_WIDTH), lambda b, c: (b, c, 0)),
        out_shape=jax.ShapeDtypeStruct((B, S, C_WIDTH), F32),
        scratch_shapes=[pltpu.VMEM((nb, P, LANES, LANES), F32)],
        compiler_params=_cparams("parallel", "arbitrary"),
        name="rwkv_state",
    )(qa, bkt, yu, vp, pcb).reshape(B * S, C_WIDTH)


def _merge_kernel(oc_ref, os_ref, ow_ref, zag_ref, yb_ref, yr_ref, bo_ref, gg_ref,
                  zga_ref, zgb_ref, zgc_ref, x_ref, wbr_ref, wout_ref, ge_ref, bdm_ref,
                  ln_ref, gain_ref, gt_ref, o_ref):
    sg = jax.nn.sigmoid(zag_ref[...].astype(F32))
    ya = (_dot2(sg, ge_ref[0]) * oc_ref[...] + _dot2(sg, ge_ref[1]) * os_ref[...]
          + _dot2(sg, ge_ref[2]) * ow_ref[...])
    yr = yr_ref[...]
    mean = _dot3(yr, bdm_ref[...])
    d = yr - mean
    var = _dot2(d * d, bdm_ref[...])
    yn = d * lax.rsqrt(var + GN_EPS) * ln_ref[0:1, :] + ln_ref[1:2, :]
    yc = (yn + bo_ref[...]) * gg_ref[...]
    sig = lambda ref: jax.nn.sigmoid(ref[...].astype(F32))
    merged = (sig(zga_ref) * _dot(ya.astype(BF16), wbr_ref[0:A_Q, :])
              + sig(zgb_ref) * _dot(yb_ref[...].astype(BF16), wbr_ref[A_Q:2 * A_Q, :])
              + sig(zgc_ref) * _dot(yc.astype(BF16), wbr_ref[2 * A_Q:3 * A_Q, :]))
    y = _dot(merged.astype(BF16), wout_ref[...])
    yn2 = y * lax.rsqrt(jnp.mean(y * y, axis=-1, keepdims=True) + EPS) * gain_ref[...]
    o_ref[...] = x_ref[...] + gt_ref[0] * yn2


def _merge(oc, os_, ow, z, yb, yr, bo, gg, x2, wbr, wout, gexp, bdm, ln, gain, gt, S, tm=256):
    N, D = x2.shape
    tpb = S // tm
    W = A_Q
    row = lambda w: pl.BlockSpec((tm, w), lambda i: (i, 0))
    zspec = lambda w, col: pl.BlockSpec((tm, w), lambda i: (i, col // w))
    full = lambda a: pl.BlockSpec(a.shape, lambda i: (0,) * a.ndim)
    return pl.pallas_call(
        _merge_kernel,
        grid=(N // tm,),
        in_specs=[row(W), row(W), row(W), zspec(LANES, Z_AG), row(W), row(W), row(W), row(W),
                  zspec(D, Z_ZG), zspec(D, Z_ZG + D), zspec(D, Z_ZG + 2 * D), row(D),
                  full(wbr), full(wout), full(gexp), full(bdm), full(ln),
                  pl.BlockSpec((1, D), lambda i: (0, 0)),
                  pl.BlockSpec((1, 1, D), lambda i: (i // tpb, 0, 0))],
        out_specs=row(D),
        out_shape=jax.ShapeDtypeStruct((N, D), F32),
        compiler_params=_cparams("parallel"),
        name="merge_out",
    )(oc, os_, ow, z, yb, yr, bo, gg, z, z, z, x2, wbr, wout, gexp, bdm, ln, gain.reshape(1, D), gt)


def _ffn_up_kernel(x_ref, g_ref, sc_ref, sh_ref, wg_ref, wv_ref, cw_ref, cb_ref, o_ref,
                   cg_ref, cv_ref, *, tpb):
    i = pl.program_id(0)
    tm = x_ref.shape[0]
    fc = FFN_SUB
    first = (i % tpb) == 0
    x = x_ref[...]
    y = x * lax.rsqrt(jnp.mean(x * x, axis=-1, keepdims=True) + EPS)
    h = ((y * g_ref[...]) * (1.0 + sc_ref[0]) + sh_ref[0]).astype(BF16)
    row = lax.broadcasted_iota(jnp.int32, (8, fc), 0)

    def up(c):
        sl = slice(c * fc, (c + 1) * fc)
        return _dot(h, wg_ref[:, sl]), _dot(h, wv_ref[:, sl])

    def conv(u, c_ref, c, off):
        sl = slice(c * fc, (c + 1) * fc)
        wl = slice(off + c * fc, off + (c + 1) * fc)
        prev = c_ref[:, sl]
        p1 = jnp.where(first, 0.0, prev[7:8])
        p2 = jnp.where(first, 0.0, prev[6:7])
        c_ref[:, sl] = u[tm - 8:tm, :]
        r1 = pltpu.roll(u, 1, axis=0)
        r2 = pltpu.roll(u, 2, axis=0)
        t1 = jnp.where(row == 0, p1, r1[0:8])
        t2 = jnp.where(row == 0, p2, jnp.where(row == 1, p1, r2[0:8]))
        u1 = jnp.concatenate([t1, r1[8:]], axis=0)
        u2 = jnp.concatenate([t2, r2[8:]], axis=0)
        return cw_ref[0:1, wl] * u2 + cw_ref[1:2, wl] * u1 + cw_ref[2:3, wl] * u + cb_ref[:, wl]

    nsub = D_FF // fc
    cur = up(0)
    for c in range(nsub):
        nxt = up(c + 1) if c + 1 < nsub else None
        a = _gelu(conv(cur[0], cg_ref, c, 0)) * conv(cur[1], cv_ref, c, D_FF)
        o_ref[:, c * fc:(c + 1) * fc] = a.astype(BF16)
        cur = nxt


def _ffn_up(x2, gain, sc, sh, w_up, conv_w, conv_b, S, tm=256):
    N, D = x2.shape
    tpb = S // tm
    full = lambda a: pl.BlockSpec(a.shape, lambda i: (0,) * a.ndim)
    return pl.pallas_call(
        functools.partial(_ffn_up_kernel, tpb=tpb),
        grid=(N // tm,),
        in_specs=[pl.BlockSpec((tm, D), lambda i: (i, 0)),
                  pl.BlockSpec((1, D), lambda i: (0, 0)),
                  pl.BlockSpec((1, 1, D), lambda i: (i // tpb, 0, 0)),
                  pl.BlockSpec((1, 1, D), lambda i: (i // tpb, 0, 0)),
                  pl.BlockSpec((D, D_FF), lambda i: (0, 0)),
                  pl.BlockSpec((D, D_FF), lambda i: (0, 1)),
                  full(conv_w), pl.BlockSpec((1, 2 * D_FF), lambda i: (0, 0))],
        out_specs=pl.BlockSpec((tm, D_FF), lambda i: (i, 0)),
        out_shape=jax.ShapeDtypeStruct((N, D_FF), BF16),
        scratch_shapes=[pltpu.VMEM((8, D_FF), F32), pltpu.VMEM((8, D_FF), F32)],
        compiler_params=_cparams("arbitrary"),
        name="ffn_up",
    )(x2, gain.reshape(1, D), sc, sh, w_up, w_up, conv_w, conv_b.reshape(1, 2 * D_FF))


def _ffn_down_kernel(a_ref, wd_ref, x_ref, gain_ref, gt_ref, o_ref):
    f = _dot(a_ref[...], wd_ref[...])
    fn = f * lax.rsqrt(jnp.mean(f * f, axis=-1, keepdims=True) + EPS) * gain_ref[...]
    o_ref[...] = x_ref[...] + gt_ref[0] * fn


def _ffn_down(a, wd, x2, gain, gt, S, tm=512):
    N, D = x2.shape
    tpb = S // tm
    return pl.pallas_call(
        _ffn_down_kernel,
        grid=(N // tm,),
        in_specs=[pl.BlockSpec((tm, D_FF), lambda i: (i, 0)),
                  pl.BlockSpec((D_FF, D), lambda i: (0, 0)),
                  pl.BlockSpec((tm, D), lambda i: (i, 0)),
                  pl.BlockSpec((1, D), lambda i: (0, 0)),
                  pl.BlockSpec((1, 1, D), lambda i: (i // tpb, 0, 0))],
        out_specs=pl.BlockSpec((tm, D), lambda i: (i, 0)),
        out_shape=jax.ShapeDtypeStruct((N, D), F32),
        compiler_params=_cparams("parallel"),
        name="ffn_down",
    )(a, wd, x2, gain.reshape(1, D), gt)


def _t5_bucket_np(dist):
    d = np.maximum(dist, 0)
    max_exact = N_BUCKETS // 2
    large = max_exact + (np.log(np.maximum(d, 1).astype(np.float32) / max_exact)
                         / math.log(MAX_DISTANCE / max_exact) * (N_BUCKETS - max_exact)).astype(np.int32)
    return np.where(d < max_exact, d, np.minimum(large, N_BUCKETS - 1))


def _bias_table_kernel(scale_ref, rb_ref, code_ref, o_ref):
    code = code_ref[0]
    sc = scale_ref[pl.program_id(0)]
    for h in range(A_HEADS):
        acc = jnp.full(code.shape, NEG, F32)
        for b in range(N_BUCKETS):
            acc = jnp.where(code == b, rb_ref[b, h], acc)
        o_ref[0, h] = acc * sc


def _bias_tables(rel_bias, codes, scales):
    nt, R, C = codes.shape
    smem = pl.BlockSpec(memory_space=pltpu.SMEM)
    return pl.pallas_call(
        _bias_table_kernel,
        grid=(nt,),
        in_specs=[smem, smem, pl.BlockSpec((1, R, C), lambda t: (t, 0, 0))],
        out_specs=pl.BlockSpec((1, A_HEADS, R, C), lambda t: (t, 0, 0, 0)),
        out_shape=jax.ShapeDtypeStruct((nt, A_HEADS, R, C), F32),
        compiler_params=_cparams("parallel"),
        name="bias_tables",
    )(jnp.asarray(scales, F32), rel_bias, jnp.asarray(codes))


def _codes(dist, lo, hi):
    return np.where((dist >= lo) & (dist < hi), _t5_bucket_np(dist), N_BUCKETS).astype(np.int32)


def _nsa_tables(rel_bias, S):
    NC = S // CMP_STRIDE
    G, HPG = A_KV_GROUPS, A_HEADS // A_KV_GROUPS
    big = 1 << 30
    i = np.arange(TQ)[:, None]
    dc = i - CMP_STRIDE * np.arange(NC)[None, :] + CMP_STRIDE * NC - TQ - (CMP_BLOCK - 1)
    cb = _bias_tables(rel_bias, _codes(dc, 0, big)[None], [1.0])[0]
    cb = cb.reshape(G, HPG // 2, 2, TQ, NC).transpose(0, 2, 1, 3, 4).reshape(G, 2, 2 * TQ, NC)
    dt = np.arange(TQ)[None, :] - np.arange(TQ)[:, None]
    far = np.full((TQ, TQ), MAX_DISTANCE)
    codes = np.stack([_codes(dt, 0, WINDOW), _codes(TQ + dt, 0, WINDOW), _codes(far, 0, big),
                      _codes(2 * TQ + dt, 0, WINDOW)])
    t = _bias_tables(rel_bias, codes, [LOG2E] * 4)
    t = t.reshape(4, G, HPG // 2, 2, TQ, TQ).transpose(1, 3, 0, 4, 2, 5).reshape(G, 2, 4, TQ, 2 * TQ)
    return cb, t


def _pad_cols(w, n):
    return jnp.pad(w, ((0, 0), (0, n - w.shape[1])))


def _prep_w_in(w):
    D = w.shape[0]
    za, zb, zc, zg = jnp.split(w, np.cumsum([A_COLS, B_COLS, C_COLS]).tolist(), axis=1)
    a_parts = jnp.split(za, np.cumsum([A_Q] + [A_KV] * 6).tolist(), axis=1)
    qa, kva, kr = jnp.split(zb, [Q_LORA, Q_LORA + KV_LORA], axis=1)
    half = ROPE_DIM // 2
    kr_rot = jnp.concatenate([-kr[:, half:], kr[:, :half]], axis=1)
    z64 = jnp.zeros((D, NOPE_DIM), w.dtype)
    z32 = jnp.zeros((D, LANES - NOPE_DIM - ROPE_DIM), w.dtype)
    c_r, c_k, c_v, c_wd, c_ad, c_gd = jnp.split(
        zc, np.cumsum([C_WIDTH] * 3 + [DECAY_LORA, AAA_LORA]).tolist(), axis=1)
    cols = [a_parts[0], qa] + a_parts[1:7] + [
        _pad_cols(a_parts[7], LANES), kva,
        jnp.concatenate([z64, kr, z32], axis=1),
        jnp.concatenate([z64, kr_rot, z32], axis=1),
        c_r, c_k, c_v, c_wd, c_ad, c_gd, jnp.zeros((D, Z_ZG - Z_CG - LANES), w.dtype), zg]
    out = jnp.concatenate(cols, axis=1)
    assert out.shape[1] == Z_COLS
    return out.astype(BF16)


def _prep_mla(w_uq, w_ukv):
    dq = NOPE_DIM + ROPE_DIM
    half = ROPE_DIM // 2
    wq = w_uq.reshape(Q_LORA, B_HEADS, dq)
    nope, r1, r2 = wq[..., :NOPE_DIM], wq[..., NOPE_DIM:NOPE_DIM + half], wq[..., NOPE_DIM + half:]
    zq = jnp.zeros((Q_LORA, B_HEADS, LANES - dq), w_uq.dtype)
    wqa = jnp.concatenate([nope, r1, r2, zq], axis=-1).reshape(Q_LORA, B_HEADS * LANES)
    wqb = jnp.concatenate([jnp.zeros_like(nope), -r2, r1, zq], axis=-1).reshape(Q_LORA, B_HEADS * LANES)
    wkv = w_ukv.reshape(KV_LORA, B_HEADS, NOPE_DIM + V_DIM)
    kn, vv = wkv[..., :NOPE_DIM], wkv[..., NOPE_DIM:]
    wk = jnp.concatenate([kn, jnp.zeros_like(kn)], axis=-1).reshape(KV_LORA, B_HEADS * LANES)
    wv = vv.reshape(KV_LORA, B_HEADS * V_DIM)
    return wqa.T.astype(BF16), wqb.T.astype(BF16), wk.astype(BF16), wv.T.astype(BF16)


def _rope_tables(S):
    half = ROPE_DIM // 2
    inv = ROPE_THETA ** (-jnp.arange(half, dtype=F32) / half)
    ang = jnp.arange(S, dtype=F32)[:, None] * inv
    cos2 = jnp.tile(jnp.cos(ang), (1, 2))
    sin2 = jnp.tile(jnp.sin(ang), (1, 2))
    scale = (NOPE_DIM + ROPE_DIM) ** -0.5 * LOG2E
    one = jnp.ones((S, NOPE_DIM), F32)
    z64 = jnp.zeros((S, NOPE_DIM), F32)
    z32 = jnp.zeros((S, LANES - NOPE_DIM - ROPE_DIM), F32)
    caq = (jnp.concatenate([one, cos2, z32], axis=1) * scale).T
    cbq = (jnp.concatenate([z64, sin2, z32], axis=1) * scale).T
    cak = jnp.concatenate([z64, cos2, z32], axis=1)
    cbk = jnp.concatenate([z64, sin2, z32], axis=1)
    return caq, cbq, cak, cbk


def _prep_compress(cmp_pos, cmp_w1, cmp_w2):
    Dh = A_HEAD_DIM
    w1 = cmp_w1.reshape(2, CMP_BLOCK, Dh, Dh)
    z = jnp.zeros_like(w1)
    w1bd = jnp.concatenate([jnp.concatenate([w1, z], axis=-1), jnp.concatenate([z, w1], axis=-1)], axis=-2)
    posrow = jnp.broadcast_to(cmp_pos.reshape(2, 1, CMP_BLOCK * Dh), (2, 8, CMP_BLOCK * Dh))
    w1cat = jnp.concatenate([cmp_w1, cmp_w1], axis=-1)
    zz = jnp.zeros((2, Dh, Dh), cmp_w2.dtype)
    blk = lambda a, b, c, d: jnp.concatenate(
        [jnp.concatenate([a, b], axis=-1), jnp.concatenate([c, d], axis=-1)], axis=-2)
    w2v = jnp.stack([jnp.stack([blk(cmp_w2, zz, zz, zz), blk(zz, cmp_w2, zz, zz)], axis=1),
                     jnp.stack([blk(zz, zz, cmp_w2, zz), blk(zz, zz, zz, cmp_w2)], axis=1)], axis=1)
    return w1bd.astype(BF16), posrow, w1cat.astype(BF16), w2v.astype(BF16)


def _gate_expand():
    e = np.zeros((3, LANES, A_Q), np.float32)
    for h in range(A_HEADS):
        for r in range(3):
            e[r, 3 * h + r, h * A_HEAD_DIM:(h + 1) * A_HEAD_DIM] = 1.0
    return jnp.asarray(e, dtype=BF16)


def _block_diag_ones(scale):
    idx = np.arange(C_WIDTH) // C_HEAD_DIM
    return jnp.asarray((idx[:, None] == idx[None, :]).astype(np.float32) * scale, dtype=BF16)


def kernel(x, c, rel_bias, ada_w, ada_b, norm_gain, w_in, nsa_cmp_pos, nsa_cmp_w1, nsa_cmp_w2, mla_q_norm, mla_kv_norm, mla_w_uq, mla_w_ukv, rwkv_mu, rwkv_w0, rwkv_a0, rwkv_k_k, rwkv_k_a, rwkv_w2, rwkv_a2, rwkv_g2, rwkv_r_k, rwkv_ln, w_branch, w_out, ffn_up, ffn_conv_w, ffn_conv_b, ffn_down):
    B, S, D = x.shape
    L = ada_w.shape[0]
    assert S % TQ == 0 and S % TM == 0 and S // SLC_BLOCK <= A_HEAD_DIM and S >= 2 * TQ
    mod = _adaln(c, ada_w, ada_b)
    cb, nsa_tab = _nsa_tables(rel_bias, S)
    caq, cbq, cak, cbk = _rope_tables(S)
    gexp = _gate_expand()
    bd1 = _block_diag_ones(1.0)
    bdm = _block_diag_ones(1.0 / C_HEAD_DIM)
    row = lambda v: v.reshape(1, -1)
    x2 = x.reshape(B * S, D)
    for l in range(L):
        m6 = mod[l].reshape(B, 6, 1, D)
        sh1, sc1, gt1, sh2, sc2, gt2 = (m6[:, i] for i in range(6))
        z = _norm_mod_matmul(x2, norm_gain[l, 0], sc1, sh1, _prep_w_in(w_in[l]), S, tn=Z_COLS // 4)
        kcp, vcp = _compress(z, *_prep_compress(nsa_cmp_pos[l], nsa_cmp_w1[l], nsa_cmp_w2[l]), B, S)
        oc, sb = _cmpsel(z, kcp, vcp, cb, B, S)
        os_, ow = _nsa_flash(z, sb, nsa_tab, B, S)
        wqa, wqb, wk, wv = _prep_mla(mla_w_uq[l], mla_w_ukv[l])
        q, k, v = _mla_proj(z, row(mla_q_norm[l]), row(mla_kv_norm[l]), wqa, wqb, wk, wv,
                            caq, cbq, cak, cbk, B, S)
        yb = _mla_flash(q, k, v, B, S)
        mu = rwkv_mu[l]
        o = 3 * C_WIDTH
        mus = [row(mu[:C_WIDTH]), row(mu[C_WIDTH:2 * C_WIDTH]), row(mu[2 * C_WIDTH:o]),
               row(mu[o:o + LANES]), row(mu[o + LANES:])]
        w2p = jnp.concatenate([rwkv_w2[l], jnp.zeros_like(rwkv_a2[l])], axis=0).astype(BF16)
        a2p = jnp.concatenate([jnp.zeros_like(rwkv_w2[l]), rwkv_a2[l]], axis=0).astype(BF16)
        rr, k2, vv, kk, be, ld, gg, bo = _rwkv_prep(
            z, mus, row(rwkv_w0[l]), row(rwkv_a0[l]), row(rwkv_k_k[l]), row(rwkv_k_a[l]),
            row(rwkv_r_k[l]), w2p, a2p, rwkv_g2[l].astype(BF16), bd1, B, S)
        qa_, bkt, yu, vp, pcb = _rwkv_chunk(rr, k2, vv, kk, be, ld, B, S)
        yr = _rwkv_state(qa_, bkt, yu, vp, pcb, B, S)
        x2 = _merge(oc, os_, ow, z, yb, yr, bo, gg, x2, w_branch[l].astype(BF16), w_out[l].astype(BF16),
                    gexp, bdm, rwkv_ln[l], norm_gain[l, 1], gt1, S)
        a = _ffn_up(x2, norm_gain[l, 2], sc2, sh2, ffn_up[l].astype(BF16), ffn_conv_w[l], ffn_conv_b[l], S)
        x2 = _ffn_down(a, ffn_down[l].astype(BF16), x2, norm_gain[l, 3], gt2, S)
    return x2.reshape(B, S, D)
```

```python
import functools
import math

import jax
import jax.numpy as jnp
import numpy as np
from jax import lax
from jax.experimental import pallas as pl
from jax.experimental.pallas import tpu as pltpu

F32 = jnp.float32
BF16 = jnp.bfloat16

D_MODEL = 1024
A_HEADS, A_KV_GROUPS, A_HEAD_DIM = 8, 2, 64
CMP_BLOCK, CMP_STRIDE = 32, 16
SLC_BLOCK, SLC_TOPK, N_LOCAL_BLOCKS = 64, 16, 2
WINDOW = 512
FORCE_SCORE = 1e9
B_HEADS, Q_LORA, KV_LORA, NOPE_DIM, ROPE_DIM, V_DIM = 8, 256, 128, 64, 32, 64
ROPE_THETA = 10000.0
C_HEADS, C_HEAD_DIM = 8, 64
C_WIDTH = C_HEADS * C_HEAD_DIM
DECAY_LORA, AAA_LORA, GATE_LORA = 64, 64, 128
GN_EPS = 64e-5
N_BUCKETS, MAX_DISTANCE = 32, 128
D_FF = 2816
EPS = 1e-6
NEG = -1e30
LOG2E = math.log2(math.e)

A_Q = A_HEADS * A_HEAD_DIM
A_KV = A_KV_GROUPS * A_HEAD_DIM
A_GATE = 3 * A_HEADS
A_COLS = A_Q + 6 * A_KV + A_GATE
B_COLS = Q_LORA + KV_LORA + ROPE_DIM
C_COLS = 3 * C_WIDTH + DECAY_LORA + AAA_LORA + GATE_LORA

LANES = 128
FFN_SUB = 256
VMEM_LIMIT = 48 * 1024 * 1024

Z_AQ = 0
Z_QA = 512
Z_AKC, Z_AVC, Z_AKS, Z_AVS, Z_AKW, Z_AVW = 768, 896, 1024, 1152, 1280, 1408
Z_AG = 1536
Z_KVA = 1664
Z_KRA = 1792
Z_KRB = 1920
Z_CR, Z_CK, Z_CV = 2048, 2560, 3072
Z_CWA = 3584
Z_CG = 3712
Z_ZG = 4096
Z_COLS = 7168

TQ = 256
TM = 512
CH = 64
RW_SUB = 4


def _cparams(*sem):
    return pltpu.CompilerParams(dimension_semantics=sem, vmem_limit_bytes=VMEM_LIMIT)


def _gelu(x):
    return 0.5 * x * (1.0 + jnp.tanh(0.7978845608028654 * (x + 0.044715 * (x * x * x))))


def _dot(a, b):
    return jnp.dot(a, b, preferred_element_type=F32)


def _split3(x):
    hi = x.astype(BF16)
    r = x - hi.astype(F32)
    mid = r.astype(BF16)
    return hi, mid, (r - mid.astype(F32)).astype(BF16)


def _dot3(x, w):
    return sum(_dot(t, w) for t in _split3(x))


def _dot2(x, w):
    hi, mid, _ = _split3(x)
    return _dot(hi, w) + _dot(mid, w)


def _dot_t(a, b):
    return lax.dot_general(a, b, (((1,), (1,)), ((), ())), preferred_element_type=F32)


def _adaln_kernel(c_ref, w_ref, b_ref, o_ref):
    c = c_ref[...]
    cond = c * jax.nn.sigmoid(c)
    o_ref[0] = _dot(cond.astype(BF16), w_ref[0].astype(BF16)) + b_ref[0]


def _adaln(c, ada_w, ada_b):
    L, D, N6 = ada_w.shape
    B = c.shape[0]
    tn = 1536
    return pl.pallas_call(
        _adaln_kernel,
        grid=(L, N6 // tn),
        in_specs=[pl.BlockSpec((B, D), lambda l, j: (0, 0)),
                  pl.BlockSpec((1, D, tn), lambda l, j: (l, 0, j)),
                  pl.BlockSpec((1, 1, tn), lambda l, j: (l, 0, j))],
        out_specs=pl.BlockSpec((1, B, tn), lambda l, j: (l, 0, j)),
        out_shape=jax.ShapeDtypeStruct((L, B, N6), F32),
        compiler_params=_cparams("parallel", "parallel"),
        name="adaln",
    )(c, ada_w, ada_b.reshape(L, 1, N6))


def _nmm_kernel(x_ref, g_ref, sc_ref, sh_ref, w_ref, o_ref, h_ref):
    @pl.when(pl.program_id(1) == 0)
    def _():
        x = x_ref[...]
        y = x * lax.rsqrt(jnp.mean(x * x, axis=-1, keepdims=True) + EPS)
        h = (y * g_ref[...]) * (1.0 + sc_ref[0]) + sh_ref[0]
        h_ref[...] = h.astype(BF16)

    o_ref[...] = _dot(h_ref[...], w_ref[...]).astype(o_ref.dtype)


def _norm_mod_matmul(x2, gain, sc, sh, w, S, tn, out_dtype=BF16):
    N, D = x2.shape
    NC = w.shape[1]
    tm = min(1024, S)
    tpb = S // tm
    return pl.pallas_call(
        _nmm_kernel,
        grid=(N // tm, NC // tn),
        in_specs=[pl.BlockSpec((tm, D), lambda i, j: (i, 0)),
                  pl.BlockSpec((1, D), lambda i, j: (0, 0)),
                  pl.BlockSpec((1, 1, D), lambda i, j: (i // tpb, 0, 0)),
                  pl.BlockSpec((1, 1, D), lambda i, j: (i // tpb, 0, 0)),
                  pl.BlockSpec((D, tn), lambda i, j: (0, j))],
        out_specs=pl.BlockSpec((tm, tn), lambda i, j: (i, j)),
        out_shape=jax.ShapeDtypeStruct((N, NC), out_dtype),
        scratch_shapes=[pltpu.VMEM((tm, D), BF16)],
        compiler_params=_cparams("parallel", "arbitrary"),
        name="norm_mod_matmul",
    )(x2, gain.reshape(1, D), sc, sh, w)


def _compress_kernel(zk_ref, zv_ref, w1_ref, pos_ref, w1c_ref, w2_ref, kc_ref, vc_ref, zf_ref, *, NC):
    for kind, (z_ref, o_ref) in enumerate(((zk_ref, kc_ref), (zv_ref, vc_ref))):
        pa = jnp.zeros((NC, LANES), F32)
        pb = jnp.zeros((NC, LANES), F32)
        zf_ref[...] = z_ref[...].astype(F32)
        for l in range(CMP_STRIDE):
            xl = zf_ref[pl.ds(l, NC, stride=CMP_STRIDE), :].astype(BF16)
            pa = pa + _dot(xl, w1_ref[kind, l])
            pb = pb + _dot(xl, w1_ref[kind, CMP_STRIDE + l])
        posb = _dot(pos_ref[kind].astype(BF16), w1c_ref[kind])[0:1, :]
        h = pa + pltpu.roll(pb, NC - 1, axis=0) + posb
        act = _gelu(h).astype(BF16)
        row = lax.broadcasted_iota(jnp.int32, (NC, LANES), 0)
        for g in range(A_KV_GROUPS):
            for eo in range(2):
                out = _dot(act, w2_ref[kind, g, eo])
                out = jnp.where(row < NC - 1, out, 0.0)
                o_ref[g, eo, pl.ds(0, NC), :] = jnp.zeros((NC, LANES), BF16)
                o_ref[g, eo, pl.ds(NC, NC), :] = out.astype(BF16)


def _compress(z, w1bd, posrow, w1cat, w2v, B, S):
    NC = S // CMP_STRIDE
    out = jax.ShapeDtypeStruct((B, A_KV_GROUPS, 2, 2 * NC, LANES), BF16)
    ospec = pl.BlockSpec((None, A_KV_GROUPS, 2, 2 * NC, LANES), lambda b: (b, 0, 0, 0, 0))
    full = lambda a: pl.BlockSpec(a.shape, lambda b: (0,) * a.ndim)
    return pl.pallas_call(
        functools.partial(_compress_kernel, NC=NC),
        grid=(B,),
        in_specs=[pl.BlockSpec((S, LANES), lambda b: (b, Z_AKC // LANES)),
                  pl.BlockSpec((S, LANES), lambda b: (b, Z_AVC // LANES)),
                  full(w1bd), full(posrow), full(w1cat), full(w2v)],
        out_specs=[ospec, ospec],
        out_shape=[out, out],
        scratch_shapes=[pltpu.VMEM((S, LANES), F32)],
        compiler_params=_cparams("parallel"),
        name="nsa_compress",
    )(z, z, w1bd, posrow, w1cat, w2v)


def _stack_pairs(zq, scale):
    q = zq.astype(F32) * scale
    return jnp.concatenate([q[:, :LANES], q[:, LANES:]], axis=0).astype(BF16)


def _cmpsel_kernel(zq_ref, kc_ref, vc_ref, cb_ref, oc_ref, sb_ref, *, NC, NS):
    qt = pl.program_id(2)
    r = TQ // CMP_STRIDE
    lhs = _stack_pairs(zq_ref[...], A_HEAD_DIM ** -0.5)
    st = pl.multiple_of(r * (qt + 1), 16)
    col = lax.broadcasted_iota(jnp.int32, (2 * TQ, NC), 1)
    exists = col >= NC - r * (qt + 1)
    acc = jnp.zeros((2 * TQ, LANES), F32)
    psum = jnp.zeros((TQ, NC), F32)
    for eo in range(2):
        kwin = kc_ref[eo, pl.ds(st, NC), :]
        vwin = vc_ref[eo, pl.ds(st, NC), :]
        logits = _dot_t(lhs, kwin) + cb_ref[eo]
        logits = jnp.where(exists, logits, NEG)
        m = jnp.max(logits, axis=-1, keepdims=True)
        p = jnp.where(logits > 0.1 * NEG, jnp.exp(logits - m), 0.0)
        l = jnp.maximum(jnp.sum(p, axis=-1, keepdims=True), 1e-30)
        p = p * (1.0 / l)
        acc = acc + _dot(p.astype(BF16), vwin)
        psum = psum + p[:TQ] + p[TQ:]
    oc_ref[:, :LANES] = acc[:TQ]
    oc_ref[:, LANES:] = acc[TQ:]

    NR = A_HEAD_DIM
    jj = lax.broadcasted_iota(jnp.int32, (NR, NC), 0)
    nn = lax.broadcasted_iota(jnp.int32, (NR, NC), 1) + (r * (qt + 1) - NC)
    delta = 4 * jj - nn
    mt = jnp.where((delta == 0) | (delta == 4), 1.0, 0.0) + jnp.where((delta >= 1) & (delta <= 3), 2.0, 0.0)
    mt = jnp.where(jj < NS, mt, 0.0)
    mtb = mt.astype(BF16)
    imp_t = sum(_dot_t(mtb, t) for t in _split3(psum))
    jb = lax.broadcasted_iota(jnp.int32, (NR, TQ), 0)
    tpos = qt * TQ + lax.broadcasted_iota(jnp.int32, (NR, TQ), 1)
    back = (tpos >> 6) - jb
    forced = (jb == 0) | ((back >= 0) & (back < N_LOCAL_BLOCKS))
    score = jnp.where(forced, FORCE_SCORE, jnp.where(back >= 0, imp_t, -1.0))
    score = jnp.where(jb < NS, score, -2.0)
    ngrp = -(-NS // 8)
    grp = [score[8 * m:8 * m + 8, :] for m in range(ngrp)]
    rank = [jnp.zeros((8, TQ), F32) for _ in range(ngrp)]
    j8 = lax.broadcasted_iota(jnp.int32, (8, TQ), 0)
    for jp in range(NS):
        row = score[jp:jp + 1, :]
        for m in range(ngrp):
            if m < jp // 8:
                beats = row > grp[m]
            elif m > jp // 8:
                beats = row >= grp[m]
            else:
                beats = (row > grp[m]) | ((row == grp[m]) & (j8 > jp - 8 * m))
            rank[m] = rank[m] + jnp.where(beats, 1.0, 0.0)
    rank = jnp.concatenate(rank + [jnp.zeros((NR - 8 * ngrp, TQ), F32)] * (NR > 8 * ngrp), axis=0)
    sel = (rank < float(min(SLC_TOPK, NS))) & (back >= 0) & (jb < NS)
    sb = jnp.where(sel, 0.0, jnp.where(jb < NS, NEG, 0.0))
    sb_ref[...] = jnp.concatenate([sb, jnp.zeros((LANES - NR, TQ), F32)], axis=0).astype(BF16)


def _cmpsel(z, kcp, vcp, cb, B, S):
    NC = S // CMP_STRIDE
    NS = S // SLC_BLOCK
    nq = S // TQ
    kspec = pl.BlockSpec((None, None, 2, 2 * NC, LANES), lambda b, g, q: (b, g, 0, 0, 0))
    return pl.pallas_call(
        functools.partial(_cmpsel_kernel, NC=NC, NS=NS),
        grid=(B, A_KV_GROUPS, nq),
        in_specs=[pl.BlockSpec((TQ, 2 * LANES), lambda b, g, q: (b * nq + q, g)),
                  kspec, kspec,
                  pl.BlockSpec((None, 2, 2 * TQ, NC), lambda b, g, q: (g, 0, 0, 0))],
        out_specs=[pl.BlockSpec((TQ, 2 * LANES), lambda b, g, q: (b * nq + q, g)),
                   pl.BlockSpec((None, None, LANES, TQ), lambda b, g, q: (b, g, 0, q))],
        out_shape=[jax.ShapeDtypeStruct((B * S, A_Q), F32),
                   jax.ShapeDtypeStruct((B, A_KV_GROUPS, LANES, S), BF16)],
        compiler_params=_cparams("parallel", "parallel", "parallel"),
        name="nsa_cmp_select",
    )(z, kcp, vcp, cb)


def _flash_update_t(s, vt, state, acc_ref):
    m, l = state
    m_new = jnp.maximum(m, jnp.max(s, axis=0, keepdims=True))
    p = jnp.exp2(s - m_new)
    alpha = jnp.exp2(m - m_new)
    acc_ref[...] = alpha * acc_ref[...] + _dot(vt, p.astype(BF16))
    return m_new, alpha * l + jnp.sum(p, axis=0, keepdims=True)


def _flash_pipeline(qk, softmax, lo, hi, state, s_ref, diag_in_loop):
    s_ref[0] = qk(0, lo)

    def step(kc, st):
        s_ref[1] = qk(1, kc)
        st0 = softmax(0, s_ref[0], kc, st[0], False)
        s_ref[0] = qk(0, jnp.minimum(kc + 1, hi))
        st1 = softmax(1, s_ref[1], kc, st[1], False)
        return st0, st1

    end = hi + 1 if diag_in_loop else hi
    npair = (end - lo) // 2
    st = lax.fori_loop(0, npair, lambda i, st: step(lo + 2 * i + 1, step(lo + 2 * i, st)), state)
    st = lax.fori_loop(lo + 2 * npair, end, step, st)
    if diag_in_loop:
        return st
    s_ref[1] = qk(1, hi)
    st0 = softmax(0, s_ref[0], hi, st[0], True)
    st1 = softmax(1, s_ref[1], hi, st[1], True)
    return st0, st1


def _split_heads_kv(slab, g):
    lane = lax.broadcasted_iota(jnp.int32, slab.shape, 1)
    own = jnp.where((lane < A_HEAD_DIM) == (g == 0), slab, 0.0)
    other = pltpu.roll(own, A_HEAD_DIM, axis=1)
    is0 = g == 0
    return jnp.where(is0, own, other), jnp.where(is0, other, own)


def _nsa_flash_kernel(zq_ref, sbt_ref, zks_ref, zvs_ref, zkw_ref, zvw_ref, tab_ref, os_ref, ow_ref,
                      kse_ref, kso_ref, kwe_ref, kwo_ref, vts_ref, vtw_ref, acc_ref, s_ref, *, S):
    g = pl.program_id(1)
    qt = pl.program_id(2)

    @pl.when(qt == 0)
    def _():
        rowb = lax.broadcasted_iota(jnp.int32, (S, LANES), 0) >> 6
        lane = lax.broadcasted_iota(jnp.int32, (S, LANES), 1)
        onehot = jnp.where(rowb == lane, 1.0, 0.0).astype(BF16)
        for zk_ref, zv_ref, ke_ref, ko_ref, vt_ref in ((zks_ref, zvs_ref, kse_ref, kso_ref, vts_ref),
                                                      (zkw_ref, zvw_ref, kwe_ref, kwo_ref, vtw_ref)):
            ke, ko = _split_heads_kv(zk_ref[...].astype(F32), g)
            ke_ref[:, :LANES] = ke.astype(BF16)
            ko_ref[:, :LANES] = ko.astype(BF16)
            for c in range(S // TQ):
                vt_ref[c] = zv_ref[c * TQ:(c + 1) * TQ, :].astype(F32).T.astype(BF16)
        kse_ref[:, LANES:] = onehot
        kso_ref[:, LANES:] = onehot

    zq = zq_ref[...].astype(F32) * (A_HEAD_DIM ** -0.5 * LOG2E)
    rhs_q = jnp.concatenate([zq[:, :LANES], zq[:, LANES:]], axis=0).T.astype(BF16)
    sbt = sbt_ref[...]
    rhs_sel = jnp.concatenate([rhs_q, jnp.concatenate([sbt, sbt], axis=1)], axis=0)
    vrow = pl.multiple_of(g * A_HEAD_DIM, A_HEAD_DIM)
    init = (jnp.full((1, 2 * TQ), 0.5 * NEG, F32), jnp.zeros((1, 2 * TQ), F32))

    def branch(k_refs, vt_ref, rhs, lo, table_of, o_ref):
        acc_ref[...] = jnp.zeros(acc_ref.shape, F32)

        def qk(eo, kc):
            ks = pl.multiple_of(kc * TQ, TQ)
            return _dot(k_refs[eo][pl.ds(ks, TQ), :], rhs)

        def softmax(eo, s, kc, state, diag):
            vt = vt_ref[kc, pl.ds(vrow, A_HEAD_DIM), :]
            return _flash_update_t(s + tab_ref[eo, table_of(qt - kc)], vt, state, acc_ref.at[eo])

        carry = _flash_pipeline(qk, softmax, lo, qt, (init, init), s_ref, True)
        out_t = jnp.concatenate([acc_ref[0] * (1.0 / carry[0][1]), acc_ref[1] * (1.0 / carry[1][1])], axis=0)
        out = out_t.T
        o_ref[:, :LANES] = out[:TQ]
        o_ref[:, LANES:] = out[TQ:]

    branch((kse_ref, kso_ref), vts_ref, rhs_sel, 0, lambda d: jnp.minimum(d, 2), os_ref)
    branch((kwe_ref, kwo_ref), vtw_ref, rhs_q, jnp.maximum(qt - 2, 0),
           lambda d: jnp.where(d == 2, 3, d), ow_ref)


def _nsa_flash(z, sb, tab, B, S):
    nq = S // TQ
    zslab = lambda col: pl.BlockSpec((S, LANES), lambda b, g, q: (b, col // LANES))
    ospec = pl.BlockSpec((TQ, 2 * LANES), lambda b, g, q: (b * nq + q, g))
    oshape = jax.ShapeDtypeStruct((B * S, A_Q), F32)
    return pl.pallas_call(
        functools.partial(_nsa_flash_kernel, S=S),
        grid=(B, A_KV_GROUPS, nq),
        in_specs=[pl.BlockSpec((TQ, 2 * LANES), lambda b, g, q: (b * nq + q, g)),
                  pl.BlockSpec((None, None, LANES, TQ), lambda b, g, q: (b, g, 0, q)),
                  zslab(Z_AKS), zslab(Z_AVS), zslab(Z_AKW), zslab(Z_AVW),
                  pl.BlockSpec((None, 2, 4, TQ, 2 * TQ), lambda b, g, q: (g, 0, 0, 0, 0))],
        out_specs=[ospec, ospec],
        out_shape=[oshape, oshape],
        scratch_shapes=[pltpu.VMEM((S, 2 * LANES), BF16), pltpu.VMEM((S, 2 * LANES), BF16),
                        pltpu.VMEM((S, LANES), BF16), pltpu.VMEM((S, LANES), BF16),
                        pltpu.VMEM((S // TQ, LANES, TQ), BF16), pltpu.VMEM((S // TQ, LANES, TQ), BF16),
                        pltpu.VMEM((2, A_HEAD_DIM, 2 * TQ), F32),
                        pltpu.VMEM((2, TQ, 2 * TQ), F32)],
        compiler_params=_cparams("parallel", "parallel", "arbitrary"),
        name="nsa_selected_window",
    )(z, sb, z, z, z, z, tab)


def _mla_proj_kernel(zqa_ref, zkva_ref, zkra_ref, zkrb_ref, qn_ref, kvn_ref, wqa_ref, wqb_ref,
                     wk_ref, wv_ref, caq_ref, cbq_ref, cak_ref, cbk_ref, q_ref, k_ref, v_ref):
    def rms(x, gain):
        return x * lax.rsqrt(jnp.mean(x * x, axis=-1, keepdims=True) + EPS) * gain

    nq = rms(zqa_ref[...].astype(F32), qn_ref[...]).astype(BF16)
    nkv = rms(zkva_ref[...].astype(F32), kvn_ref[...]).astype(BF16)
    qa = _dot_t(wqa_ref[...], nq)
    qb = _dot_t(wqb_ref[...], nq)
    kn = _dot(nkv, wk_ref[...])
    vt = _dot_t(wv_ref[...], nkv)
    caq, cbq = caq_ref[...], cbq_ref[...]
    kr = zkra_ref[...].astype(F32) * cak_ref[...] + zkrb_ref[...].astype(F32) * cbk_ref[...]
    for h in range(B_HEADS):
        sl = slice(h * LANES, (h + 1) * LANES)
        q_ref[h] = (qa[sl, :] * caq + qb[sl, :] * cbq).astype(BF16)
        k_ref[h] = (kn[:, sl] + kr).astype(BF16)
        v_ref[h] = vt[h * V_DIM:(h + 1) * V_DIM, :].astype(BF16)


def _mla_proj(z, qn, kvn, wqa, wqb, wk, wv, caq, cbq, cak, cbk, B, S):
    tm = TM
    nt = S // tm
    zspec = lambda w, col: pl.BlockSpec((tm, w), lambda b, i: (b * nt + i, col // w))
    full = lambda a: pl.BlockSpec(a.shape, lambda b, i: (0,) * a.ndim)
    tspec = pl.BlockSpec((tm, LANES), lambda b, i: (i, 0))
    tspec_t = pl.BlockSpec((LANES, tm), lambda b, i: (0, i))
    return pl.pallas_call(
        _mla_proj_kernel,
        grid=(B, nt),
        in_specs=[zspec(Q_LORA, Z_QA), zspec(LANES, Z_KVA), zspec(LANES, Z_KRA), zspec(LANES, Z_KRB),
                  full(qn), full(kvn), full(wqa), full(wqb), full(wk), full(wv),
                  tspec_t, tspec_t, tspec, tspec],
        out_specs=[pl.BlockSpec((None, B_HEADS, LANES, tm), lambda b, i: (b, 0, 0, i)),
                   pl.BlockSpec((None, B_HEADS, tm, LANES), lambda b, i: (b, 0, i, 0)),
                   pl.BlockSpec((None, B_HEADS, None, V_DIM, tm), lambda b, i: (b, 0, i, 0, 0))],
        out_shape=[jax.ShapeDtypeStruct((B, B_HEADS, LANES, S), BF16),
                   jax.ShapeDtypeStruct((B, B_HEADS, S, LANES), BF16),
                   jax.ShapeDtypeStruct((B, B_HEADS, nt, V_DIM, tm), BF16)],
        compiler_params=_cparams("parallel", "parallel"),
        name="mla_proj",
    )(z, z, z, z, qn, kvn, wqa, wqb, wk, wv, caq, cbq, cak, cbk)


def _mla_flash_kernel(qt_ref, k_ref, vt_ref, o_ref, acc_ref, s_ref):
    qi = pl.program_id(2)
    acc_ref[...] = jnp.zeros(acc_ref.shape, F32)
    ri = lax.broadcasted_iota(jnp.int32, (TM, TM), 0)
    ci = lax.broadcasted_iota(jnp.int32, (TM, TM), 1)
    causal = ri <= ci

    def qk(hh, kc):
        ks = pl.multiple_of(kc * TM, TM)
        return _dot(k_ref[hh, pl.ds(ks, TM), :], qt_ref[hh])

    def softmax(hh, s, kc, state, diag):
        if diag:
            s = jnp.where(causal, s, NEG)
        return _flash_update_t(s, vt_ref[hh, kc], state, acc_ref.at[hh])

    init = (jnp.full((1, TM), 0.5 * NEG, F32), jnp.zeros((1, TM), F32))
    carry = _flash_pipeline(qk, softmax, 0, qi, (init, init), s_ref, False)
    out_t = jnp.concatenate([acc_ref[0] * (1.0 / carry[0][1]), acc_ref[1] * (1.0 / carry[1][1])], axis=0)
    o_ref[...] = out_t.T


def _mla_flash(qt, k, vt, B, S):
    nq = S // TM
    hp = B_HEADS // 2
    return pl.pallas_call(
        _mla_flash_kernel,
        grid=(B, hp, nq),
        in_specs=[pl.BlockSpec((None, 2, LANES, TM), lambda b, h, i: (b, h, 0, i)),
                  pl.BlockSpec((None, 2, S, LANES), lambda b, h, i: (b, h, 0, 0)),
                  pl.BlockSpec((None, 2, nq, V_DIM, TM), lambda b, h, i: (b, h, 0, 0, 0))],
        out_specs=pl.BlockSpec((TM, LANES), lambda b, h, i: (b * nq + i, h)),
        out_shape=jax.ShapeDtypeStruct((B * S, B_HEADS * V_DIM), F32),
        scratch_shapes=[pltpu.VMEM((2, V_DIM, TM), F32), pltpu.VMEM((2, TM, TM), F32)],
        compiler_params=_cparams("parallel", "parallel", "arbitrary"),
        name="mla_flash",
    )(qt, k, vt)


def _rwkv_prep_kernel(zr_ref, zk_ref, zv_ref, zwa_ref, zg_ref, mu_r, mu_k, mu_v, mu_wa, mu_g,
                      w0_ref, a0_ref, kk_ref, ka_ref, rk_ref, w2_ref, a2_ref, g2_ref, bd_ref,
                      r_o, k_o, v_o, kk_o, be_o, ld_o, g_o, bo_o,
                      c_r, c_k, c_v, c_wa, c_g):
    t = pl.program_id(1)
    tm = zr_ref.shape[0]

    def shifted(z_ref, mu_ref, c_ref):
        x = z_ref[...].astype(F32)
        row = lax.broadcasted_iota(jnp.int32, x.shape, 0)
        prev = jnp.where(t == 0, 0.0, c_ref[0:1, :])
        xs = jnp.where(row == 0, prev, pltpu.roll(x, 1, axis=0))
        c_ref[0:1, :] = x[tm - 1:tm, :]
        return x + (xs - x) * mu_ref[...]

    r = shifted(zr_ref, mu_r, c_r)
    k = shifted(zk_ref, mu_k, c_k)
    v = shifted(zv_ref, mu_v, c_v)
    wa = shifted(zwa_ref, mu_wa, c_wa)
    gd = shifted(zg_ref, mu_g, c_g)
    w = w0_ref[...] + _dot(jnp.tanh(wa).astype(BF16), w2_ref[...])
    ld_o[...] = -jax.nn.sigmoid(w) * math.exp(-0.5)
    a = jax.nn.sigmoid(a0_ref[...] + _dot(wa.astype(BF16), a2_ref[...]))
    g_o[...] = _dot(jax.nn.sigmoid(gd).astype(BF16), g2_ref[...])
    kk = k * kk_ref[...]
    nsq = _dot3(kk * kk, bd_ref[...])
    kk = kk / jnp.maximum(jnp.sqrt(nsq), 1e-12)
    k2 = k * (1.0 + (a - 1.0) * ka_ref[...])
    rks = _dot3(r * k2 * rk_ref[...], bd_ref[...])
    r_o[...] = r
    k_o[...] = k2
    v_o[...] = v
    kk_o[...] = kk
    be_o[...] = kk * a
    bo_o[...] = rks * v


def _rwkv_prep(z, mus, w0, a0, k_k, k_a, r_k, w2p, a2p, g2, bd, B, S, tm=256):
    nt = S // tm
    W = C_WIDTH
    zspec = lambda w, col: pl.BlockSpec((tm, w), lambda b, i: (b * nt + i, col // w))
    full = lambda a: pl.BlockSpec(a.shape, lambda b, i: (0,) * a.ndim)
    ospec = pl.BlockSpec((tm, W), lambda b, i: (b * nt + i, 0))
    oshape = jax.ShapeDtypeStruct((B * S, W), F32)
    consts = list(mus) + [w0, a0, k_k, k_a, r_k, w2p, a2p, g2, bd]
    return pl.pallas_call(
        _rwkv_prep_kernel,
        grid=(B, nt),
        in_specs=[zspec(W, Z_CR), zspec(W, Z_CK), zspec(W, Z_CV), zspec(LANES, Z_CWA), zspec(LANES, Z_CG)]
                 + [full(a) for a in consts],
        out_specs=[ospec] * 8,
        out_shape=[oshape] * 8,
        scratch_shapes=[pltpu.VMEM((8, W), F32)] * 3 + [pltpu.VMEM((8, LANES), F32)] * 2,
        compiler_params=_cparams("parallel", "arbitrary"),
        name="rwkv_prep",
    )(z, z, z, z, z, *consts)


def _rwkv_chunk_kernel(r_ref, k_ref, v_ref, kk_ref, be_ref, ld_ref,
                       qa_o, bkt_o, yu_o, vp_o, pcb_o):
    C = CH
    R = ld_ref.shape[0]
    ld = ld_ref[...]
    rr = lax.broadcasted_iota(jnp.int32, (R, R), 0)
    cc = lax.broadcasted_iota(jnp.int32, (R, R), 1)
    sh = int(math.log2(C))
    tri = jnp.where((cc <= rr) & ((cc >> sh) == (rr >> sh)), 1.0, 0.0).astype(BF16)
    cs = sum(_dot(tri, t) for t in _split3(ld))
    cl = jnp.concatenate([jnp.broadcast_to(cs[(s + 1) * C - 1:(s + 1) * C, :], (C, C_WIDTH))
                          for s in range(R // C)], axis=0)
    ex, exn, exx, exc = jnp.exp(cs), jnp.exp(-cs), jnp.exp(cs - ld), jnp.exp(cl - cs)
    kk, be, k2, v = kk_ref[...], be_ref[...], k_ref[...], v_ref[...]
    at = -kk * exx
    rt = r_ref[...] * ex
    bt = be * exn
    kt = k2 * exn
    btc = be * exc
    ktc = k2 * exc
    pc = jnp.exp(cl)
    ri = lax.broadcasted_iota(jnp.int32, (C, C), 0)
    ci = lax.broadcasted_iota(jnp.int32, (C, C), 1)
    incl = ci <= ri
    strict = ci < ri
    eye = jnp.where(ci == ri, 1.0, 0.0)
    lane = lax.broadcasted_iota(jnp.int32, (C, LANES), 1)
    P = C_HEADS // 2
    pairs = [(s, p) for s in range(R // C) for p in range(P)]
    heads = [(i, hh) for i in range(len(pairs)) for hh in range(2)]
    blk = lambda x, i: x[pairs[i][0] * C:(pairs[i][0] + 1) * C, pairs[i][1] * LANES:(pairs[i][1] + 1) * LANES]
    lhs2 = [jnp.concatenate([blk(at, i), blk(rt, i)], axis=0).astype(BF16) for i in range(len(pairs))]
    bm, km, vm, am = [], [], [], []
    for i, hh in heads:
        msk = (lane < C_HEAD_DIM) if hh == 0 else (lane >= C_HEAD_DIM)
        bm.append(jnp.where(msk, blk(bt, i), 0.0).astype(BF16))
        km.append(jnp.where(msk, blk(kt, i), 0.0).astype(BF16))
        vm.append(jnp.where(msk, blk(v, i), 0.0).astype(BF16))
        am.append(jnp.where(msk, blk(at, i), 0.0).astype(BF16))
    g1 = [_dot_t(lhs2[i], bm[j]) for j, (i, hh) in enumerate(heads)]
    g2 = [_dot_t(lhs2[i], km[j]) for j, (i, hh) in enumerate(heads)]
    lab = [jnp.where(strict, g[:C], 0.0) for g in g1]
    mrb = [jnp.where(incl, g[C:], 0.0).astype(BF16) for g in g1]
    lak = [jnp.where(strict, g[:C], 0.0).astype(BF16) for g in g2]
    mrk = [jnp.where(incl, g[C:], 0.0).astype(BF16) for g in g2]
    w2 = [_dot(a, b) for a, b in zip(lak, vm)]
    yk = [_dot(a, b) for a, b in zip(mrk, vm)]
    tinv = [eye + x for x in lab]
    lp = lab
    for _ in range(int(math.log2(C)) - 1):
        lpb = [x.astype(BF16) for x in lp]
        lp = [_dot(x, x) for x in lpb]
        tinv = [t + _dot(t.astype(BF16), x.astype(BF16)) for t, x in zip(tinv, lp)]
    au = [_dot(t.astype(BF16), jnp.concatenate([a, w.astype(BF16)], axis=1)) for t, a, w in zip(tinv, am, w2)]
    qy = [_dot(m, x.astype(BF16)) for m, x in zip(mrb, au)]
    for i, (s, p) in enumerate(pairs):
        e, o = 2 * i, 2 * i + 1
        ahat = au[e][:, :LANES] + au[o][:, :LANES]
        uhat = au[e][:, LANES:] + au[o][:, LANES:]
        qhat = blk(rt, i) + qy[e][:, :LANES] + qy[o][:, :LANES]
        yi = qy[e][:, LANES:] + qy[o][:, LANES:] + yk[e] + yk[o]
        qa_o[s, p] = jnp.concatenate([qhat, ahat], axis=0).astype(BF16)
        yu_o[s, p] = jnp.concatenate([yi, uhat], axis=0)
        bkt_o[s, p] = jnp.concatenate([blk(btc, i), blk(ktc, i)], axis=0).T.astype(BF16)
        vp_o[s, p] = blk(v, i).astype(BF16)
        pcb_o[s, p] = jnp.concatenate([blk(pc, i), blk(pc, i)], axis=0).T


def _rwkv_chunk(r, k2, v, kk, be, ld, B, S):
    nch = S // CH
    W = C_WIDTH
    P = C_HEADS // 2
    ns = RW_SUB
    ispec = pl.BlockSpec((ns * CH, W), lambda b, c: (b * (nch // ns) + c, 0))
    sq = lambda rows: pl.BlockSpec((None, ns, P, rows, LANES), lambda b, c: (b, c, 0, 0, 0))
    shp = lambda rows, dt: jax.ShapeDtypeStruct((B, nch, P, rows, LANES), dt)
    return pl.pallas_call(
        _rwkv_chunk_kernel,
        grid=(B, nch // ns),
        in_specs=[ispec] * 6,
        out_specs=[sq(2 * CH), sq(LANES), sq(2 * CH), sq(CH), sq(LANES)],
        out_shape=[shp(2 * CH, BF16), shp(LANES, BF16), shp(2 * CH, F32), shp(CH, BF16), shp(LANES, F32)],
        compiler_params=_cparams("parallel", "parallel"),
        name="rwkv_chunk",
    )(r, k2, v, kk, be, ld)


def _rwkv_state_kernel(qa_ref, bkt_ref, yu_ref, vp_ref, pcb_ref, y_ref, ap_ref):
    @pl.when(pl.program_id(1) == 0)
    def _():
        ap_ref[...] = jnp.zeros(ap_ref.shape, F32)

    ri = lax.broadcasted_iota(jnp.int32, (LANES, LANES), 0)
    ci = lax.broadcasted_iota(jnp.int32, (LANES, LANES), 1)
    same_head = (ri < C_HEAD_DIM) == (ci < C_HEAD_DIM)
    idx = [(b, p) for b in range(qa_ref.shape[0]) for p in range(C_HEADS // 2)]
    a = [ap_ref[b, p] for b, p in idx]
    for c in range(qa_ref.shape[1]):
        x = [_dot(qa_ref[b, c, p], s.astype(BF16)) for (b, p), s in zip(idx, a)]
        yu = [yu_ref[b, c, p] for b, p in idx]
        for (b, p), xi, yi in zip(idx, x, yu):
            y_ref[b, c * CH:(c + 1) * CH, p * LANES:(p + 1) * LANES] = xi[:CH] + yi[:CH]
        uv = [jnp.concatenate([(xi[CH:] + yi[CH:]).astype(BF16), vp_ref[b, c, p]], axis=0)
              for (b, p), xi, yi in zip(idx, x, yu)]
        upd = [_dot(bkt_ref[b, c, p], t) for (b, p), t in zip(idx, uv)]
        a = [pcb_ref[b, c, p] * s + jnp.where(same_head, t, 0.0) for (b, p), s, t in zip(idx, a, upd)]
    for (b, p), s in zip(idx, a):
        ap_ref[b, p] = s


def _rwkv_state(qa, bkt, yu, vp, pcb, B, S):
    nch = S // CH
    P = C_HEADS // 2
    nb = 2 if B % 2 == 0 else 1
    nc = 4 if nch % 4 == 0 else 1
    sq = lambda rows: pl.BlockSpec((nb, nc, P, rows, LANES), lambda b, c: (b, c, 0, 0, 0))
    return pl.pallas_call(
        _rwkv_state_kernel,
        grid=(B // nb, nch // nc),
        in_specs=[sq(2 * CH), sq(LANES), sq(2 * CH), sq(CH), sq(LANES)],
        out_specs=pl.BlockSpec((nb, nc * CH, C_WIDTH), lambda b, c: (b, c, 0)),
        out_shape=jax.ShapeDtypeStruct((B, S, C_WIDTH), F32),
        scratch_shapes=[pltpu.VMEM((nb, P, LANES, LANES), F32)],
        compiler_params=_cparams("parallel", "arbitrary"),
        name="rwkv_state",
    )(qa, bkt, yu, vp, pcb).reshape(B * S, C_WIDTH)


def _merge_kernel(oc_ref, os_ref, ow_ref, zag_ref, yb_ref, yr_ref, bo_ref, gg_ref,
                  zga_ref, zgb_ref, zgc_ref, x_ref, wbr_ref, wout_ref, ge_ref, bdm_ref,
                  ln_ref, gain_ref, gt_ref, o_ref):
    sg = jax.nn.sigmoid(zag_ref[...].astype(F32))
    ya = (_dot2(sg, ge_ref[0]) * oc_ref[...] + _dot2(sg, ge_ref[1]) * os_ref[...]
          + _dot2(sg, ge_ref[2]) * ow_ref[...])
    yr = yr_ref[...]
    mean = _dot3(yr, bdm_ref[...])
    d = yr - mean
    var = _dot2(d * d, bdm_ref[...])
    yn = d * lax.rsqrt(var + GN_EPS) * ln_ref[0:1, :] + ln_ref[1:2, :]
    yc = (yn + bo_ref[...]) * gg_ref[...]
    sig = lambda ref: jax.nn.sigmoid(ref[...].astype(F32))
    merged = (sig(zga_ref) * _dot(ya.astype(BF16), wbr_ref[0:A_Q, :])
              + sig(zgb_ref) * _dot(yb_ref[...].astype(BF16), wbr_ref[A_Q:2 * A_Q, :])
              + sig(zgc_ref) * _dot(yc.astype(BF16), wbr_ref[2 * A_Q:3 * A_Q, :]))
    y = _dot(merged.astype(BF16), wout_ref[...])
    yn2 = y * lax.rsqrt(jnp.mean(y * y, axis=-1, keepdims=True) + EPS) * gain_ref[...]
    o_ref[...] = x_ref[...] + gt_ref[0] * yn2


def _merge(oc, os_, ow, z, yb, yr, bo, gg, x2, wbr, wout, gexp, bdm, ln, gain, gt, S, tm=256):
    N, D = x2.shape
    tpb = S // tm
    W = A_Q
    row = lambda w: pl.BlockSpec((tm, w), lambda i: (i, 0))
    zspec = lambda w, col: pl.BlockSpec((tm, w), lambda i: (i, col // w))
    full = lambda a: pl.BlockSpec(a.shape, lambda i: (0,) * a.ndim)
    return pl.pallas_call(
        _merge_kernel,
        grid=(N // tm,),
        in_specs=[row(W), row(W), row(W), zspec(LANES, Z_AG), row(W), row(W), row(W), row(W),
                  zspec(D, Z_ZG), zspec(D, Z_ZG + D), zspec(D, Z_ZG + 2 * D), row(D),
                  full(wbr), full(wout), full(gexp), full(bdm), full(ln),
                  pl.BlockSpec((1, D), lambda i: (0, 0)),
                  pl.BlockSpec((1, 1, D), lambda i: (i // tpb, 0, 0))],
        out_specs=row(D),
        out_shape=jax.ShapeDtypeStruct((N, D), F32),
        compiler_params=_cparams("parallel"),
        name="merge_out",
    )(oc, os_, ow, z, yb, yr, bo, gg, z, z, z, x2, wbr, wout, gexp, bdm, ln, gain.reshape(1, D), gt)


def _ffn_up_kernel(x_ref, g_ref, sc_ref, sh_ref, wg_ref, wv_ref, cw_ref, cb_ref, o_ref,
                   cg_ref, cv_ref, *, tpb):
    i = pl.program_id(0)
    tm = x_ref.shape[0]
    fc = FFN_SUB
    first = (i % tpb) == 0
    x = x_ref[...]
    y = x * lax.rsqrt(jnp.mean(x * x, axis=-1, keepdims=True) + EPS)
    h = ((y * g_ref[...]) * (1.0 + sc_ref[0]) + sh_ref[0]).astype(BF16)
    row = lax.broadcasted_iota(jnp.int32, (8, fc), 0)

    def up(c):
        sl = slice(c * fc, (c + 1) * fc)
        return _dot(h, wg_ref[:, sl]), _dot(h, wv_ref[:, sl])

    def conv(u, c_ref, c, off):
        sl = slice(c * fc, (c + 1) * fc)
        wl = slice(off + c * fc, off + (c + 1) * fc)
        prev = c_ref[:, sl]
        p1 = jnp.where(first, 0.0, prev[7:8])
        p2 = jnp.where(first, 0.0, prev[6:7])
        c_ref[:, sl] = u[tm - 8:tm, :]
        r1 = pltpu.roll(u, 1, axis=0)
        r2 = pltpu.roll(u, 2, axis=0)
        t1 = jnp.where(row == 0, p1, r1[0:8])
        t2 = jnp.where(row == 0, p2, jnp.where(row == 1, p1, r2[0:8]))
        u1 = jnp.concatenate([t1, r1[8:]], axis=0)
        u2 = jnp.concatenate([t2, r2[8:]], axis=0)
        return cw_ref[0:1, wl] * u2 + cw_ref[1:2, wl] * u1 + cw_ref[2:3, wl] * u + cb_ref[:, wl]

    nsub = D_FF // fc
    cur = up(0)
    for c in range(nsub):
        nxt = up(c + 1) if c + 1 < nsub else None
        a = _gelu(conv(cur[0], cg_ref, c, 0)) * conv(cur[1], cv_ref, c, D_FF)
        o_ref[:, c * fc:(c + 1) * fc] = a.astype(BF16)
        cur = nxt


def _ffn_up(x2, gain, sc, sh, w_up, conv_w, conv_b, S, tm=256):
    N, D = x2.shape
    tpb = S // tm
    full = lambda a: pl.BlockSpec(a.shape, lambda i: (0,) * a.ndim)
    return pl.pallas_call(
        functools.partial(_ffn_up_kernel, tpb=tpb),
        grid=(N // tm,),
        in_specs=[pl.BlockSpec((tm, D), lambda i: (i, 0)),
                  pl.BlockSpec((1, D), lambda i: (0, 0)),
                  pl.BlockSpec((1, 1, D), lambda i: (i // tpb, 0, 0)),
                  pl.BlockSpec((1, 1, D), lambda i: (i // tpb, 0, 0)),
                  pl.BlockSpec((D, D_FF), lambda i: (0, 0)),
                  pl.BlockSpec((D, D_FF), lambda i: (0, 1)),
                  full(conv_w), pl.BlockSpec((1, 2 * D_FF), lambda i: (0, 0))],
        out_specs=pl.BlockSpec((tm, D_FF), lambda i: (i, 0)),
        out_shape=jax.ShapeDtypeStruct((N, D_FF), BF16),
        scratch_shapes=[pltpu.VMEM((8, D_FF), F32), pltpu.VMEM((8, D_FF), F32)],
        compiler_params=_cparams("arbitrary"),
        name="ffn_up",
    )(x2, gain.reshape(1, D), sc, sh, w_up, w_up, conv_w, conv_b.reshape(1, 2 * D_FF))


def _ffn_down_kernel(a_ref, wd_ref, x_ref, gain_ref, gt_ref, o_ref):
    f = _dot(a_ref[...], wd_ref[...])
    fn = f * lax.rsqrt(jnp.mean(f * f, axis=-1, keepdims=True) + EPS) * gain_ref[...]
    o_ref[...] = x_ref[...] + gt_ref[0] * fn


def _ffn_down(a, wd, x2, gain, gt, S, tm=512):
    N, D = x2.shape
    tpb = S // tm
    return pl.pallas_call(
        _ffn_down_kernel,
        grid=(N // tm,),
        in_specs=[pl.BlockSpec((tm, D_FF), lambda i: (i, 0)),
                  pl.BlockSpec((D_FF, D), lambda i: (0, 0)),
                  pl.BlockSpec((tm, D), lambda i: (i, 0)),
                  pl.BlockSpec((1, D), lambda i: (0, 0)),
                  pl.BlockSpec((1, 1, D), lambda i: (i // tpb, 0, 0))],
        out_specs=pl.BlockSpec((tm, D), lambda i: (i, 0)),
        out_shape=jax.ShapeDtypeStruct((N, D), F32),
        compiler_params=_cparams("parallel"),
        name="ffn_down",
    )(a, wd, x2, gain.reshape(1, D), gt)


def _t5_bucket_np(dist):
    d = np.maximum(dist, 0)
    max_exact = N_BUCKETS // 2
    large = max_exact + (np.log(np.maximum(d, 1).astype(np.float32) / max_exact)
                         / math.log(MAX_DISTANCE / max_exact) * (N_BUCKETS - max_exact)).astype(np.int32)
    return np.where(d < max_exact, d, np.minimum(large, N_BUCKETS - 1))


def _bias_table_kernel(scale_ref, rb_ref, code_ref, o_ref):
    code = code_ref[0]
    sc = scale_ref[pl.program_id(0)]
    for h in range(A_HEADS):
        acc = jnp.full(code.shape, NEG, F32)
        for b in range(N_BUCKETS):
            acc = jnp.where(code == b, rb_ref[b, h], acc)
        o_ref[0, h] = acc * sc


def _bias_tables(rel_bias, codes, scales):
    nt, R, C = codes.shape
    smem = pl.BlockSpec(memory_space=pltpu.SMEM)
    return pl.pallas_call(
        _bias_table_kernel,
        grid=(nt,),
        in_specs=[smem, smem, pl.BlockSpec((1, R, C), lambda t: (t, 0, 0))],
        out_specs=pl.BlockSpec((1, A_HEADS, R, C), lambda t: (t, 0, 0, 0)),
        out_shape=jax.ShapeDtypeStruct((nt, A_HEADS, R, C), F32),
        compiler_params=_cparams("parallel"),
        name="bias_tables",
    )(jnp.asarray(scales, F32), rel_bias, jnp.asarray(codes))


def _codes(dist, lo, hi):
    return np.where((dist >= lo) & (dist < hi), _t5_bucket_np(dist), N_BUCKETS).astype(np.int32)


def _nsa_tables(rel_bias, S):
    NC = S // CMP_STRIDE
    G, HPG = A_KV_GROUPS, A_HEADS // A_KV_GROUPS
    big = 1 << 30
    i = np.arange(TQ)[:, None]
    dc = i - CMP_STRIDE * np.arange(NC)[None, :] + CMP_STRIDE * NC - TQ - (CMP_BLOCK - 1)
    cb = _bias_tables(rel_bias, _codes(dc, 0, big)[None], [1.0])[0]
    cb = cb.reshape(G, HPG // 2, 2, TQ, NC).transpose(0, 2, 1, 3, 4).reshape(G, 2, 2 * TQ, NC)
    dt = np.arange(TQ)[None, :] - np.arange(TQ)[:, None]
    far = np.full((TQ, TQ), MAX_DISTANCE)
    codes = np.stack([_codes(dt, 0, WINDOW), _codes(TQ + dt, 0, WINDOW), _codes(far, 0, big),
                      _codes(2 * TQ + dt, 0, WINDOW)])
    t = _bias_tables(rel_bias, codes, [LOG2E] * 4)
    t = t.reshape(4, G, HPG // 2, 2, TQ, TQ).transpose(1, 3, 0, 4, 2, 5).reshape(G, 2, 4, TQ, 2 * TQ)
    return cb, t


def _pad_cols(w, n):
    return jnp.pad(w, ((0, 0), (0, n - w.shape[1])))


def _prep_w_in(w):
    D = w.shape[0]
    za, zb, zc, zg = jnp.split(w, np.cumsum([A_COLS, B_COLS, C_COLS]).tolist(), axis=1)
    a_parts = jnp.split(za, np.cumsum([A_Q] + [A_KV] * 6).tolist(), axis=1)
    qa, kva, kr = jnp.split(zb, [Q_LORA, Q_LORA + KV_LORA], axis=1)
    half = ROPE_DIM // 2
    kr_rot = jnp.concatenate([-kr[:, half:], kr[:, :half]], axis=1)
    z64 = jnp.zeros((D, NOPE_DIM), w.dtype)
    z32 = jnp.zeros((D, LANES - NOPE_DIM - ROPE_DIM), w.dtype)
    c_r, c_k, c_v, c_wd, c_ad, c_gd = jnp.split(
        zc, np.cumsum([C_WIDTH] * 3 + [DECAY_LORA, AAA_LORA]).tolist(), axis=1)
    cols = [a_parts[0], qa] + a_parts[1:7] + [
        _pad_cols(a_parts[7], LANES), kva,
        jnp.concatenate([z64, kr, z32], axis=1),
        jnp.concatenate([z64, kr_rot, z32], axis=1),
        c_r, c_k, c_v, c_wd, c_ad, c_gd, jnp.zeros((D, Z_ZG - Z_CG - LANES), w.dtype), zg]
    out = jnp.concatenate(cols, axis=1)
    assert out.shape[1] == Z_COLS
    return out.astype(BF16)


def _prep_mla(w_uq, w_ukv):
    dq = NOPE_DIM + ROPE_DIM
    half = ROPE_DIM // 2
    wq = w_uq.reshape(Q_LORA, B_HEADS, dq)
    nope, r1, r2 = wq[..., :NOPE_DIM], wq[..., NOPE_DIM:NOPE_DIM + half], wq[..., NOPE_DIM + half:]
    zq = jnp.zeros((Q_LORA, B_HEADS, LANES - dq), w_uq.dtype)
    wqa = jnp.concatenate([nope, r1, r2, zq], axis=-1).reshape(Q_LORA, B_HEADS * LANES)
    wqb = jnp.concatenate([jnp.zeros_like(nope), -r2, r1, zq], axis=-1).reshape(Q_LORA, B_HEADS * LANES)
    wkv = w_ukv.reshape(KV_LORA, B_HEADS, NOPE_DIM + V_DIM)
    kn, vv = wkv[..., :NOPE_DIM], wkv[..., NOPE_DIM:]
    wk = jnp.concatenate([kn, jnp.zeros_like(kn)], axis=-1).reshape(KV_LORA, B_HEADS * LANES)
    wv = vv.reshape(KV_LORA, B_HEADS * V_DIM)
    return wqa.T.astype(BF16), wqb.T.astype(BF16), wk.astype(BF16), wv.T.astype(BF16)


def _rope_tables(S):
    half = ROPE_DIM // 2
    inv = ROPE_THETA ** (-jnp.arange(half, dtype=F32) / half)
    ang = jnp.arange(S, dtype=F32)[:, None] * inv
    cos2 = jnp.tile(jnp.cos(ang), (1, 2))
    sin2 = jnp.tile(jnp.sin(ang), (1, 2))
    scale = (NOPE_DIM + ROPE_DIM) ** -0.5 * LOG2E
    one = jnp.ones((S, NOPE_DIM), F32)
    z64 = jnp.zeros((S, NOPE_DIM), F32)
    z32 = jnp.zeros((S, LANES - NOPE_DIM - ROPE_DIM), F32)
    caq = (jnp.concatenate([one, cos2, z32], axis=1) * scale).T
    cbq = (jnp.concatenate([z64, sin2, z32], axis=1) * scale).T
    cak = jnp.concatenate([z64, cos2, z32], axis=1)
    cbk = jnp.concatenate([z64, sin2, z32], axis=1)
    return caq, cbq, cak, cbk


def _prep_compress(cmp_pos, cmp_w1, cmp_w2):
    Dh = A_HEAD_DIM
    w1 = cmp_w1.reshape(2, CMP_BLOCK, Dh, Dh)
    z = jnp.zeros_like(w1)
    w1bd = jnp.concatenate([jnp.concatenate([w1, z], axis=-1), jnp.concatenate([z, w1], axis=-1)], axis=-2)
    posrow = jnp.broadcast_to(cmp_pos.reshape(2, 1, CMP_BLOCK * Dh), (2, 8, CMP_BLOCK * Dh))
    w1cat = jnp.concatenate([cmp_w1, cmp_w1], axis=-1)
    zz = jnp.zeros((2, Dh, Dh), cmp_w2.dtype)
    blk = lambda a, b, c, d: jnp.concatenate(
        [jnp.concatenate([a, b], axis=-1), jnp.concatenate([c, d], axis=-1)], axis=-2)
    w2v = jnp.stack([jnp.stack([blk(cmp_w2, zz, zz, zz), blk(zz, cmp_w2, zz, zz)], axis=1),
                     jnp.stack([blk(zz, zz, cmp_w2, zz), blk(zz, zz, zz, cmp_w2)], axis=1)], axis=1)
    return w1bd.astype(BF16), posrow, w1cat.astype(BF16), w2v.astype(BF16)


def _gate_expand():
    e = np.zeros((3, LANES, A_Q), np.float32)
    for h in range(A_HEADS):
        for r in range(3):
            e[r, 3 * h + r, h * A_HEAD_DIM:(h + 1) * A_HEAD_DIM] = 1.0
    return jnp.asarray(e, dtype=BF16)


def _block_diag_ones(scale):
    idx = np.arange(C_WIDTH) // C_HEAD_DIM
    return jnp.asarray((idx[:, None] == idx[None, :]).astype(np.float32) * scale, dtype=BF16)


def kernel(x, c, rel_bias, ada_w, ada_b, norm_gain, w_in, nsa_cmp_pos, nsa_cmp_w1, nsa_cmp_w2, mla_q_norm, mla_kv_norm, mla_w_uq, mla_w_ukv, rwkv_mu, rwkv_w0, rwkv_a0, rwkv_k_k, rwkv_k_a, rwkv_w2, rwkv_a2, rwkv_g2, rwkv_r_k, rwkv_ln, w_branch, w_out, ffn_up, ffn_conv_w, ffn_conv_b, ffn_down):
    B, S, D = x.shape
    L = ada_w.shape[0]
    assert S % TQ == 0 and S % TM == 0 and S // SLC_BLOCK <= A_HEAD_DIM and S >= 2 * TQ
    mod = _adaln(c, ada_w, ada_b)
    cb, nsa_tab = _nsa_tables(rel_bias, S)
    caq, cbq, cak, cbk = _rope_tables(S)
    gexp = _gate_expand()
    bd1 = _block_diag_ones(1.0)
    bdm = _block_diag_ones(1.0 / C_HEAD_DIM)
    row = lambda v: v.reshape(1, -1)
    x2 = x.reshape(B * S, D)
    for l in range(L):
        m6 = mod[l].reshape(B, 6, 1, D)
        sh1, sc1, gt1, sh2, sc2, gt2 = (m6[:, i] for i in range(6))
        z = _norm_mod_matmul(x2, norm_gain[l, 0], sc1, sh1, _prep_w_in(w_in[l]), S, tn=Z_COLS // 4)
        kcp, vcp = _compress(z, *_prep_compress(nsa_cmp_pos[l], nsa_cmp_w1[l], nsa_cmp_w2[l]), B, S)
        oc, sb = _cmpsel(z, kcp, vcp, cb, B, S)
        os_, ow = _nsa_flash(z, sb, nsa_tab, B, S)
        wqa, wqb, wk, wv = _prep_mla(mla_w_uq[l], mla_w_ukv[l])
        q, k, v = _mla_proj(z, row(mla_q_norm[l]), row(mla_kv_norm[l]), wqa, wqb, wk, wv,
                            caq, cbq, cak, cbk, B, S)
        yb = _mla_flash(q, k, v, B, S)
        mu = rwkv_mu[l]
        o = 3 * C_WIDTH
        mus = [row(mu[:C_WIDTH]), row(mu[C_WIDTH:2 * C_WIDTH]), row(mu[2 * C_WIDTH:o]),
               row(mu[o:o + LANES]), row(mu[o + LANES:])]
        w2p = jnp.concatenate([rwkv_w2[l], jnp.zeros_like(rwkv_a2[l])], axis=0).astype(BF16)
        a2p = jnp.concatenate([jnp.zeros_like(rwkv_w2[l]), rwkv_a2[l]], axis=0).astype(BF16)
        rr, k2, vv, kk, be, ld, gg, bo = _rwkv_prep(
            z, mus, row(rwkv_w0[l]), row(rwkv_a0[l]), row(rwkv_k_k[l]), row(rwkv_k_a[l]),
            row(rwkv_r_k[l]), w2p, a2p, rwkv_g2[l].astype(BF16), bd1, B, S)
        qa_, bkt, yu, vp, pcb = _rwkv_chunk(rr, k2, vv, kk, be, ld, B, S)
        yr = _rwkv_state(qa_, bkt, yu, vp, pcb, B, S)
        x2 = _merge(oc, os_, ow, z, yb, yr, bo, gg, x2, w_branch[l].astype(BF16), w_out[l].astype(BF16),
                    gexp, bdm, rwkv_ln[l], norm_gain[l, 1], gt1, S)
        a = _ffn_up(x2, norm_gain[l, 2], sc2, sh2, ffn_up[l].astype(BF16), ffn_conv_w[l], ffn_conv_b[l], S)
        x2 = _ffn_down(a, ffn_down[l].astype(BF16), x2, norm_gain[l, 3], gt2, S)
    return x2.reshape(B, S, D)
```

```python
import functools
import math

import jax
import jax.numpy as jnp
import numpy as np
from jax import lax
from jax.experimental import pallas as pl
from jax.experimental.pallas import tpu as pltpu

F32 = jnp.float32
BF16 = jnp.bfloat16

D_MODEL = 1024
A_HEADS, A_KV_GROUPS, A_HEAD_DIM = 8, 2, 64
CMP_BLOCK, CMP_STRIDE = 32, 16
SLC_BLOCK, SLC_TOPK, N_LOCAL_BLOCKS = 64, 16, 2
WINDOW = 512
FORCE_SCORE = 1e9
B_HEADS, Q_LORA, KV_LORA, NOPE_DIM, ROPE_DIM, V_DIM = 8, 256, 128, 64, 32, 64
ROPE_THETA = 10000.0
C_HEADS, C_HEAD_DIM = 8, 64
C_WIDTH = C_HEADS * C_HEAD_DIM
DECAY_LORA, AAA_LORA, GATE_LORA = 64, 64, 128
GN_EPS = 64e-5
N_BUCKETS, MAX_DISTANCE = 32, 128
D_FF = 2816
EPS = 1e-6
NEG = -1e30
LOG2E = math.log2(math.e)

A_Q = A_HEADS * A_HEAD_DIM
A_KV = A_KV_GROUPS * A_HEAD_DIM
A_GATE = 3 * A_HEADS
A_COLS = A_Q + 6 * A_KV + A_GATE
B_COLS = Q_LORA + KV_LORA + ROPE_DIM
C_COLS = 3 * C_WIDTH + DECAY_LORA + AAA_LORA + GATE_LORA

LANES = 128
FFN_SUB = 256
VMEM_LIMIT = 48 * 1024 * 1024

Z_AQ = 0
Z_QA = 512
Z_AKC, Z_AVC, Z_AKS, Z_AVS, Z_AKW, Z_AVW = 768, 896, 1024, 1152, 1280, 1408
Z_AG = 1536
Z_KVA = 1664
Z_KRA = 1792
Z_KRB = 1920
Z_CR, Z_CK, Z_CV = 2048, 2560, 3072
Z_CWA = 3584
Z_CG = 3712
Z_ZG = 4096
Z_COLS = 7168

TQ = 256
TM = 512
MLA_PAIRS = 2
CH = 64
RW_SUB = 4


def _cparams(*sem):
    return pltpu.CompilerParams(dimension_semantics=sem, vmem_limit_bytes=VMEM_LIMIT)


def _gelu(x):
    return 0.5 * x * (1.0 + jnp.tanh(0.7978845608028654 * (x + 0.044715 * (x * x * x))))


def _dot(a, b):
    return jnp.dot(a, b, preferred_element_type=F32)


def _split3(x):
    hi = x.astype(BF16)
    r = x - hi.astype(F32)
    mid = r.astype(BF16)
    return hi, mid, (r - mid.astype(F32)).astype(BF16)


def _dot3(x, w):
    return sum(_dot(t, w) for t in _split3(x))


def _dot2(x, w):
    hi, mid, _ = _split3(x)
    return _dot(hi, w) + _dot(mid, w)


def _dot_t(a, b):
    return lax.dot_general(a, b, (((1,), (1,)), ((), ())), preferred_element_type=F32)


def _adaln_kernel(c_ref, w_ref, b_ref, o_ref):
    c = c_ref[...]
    cond = c * jax.nn.sigmoid(c)
    o_ref[0] = _dot(cond.astype(BF16), w_ref[0].astype(BF16)) + b_ref[0]


def _adaln(c, ada_w, ada_b):
    L, D, N6 = ada_w.shape
    B = c.shape[0]
    tn = 1536
    return pl.pallas_call(
        _adaln_kernel,
        grid=(L, N6 // tn),
        in_specs=[pl.BlockSpec((B, D), lambda l, j: (0, 0)),
                  pl.BlockSpec((1, D, tn), lambda l, j: (l, 0, j)),
                  pl.BlockSpec((1, 1, tn), lambda l, j: (l, 0, j))],
        out_specs=pl.BlockSpec((1, B, tn), lambda l, j: (l, 0, j)),
        out_shape=jax.ShapeDtypeStruct((L, B, N6), F32),
        compiler_params=_cparams("parallel", "parallel"),
        name="adaln",
    )(c, ada_w, ada_b.reshape(L, 1, N6))


def _nmm_kernel(x_ref, g_ref, sc_ref, sh_ref, w_ref, o_ref, h_ref):
    @pl.when(pl.program_id(1) == 0)
    def _():
        x = x_ref[...]
        y = x * lax.rsqrt(jnp.mean(x * x, axis=-1, keepdims=True) + EPS)
        h = (y * g_ref[...]) * (1.0 + sc_ref[0]) + sh_ref[0]
        h_ref[...] = h.astype(BF16)

    o_ref[...] = _dot(h_ref[...], w_ref[...]).astype(o_ref.dtype)


def _norm_mod_matmul(x2, gain, sc, sh, w, S, tn, out_dtype=BF16):
    N, D = x2.shape
    NC = w.shape[1]
    tm = min(1024, S)
    tpb = S // tm
    return pl.pallas_call(
        _nmm_kernel,
        grid=(N // tm, NC // tn),
        in_specs=[pl.BlockSpec((tm, D), lambda i, j: (i, 0)),
                  pl.BlockSpec((1, D), lambda i, j: (0, 0)),
                  pl.BlockSpec((1, 1, D), lambda i, j: (i // tpb, 0, 0)),
                  pl.BlockSpec((1, 1, D), lambda i, j: (i // tpb, 0, 0)),
                  pl.BlockSpec((D, tn), lambda i, j: (0, j))],
        out_specs=pl.BlockSpec((tm, tn), lambda i, j: (i, j)),
        out_shape=jax.ShapeDtypeStruct((N, NC), out_dtype),
        scratch_shapes=[pltpu.VMEM((tm, D), BF16)],
        compiler_params=_cparams("parallel", "arbitrary"),
        name="norm_mod_matmul",
    )(x2, gain.reshape(1, D), sc, sh, w)


def _compress_kernel(zk_ref, zv_ref, w1_ref, pos_ref, w1c_ref, w2_ref, kc_ref, vc_ref, zf_ref, *, NC):
    for kind, (z_ref, o_ref) in enumerate(((zk_ref, kc_ref), (zv_ref, vc_ref))):
        pa = jnp.zeros((NC, LANES), F32)
        pb = jnp.zeros((NC, LANES), F32)
        zf_ref[...] = z_ref[...].astype(F32)
        for l in range(CMP_STRIDE):
            xl = zf_ref[pl.ds(l, NC, stride=CMP_STRIDE), :].astype(BF16)
            pa = pa + _dot(xl, w1_ref[kind, l])
            pb = pb + _dot(xl, w1_ref[kind, CMP_STRIDE + l])
        posb = _dot(pos_ref[kind].astype(BF16), w1c_ref[kind])[0:1, :]
        h = pa + pltpu.roll(pb, NC - 1, axis=0) + posb
        act = _gelu(h).astype(BF16)
        row = lax.broadcasted_iota(jnp.int32, (NC, LANES), 0)
        for g in range(A_KV_GROUPS):
            for eo in range(2):
                out = _dot(act, w2_ref[kind, g, eo])
                out = jnp.where(row < NC - 1, out, 0.0)
                o_ref[g, eo, pl.ds(0, NC), :] = jnp.zeros((NC, LANES), BF16)
                o_ref[g, eo, pl.ds(NC, NC), :] = out.astype(BF16)


def _compress(z, w1bd, posrow, w1cat, w2v, B, S):
    NC = S // CMP_STRIDE
    out = jax.ShapeDtypeStruct((B, A_KV_GROUPS, 2, 2 * NC, LANES), BF16)
    ospec = pl.BlockSpec((None, A_KV_GROUPS, 2, 2 * NC, LANES), lambda b: (b, 0, 0, 0, 0))
    full = lambda a: pl.BlockSpec(a.shape, lambda b: (0,) * a.ndim)
    return pl.pallas_call(
        functools.partial(_compress_kernel, NC=NC),
        grid=(B,),
        in_specs=[pl.BlockSpec((S, LANES), lambda b: (b, Z_AKC // LANES)),
                  pl.BlockSpec((S, LANES), lambda b: (b, Z_AVC // LANES)),
                  full(w1bd), full(posrow), full(w1cat), full(w2v)],
        out_specs=[ospec, ospec],
        out_shape=[out, out],
        scratch_shapes=[pltpu.VMEM((S, LANES), F32)],
        compiler_params=_cparams("parallel"),
        name="nsa_compress",
    )(z, z, w1bd, posrow, w1cat, w2v)


def _stack_pairs(zq, scale):
    q = zq.astype(F32) * scale
    return jnp.concatenate([q[:, :LANES], q[:, LANES:]], axis=0).astype(BF16)


def _cmpsel_kernel(zq_ref, kc_ref, vc_ref, cb_ref, oc_ref, sb_ref, *, NC, NS):
    qt = pl.program_id(1)
    for g in range(A_KV_GROUPS):
        cols = pl.ds(g * 2 * LANES, 2 * LANES)
        _cmpsel_group(zq_ref.at[:, cols], kc_ref.at[g], vc_ref.at[g], cb_ref.at[g],
                      oc_ref.at[:, cols], sb_ref.at[g], qt, NC, NS)


def _cmpsel_group(zq_ref, kc_ref, vc_ref, cb_ref, oc_ref, sb_ref, qt, NC, NS):
    r = TQ // CMP_STRIDE
    lhs = _stack_pairs(zq_ref[...], A_HEAD_DIM ** -0.5)
    st = pl.multiple_of(r * (qt + 1), 16)
    col = lax.broadcasted_iota(jnp.int32, (2 * TQ, NC), 1)
    exists = col >= NC - r * (qt + 1)
    acc = jnp.zeros((2 * TQ, LANES), F32)
    psum = jnp.zeros((TQ, NC), F32)
    for eo in range(2):
        kwin = kc_ref[eo, pl.ds(st, NC), :]
        vwin = vc_ref[eo, pl.ds(st, NC), :]
        logits = _dot_t(lhs, kwin) + cb_ref[eo]
        logits = jnp.where(exists, logits, NEG)
        m = jnp.max(logits, axis=-1, keepdims=True)
        p = jnp.where(logits > 0.1 * NEG, jnp.exp(logits - m), 0.0)
        l = jnp.maximum(jnp.sum(p, axis=-1, keepdims=True), 1e-30)
        p = p * (1.0 / l)
        acc = acc + _dot(p.astype(BF16), vwin)
        psum = psum + p[:TQ] + p[TQ:]
    oc_ref[:, :LANES] = acc[:TQ]
    oc_ref[:, LANES:] = acc[TQ:]

    NR = A_HEAD_DIM
    jj = lax.broadcasted_iota(jnp.int32, (NR, NC), 0)
    nn = lax.broadcasted_iota(jnp.int32, (NR, NC), 1) + (r * (qt + 1) - NC)
    delta = 4 * jj - nn
    mt = jnp.where((delta == 0) | (delta == 4), 1.0, 0.0) + jnp.where((delta >= 1) & (delta <= 3), 2.0, 0.0)
    mt = jnp.where(jj < NS, mt, 0.0)
    mtb = mt.astype(BF16)
    imp_t = sum(_dot_t(mtb, t) for t in _split3(psum))
    jb = lax.broadcasted_iota(jnp.int32, (NR, TQ), 0)
    tpos = qt * TQ + lax.broadcasted_iota(jnp.int32, (NR, TQ), 1)
    back = (tpos >> 6) - jb
    forced = (jb == 0) | ((back >= 0) & (back < N_LOCAL_BLOCKS))
    score = jnp.where(forced, FORCE_SCORE, jnp.where(back >= 0, imp_t, -1.0))
    score = jnp.where(jb < NS, score, -2.0)
    ngrp = -(-NS // 8)
    grp = [score[8 * m:8 * m + 8, :] for m in range(ngrp)]
    rank = [jnp.zeros((8, TQ), F32) for _ in range(ngrp)]
    j8 = lax.broadcasted_iota(jnp.int32, (8, TQ), 0)
    for jp in range(NS):
        row = score[jp:jp + 1, :]
        for m in range(ngrp):
            if m < jp // 8:
                beats = row > grp[m]
            elif m > jp // 8:
                beats = row >= grp[m]
            else:
                beats = (row > grp[m]) | ((row == grp[m]) & (j8 > jp - 8 * m))
            rank[m] = rank[m] + jnp.where(beats, 1.0, 0.0)
    rank = jnp.concatenate(rank + [jnp.zeros((NR - 8 * ngrp, TQ), F32)] * (NR > 8 * ngrp), axis=0)
    sel = (rank < float(min(SLC_TOPK, NS))) & (back >= 0) & (jb < NS)
    sb = jnp.where(sel, 0.0, jnp.where(jb < NS, NEG, 0.0))
    sb_ref[...] = jnp.concatenate([sb, jnp.zeros((LANES - NR, TQ), F32)], axis=0).astype(BF16)


def _cmpsel(z, kcp, vcp, cb, B, S):
    NC = S // CMP_STRIDE
    NS = S // SLC_BLOCK
    nq = S // TQ
    G = A_KV_GROUPS
    kspec = pl.BlockSpec((None, G, 2, 2 * NC, LANES), lambda b, q: (b, 0, 0, 0, 0))
    return pl.pallas_call(
        functools.partial(_cmpsel_kernel, NC=NC, NS=NS),
        grid=(B, nq),
        in_specs=[pl.BlockSpec((TQ, A_Q), lambda b, q: (b * nq + q, 0)),
                  kspec, kspec,
                  pl.BlockSpec((G, 2, 2 * TQ, NC), lambda b, q: (0, 0, 0, 0))],
        out_specs=[pl.BlockSpec((TQ, A_Q), lambda b, q: (b * nq + q, 0)),
                   pl.BlockSpec((None, G, LANES, TQ), lambda b, q: (b, 0, 0, q))],
        out_shape=[jax.ShapeDtypeStruct((B * S, A_Q), F32),
                   jax.ShapeDtypeStruct((B, G, LANES, S), BF16)],
        compiler_params=_cparams("parallel", "parallel"),
        name="nsa_cmp_select",
    )(z, kcp, vcp, cb)


def _flash_update_t(s, vt, state, acc_ref):
    m, l = state
    m_new = jnp.maximum(m, jnp.max(s, axis=0, keepdims=True))
    p = jnp.exp2(s - m_new)
    alpha = jnp.exp2(m - m_new)
    acc_ref[...] = alpha * acc_ref[...] + _dot(vt, p.astype(BF16))
    return m_new, alpha * l + jnp.sum(p, axis=0, keepdims=True)


def _flash_pipeline(qk, softmax, lo, hi, state, s_ref, diag_in_loop):
    s_ref[0] = qk(0, lo)

    def step(kc, st):
        s_ref[1] = qk(1, kc)
        st0 = softmax(0, s_ref[0], kc, st[0], False)
        s_ref[0] = qk(0, jnp.minimum(kc + 1, hi))
        st1 = softmax(1, s_ref[1], kc, st[1], False)
        return st0, st1

    end = hi + 1 if diag_in_loop else hi
    npair = (end - lo) // 2
    st = lax.fori_loop(0, npair, lambda i, st: step(lo + 2 * i + 1, step(lo + 2 * i, st)), state)
    st = lax.fori_loop(lo + 2 * npair, end, step, st)
    if diag_in_loop:
        return st
    s_ref[1] = qk(1, hi)
    st0 = softmax(0, s_ref[0], hi, st[0], True)
    st1 = softmax(1, s_ref[1], hi, st[1], True)
    return st0, st1


def _split_heads_kv(slab, g):
    lane = lax.broadcasted_iota(jnp.int32, slab.shape, 1)
    own = jnp.where((lane < A_HEAD_DIM) == (g == 0), slab, 0.0)
    other = pltpu.roll(own, A_HEAD_DIM, axis=1)
    is0 = g == 0
    return jnp.where(is0, own, other), jnp.where(is0, other, own)


def _nsa_flash_kernel(zq_ref, sbt_ref, zks_ref, zvs_ref, zkw_ref, zvw_ref, tab_ref, os_ref, ow_ref,
                      kse_ref, kso_ref, kwe_ref, kwo_ref, vts_ref, vtw_ref, acc_ref, s_ref, *, S):
    g = pl.program_id(1)
    qt = pl.program_id(2)

    @pl.when(qt == 0)
    def _():
        rowb = lax.broadcasted_iota(jnp.int32, (S, LANES), 0) >> 6
        lane = lax.broadcasted_iota(jnp.int32, (S, LANES), 1)
        onehot = jnp.where(rowb == lane, 1.0, 0.0).astype(BF16)
        for zk_ref, zv_ref, ke_ref, ko_ref, vt_ref in ((zks_ref, zvs_ref, kse_ref, kso_ref, vts_ref),
                                                      (zkw_ref, zvw_ref, kwe_ref, kwo_ref, vtw_ref)):
            ke, ko = _split_heads_kv(zk_ref[...].astype(F32), g)
            ke_ref[:, :LANES] = ke.astype(BF16)
            ko_ref[:, :LANES] = ko.astype(BF16)
            for c in range(S // TQ):
                vt_ref[c] = zv_ref[c * TQ:(c + 1) * TQ, :].astype(F32).T.astype(BF16)
        kse_ref[:, LANES:] = onehot
        kso_ref[:, LANES:] = onehot

    zq = zq_ref[...].astype(F32) * (A_HEAD_DIM ** -0.5 * LOG2E)
    rhs_q = jnp.concatenate([zq[:, :LANES], zq[:, LANES:]], axis=0).T.astype(BF16)
    sbt = sbt_ref[...]
    rhs_sel = jnp.concatenate([rhs_q, jnp.concatenate([sbt, sbt], axis=1)], axis=0)
    vrow = pl.multiple_of(g * A_HEAD_DIM, A_HEAD_DIM)
    init = (jnp.full((1, 2 * TQ), 0.5 * NEG, F32), jnp.zeros((1, 2 * TQ), F32))

    def branch(k_refs, vt_ref, rhs, lo, table_of, o_ref):
        acc_ref[...] = jnp.zeros(acc_ref.shape, F32)

        def qk(eo, kc):
            ks = pl.multiple_of(kc * TQ, TQ)
            return _dot(k_refs[eo][pl.ds(ks, TQ), :], rhs)

        def softmax(eo, s, kc, state, diag):
            vt = vt_ref[kc, pl.ds(vrow, A_HEAD_DIM), :]
            return _flash_update_t(s + tab_ref[eo, table_of(qt - kc)], vt, state, acc_ref.at[eo])

        carry = _flash_pipeline(qk, softmax, lo, qt, (init, init), s_ref, True)
        out_t = jnp.concatenate([acc_ref[0] * (1.0 / carry[0][1]), acc_ref[1] * (1.0 / carry[1][1])], axis=0)
        out = out_t.T
        o_ref[:, :LANES] = out[:TQ]
        o_ref[:, LANES:] = out[TQ:]

    branch((kse_ref, kso_ref), vts_ref, rhs_sel, 0, lambda d: jnp.minimum(d, 2), os_ref)
    branch((kwe_ref, kwo_ref), vtw_ref, rhs_q, jnp.maximum(qt - 2, 0),
           lambda d: jnp.where(d == 2, 3, d), ow_ref)


def _nsa_flash(z, sb, tab, B, S):
    nq = S // TQ
    zslab = lambda col: pl.BlockSpec((S, LANES), lambda b, g, q: (b, col // LANES))
    ospec = pl.BlockSpec((TQ, 2 * LANES), lambda b, g, q: (b * nq + q, g))
    oshape = jax.ShapeDtypeStruct((B * S, A_Q), F32)
    return pl.pallas_call(
        functools.partial(_nsa_flash_kernel, S=S),
        grid=(B, A_KV_GROUPS, nq),
        in_specs=[pl.BlockSpec((TQ, 2 * LANES), lambda b, g, q: (b * nq + q, g)),
                  pl.BlockSpec((None, None, LANES, TQ), lambda b, g, q: (b, g, 0, q)),
                  zslab(Z_AKS), zslab(Z_AVS), zslab(Z_AKW), zslab(Z_AVW),
                  pl.BlockSpec((None, 2, 4, TQ, 2 * TQ), lambda b, g, q: (g, 0, 0, 0, 0))],
        out_specs=[ospec, ospec],
        out_shape=[oshape, oshape],
        scratch_shapes=[pltpu.VMEM((S, 2 * LANES), BF16), pltpu.VMEM((S, 2 * LANES), BF16),
                        pltpu.VMEM((S, LANES), BF16), pltpu.VMEM((S, LANES), BF16),
                        pltpu.VMEM((S // TQ, LANES, TQ), BF16), pltpu.VMEM((S // TQ, LANES, TQ), BF16),
                        pltpu.VMEM((2, A_HEAD_DIM, 2 * TQ), F32),
                        pltpu.VMEM((2, TQ, 2 * TQ), F32)],
        compiler_params=_cparams("parallel", "parallel", "arbitrary"),
        name="nsa_selected_window",
    )(z, sb, z, z, z, z, tab)


def _mla_proj_kernel(zqa_ref, zkva_ref, zkra_ref, zkrb_ref, qn_ref, kvn_ref, wqa_ref, wqb_ref,
                     wk_ref, wv_ref, caq_ref, cbq_ref, cak_ref, cbk_ref, q_ref, k_ref, v_ref):
    def rms(x, gain):
        return x * lax.rsqrt(jnp.mean(x * x, axis=-1, keepdims=True) + EPS) * gain

    nq = rms(zqa_ref[...].astype(F32), qn_ref[...]).astype(BF16)
    nkv = rms(zkva_ref[...].astype(F32), kvn_ref[...]).astype(BF16)
    qa = _dot_t(wqa_ref[...], nq)
    qb = _dot_t(wqb_ref[...], nq)
    kn = _dot(nkv, wk_ref[...])
    vt = _dot_t(wv_ref[...], nkv)
    caq, cbq = caq_ref[...], cbq_ref[...]
    kr = zkra_ref[...].astype(F32) * cak_ref[...] + zkrb_ref[...].astype(F32) * cbk_ref[...]
    for h in range(B_HEADS):
        sl = slice(h * LANES, (h + 1) * LANES)
        q_ref[h] = (qa[sl, :] * caq + qb[sl, :] * cbq).astype(BF16)
        k_ref[h] = (kn[:, sl] + kr).astype(BF16)
        v_ref[h] = vt[h * V_DIM:(h + 1) * V_DIM, :].astype(BF16)


def _mla_proj(z, qn, kvn, wqa, wqb, wk, wv, caq, cbq, cak, cbk, B, S):
    tm = TM
    nt = S // tm
    zspec = lambda w, col: pl.BlockSpec((tm, w), lambda b, i: (b * nt + i, col // w))
    full = lambda a: pl.BlockSpec(a.shape, lambda b, i: (0,) * a.ndim)
    tspec = pl.BlockSpec((tm, LANES), lambda b, i: (i, 0))
    tspec_t = pl.BlockSpec((LANES, tm), lambda b, i: (0, i))
    return pl.pallas_call(
        _mla_proj_kernel,
        grid=(B, nt),
        in_specs=[zspec(Q_LORA, Z_QA), zspec(LANES, Z_KVA), zspec(LANES, Z_KRA), zspec(LANES, Z_KRB),
                  full(qn), full(kvn), full(wqa), full(wqb), full(wk), full(wv),
                  tspec_t, tspec_t, tspec, tspec],
        out_specs=[pl.BlockSpec((None, B_HEADS, LANES, tm), lambda b, i: (b, 0, 0, i)),
                   pl.BlockSpec((None, B_HEADS, tm, LANES), lambda b, i: (b, 0, i, 0)),
                   pl.BlockSpec((None, B_HEADS, None, V_DIM, tm), lambda b, i: (b, 0, i, 0, 0))],
        out_shape=[jax.ShapeDtypeStruct((B, B_HEADS, LANES, S), BF16),
                   jax.ShapeDtypeStruct((B, B_HEADS, S, LANES), BF16),
                   jax.ShapeDtypeStruct((B, B_HEADS, nt, V_DIM, tm), BF16)],
        compiler_params=_cparams("parallel", "parallel"),
        name="mla_proj",
    )(z, z, z, z, qn, kvn, wqa, wqb, wk, wv, caq, cbq, cak, cbk)


def _mla_flash_kernel(qt_ref, k_ref, vt_ref, o_ref, acc_ref, s_ref):
    qi = pl.program_id(2)
    ri = lax.broadcasted_iota(jnp.int32, (TM, TM), 0)
    ci = lax.broadcasted_iota(jnp.int32, (TM, TM), 1)
    causal = ri <= ci
    init = (jnp.full((1, TM), 0.5 * NEG, F32), jnp.zeros((1, TM), F32))

    for pr in range(MLA_PAIRS):
        acc_ref[...] = jnp.zeros(acc_ref.shape, F32)

        def qk(hh, kc, pr=pr):
            ks = pl.multiple_of(kc * TM, TM)
            return _dot(k_ref[2 * pr + hh, pl.ds(ks, TM), :], qt_ref[2 * pr + hh])

        def softmax(hh, s, kc, state, diag, pr=pr):
            if diag:
                s = jnp.where(causal, s, NEG)
            return _flash_update_t(s, vt_ref[2 * pr + hh, kc], state, acc_ref.at[hh])

        carry = _flash_pipeline(qk, softmax, 0, qi, (init, init), s_ref, False)
        out_t = jnp.concatenate([acc_ref[0] * (1.0 / carry[0][1]), acc_ref[1] * (1.0 / carry[1][1])], axis=0)
        o_ref[:, pr * LANES:(pr + 1) * LANES] = out_t.T


def _mla_flash(qt, k, vt, B, S):
    nq = S // TM
    hb = 2 * MLA_PAIRS
    hp = B_HEADS // hb
    return pl.pallas_call(
        _mla_flash_kernel,
        grid=(B, hp, nq),
        in_specs=[pl.BlockSpec((None, hb, LANES, TM), lambda b, h, i: (b, h, 0, i)),
                  pl.BlockSpec((None, hb, S, LANES), lambda b, h, i: (b, h, 0, 0)),
                  pl.BlockSpec((None, hb, nq, V_DIM, TM), lambda b, h, i: (b, h, 0, 0, 0))],
        out_specs=pl.BlockSpec((TM, MLA_PAIRS * LANES), lambda b, h, i: (b * nq + i, h)),
        out_shape=jax.ShapeDtypeStruct((B * S, B_HEADS * V_DIM), F32),
        scratch_shapes=[pltpu.VMEM((2, V_DIM, TM), F32), pltpu.VMEM((2, TM, TM), F32)],
        compiler_params=_cparams("parallel", "parallel", "arbitrary"),
        name="mla_flash",
    )(qt, k, vt)


def _rwkv_prep_kernel(zr_ref, zk_ref, zv_ref, zwa_ref, zg_ref, mu_r, mu_k, mu_v, mu_wa, mu_g,
                      w0_ref, a0_ref, kk_ref, ka_ref, rk_ref, w2_ref, a2_ref, g2_ref, bd_ref,
                      r_o, k_o, v_o, kk_o, be_o, ld_o, g_o, bo_o,
                      c_r, c_k, c_v, c_wa, c_g):
    t = pl.program_id(1)
    tm = zr_ref.shape[0]

    def shifted(z_ref, mu_ref, c_ref):
        x = z_ref[...].astype(F32)
        row = lax.broadcasted_iota(jnp.int32, x.shape, 0)
        prev = jnp.where(t == 0, 0.0, c_ref[0:1, :])
        xs = jnp.where(row == 0, prev, pltpu.roll(x, 1, axis=0))
        c_ref[0:1, :] = x[tm - 1:tm, :]
        return x + (xs - x) * mu_ref[...]

    r = shifted(zr_ref, mu_r, c_r)
    k = shifted(zk_ref, mu_k, c_k)
    v = shifted(zv_ref, mu_v, c_v)
    wa = shifted(zwa_ref, mu_wa, c_wa)
    gd = shifted(zg_ref, mu_g, c_g)
    w = w0_ref[...] + _dot(jnp.tanh(wa).astype(BF16), w2_ref[...])
    ld_o[...] = -jax.nn.sigmoid(w) * math.exp(-0.5)
    a = jax.nn.sigmoid(a0_ref[...] + _dot(wa.astype(BF16), a2_ref[...]))
    g_o[...] = _dot(jax.nn.sigmoid(gd).astype(BF16), g2_ref[...])
    kk = k * kk_ref[...]
    nsq = _dot3(kk * kk, bd_ref[...])
    kk = kk / jnp.maximum(jnp.sqrt(nsq), 1e-12)
    k2 = k * (1.0 + (a - 1.0) * ka_ref[...])
    rks = _dot3(r * k2 * rk_ref[...], bd_ref[...])
    r_o[...] = r
    k_o[...] = k2
    v_o[...] = v
    kk_o[...] = kk
    be_o[...] = kk * a
    bo_o[...] = rks * v


def _rwkv_prep(z, mus, w0, a0, k_k, k_a, r_k, w2p, a2p, g2, bd, B, S, tm=256):
    nt = S // tm
    W = C_WIDTH
    zspec = lambda w, col: pl.BlockSpec((tm, w), lambda b, i: (b * nt + i, col // w))
    full = lambda a: pl.BlockSpec(a.shape, lambda b, i: (0,) * a.ndim)
    ospec = pl.BlockSpec((tm, W), lambda b, i: (b * nt + i, 0))
    oshape = jax.ShapeDtypeStruct((B * S, W), F32)
    consts = list(mus) + [w0, a0, k_k, k_a, r_k, w2p, a2p, g2, bd]
    return pl.pallas_call(
        _rwkv_prep_kernel,
        grid=(B, nt),
        in_specs=[zspec(W, Z_CR), zspec(W, Z_CK), zspec(W, Z_CV), zspec(LANES, Z_CWA), zspec(LANES, Z_CG)]
                 + [full(a) for a in consts],
        out_specs=[ospec] * 8,
        out_shape=[oshape] * 8,
        scratch_shapes=[pltpu.VMEM((8, W), F32)] * 3 + [pltpu.VMEM((8, LANES), F32)] * 2,
        compiler_params=_cparams("parallel", "arbitrary"),
        name="rwkv_prep",
    )(z, z, z, z, z, *consts)


def _rwkv_chunk_kernel(r_ref, k_ref, v_ref, kk_ref, be_ref, ld_ref,
                       qa_o, bkt_o, yu_o, vp_o, pcb_o):
    C = CH
    R = ld_ref.shape[0]
    ld = ld_ref[...]
    rr = lax.broadcasted_iota(jnp.int32, (R, R), 0)
    cc = lax.broadcasted_iota(jnp.int32, (R, R), 1)
    sh = int(math.log2(C))
    tri = jnp.where((cc <= rr) & ((cc >> sh) == (rr >> sh)), 1.0, 0.0).astype(BF16)
    cs = sum(_dot(tri, t) for t in _split3(ld))
    cl = jnp.concatenate([jnp.broadcast_to(cs[(s + 1) * C - 1:(s + 1) * C, :], (C, C_WIDTH))
                          for s in range(R // C)], axis=0)
    ex, exn, exx, exc = jnp.exp(cs), jnp.exp(-cs), jnp.exp(cs - ld), jnp.exp(cl - cs)
    kk, be, k2, v = kk_ref[...], be_ref[...], k_ref[...], v_ref[...]
    at = -kk * exx
    rt = r_ref[...] * ex
    bt = be * exn
    kt = k2 * exn
    btc = be * exc
    ktc = k2 * exc
    pc = jnp.exp(cl)
    ri = lax.broadcasted_iota(jnp.int32, (C, C), 0)
    ci = lax.broadcasted_iota(jnp.int32, (C, C), 1)
    incl = ci <= ri
    strict = ci < ri
    eye = jnp.where(ci == ri, 1.0, 0.0)
    lane = lax.broadcasted_iota(jnp.int32, (C, LANES), 1)
    P = C_HEADS // 2
    pairs = [(s, p) for s in range(R // C) for p in range(P)]
    heads = [(i, hh) for i in range(len(pairs)) for hh in range(2)]
    blk = lambda x, i: x[pairs[i][0] * C:(pairs[i][0] + 1) * C, pairs[i][1] * LANES:(pairs[i][1] + 1) * LANES]
    lhs2 = [jnp.concatenate([blk(at, i), blk(rt, i)], axis=0).astype(BF16) for i in range(len(pairs))]
    bm, km, vm, am = [], [], [], []
    for i, hh in heads:
        msk = (lane < C_HEAD_DIM) if hh == 0 else (lane >= C_HEAD_DIM)
        bm.append(jnp.where(msk, blk(bt, i), 0.0).astype(BF16))
        km.append(jnp.where(msk, blk(kt, i), 0.0).astype(BF16))
        vm.append(jnp.where(msk, blk(v, i), 0.0).astype(BF16))
        am.append(jnp.where(msk, blk(at, i), 0.0).astype(BF16))
    g1 = [_dot_t(lhs2[i], bm[j]) for j, (i, hh) in enumerate(heads)]
    g2 = [_dot_t(lhs2[i], km[j]) for j, (i, hh) in enumerate(heads)]
    lab = [jnp.where(strict, g[:C], 0.0) for g in g1]
    mrb = [jnp.where(incl, g[C:], 0.0).astype(BF16) for g in g1]
    lak = [jnp.where(strict, g[:C], 0.0).astype(BF16) for g in g2]
    mrk = [jnp.where(incl, g[C:], 0.0).astype(BF16) for g in g2]
    w2 = [_dot(a, b) for a, b in zip(lak, vm)]
    yk = [_dot(a, b) for a, b in zip(mrk, vm)]
    tinv = [eye + x for x in lab]
    lp = lab
    for _ in range(int(math.log2(C)) - 1):
        lpb = [x.astype(BF16) for x in lp]
        lp = [_dot(x, x) for x in lpb]
        tinv = [t + _dot(t.astype(BF16), x.astype(BF16)) for t, x in zip(tinv, lp)]
    au = [_dot(t.astype(BF16), jnp.concatenate([a, w.astype(BF16)], axis=1)) for t, a, w in zip(tinv, am, w2)]
    qy = [_dot(m, x.astype(BF16)) for m, x in zip(mrb, au)]
    for i, (s, p) in enumerate(pairs):
        e, o = 2 * i, 2 * i + 1
        ahat = au[e][:, :LANES] + au[o][:, :LANES]
        uhat = au[e][:, LANES:] + au[o][:, LANES:]
        qhat = blk(rt, i) + qy[e][:, :LANES] + qy[o][:, :LANES]
        yi = qy[e][:, LANES:] + qy[o][:, LANES:] + yk[e] + yk[o]
        qa_o[s, p] = jnp.concatenate([qhat, ahat], axis=0).astype(BF16)
        yu_o[s, p] = jnp.concatenate([yi, uhat], axis=0)
        bkt_o[s, p] = jnp.concatenate([blk(btc, i), blk(ktc, i)], axis=0).T.astype(BF16)
        vp_o[s, p] = blk(v, i).astype(BF16)
        pcb_o[s, p] = jnp.concatenate([blk(pc, i), blk(pc, i)], axis=0).T


def _rwkv_chunk(r, k2, v, kk, be, ld, B, S):
    nch = S // CH
    W = C_WIDTH
    P = C_HEADS // 2
    ns = RW_SUB
    ispec = pl.BlockSpec((ns * CH, W), lambda b, c: (b * (nch // ns) + c, 0))
    sq = lambda rows: pl.BlockSpec((None, ns, P, rows, LANES), lambda b, c: (b, c, 0, 0, 0))
    shp = lambda rows, dt: jax.ShapeDtypeStruct((B, nch, P, rows, LANES), dt)
    return pl.pallas_call(
        _rwkv_chunk_kernel,
        grid=(B, nch // ns),
        in_specs=[ispec] * 6,
        out_specs=[sq(2 * CH), sq(LANES), sq(2 * CH), sq(CH), sq(LANES)],
        out_shape=[shp(2 * CH, BF16), shp(LANES, BF16), shp(2 * CH, F32), shp(CH, BF16), shp(LANES, F32)],
        compiler_params=_cparams("parallel", "parallel"),
        name="rwkv_chunk",
    )(r, k2, v, kk, be, ld)


def _rwkv_state_kernel(qa_ref, bkt_ref, yu_ref, vp_ref, pcb_ref, y_ref, ap_ref):
    @pl.when(pl.program_id(1) == 0)
    def _():
        ap_ref[...] = jnp.zeros(ap_ref.shape, F32)

    ri = lax.broadcasted_iota(jnp.int32, (LANES, LANES), 0)
    ci = lax.broadcasted_iota(jnp.int32, (LANES, LANES), 1)
    same_head = (ri < C_HEAD_DIM) == (ci < C_HEAD_DIM)
    idx = [(b, p) for b in range(qa_ref.shape[0]) for p in range(C_HEADS // 2)]
    a = [ap_ref[b, p] for b, p in idx]
    for c in range(qa_ref.shape[1]):
        x = [_dot(qa_ref[b, c, p], s.astype(BF16)) for (b, p), s in zip(idx, a)]
        yu = [yu_ref[b, c, p] for b, p in idx]
        for (b, p), xi, yi in zip(idx, x, yu):
            y_ref[b, c * CH:(c + 1) * CH, p * LANES:(p + 1) * LANES] = xi[:CH] + yi[:CH]
        uv = [jnp.concatenate([(xi[CH:] + yi[CH:]).astype(BF16), vp_ref[b, c, p]], axis=0)
              for (b, p), xi, yi in zip(idx, x, yu)]
        upd = [_dot(bkt_ref[b, c, p], t) for (b, p), t in zip(idx, uv)]
        a = [pcb_ref[b, c, p] * s + jnp.where(same_head, t, 0.0) for (b, p), s, t in zip(idx, a, upd)]
    for (b, p), s in zip(idx, a):
        ap_ref[b, p] = s


def _rwkv_state(qa, bkt, yu, vp, pcb, B, S):
    nch = S // CH
    P = C_HEADS // 2
    nb = 2 if B % 2 == 0 else 1
    nc = 4 if nch % 4 == 0 else 1
    sq = lambda rows: pl.BlockSpec((nb, nc, P, rows, LANES), lambda b, c: (b, c, 0, 0, 0))
    return pl.pallas_call(
        _rwkv_state_kernel,
        grid=(B // nb, nch // nc),
        in_specs=[sq(2 * CH), sq(LANES), sq(2 * CH), sq(CH), sq(LANES)],
        out_specs=pl.BlockSpec((nb, nc * CH, C_WIDTH), lambda b, c: (b, c, 0)),
        out_shape=jax.ShapeDtypeStruct((B, S, C_WIDTH), F32),
        scratch_shapes=[pltpu.VMEM((nb, P, LANES, LANES), F32)],
        compiler_params=_cparams("parallel", "arbitrary"),
        name="rwkv_state",
    )(qa, bkt, yu, vp, pcb).reshape(B * S, C_WIDTH)


def _merge_kernel(oc_ref, os_ref, ow_ref, zag_ref, yb_ref, yr_ref, bo_ref, gg_ref,
                  zga_ref, zgb_ref, zgc_ref, x_ref, wbr_ref, wout_ref, ge_ref, bdm_ref,
                  ln_ref, gain_ref, gt_ref, o_ref):
    sg = jax.nn.sigmoid(zag_ref[...].astype(F32))
    ya = (_dot2(sg, ge_ref[0]) * oc_ref[...] + _dot2(sg, ge_ref[1]) * os_ref[...]
          + _dot2(sg, ge_ref[2]) * ow_ref[...])
    yr = yr_ref[...]
    mean = _dot3(yr, bdm_ref[...])
    d = yr - mean
    var = _dot2(d * d, bdm_ref[...])
    yn = d * lax.rsqrt(var + GN_EPS) * ln_ref[0:1, :] + ln_ref[1:2, :]
    yc = (yn + bo_ref[...]) * gg_ref[...]
    sig = lambda ref: jax.nn.sigmoid(ref[...].astype(F32))
    merged = (sig(zga_ref) * _dot(ya.astype(BF16), wbr_ref[0:A_Q, :])
              + sig(zgb_ref) * _dot(yb_ref[...].astype(BF16), wbr_ref[A_Q:2 * A_Q, :])
              + sig(zgc_ref) * _dot(yc.astype(BF16), wbr_ref[2 * A_Q:3 * A_Q, :]))
    y = _dot(merged.astype(BF16), wout_ref[...])
    yn2 = y * lax.rsqrt(jnp.mean(y * y, axis=-1, keepdims=True) + EPS) * gain_ref[...]
    o_ref[...] = x_ref[...] + gt_ref[0] * yn2


def _merge(oc, os_, ow, z, yb, yr, bo, gg, x2, wbr, wout, gexp, bdm, ln, gain, gt, S, tm=256):
    N, D = x2.shape
    tpb = S // tm
    W = A_Q
    row = lambda w: pl.BlockSpec((tm, w), lambda i: (i, 0))
    zspec = lambda w, col: pl.BlockSpec((tm, w), lambda i: (i, col // w))
    full = lambda a: pl.BlockSpec(a.shape, lambda i: (0,) * a.ndim)
    return pl.pallas_call(
        _merge_kernel,
        grid=(N // tm,),
        in_specs=[row(W), row(W), row(W), zspec(LANES, Z_AG), row(W), row(W), row(W), row(W),
                  zspec(D, Z_ZG), zspec(D, Z_ZG + D), zspec(D, Z_ZG + 2 * D), row(D),
                  full(wbr), full(wout), full(gexp), full(bdm), full(ln),
                  pl.BlockSpec((1, D), lambda i: (0, 0)),
                  pl.BlockSpec((1, 1, D), lambda i: (i // tpb, 0, 0))],
        out_specs=row(D),
        out_shape=jax.ShapeDtypeStruct((N, D), F32),
        compiler_params=_cparams("parallel"),
        name="merge_out",
    )(oc, os_, ow, z, yb, yr, bo, gg, z, z, z, x2, wbr, wout, gexp, bdm, ln, gain.reshape(1, D), gt)


def _ffn_up_kernel(x_ref, g_ref, sc_ref, sh_ref, wg_ref, wv_ref, cw_ref, cb_ref, o_ref,
                   cg_ref, cv_ref, *, tpb):
    i = pl.program_id(0)
    tm = x_ref.shape[0]
    fc = FFN_SUB
    first = (i % tpb) == 0
    x = x_ref[...]
    y = x * lax.rsqrt(jnp.mean(x * x, axis=-1, keepdims=True) + EPS)
    h = ((y * g_ref[...]) * (1.0 + sc_ref[0]) + sh_ref[0]).astype(BF16)
    row = lax.broadcasted_iota(jnp.int32, (8, fc), 0)

    def up(c):
        sl = slice(c * fc, (c + 1) * fc)
        return _dot(h, wg_ref[:, sl]), _dot(h, wv_ref[:, sl])

    def conv(u, c_ref, c, off):
        sl = slice(c * fc, (c + 1) * fc)
        wl = slice(off + c * fc, off + (c + 1) * fc)
        prev = c_ref[:, sl]
        p1 = jnp.where(first, 0.0, prev[7:8])
        p2 = jnp.where(first, 0.0, prev[6:7])
        c_ref[:, sl] = u[tm - 8:tm, :]
        r1 = pltpu.roll(u, 1, axis=0)
        r2 = pltpu.roll(u, 2, axis=0)
        t1 = jnp.where(row == 0, p1, r1[0:8])
        t2 = jnp.where(row == 0, p2, jnp.where(row == 1, p1, r2[0:8]))
        u1 = jnp.concatenate([t1, r1[8:]], axis=0)
        u2 = jnp.concatenate([t2, r2[8:]], axis=0)
        return cw_ref[0:1, wl] * u2 + cw_ref[1:2, wl] * u1 + cw_ref[2:3, wl] * u + cb_ref[:, wl]

    nsub = D_FF // fc
    cur = up(0)
    for c in range(nsub):
        nxt = up(c + 1) if c + 1 < nsub else None
        a = _gelu(conv(cur[0], cg_ref, c, 0)) * conv(cur[1], cv_ref, c, D_FF)
        o_ref[:, c * fc:(c + 1) * fc] = a.astype(BF16)
        cur = nxt


def _ffn_up(x2, gain, sc, sh, w_up, conv_w, conv_b, S, tm=256):
    N, D = x2.shape
    tpb = S // tm
    full = lambda a: pl.BlockSpec(a.shape, lambda i: (0,) * a.ndim)
    return pl.pallas_call(
        functools.partial(_ffn_up_kernel, tpb=tpb),
        grid=(N // tm,),
        in_specs=[pl.BlockSpec((tm, D), lambda i: (i, 0)),
                  pl.BlockSpec((1, D), lambda i: (0, 0)),
                  pl.BlockSpec((1, 1, D), lambda i: (i // tpb, 0, 0)),
                  pl.BlockSpec((1, 1, D), lambda i: (i // tpb, 0, 0)),
                  pl.BlockSpec((D, D_FF), lambda i: (0, 0)),
                  pl.BlockSpec((D, D_FF), lambda i: (0, 1)),
                  full(conv_w), pl.BlockSpec((1, 2 * D_FF), lambda i: (0, 0))],
        out_specs=pl.BlockSpec((tm, D_FF), lambda i: (i, 0)),
        out_shape=jax.ShapeDtypeStruct((N, D_FF), BF16),
        scratch_shapes=[pltpu.VMEM((8, D_FF), F32), pltpu.VMEM((8, D_FF), F32)],
        compiler_params=_cparams("arbitrary"),
        name="ffn_up",
    )(x2, gain.reshape(1, D), sc, sh, w_up, w_up, conv_w, conv_b.reshape(1, 2 * D_FF))


def _ffn_down_kernel(a_ref, wd_ref, x_ref, gain_ref, gt_ref, o_ref):
    f = _dot(a_ref[...], wd_ref[...])
    fn = f * lax.rsqrt(jnp.mean(f * f, axis=-1, keepdims=True) + EPS) * gain_ref[...]
    o_ref[...] = x_ref[...] + gt_ref[0] * fn


def _ffn_down(a, wd, x2, gain, gt, S, tm=512):
    N, D = x2.shape
    tpb = S // tm
    return pl.pallas_call(
        _ffn_down_kernel,
        grid=(N // tm,),
        in_specs=[pl.BlockSpec((tm, D_FF), lambda i: (i, 0)),
                  pl.BlockSpec((D_FF, D), lambda i: (0, 0)),
                  pl.BlockSpec((tm, D), lambda i: (i, 0)),
                  pl.BlockSpec((1, D), lambda i: (0, 0)),
                  pl.BlockSpec((1, 1, D), lambda i: (i // tpb, 0, 0))],
        out_specs=pl.BlockSpec((tm, D), lambda i: (i, 0)),
        out_shape=jax.ShapeDtypeStruct((N, D), F32),
        compiler_params=_cparams("parallel"),
        name="ffn_down",
    )(a, wd, x2, gain.reshape(1, D), gt)


def _t5_bucket_np(dist):
    d = np.maximum(dist, 0)
    max_exact = N_BUCKETS // 2
    large = max_exact + (np.log(np.maximum(d, 1).astype(np.float32) / max_exact)
                         / math.log(MAX_DISTANCE / max_exact) * (N_BUCKETS - max_exact)).astype(np.int32)
    return np.where(d < max_exact, d, np.minimum(large, N_BUCKETS - 1))


def _bias_table_kernel(scale_ref, rb_ref, code_ref, o_ref):
    code = code_ref[0]
    sc = scale_ref[pl.program_id(0)]
    for h in range(A_HEADS):
        acc = jnp.full(code.shape, NEG, F32)
        for b in range(N_BUCKETS):
            acc = jnp.where(code == b, rb_ref[b, h], acc)
        o_ref[0, h] = acc * sc


def _bias_tables(rel_bias, codes, scales):
    nt, R, C = codes.shape
    smem = pl.BlockSpec(memory_space=pltpu.SMEM)
    return pl.pallas_call(
        _bias_table_kernel,
        grid=(nt,),
        in_specs=[smem, smem, pl.BlockSpec((1, R, C), lambda t: (t, 0, 0))],
        out_specs=pl.BlockSpec((1, A_HEADS, R, C), lambda t: (t, 0, 0, 0)),
        out_shape=jax.ShapeDtypeStruct((nt, A_HEADS, R, C), F32),
        compiler_params=_cparams("parallel"),
        name="bias_tables",
    )(jnp.asarray(scales, F32), rel_bias, jnp.asarray(codes))


def _codes(dist, lo, hi):
    return np.where((dist >= lo) & (dist < hi), _t5_bucket_np(dist), N_BUCKETS).astype(np.int32)


def _nsa_tables(rel_bias, S):
    NC = S // CMP_STRIDE
    G, HPG = A_KV_GROUPS, A_HEADS // A_KV_GROUPS
    big = 1 << 30
    i = np.arange(TQ)[:, None]
    dc = i - CMP_STRIDE * np.arange(NC)[None, :] + CMP_STRIDE * NC - TQ - (CMP_BLOCK - 1)
    cb = _bias_tables(rel_bias, _codes(dc, 0, big)[None], [1.0])[0]
    cb = cb.reshape(G, HPG // 2, 2, TQ, NC).transpose(0, 2, 1, 3, 4).reshape(G, 2, 2 * TQ, NC)
    dt = np.arange(TQ)[None, :] - np.arange(TQ)[:, None]
    far = np.full((TQ, TQ), MAX_DISTANCE)
    codes = np.stack([_codes(dt, 0, WINDOW), _codes(TQ + dt, 0, WINDOW), _codes(far, 0, big),
                      _codes(2 * TQ + dt, 0, WINDOW)])
    t = _bias_tables(rel_bias, codes, [LOG2E] * 4)
    t = t.reshape(4, G, HPG // 2, 2, TQ, TQ).transpose(1, 3, 0, 4, 2, 5).reshape(G, 2, 4, TQ, 2 * TQ)
    return cb, t


def _pad_cols(w, n):
    return jnp.pad(w, ((0, 0), (0, n - w.shape[1])))


def _prep_w_in(w):
    D = w.shape[0]
    za, zb, zc, zg = jnp.split(w, np.cumsum([A_COLS, B_COLS, C_COLS]).tolist(), axis=1)
    a_parts = jnp.split(za, np.cumsum([A_Q] + [A_KV] * 6).tolist(), axis=1)
    qa, kva, kr = jnp.split(zb, [Q_LORA, Q_LORA + KV_LORA], axis=1)
    half = ROPE_DIM // 2
    kr_rot = jnp.concatenate([-kr[:, half:], kr[:, :half]], axis=1)
    z64 = jnp.zeros((D, NOPE_DIM), w.dtype)
    z32 = jnp.zeros((D, LANES - NOPE_DIM - ROPE_DIM), w.dtype)
    c_r, c_k, c_v, c_wd, c_ad, c_gd = jnp.split(
        zc, np.cumsum([C_WIDTH] * 3 + [DECAY_LORA, AAA_LORA]).tolist(), axis=1)
    cols = [a_parts[0], qa] + a_parts[1:7] + [
        _pad_cols(a_parts[7], LANES), kva,
        jnp.concatenate([z64, kr, z32], axis=1),
        jnp.concatenate([z64, kr_rot, z32], axis=1),
        c_r, c_k, c_v, c_wd, c_ad, c_gd, jnp.zeros((D, Z_ZG - Z_CG - LANES), w.dtype), zg]
    out = jnp.concatenate(cols, axis=1)
    assert out.shape[1] == Z_COLS
    return out.astype(BF16)


def _prep_mla(w_uq, w_ukv):
    dq = NOPE_DIM + ROPE_DIM
    half = ROPE_DIM // 2
    wq = w_uq.reshape(Q_LORA, B_HEADS, dq)
    nope, r1, r2 = wq[..., :NOPE_DIM], wq[..., NOPE_DIM:NOPE_DIM + half], wq[..., NOPE_DIM + half:]
    zq = jnp.zeros((Q_LORA, B_HEADS, LANES - dq), w_uq.dtype)
    wqa = jnp.concatenate([nope, r1, r2, zq], axis=-1).reshape(Q_LORA, B_HEADS * LANES)
    wqb = jnp.concatenate([jnp.zeros_like(nope), -r2, r1, zq], axis=-1).reshape(Q_LORA, B_HEADS * LANES)
    wkv = w_ukv.reshape(KV_LORA, B_HEADS, NOPE_DIM + V_DIM)
    kn, vv = wkv[..., :NOPE_DIM], wkv[..., NOPE_DIM:]
    wk = jnp.concatenate([kn, jnp.zeros_like(kn)], axis=-1).reshape(KV_LORA, B_HEADS * LANES)
    wv = vv.reshape(KV_LORA, B_HEADS * V_DIM)
    return wqa.T.astype(BF16), wqb.T.astype(BF16), wk.astype(BF16), wv.T.astype(BF16)


def _rope_tables(S):
    half = ROPE_DIM // 2
    inv = ROPE_THETA ** (-jnp.arange(half, dtype=F32) / half)
    ang = jnp.arange(S, dtype=F32)[:, None] * inv
    cos2 = jnp.tile(jnp.cos(ang), (1, 2))
    sin2 = jnp.tile(jnp.sin(ang), (1, 2))
    scale = (NOPE_DIM + ROPE_DIM) ** -0.5 * LOG2E
    one = jnp.ones((S, NOPE_DIM), F32)
    z64 = jnp.zeros((S, NOPE_DIM), F32)
    z32 = jnp.zeros((S, LANES - NOPE_DIM - ROPE_DIM), F32)
    caq = (jnp.concatenate([one, cos2, z32], axis=1) * scale).T
    cbq = (jnp.concatenate([z64, sin2, z32], axis=1) * scale).T
    cak = jnp.concatenate([z64, cos2, z32], axis=1)
    cbk = jnp.concatenate([z64, sin2, z32], axis=1)
    return caq, cbq, cak, cbk


def _prep_compress(cmp_pos, cmp_w1, cmp_w2):
    Dh = A_HEAD_DIM
    w1 = cmp_w1.reshape(2, CMP_BLOCK, Dh, Dh)
    z = jnp.zeros_like(w1)
    w1bd = jnp.concatenate([jnp.concatenate([w1, z], axis=-1), jnp.concatenate([z, w1], axis=-1)], axis=-2)
    posrow = jnp.broadcast_to(cmp_pos.reshape(2, 1, CMP_BLOCK * Dh), (2, 8, CMP_BLOCK * Dh))
    w1cat = jnp.concatenate([cmp_w1, cmp_w1], axis=-1)
    zz = jnp.zeros((2, Dh, Dh), cmp_w2.dtype)
    blk = lambda a, b, c, d: jnp.concatenate(
        [jnp.concatenate([a, b], axis=-1), jnp.concatenate([c, d], axis=-1)], axis=-2)
    w2v = jnp.stack([jnp.stack([blk(cmp_w2, zz, zz, zz), blk(zz, cmp_w2, zz, zz)], axis=1),
                     jnp.stack([blk(zz, zz, cmp_w2, zz), blk(zz, zz, zz, cmp_w2)], axis=1)], axis=1)
    return w1bd.astype(BF16), posrow, w1cat.astype(BF16), w2v.astype(BF16)


def _gate_expand():
    e = np.zeros((3, LANES, A_Q), np.float32)
    for h in range(A_HEADS):
        for r in range(3):
            e[r, 3 * h + r, h * A_HEAD_DIM:(h + 1) * A_HEAD_DIM] = 1.0
    return jnp.asarray(e, dtype=BF16)


def _block_diag_ones(scale):
    idx = np.arange(C_WIDTH) // C_HEAD_DIM
    return jnp.asarray((idx[:, None] == idx[None, :]).astype(np.float32) * scale, dtype=BF16)


def kernel(x, c, rel_bias, ada_w, ada_b, norm_gain, w_in, nsa_cmp_pos, nsa_cmp_w1, nsa_cmp_w2, mla_q_norm, mla_kv_norm, mla_w_uq, mla_w_ukv, rwkv_mu, rwkv_w0, rwkv_a0, rwkv_k_k, rwkv_k_a, rwkv_w2, rwkv_a2, rwkv_g2, rwkv_r_k, rwkv_ln, w_branch, w_out, ffn_up, ffn_conv_w, ffn_conv_b, ffn_down):
    B, S, D = x.shape
    L = ada_w.shape[0]
    assert S % TQ == 0 and S % TM == 0 and S // SLC_BLOCK <= A_HEAD_DIM and S >= 2 * TQ
    mod = _adaln(c, ada_w, ada_b)
    cb, nsa_tab = _nsa_tables(rel_bias, S)
    caq, cbq, cak, cbk = _rope_tables(S)
    gexp = _gate_expand()
    bd1 = _block_diag_ones(1.0)
    bdm = _block_diag_ones(1.0 / C_HEAD_DIM)
    row = lambda v: v.reshape(1, -1)
    x2 = x.reshape(B * S, D)
    for l in range(L):
        m6 = mod[l].reshape(B, 6, 1, D)
        sh1, sc1, gt1, sh2, sc2, gt2 = (m6[:, i] for i in range(6))
        z = _norm_mod_matmul(x2, norm_gain[l, 0], sc1, sh1, _prep_w_in(w_in[l]), S, tn=Z_COLS // 4)
        kcp, vcp = _compress(z, *_prep_compress(nsa_cmp_pos[l], nsa_cmp_w1[l], nsa_cmp_w2[l]), B, S)
        oc, sb = _cmpsel(z, kcp, vcp, cb, B, S)
        os_, ow = _nsa_flash(z, sb, nsa_tab, B, S)
        wqa, wqb, wk, wv = _prep_mla(mla_w_uq[l], mla_w_ukv[l])
        q, k, v = _mla_proj(z, row(mla_q_norm[l]), row(mla_kv_norm[l]), wqa, wqb, wk, wv,
                            caq, cbq, cak, cbk, B, S)
        yb = _mla_flash(q, k, v, B, S)
        mu = rwkv_mu[l]
        o = 3 * C_WIDTH
        mus = [row(mu[:C_WIDTH]), row(mu[C_WIDTH:2 * C_WIDTH]), row(mu[2 * C_WIDTH:o]),
               row(mu[o:o + LANES]), row(mu[o + LANES:])]
        w2p = jnp.concatenate([rwkv_w2[l], jnp.zeros_like(rwkv_a2[l])], axis=0).astype(BF16)
        a2p = jnp.concatenate([jnp.zeros_like(rwkv_w2[l]), rwkv_a2[l]], axis=0).astype(BF16)
        rr, k2, vv, kk, be, ld, gg, bo = _rwkv_prep(
            z, mus, row(rwkv_w0[l]), row(rwkv_a0[l]), row(rwkv_k_k[l]), row(rwkv_k_a[l]),
            row(rwkv_r_k[l]), w2p, a2p, rwkv_g2[l].astype(BF16), bd1, B, S)
        qa_, bkt, yu, vp, pcb = _rwkv_chunk(rr, k2, vv, kk, be, ld, B, S)
        yr = _rwkv_state(qa_, bkt, yu, vp, pcb, B, S)
        x2 = _merge(oc, os_, ow, z, yb, yr, bo, gg, x2, w_branch[l].astype(BF16), w_out[l].astype(BF16),
                    gexp, bdm, rwkv_ln[l], norm_gain[l, 1], gt1, S)
        a = _ffn_up(x2, norm_gain[l, 2], sc2, sh2, ffn_up[l].astype(BF16), ffn_conv_w[l], ffn_conv_b[l], S)
        x2 = _ffn_down(a, ffn_down[l].astype(BF16), x2, norm_gain[l, 3], gt2, S)
    return x2.reshape(B, S, D)
```

```python
import functools
import math

import jax
import jax.numpy as jnp
import numpy as np
from jax import lax
from jax.experimental import pallas as pl
from jax.experimental.pallas import tpu as pltpu

F32 = jnp.float32
BF16 = jnp.bfloat16

D_MODEL = 1024
A_HEADS, A_KV_GROUPS, A_HEAD_DIM = 8, 2, 64
CMP_BLOCK, CMP_STRIDE = 32, 16
SLC_BLOCK, SLC_TOPK, N_LOCAL_BLOCKS = 64, 16, 2
WINDOW = 512
FORCE_SCORE = 1e9
B_HEADS, Q_LORA, KV_LORA, NOPE_DIM, ROPE_DIM, V_DIM = 8, 256, 128, 64, 32, 64
ROPE_THETA = 10000.0
C_HEADS, C_HEAD_DIM = 8, 64
C_WIDTH = C_HEADS * C_HEAD_DIM
DECAY_LORA, AAA_LORA, GATE_LORA = 64, 64, 128
GN_EPS = 64e-5
N_BUCKETS, MAX_DISTANCE = 32, 128
D_FF = 2816
EPS = 1e-6
NEG = -1e30
LOG2E = math.log2(math.e)

A_Q = A_HEADS * A_HEAD_DIM
A_KV = A_KV_GROUPS * A_HEAD_DIM
A_GATE = 3 * A_HEADS
A_COLS = A_Q + 6 * A_KV + A_GATE
B_COLS = Q_LORA + KV_LORA + ROPE_DIM
C_COLS = 3 * C_WIDTH + DECAY_LORA + AAA_LORA + GATE_LORA

LANES = 128
FFN_SUB = 256
VMEM_LIMIT = 48 * 1024 * 1024

Z_AQ = 0
Z_QA = 512
Z_AKC, Z_AVC, Z_AKS, Z_AVS, Z_AKW, Z_AVW = 768, 896, 1024, 1152, 1280, 1408
Z_AG = 1536
Z_KVA = 1664
Z_KRA = 1792
Z_KRB = 1920
Z_CR, Z_CK, Z_CV = 2048, 2560, 3072
Z_CWA = 3584
Z_CG = 3712
Z_ZG = 4096
Z_COLS = 7168

TQ = 256
TM = 512
MLA_PAIRS = 4
CH = 64
RW_SUB = 4


def _cparams(*sem):
    return pltpu.CompilerParams(dimension_semantics=sem, vmem_limit_bytes=VMEM_LIMIT)


def _gelu(x):
    return 0.5 * x * (1.0 + jnp.tanh(0.7978845608028654 * (x + 0.044715 * (x * x * x))))


def _dot(a, b):
    return jnp.dot(a, b, preferred_element_type=F32)


def _split3(x):
    hi = x.astype(BF16)
    r = x - hi.astype(F32)
    mid = r.astype(BF16)
    return hi, mid, (r - mid.astype(F32)).astype(BF16)


def _dot3(x, w):
    return sum(_dot(t, w) for t in _split3(x))


def _dot2(x, w):
    hi, mid, _ = _split3(x)
    return _dot(hi, w) + _dot(mid, w)


def _dot_t(a, b):
    return lax.dot_general(a, b, (((1,), (1,)), ((), ())), preferred_element_type=F32)


def _adaln_kernel(c_ref, w_ref, b_ref, o_ref):
    c = c_ref[...]
    cond = c * jax.nn.sigmoid(c)
    o_ref[0] = _dot(cond.astype(BF16), w_ref[0].astype(BF16)) + b_ref[0]


def _adaln(c, ada_w, ada_b):
    L, D, N6 = ada_w.shape
    B = c.shape[0]
    tn = 1536
    return pl.pallas_call(
        _adaln_kernel,
        grid=(L, N6 // tn),
        in_specs=[pl.BlockSpec((B, D), lambda l, j: (0, 0)),
                  pl.BlockSpec((1, D, tn), lambda l, j: (l, 0, j)),
                  pl.BlockSpec((1, 1, tn), lambda l, j: (l, 0, j))],
        out_specs=pl.BlockSpec((1, B, tn), lambda l, j: (l, 0, j)),
        out_shape=jax.ShapeDtypeStruct((L, B, N6), F32),
        compiler_params=_cparams("parallel", "parallel"),
        name="adaln",
    )(c, ada_w, ada_b.reshape(L, 1, N6))


def _nmm_kernel(x_ref, g_ref, sc_ref, sh_ref, w_ref, o_ref, h_ref):
    @pl.when(pl.program_id(1) == 0)
    def _():
        x = x_ref[...]
        y = x * lax.rsqrt(jnp.mean(x * x, axis=-1, keepdims=True) + EPS)
        h = (y * g_ref[...]) * (1.0 + sc_ref[0]) + sh_ref[0]
        h_ref[...] = h.astype(BF16)

    o_ref[...] = _dot(h_ref[...], w_ref[...]).astype(o_ref.dtype)


def _norm_mod_matmul(x2, gain, sc, sh, w, S, tn, out_dtype=BF16):
    N, D = x2.shape
    NC = w.shape[1]
    tm = min(1024, S)
    tpb = S // tm
    return pl.pallas_call(
        _nmm_kernel,
        grid=(N // tm, NC // tn),
        in_specs=[pl.BlockSpec((tm, D), lambda i, j: (i, 0)),
                  pl.BlockSpec((1, D), lambda i, j: (0, 0)),
                  pl.BlockSpec((1, 1, D), lambda i, j: (i // tpb, 0, 0)),
                  pl.BlockSpec((1, 1, D), lambda i, j: (i // tpb, 0, 0)),
                  pl.BlockSpec((D, tn), lambda i, j: (0, j))],
        out_specs=pl.BlockSpec((tm, tn), lambda i, j: (i, j)),
        out_shape=jax.ShapeDtypeStruct((N, NC), out_dtype),
        scratch_shapes=[pltpu.VMEM((tm, D), BF16)],
        compiler_params=_cparams("parallel", "arbitrary"),
        name="norm_mod_matmul",
    )(x2, gain.reshape(1, D), sc, sh, w)


def _compress_kernel(zk_ref, zv_ref, w1_ref, pos_ref, w1c_ref, w2_ref, kc_ref, vc_ref, zf_ref, *, NC):
    for kind, (z_ref, o_ref) in enumerate(((zk_ref, kc_ref), (zv_ref, vc_ref))):
        pa = jnp.zeros((NC, LANES), F32)
        pb = jnp.zeros((NC, LANES), F32)
        zf_ref[...] = z_ref[...].astype(F32)
        for l in range(CMP_STRIDE):
            xl = zf_ref[pl.ds(l, NC, stride=CMP_STRIDE), :].astype(BF16)
            pa = pa + _dot(xl, w1_ref[kind, l])
            pb = pb + _dot(xl, w1_ref[kind, CMP_STRIDE + l])
        posb = _dot(pos_ref[kind].astype(BF16), w1c_ref[kind])[0:1, :]
        h = pa + pltpu.roll(pb, NC - 1, axis=0) + posb
        act = _gelu(h).astype(BF16)
        row = lax.broadcasted_iota(jnp.int32, (NC, LANES), 0)
        for g in range(A_KV_GROUPS):
            for eo in range(2):
                out = _dot(act, w2_ref[kind, g, eo])
                out = jnp.where(row < NC - 1, out, 0.0)
                o_ref[g, eo, pl.ds(0, NC), :] = jnp.zeros((NC, LANES), BF16)
                o_ref[g, eo, pl.ds(NC, NC), :] = out.astype(BF16)


def _compress(z, w1bd, posrow, w1cat, w2v, B, S):
    NC = S // CMP_STRIDE
    out = jax.ShapeDtypeStruct((B, A_KV_GROUPS, 2, 2 * NC, LANES), BF16)
    ospec = pl.BlockSpec((None, A_KV_GROUPS, 2, 2 * NC, LANES), lambda b: (b, 0, 0, 0, 0))
    full = lambda a: pl.BlockSpec(a.shape, lambda b: (0,) * a.ndim)
    return pl.pallas_call(
        functools.partial(_compress_kernel, NC=NC),
        grid=(B,),
        in_specs=[pl.BlockSpec((S, LANES), lambda b: (b, Z_AKC // LANES)),
                  pl.BlockSpec((S, LANES), lambda b: (b, Z_AVC // LANES)),
                  full(w1bd), full(posrow), full(w1cat), full(w2v)],
        out_specs=[ospec, ospec],
        out_shape=[out, out],
        scratch_shapes=[pltpu.VMEM((S, LANES), F32)],
        compiler_params=_cparams("parallel"),
        name="nsa_compress",
    )(z, z, w1bd, posrow, w1cat, w2v)


def _stack_pairs(zq, scale):
    q = zq.astype(F32) * scale
    return jnp.concatenate([q[:, :LANES], q[:, LANES:]], axis=0).astype(BF16)


def _cmpsel_kernel(zq_ref, kc_ref, vc_ref, cb_ref, oc_ref, sb_ref, *, NC, NS):
    qt = pl.program_id(1)
    for g in range(A_KV_GROUPS):
        cols = pl.ds(g * 2 * LANES, 2 * LANES)
        _cmpsel_group(zq_ref.at[:, cols], kc_ref.at[g], vc_ref.at[g], cb_ref.at[g],
                      oc_ref.at[:, cols], sb_ref.at[g], qt, NC, NS)


def _cmpsel_group(zq_ref, kc_ref, vc_ref, cb_ref, oc_ref, sb_ref, qt, NC, NS):
    r = TQ // CMP_STRIDE
    lhs = _stack_pairs(zq_ref[...], A_HEAD_DIM ** -0.5)
    st = pl.multiple_of(r * (qt + 1), 16)
    col = lax.broadcasted_iota(jnp.int32, (2 * TQ, NC), 1)
    exists = col >= NC - r * (qt + 1)
    acc = jnp.zeros((2 * TQ, LANES), F32)
    psum = jnp.zeros((TQ, NC), F32)
    for eo in range(2):
        kwin = kc_ref[eo, pl.ds(st, NC), :]
        vwin = vc_ref[eo, pl.ds(st, NC), :]
        logits = _dot_t(lhs, kwin) + cb_ref[eo]
        logits = jnp.where(exists, logits, NEG)
        m = jnp.max(logits, axis=-1, keepdims=True)
        p = jnp.where(logits > 0.1 * NEG, jnp.exp(logits - m), 0.0)
        l = jnp.maximum(jnp.sum(p, axis=-1, keepdims=True), 1e-30)
        p = p * (1.0 / l)
        acc = acc + _dot(p.astype(BF16), vwin)
        psum = psum + p[:TQ] + p[TQ:]
    oc_ref[:, :LANES] = acc[:TQ]
    oc_ref[:, LANES:] = acc[TQ:]

    NR = A_HEAD_DIM
    jj = lax.broadcasted_iota(jnp.int32, (NR, NC), 0)
    nn = lax.broadcasted_iota(jnp.int32, (NR, NC), 1) + (r * (qt + 1) - NC)
    delta = 4 * jj - nn
    mt = jnp.where((delta == 0) | (delta == 4), 1.0, 0.0) + jnp.where((delta >= 1) & (delta <= 3), 2.0, 0.0)
    mt = jnp.where(jj < NS, mt, 0.0)
    mtb = mt.astype(BF16)
    imp_t = sum(_dot_t(mtb, t) for t in _split3(psum))
    jb = lax.broadcasted_iota(jnp.int32, (NR, TQ), 0)
    tpos = qt * TQ + lax.broadcasted_iota(jnp.int32, (NR, TQ), 1)
    back = (tpos >> 6) - jb
    forced = (jb == 0) | ((back >= 0) & (back < N_LOCAL_BLOCKS))
    score = jnp.where(forced, FORCE_SCORE, jnp.where(back >= 0, imp_t, -1.0))
    score = jnp.where(jb < NS, score, -2.0)
    ngrp = -(-NS // 8)
    grp = [score[8 * m:8 * m + 8, :] for m in range(ngrp)]
    rank = [jnp.zeros((8, TQ), F32) for _ in range(ngrp)]
    j8 = lax.broadcasted_iota(jnp.int32, (8, TQ), 0)
    for jp in range(NS):
        row = score[jp:jp + 1, :]
        for m in range(ngrp):
            if m < jp // 8:
                beats = row > grp[m]
            elif m > jp // 8:
                beats = row >= grp[m]
            else:
                beats = (row > grp[m]) | ((row == grp[m]) & (j8 > jp - 8 * m))
            rank[m] = rank[m] + jnp.where(beats, 1.0, 0.0)
    rank = jnp.concatenate(rank + [jnp.zeros((NR - 8 * ngrp, TQ), F32)] * (NR > 8 * ngrp), axis=0)
    sel = (rank < float(min(SLC_TOPK, NS))) & (back >= 0) & (jb < NS)
    sb = jnp.where(sel, 0.0, jnp.where(jb < NS, NEG, 0.0))
    sb_ref[...] = jnp.concatenate([sb, jnp.zeros((LANES - NR, TQ), F32)], axis=0).astype(BF16)


def _cmpsel(z, kcp, vcp, cb, B, S):
    NC = S // CMP_STRIDE
    NS = S // SLC_BLOCK
    nq = S // TQ
    G = A_KV_GROUPS
    kspec = pl.BlockSpec((None, G, 2, 2 * NC, LANES), lambda b, q: (b, 0, 0, 0, 0))
    return pl.pallas_call(
        functools.partial(_cmpsel_kernel, NC=NC, NS=NS),
        grid=(B, nq),
        in_specs=[pl.BlockSpec((TQ, A_Q), lambda b, q: (b * nq + q, 0)),
                  kspec, kspec,
                  pl.BlockSpec((G, 2, 2 * TQ, NC), lambda b, q: (0, 0, 0, 0))],
        out_specs=[pl.BlockSpec((TQ, A_Q), lambda b, q: (b * nq + q, 0)),
                   pl.BlockSpec((None, G, LANES, TQ), lambda b, q: (b, 0, 0, q))],
        out_shape=[jax.ShapeDtypeStruct((B * S, A_Q), F32),
                   jax.ShapeDtypeStruct((B, G, LANES, S), BF16)],
        compiler_params=_cparams("parallel", "parallel"),
        name="nsa_cmp_select",
    )(z, kcp, vcp, cb)


def _flash_update_t(s, vt, state, acc_ref):
    m, l = state
    m_new = jnp.maximum(m, jnp.max(s, axis=0, keepdims=True))
    p = jnp.exp2(s - m_new)
    alpha = jnp.exp2(m - m_new)
    acc_ref[...] = alpha * acc_ref[...] + _dot(vt, p.astype(BF16))
    return m_new, alpha * l + jnp.sum(p, axis=0, keepdims=True)


def _flash_pipeline(qk, softmax, lo, hi, state, s_ref, diag_in_loop):
    s_ref[0] = qk(0, lo)

    def step(kc, st):
        s_ref[1] = qk(1, kc)
        st0 = softmax(0, s_ref[0], kc, st[0], False)
        s_ref[0] = qk(0, jnp.minimum(kc + 1, hi))
        st1 = softmax(1, s_ref[1], kc, st[1], False)
        return st0, st1

    end = hi + 1 if diag_in_loop else hi
    npair = (end - lo) // 2
    st = lax.fori_loop(0, npair, lambda i, st: step(lo + 2 * i + 1, step(lo + 2 * i, st)), state)
    st = lax.fori_loop(lo + 2 * npair, end, step, st)
    if diag_in_loop:
        return st
    s_ref[1] = qk(1, hi)
    st0 = softmax(0, s_ref[0], hi, st[0], True)
    st1 = softmax(1, s_ref[1], hi, st[1], True)
    return st0, st1


def _split_heads_kv(slab, g):
    lane = lax.broadcasted_iota(jnp.int32, slab.shape, 1)
    own = jnp.where((lane < A_HEAD_DIM) == (g == 0), slab, 0.0)
    other = pltpu.roll(own, A_HEAD_DIM, axis=1)
    is0 = g == 0
    return jnp.where(is0, own, other), jnp.where(is0, other, own)


def _nsa_flash_kernel(zq_ref, sbt_ref, zks_ref, zvs_ref, zkw_ref, zvw_ref, tab_ref, os_ref, ow_ref,
                      kse_ref, kso_ref, kwe_ref, kwo_ref, vts_ref, vtw_ref, acc_ref, s_ref, *, S):
    g = pl.program_id(1)
    qt = pl.program_id(2)

    @pl.when(qt == 0)
    def _():
        rowb = lax.broadcasted_iota(jnp.int32, (S, LANES), 0) >> 6
        lane = lax.broadcasted_iota(jnp.int32, (S, LANES), 1)
        onehot = jnp.where(rowb == lane, 1.0, 0.0).astype(BF16)
        for zk_ref, zv_ref, ke_ref, ko_ref, vt_ref in ((zks_ref, zvs_ref, kse_ref, kso_ref, vts_ref),
                                                      (zkw_ref, zvw_ref, kwe_ref, kwo_ref, vtw_ref)):
            ke, ko = _split_heads_kv(zk_ref[...].astype(F32), g)
            ke_ref[:, :LANES] = ke.astype(BF16)
            ko_ref[:, :LANES] = ko.astype(BF16)
            for c in range(S // TQ):
                vt_ref[c] = zv_ref[c * TQ:(c + 1) * TQ, :].astype(F32).T.astype(BF16)
        kse_ref[:, LANES:] = onehot
        kso_ref[:, LANES:] = onehot

    zq = zq_ref[...].astype(F32) * (A_HEAD_DIM ** -0.5 * LOG2E)
    rhs_q = jnp.concatenate([zq[:, :LANES], zq[:, LANES:]], axis=0).T.astype(BF16)
    sbt = sbt_ref[...]
    rhs_sel = jnp.concatenate([rhs_q, jnp.concatenate([sbt, sbt], axis=1)], axis=0)
    vrow = pl.multiple_of(g * A_HEAD_DIM, A_HEAD_DIM)
    init = (jnp.full((1, 2 * TQ), 0.5 * NEG, F32), jnp.zeros((1, 2 * TQ), F32))

    def branch(k_refs, vt_ref, rhs, lo, table_of, o_ref):
        acc_ref[...] = jnp.zeros(acc_ref.shape, F32)

        def qk(eo, kc):
            ks = pl.multiple_of(kc * TQ, TQ)
            return _dot(k_refs[eo][pl.ds(ks, TQ), :], rhs)

        def softmax(eo, s, kc, state, diag):
            vt = vt_ref[kc, pl.ds(vrow, A_HEAD_DIM), :]
            return _flash_update_t(s + tab_ref[eo, table_of(qt - kc)], vt, state, acc_ref.at[eo])

        carry = _flash_pipeline(qk, softmax, lo, qt, (init, init), s_ref, True)
        out_t = jnp.concatenate([acc_ref[0] * (1.0 / carry[0][1]), acc_ref[1] * (1.0 / carry[1][1])], axis=0)
        out = out_t.T
        o_ref[:, :LANES] = out[:TQ]
        o_ref[:, LANES:] = out[TQ:]

    branch((kse_ref, kso_ref), vts_ref, rhs_sel, 0, lambda d: jnp.minimum(d, 2), os_ref)
    branch((kwe_ref, kwo_ref), vtw_ref, rhs_q, jnp.maximum(qt - 2, 0),
           lambda d: jnp.where(d == 2, 3, d), ow_ref)


def _nsa_flash(z, sb, tab, B, S):
    nq = S // TQ
    zslab = lambda col: pl.BlockSpec((S, LANES), lambda b, g, q: (b, col // LANES))
    ospec = pl.BlockSpec((TQ, 2 * LANES), lambda b, g, q: (b * nq + q, g))
    oshape = jax.ShapeDtypeStruct((B * S, A_Q), F32)
    return pl.pallas_call(
        functools.partial(_nsa_flash_kernel, S=S),
        grid=(B, A_KV_GROUPS, nq),
        in_specs=[pl.BlockSpec((TQ, 2 * LANES), lambda b, g, q: (b * nq + q, g)),
                  pl.BlockSpec((None, None, LANES, TQ), lambda b, g, q: (b, g, 0, q)),
                  zslab(Z_AKS), zslab(Z_AVS), zslab(Z_AKW), zslab(Z_AVW),
                  pl.BlockSpec((None, 2, 4, TQ, 2 * TQ), lambda b, g, q: (g, 0, 0, 0, 0))],
        out_specs=[ospec, ospec],
        out_shape=[oshape, oshape],
        scratch_shapes=[pltpu.VMEM((S, 2 * LANES), BF16), pltpu.VMEM((S, 2 * LANES), BF16),
                        pltpu.VMEM((S, LANES), BF16), pltpu.VMEM((S, LANES), BF16),
                        pltpu.VMEM((S // TQ, LANES, TQ), BF16), pltpu.VMEM((S // TQ, LANES, TQ), BF16),
                        pltpu.VMEM((2, A_HEAD_DIM, 2 * TQ), F32),
                        pltpu.VMEM((2, TQ, 2 * TQ), F32)],
        compiler_params=_cparams("parallel", "parallel", "arbitrary"),
        name="nsa_selected_window",
    )(z, sb, z, z, z, z, tab)


def _mla_proj_kernel(zqa_ref, zkva_ref, zkra_ref, zkrb_ref, qn_ref, kvn_ref, wqa_ref, wqb_ref,
                     wk_ref, wv_ref, caq_ref, cbq_ref, cak_ref, cbk_ref, q_ref, k_ref, v_ref):
    def rms(x, gain):
        return x * lax.rsqrt(jnp.mean(x * x, axis=-1, keepdims=True) + EPS) * gain

    nq = rms(zqa_ref[...].astype(F32), qn_ref[...]).astype(BF16)
    nkv = rms(zkva_ref[...].astype(F32), kvn_ref[...]).astype(BF16)
    qa = _dot_t(wqa_ref[...], nq)
    qb = _dot_t(wqb_ref[...], nq)
    kn = _dot(nkv, wk_ref[...])
    vt = _dot_t(wv_ref[...], nkv)
    caq, cbq = caq_ref[...], cbq_ref[...]
    kr = zkra_ref[...].astype(F32) * cak_ref[...] + zkrb_ref[...].astype(F32) * cbk_ref[...]
    for h in range(B_HEADS):
        sl = slice(h * LANES, (h + 1) * LANES)
        q_ref[h] = (qa[sl, :] * caq + qb[sl, :] * cbq).astype(BF16)
        k_ref[h] = (kn[:, sl] + kr).astype(BF16)
        v_ref[h] = vt[h * V_DIM:(h + 1) * V_DIM, :].astype(BF16)


def _mla_proj(z, qn, kvn, wqa, wqb, wk, wv, caq, cbq, cak, cbk, B, S):
    tm = TM
    nt = S // tm
    zspec = lambda w, col: pl.BlockSpec((tm, w), lambda b, i: (b * nt + i, col // w))
    full = lambda a: pl.BlockSpec(a.shape, lambda b, i: (0,) * a.ndim)
    tspec = pl.BlockSpec((tm, LANES), lambda b, i: (i, 0))
    tspec_t = pl.BlockSpec((LANES, tm), lambda b, i: (0, i))
    return pl.pallas_call(
        _mla_proj_kernel,
        grid=(B, nt),
        in_specs=[zspec(Q_LORA, Z_QA), zspec(LANES, Z_KVA), zspec(LANES, Z_KRA), zspec(LANES, Z_KRB),
                  full(qn), full(kvn), full(wqa), full(wqb), full(wk), full(wv),
                  tspec_t, tspec_t, tspec, tspec],
        out_specs=[pl.BlockSpec((None, B_HEADS, LANES, tm), lambda b, i: (b, 0, 0, i)),
                   pl.BlockSpec((None, B_HEADS, tm, LANES), lambda b, i: (b, 0, i, 0)),
                   pl.BlockSpec((None, B_HEADS, None, V_DIM, tm), lambda b, i: (b, 0, i, 0, 0))],
        out_shape=[jax.ShapeDtypeStruct((B, B_HEADS, LANES, S), BF16),
                   jax.ShapeDtypeStruct((B, B_HEADS, S, LANES), BF16),
                   jax.ShapeDtypeStruct((B, B_HEADS, nt, V_DIM, tm), BF16)],
        compiler_params=_cparams("parallel", "parallel"),
        name="mla_proj",
    )(z, z, z, z, qn, kvn, wqa, wqb, wk, wv, caq, cbq, cak, cbk)


def _mla_flash_kernel(qt_ref, k_ref, vt_ref, o_ref, acc_ref, s_ref):
    qi = pl.program_id(2)
    ri = lax.broadcasted_iota(jnp.int32, (TM, TM), 0)
    ci = lax.broadcasted_iota(jnp.int32, (TM, TM), 1)
    causal = ri <= ci
    init = (jnp.full((1, TM), 0.5 * NEG, F32), jnp.zeros((1, TM), F32))

    for pr in range(MLA_PAIRS):
        acc_ref[...] = jnp.zeros(acc_ref.shape, F32)

        def qk(hh, kc, pr=pr):
            ks = pl.multiple_of(kc * TM, TM)
            return _dot(k_ref[2 * pr + hh, pl.ds(ks, TM), :], qt_ref[2 * pr + hh])

        def softmax(hh, s, kc, state, diag, pr=pr):
            if diag:
                s = jnp.where(causal, s, NEG)
            return _flash_update_t(s, vt_ref[2 * pr + hh, kc], state, acc_ref.at[hh])

        carry = _flash_pipeline(qk, softmax, 0, qi, (init, init), s_ref, False)
        out_t = jnp.concatenate([acc_ref[0] * (1.0 / carry[0][1]), acc_ref[1] * (1.0 / carry[1][1])], axis=0)
        o_ref[:, pr * LANES:(pr + 1) * LANES] = out_t.T


def _mla_flash(qt, k, vt, B, S):
    nq = S // TM
    hb = 2 * MLA_PAIRS
    hp = B_HEADS // hb
    return pl.pallas_call(
        _mla_flash_kernel,
        grid=(B, hp, nq),
        in_specs=[pl.BlockSpec((None, hb, LANES, TM), lambda b, h, i: (b, h, 0, i)),
                  pl.BlockSpec((None, hb, S, LANES), lambda b, h, i: (b, h, 0, 0)),
                  pl.BlockSpec((None, hb, nq, V_DIM, TM), lambda b, h, i: (b, h, 0, 0, 0))],
        out_specs=pl.BlockSpec((TM, MLA_PAIRS * LANES), lambda b, h, i: (b * nq + i, h)),
        out_shape=jax.ShapeDtypeStruct((B * S, B_HEADS * V_DIM), F32),
        scratch_shapes=[pltpu.VMEM((2, V_DIM, TM), F32), pltpu.VMEM((2, TM, TM), F32)],
        compiler_params=_cparams("parallel", "parallel", "arbitrary"),
        name="mla_flash",
    )(qt, k, vt)


def _rwkv_prep_kernel(zr_ref, zk_ref, zv_ref, zwa_ref, zg_ref, mu_r, mu_k, mu_v, mu_wa, mu_g,
                      w0_ref, a0_ref, kk_ref, ka_ref, rk_ref, w2_ref, a2_ref, g2_ref, bd_ref,
                      r_o, k_o, v_o, kk_o, be_o, ld_o, g_o, bo_o,
                      c_r, c_k, c_v, c_wa, c_g):
    t = pl.program_id(1)
    tm = zr_ref.shape[0]

    def shifted(z_ref, mu_ref, c_ref):
        x = z_ref[...].astype(F32)
        row = lax.broadcasted_iota(jnp.int32, x.shape, 0)
        prev = jnp.where(t == 0, 0.0, c_ref[0:1, :])
        xs = jnp.where(row == 0, prev, pltpu.roll(x, 1, axis=0))
        c_ref[0:1, :] = x[tm - 1:tm, :]
        return x + (xs - x) * mu_ref[...]

    r = shifted(zr_ref, mu_r, c_r)
    k = shifted(zk_ref, mu_k, c_k)
    v = shifted(zv_ref, mu_v, c_v)
    wa = shifted(zwa_ref, mu_wa, c_wa)
    gd = shifted(zg_ref, mu_g, c_g)
    w = w0_ref[...] + _dot(jnp.tanh(wa).astype(BF16), w2_ref[...])
    ld_o[...] = -jax.nn.sigmoid(w) * math.exp(-0.5)
    a = jax.nn.sigmoid(a0_ref[...] + _dot(wa.astype(BF16), a2_ref[...]))
    g_o[...] = _dot(jax.nn.sigmoid(gd).astype(BF16), g2_ref[...])
    kk = k * kk_ref[...]
    nsq = _dot3(kk * kk, bd_ref[...])
    kk = kk / jnp.maximum(jnp.sqrt(nsq), 1e-12)
    k2 = k * (1.0 + (a - 1.0) * ka_ref[...])
    rks = _dot3(r * k2 * rk_ref[...], bd_ref[...])
    r_o[...] = r
    k_o[...] = k2
    v_o[...] = v
    kk_o[...] = kk
    be_o[...] = kk * a
    bo_o[...] = rks * v


def _rwkv_prep(z, mus, w0, a0, k_k, k_a, r_k, w2p, a2p, g2, bd, B, S, tm=512):
    nt = S // tm
    W = C_WIDTH
    zspec = lambda w, col: pl.BlockSpec((tm, w), lambda b, i: (b * nt + i, col // w))
    full = lambda a: pl.BlockSpec(a.shape, lambda b, i: (0,) * a.ndim)
    ospec = pl.BlockSpec((tm, W), lambda b, i: (b * nt + i, 0))
    oshape = jax.ShapeDtypeStruct((B * S, W), F32)
    consts = list(mus) + [w0, a0, k_k, k_a, r_k, w2p, a2p, g2, bd]
    return pl.pallas_call(
        _rwkv_prep_kernel,
        grid=(B, nt),
        in_specs=[zspec(W, Z_CR), zspec(W, Z_CK), zspec(W, Z_CV), zspec(LANES, Z_CWA), zspec(LANES, Z_CG)]
                 + [full(a) for a in consts],
        out_specs=[ospec] * 8,
        out_shape=[oshape] * 8,
        scratch_shapes=[pltpu.VMEM((8, W), F32)] * 3 + [pltpu.VMEM((8, LANES), F32)] * 2,
        compiler_params=_cparams("parallel", "arbitrary"),
        name="rwkv_prep",
    )(z, z, z, z, z, *consts)


def _rwkv_chunk_kernel(r_ref, k_ref, v_ref, kk_ref, be_ref, ld_ref,
                       qa_o, bkt_o, yu_o, vp_o, pcb_o):
    C = CH
    R = ld_ref.shape[0]
    ld = ld_ref[...]
    rr = lax.broadcasted_iota(jnp.int32, (R, R), 0)
    cc = lax.broadcasted_iota(jnp.int32, (R, R), 1)
    sh = int(math.log2(C))
    tri = jnp.where((cc <= rr) & ((cc >> sh) == (rr >> sh)), 1.0, 0.0).astype(BF16)
    cs = sum(_dot(tri, t) for t in _split3(ld))
    cl = jnp.concatenate([jnp.broadcast_to(cs[(s + 1) * C - 1:(s + 1) * C, :], (C, C_WIDTH))
                          for s in range(R // C)], axis=0)
    ex, exn, exx, exc = jnp.exp(cs), jnp.exp(-cs), jnp.exp(cs - ld), jnp.exp(cl - cs)
    kk, be, k2, v = kk_ref[...], be_ref[...], k_ref[...], v_ref[...]
    at = -kk * exx
    rt = r_ref[...] * ex
    bt = be * exn
    kt = k2 * exn
    btc = be * exc
    ktc = k2 * exc
    pc = jnp.exp(cl)
    ri = lax.broadcasted_iota(jnp.int32, (C, C), 0)
    ci = lax.broadcasted_iota(jnp.int32, (C, C), 1)
    incl = ci <= ri
    strict = ci < ri
    eye = jnp.where(ci == ri, 1.0, 0.0)
    lane = lax.broadcasted_iota(jnp.int32, (C, LANES), 1)
    P = C_HEADS // 2
    pairs = [(s, p) for s in range(R // C) for p in range(P)]
    heads = [(i, hh) for i in range(len(pairs)) for hh in range(2)]
    blk = lambda x, i: x[pairs[i][0] * C:(pairs[i][0] + 1) * C, pairs[i][1] * LANES:(pairs[i][1] + 1) * LANES]
    lhs2 = [jnp.concatenate([blk(at, i), blk(rt, i)], axis=0).astype(BF16) for i in range(len(pairs))]
    bm, km, vm, am = [], [], [], []
    for i, hh in heads:
        msk = (lane < C_HEAD_DIM) if hh == 0 else (lane >= C_HEAD_DIM)
        bm.append(jnp.where(msk, blk(bt, i), 0.0).astype(BF16))
        km.append(jnp.where(msk, blk(kt, i), 0.0).astype(BF16))
        vm.append(jnp.where(msk, blk(v, i), 0.0).astype(BF16))
        am.append(jnp.where(msk, blk(at, i), 0.0).astype(BF16))
    g1 = [_dot_t(lhs2[i], bm[j]) for j, (i, hh) in enumerate(heads)]
    g2 = [_dot_t(lhs2[i], km[j]) for j, (i, hh) in enumerate(heads)]
    lab = [jnp.where(strict, g[:C], 0.0) for g in g1]
    mrb = [jnp.where(incl, g[C:], 0.0).astype(BF16) for g in g1]
    lak = [jnp.where(strict, g[:C], 0.0).astype(BF16) for g in g2]
    mrk = [jnp.where(incl, g[C:], 0.0).astype(BF16) for g in g2]
    w2 = [_dot(a, b) for a, b in zip(lak, vm)]
    yk = [_dot(a, b) for a, b in zip(mrk, vm)]
    tinv = [eye + x for x in lab]
    lp = lab
    for _ in range(int(math.log2(C)) - 1):
        lpb = [x.astype(BF16) for x in lp]
        lp = [_dot(x, x) for x in lpb]
        tinv = [t + _dot(t.astype(BF16), x.astype(BF16)) for t, x in zip(tinv, lp)]
    au = [_dot(t.astype(BF16), jnp.concatenate([a, w.astype(BF16)], axis=1)) for t, a, w in zip(tinv, am, w2)]
    qy = [_dot(m, x.astype(BF16)) for m, x in zip(mrb, au)]
    for i, (s, p) in enumerate(pairs):
        e, o = 2 * i, 2 * i + 1
        ahat = au[e][:, :LANES] + au[o][:, :LANES]
        uhat = au[e][:, LANES:] + au[o][:, LANES:]
        qhat = blk(rt, i) + qy[e][:, :LANES] + qy[o][:, :LANES]
        yi = qy[e][:, LANES:] + qy[o][:, LANES:] + yk[e] + yk[o]
        qa_o[s, p] = jnp.concatenate([qhat, ahat], axis=0).astype(BF16)
        yu_o[s, p] = jnp.concatenate([yi, uhat], axis=0)
        bkt_o[s, p] = jnp.concatenate([blk(btc, i), blk(ktc, i)], axis=0).T.astype(BF16)
        vp_o[s, p] = blk(v, i).astype(BF16)
        pcb_o[s, p] = jnp.concatenate([blk(pc, i), blk(pc, i)], axis=0).T


def _rwkv_chunk(r, k2, v, kk, be, ld, B, S):
    nch = S // CH
    W = C_WIDTH
    P = C_HEADS // 2
    ns = RW_SUB
    ispec = pl.BlockSpec((ns * CH, W), lambda b, c: (b * (nch // ns) + c, 0))
    sq = lambda rows: pl.BlockSpec((None, ns, P, rows, LANES), lambda b, c: (b, c, 0, 0, 0))
    shp = lambda rows, dt: jax.ShapeDtypeStruct((B, nch, P, rows, LANES), dt)
    return pl.pallas_call(
        _rwkv_chunk_kernel,
        grid=(B, nch // ns),
        in_specs=[ispec] * 6,
        out_specs=[sq(2 * CH), sq(LANES), sq(2 * CH), sq(CH), sq(LANES)],
        out_shape=[shp(2 * CH, BF16), shp(LANES, BF16), shp(2 * CH, F32), shp(CH, BF16), shp(LANES, F32)],
        compiler_params=_cparams("parallel", "parallel"),
        name="rwkv_chunk",
    )(r, k2, v, kk, be, ld)


def _rwkv_state_kernel(qa_ref, bkt_ref, yu_ref, vp_ref, pcb_ref, y_ref, ap_ref):
    @pl.when(pl.program_id(1) == 0)
    def _():
        ap_ref[...] = jnp.zeros(ap_ref.shape, F32)

    ri = lax.broadcasted_iota(jnp.int32, (LANES, LANES), 0)
    ci = lax.broadcasted_iota(jnp.int32, (LANES, LANES), 1)
    same_head = (ri < C_HEAD_DIM) == (ci < C_HEAD_DIM)
    idx = [(b, p) for b in range(qa_ref.shape[0]) for p in range(C_HEADS // 2)]
    a = [ap_ref[b, p] for b, p in idx]
    for c in range(qa_ref.shape[1]):
        x = [_dot(qa_ref[b, c, p], s.astype(BF16)) for (b, p), s in zip(idx, a)]
        yu = [yu_ref[b, c, p] for b, p in idx]
        for (b, p), xi, yi in zip(idx, x, yu):
            y_ref[b, c * CH:(c + 1) * CH, p * LANES:(p + 1) * LANES] = xi[:CH] + yi[:CH]
        uv = [jnp.concatenate([(xi[CH:] + yi[CH:]).astype(BF16), vp_ref[b, c, p]], axis=0)
              for (b, p), xi, yi in zip(idx, x, yu)]
        upd = [_dot(bkt_ref[b, c, p], t) for (b, p), t in zip(idx, uv)]
        a = [pcb_ref[b, c, p] * s + jnp.where(same_head, t, 0.0) for (b, p), s, t in zip(idx, a, upd)]
    for (b, p), s in zip(idx, a):
        ap_ref[b, p] = s


def _rwkv_state(qa, bkt, yu, vp, pcb, B, S):
    nch = S // CH
    P = C_HEADS // 2
    nb = 2 if B % 2 == 0 else 1
    nc = 4 if nch % 4 == 0 else 1
    sq = lambda rows: pl.BlockSpec((nb, nc, P, rows, LANES), lambda b, c: (b, c, 0, 0, 0))
    return pl.pallas_call(
        _rwkv_state_kernel,
        grid=(B // nb, nch // nc),
        in_specs=[sq(2 * CH), sq(LANES), sq(2 * CH), sq(CH), sq(LANES)],
        out_specs=pl.BlockSpec((nb, nc * CH, C_WIDTH), lambda b, c: (b, c, 0)),
        out_shape=jax.ShapeDtypeStruct((B, S, C_WIDTH), F32),
        scratch_shapes=[pltpu.VMEM((nb, P, LANES, LANES), F32)],
        compiler_params=_cparams("parallel", "arbitrary"),
        name="rwkv_state",
    )(qa, bkt, yu, vp, pcb).reshape(B * S, C_WIDTH)


def _merge_kernel(oc_ref, os_ref, ow_ref, zag_ref, yb_ref, yr_ref, bo_ref, gg_ref,
                  zga_ref, zgb_ref, zgc_ref, x_ref, wbr_ref, wout_ref, ge_ref, bdm_ref,
                  ln_ref, gain_ref, gt_ref, o_ref):
    sg = jax.nn.sigmoid(zag_ref[...].astype(F32))
    ya = (_dot2(sg, ge_ref[0]) * oc_ref[...] + _dot2(sg, ge_ref[1]) * os_ref[...]
          + _dot2(sg, ge_ref[2]) * ow_ref[...])
    yr = yr_ref[...]
    mean = _dot3(yr, bdm_ref[...])
    d = yr - mean
    var = _dot2(d * d, bdm_ref[...])
    yn = d * lax.rsqrt(var + GN_EPS) * ln_ref[0:1, :] + ln_ref[1:2, :]
    yc = (yn + bo_ref[...]) * gg_ref[...]
    sig = lambda ref: jax.nn.sigmoid(ref[...].astype(F32))
    merged = (sig(zga_ref) * _dot(ya.astype(BF16), wbr_ref[0:A_Q, :])
              + sig(zgb_ref) * _dot(yb_ref[...].astype(BF16), wbr_ref[A_Q:2 * A_Q, :])
              + sig(zgc_ref) * _dot(yc.astype(BF16), wbr_ref[2 * A_Q:3 * A_Q, :]))
    y = _dot(merged.astype(BF16), wout_ref[...])
    yn2 = y * lax.rsqrt(jnp.mean(y * y, axis=-1, keepdims=True) + EPS) * gain_ref[...]
    o_ref[...] = x_ref[...] + gt_ref[0] * yn2


def _merge(oc, os_, ow, z, yb, yr, bo, gg, x2, wbr, wout, gexp, bdm, ln, gain, gt, S, tm=256):
    N, D = x2.shape
    tpb = S // tm
    W = A_Q
    row = lambda w: pl.BlockSpec((tm, w), lambda i: (i, 0))
    zspec = lambda w, col: pl.BlockSpec((tm, w), lambda i: (i, col // w))
    full = lambda a: pl.BlockSpec(a.shape, lambda i: (0,) * a.ndim)
    return pl.pallas_call(
        _merge_kernel,
        grid=(N // tm,),
        in_specs=[row(W), row(W), row(W), zspec(LANES, Z_AG), row(W), row(W), row(W), row(W),
                  zspec(D, Z_ZG), zspec(D, Z_ZG + D), zspec(D, Z_ZG + 2 * D), row(D),
                  full(wbr), full(wout), full(gexp), full(bdm), full(ln),
                  pl.BlockSpec((1, D), lambda i: (0, 0)),
                  pl.BlockSpec((1, 1, D), lambda i: (i // tpb, 0, 0))],
        out_specs=row(D),
        out_shape=jax.ShapeDtypeStruct((N, D), F32),
        compiler_params=_cparams("parallel"),
        name="merge_out",
    )(oc, os_, ow, z, yb, yr, bo, gg, z, z, z, x2, wbr, wout, gexp, bdm, ln, gain.reshape(1, D), gt)


def _ffn_up_kernel(x_ref, g_ref, sc_ref, sh_ref, wg_ref, wv_ref, cw_ref, cb_ref, o_ref,
                   cg_ref, cv_ref, *, tpb):
    i = pl.program_id(0)
    tm = x_ref.shape[0]
    fc = FFN_SUB
    first = (i % tpb) == 0
    x = x_ref[...]
    y = x * lax.rsqrt(jnp.mean(x * x, axis=-1, keepdims=True) + EPS)
    h = ((y * g_ref[...]) * (1.0 + sc_ref[0]) + sh_ref[0]).astype(BF16)
    row = lax.broadcasted_iota(jnp.int32, (8, fc), 0)

    def up(c):
        sl = slice(c * fc, (c + 1) * fc)
        return _dot(h, wg_ref[:, sl]), _dot(h, wv_ref[:, sl])

    def conv(u, c_ref, c, off):
        sl = slice(c * fc, (c + 1) * fc)
        wl = slice(off + c * fc, off + (c + 1) * fc)
        prev = c_ref[:, sl]
        p1 = jnp.where(first, 0.0, prev[7:8])
        p2 = jnp.where(first, 0.0, prev[6:7])
        c_ref[:, sl] = u[tm - 8:tm, :]
        r1 = pltpu.roll(u, 1, axis=0)
        r2 = pltpu.roll(u, 2, axis=0)
        t1 = jnp.where(row == 0, p1, r1[0:8])
        t2 = jnp.where(row == 0, p2, jnp.where(row == 1, p1, r2[0:8]))
        u1 = jnp.concatenate([t1, r1[8:]], axis=0)
        u2 = jnp.concatenate([t2, r2[8:]], axis=0)
        return cw_ref[0:1, wl] * u2 + cw_ref[1:2, wl] * u1 + cw_ref[2:3, wl] * u + cb_ref[:, wl]

    nsub = D_FF // fc
    cur = up(0)
    for c in range(nsub):
        nxt = up(c + 1) if c + 1 < nsub else None
        a = _gelu(conv(cur[0], cg_ref, c, 0)) * conv(cur[1], cv_ref, c, D_FF)
        o_ref[:, c * fc:(c + 1) * fc] = a.astype(BF16)
        cur = nxt


def _ffn_up(x2, gain, sc, sh, w_up, conv_w, conv_b, S, tm=256):
    N, D = x2.shape
    tpb = S // tm
    full = lambda a: pl.BlockSpec(a.shape, lambda i: (0,) * a.ndim)
    return pl.pallas_call(
        functools.partial(_ffn_up_kernel, tpb=tpb),
        grid=(N // tm,),
        in_specs=[pl.BlockSpec((tm, D), lambda i: (i, 0)),
                  pl.BlockSpec((1, D), lambda i: (0, 0)),
                  pl.BlockSpec((1, 1, D), lambda i: (i // tpb, 0, 0)),
                  pl.BlockSpec((1, 1, D), lambda i: (i // tpb, 0, 0)),
                  pl.BlockSpec((D, D_FF), lambda i: (0, 0)),
                  pl.BlockSpec((D, D_FF), lambda i: (0, 1)),
                  full(conv_w), pl.BlockSpec((1, 2 * D_FF), lambda i: (0, 0))],
        out_specs=pl.BlockSpec((tm, D_FF), lambda i: (i, 0)),
        out_shape=jax.ShapeDtypeStruct((N, D_FF), BF16),
        scratch_shapes=[pltpu.VMEM((8, D_FF), F32), pltpu.VMEM((8, D_FF), F32)],
        compiler_params=_cparams("arbitrary"),
        name="ffn_up",
    )(x2, gain.reshape(1, D), sc, sh, w_up, w_up, conv_w, conv_b.reshape(1, 2 * D_FF))


def _ffn_down_kernel(a_ref, wd_ref, x_ref, gain_ref, gt_ref, o_ref):
    f = _dot(a_ref[...], wd_ref[...])
    fn = f * lax.rsqrt(jnp.mean(f * f, axis=-1, keepdims=True) + EPS) * gain_ref[...]
    o_ref[...] = x_ref[...] + gt_ref[0] * fn


def _ffn_down(a, wd, x2, gain, gt, S, tm=512):
    N, D = x2.shape
    tpb = S // tm
    return pl.pallas_call(
        _ffn_down_kernel,
        grid=(N // tm,),
        in_specs=[pl.BlockSpec((tm, D_FF), lambda i: (i, 0)),
                  pl.BlockSpec((D_FF, D), lambda i: (0, 0)),
                  pl.BlockSpec((tm, D), lambda i: (i, 0)),
                  pl.BlockSpec((1, D), lambda i: (0, 0)),
                  pl.BlockSpec((1, 1, D), lambda i: (i // tpb, 0, 0))],
        out_specs=pl.BlockSpec((tm, D), lambda i: (i, 0)),
        out_shape=jax.ShapeDtypeStruct((N, D), F32),
        compiler_params=_cparams("parallel"),
        name="ffn_down",
    )(a, wd, x2, gain.reshape(1, D), gt)


def _t5_bucket_np(dist):
    d = np.maximum(dist, 0)
    max_exact = N_BUCKETS // 2
    large = max_exact + (np.log(np.maximum(d, 1).astype(np.float32) / max_exact)
                         / math.log(MAX_DISTANCE / max_exact) * (N_BUCKETS - max_exact)).astype(np.int32)
    return np.where(d < max_exact, d, np.minimum(large, N_BUCKETS - 1))


def _bias_table_kernel(scale_ref, rb_ref, code_ref, o_ref):
    code = code_ref[0]
    sc = scale_ref[pl.program_id(0)]
    for h in range(A_HEADS):
        acc = jnp.full(code.shape, NEG, F32)
        for b in range(N_BUCKETS):
            acc = jnp.where(code == b, rb_ref[b, h], acc)
        o_ref[0, h] = acc * sc


def _bias_tables(rel_bias, codes, scales):
    nt, R, C = codes.shape
    smem = pl.BlockSpec(memory_space=pltpu.SMEM)
    return pl.pallas_call(
        _bias_table_kernel,
        grid=(nt,),
        in_specs=[smem, smem, pl.BlockSpec((1, R, C), lambda t: (t, 0, 0))],
        out_specs=pl.BlockSpec((1, A_HEADS, R, C), lambda t: (t, 0, 0, 0)),
        out_shape=jax.ShapeDtypeStruct((nt, A_HEADS, R, C), F32),
        compiler_params=_cparams("parallel"),
        name="bias_tables",
    )(jnp.asarray(scales, F32), rel_bias, jnp.asarray(codes))


def _codes(dist, lo, hi):
    return np.where((dist >= lo) & (dist < hi), _t5_bucket_np(dist), N_BUCKETS).astype(np.int32)


def _nsa_tables(rel_bias, S):
    NC = S // CMP_STRIDE
    G, HPG = A_KV_GROUPS, A_HEADS // A_KV_GROUPS
    big = 1 << 30
    i = np.arange(TQ)[:, None]
    dc = i - CMP_STRIDE * np.arange(NC)[None, :] + CMP_STRIDE * NC - TQ - (CMP_BLOCK - 1)
    cb = _bias_tables(rel_bias, _codes(dc, 0, big)[None], [1.0])[0]
    cb = cb.reshape(G, HPG // 2, 2, TQ, NC).transpose(0, 2, 1, 3, 4).reshape(G, 2, 2 * TQ, NC)
    dt = np.arange(TQ)[None, :] - np.arange(TQ)[:, None]
    far = np.full((TQ, TQ), MAX_DISTANCE)
    codes = np.stack([_codes(dt, 0, WINDOW), _codes(TQ + dt, 0, WINDOW), _codes(far, 0, big),
                      _codes(2 * TQ + dt, 0, WINDOW)])
    t = _bias_tables(rel_bias, codes, [LOG2E] * 4)
    t = t.reshape(4, G, HPG // 2, 2, TQ, TQ).transpose(1, 3, 0, 4, 2, 5).reshape(G, 2, 4, TQ, 2 * TQ)
    return cb, t


def _pad_cols(w, n):
    return jnp.pad(w, ((0, 0), (0, n - w.shape[1])))


def _prep_w_in(w):
    D = w.shape[0]
    za, zb, zc, zg = jnp.split(w, np.cumsum([A_COLS, B_COLS, C_COLS]).tolist(), axis=1)
    a_parts = jnp.split(za, np.cumsum([A_Q] + [A_KV] * 6).tolist(), axis=1)
    qa, kva, kr = jnp.split(zb, [Q_LORA, Q_LORA + KV_LORA], axis=1)
    half = ROPE_DIM // 2
    kr_rot = jnp.concatenate([-kr[:, half:], kr[:, :half]], axis=1)
    z64 = jnp.zeros((D, NOPE_DIM), w.dtype)
    z32 = jnp.zeros((D, LANES - NOPE_DIM - ROPE_DIM), w.dtype)
    c_r, c_k, c_v, c_wd, c_ad, c_gd = jnp.split(
        zc, np.cumsum([C_WIDTH] * 3 + [DECAY_LORA, AAA_LORA]).tolist(), axis=1)
    cols = [a_parts[0], qa] + a_parts[1:7] + [
        _pad_cols(a_parts[7], LANES), kva,
        jnp.concatenate([z64, kr, z32], axis=1),
        jnp.concatenate([z64, kr_rot, z32], axis=1),
        c_r, c_k, c_v, c_wd, c_ad, c_gd, jnp.zeros((D, Z_ZG - Z_CG - LANES), w.dtype), zg]
    out = jnp.concatenate(cols, axis=1)
    assert out.shape[1] == Z_COLS
    return out.astype(BF16)


def _prep_mla(w_uq, w_ukv):
    dq = NOPE_DIM + ROPE_DIM
    half = ROPE_DIM // 2
    wq = w_uq.reshape(Q_LORA, B_HEADS, dq)
    nope, r1, r2 = wq[..., :NOPE_DIM], wq[..., NOPE_DIM:NOPE_DIM + half], wq[..., NOPE_DIM + half:]
    zq = jnp.zeros((Q_LORA, B_HEADS, LANES - dq), w_uq.dtype)
    wqa = jnp.concatenate([nope, r1, r2, zq], axis=-1).reshape(Q_LORA, B_HEADS * LANES)
    wqb = jnp.concatenate([jnp.zeros_like(nope), -r2, r1, zq], axis=-1).reshape(Q_LORA, B_HEADS * LANES)
    wkv = w_ukv.reshape(KV_LORA, B_HEADS, NOPE_DIM + V_DIM)
    kn, vv = wkv[..., :NOPE_DIM], wkv[..., NOPE_DIM:]
    wk = jnp.concatenate([kn, jnp.zeros_like(kn)], axis=-1).reshape(KV_LORA, B_HEADS * LANES)
    wv = vv.reshape(KV_LORA, B_HEADS * V_DIM)
    return wqa.T.astype(BF16), wqb.T.astype(BF16), wk.astype(BF16), wv.T.astype(BF16)


def _rope_tables(S):
    half = ROPE_DIM // 2
    inv = ROPE_THETA ** (-jnp.arange(half, dtype=F32) / half)
    ang = jnp.arange(S, dtype=F32)[:, None] * inv
    cos2 = jnp.tile(jnp.cos(ang), (1, 2))
    sin2 = jnp.tile(jnp.sin(ang), (1, 2))
    scale = (NOPE_DIM + ROPE_DIM) ** -0.5 * LOG2E
    one = jnp.ones((S, NOPE_DIM), F32)
    z64 = jnp.zeros((S, NOPE_DIM), F32)
    z32 = jnp.zeros((S, LANES - NOPE_DIM - ROPE_DIM), F32)
    caq = (jnp.concatenate([one, cos2, z32], axis=1) * scale).T
    cbq = (jnp.concatenate([z64, sin2, z32], axis=1) * scale).T
    cak = jnp.concatenate([z64, cos2, z32], axis=1)
    cbk = jnp.concatenate([z64, sin2, z32], axis=1)
    return caq, cbq, cak, cbk


def _prep_compress(cmp_pos, cmp_w1, cmp_w2):
    Dh = A_HEAD_DIM
    w1 = cmp_w1.reshape(2, CMP_BLOCK, Dh, Dh)
    z = jnp.zeros_like(w1)
    w1bd = jnp.concatenate([jnp.concatenate([w1, z], axis=-1), jnp.concatenate([z, w1], axis=-1)], axis=-2)
    posrow = jnp.broadcast_to(cmp_pos.reshape(2, 1, CMP_BLOCK * Dh), (2, 8, CMP_BLOCK * Dh))
    w1cat = jnp.concatenate([cmp_w1, cmp_w1], axis=-1)
    zz = jnp.zeros((2, Dh, Dh), cmp_w2.dtype)
    blk = lambda a, b, c, d: jnp.concatenate(
        [jnp.concatenate([a, b], axis=-1), jnp.concatenate([c, d], axis=-1)], axis=-2)
    w2v = jnp.stack([jnp.stack([blk(cmp_w2, zz, zz, zz), blk(zz, cmp_w2, zz, zz)], axis=1),
                     jnp.stack([blk(zz, zz, cmp_w2, zz), blk(zz, zz, zz, cmp_w2)], axis=1)], axis=1)
    return w1bd.astype(BF16), posrow, w1cat.astype(BF16), w2v.astype(BF16)


def _gate_expand():
    e = np.zeros((3, LANES, A_Q), np.float32)
    for h in range(A_HEADS):
        for r in range(3):
            e[r, 3 * h + r, h * A_HEAD_DIM:(h + 1) * A_HEAD_DIM] = 1.0
    return jnp.asarray(e, dtype=BF16)


def _block_diag_ones(scale):
    idx = np.arange(C_WIDTH) // C_HEAD_DIM
    return jnp.asarray((idx[:, None] == idx[None, :]).astype(np.float32) * scale, dtype=BF16)


def kernel(x, c, rel_bias, ada_w, ada_b, norm_gain, w_in, nsa_cmp_pos, nsa_cmp_w1, nsa_cmp_w2, mla_q_norm, mla_kv_norm, mla_w_uq, mla_w_ukv, rwkv_mu, rwkv_w0, rwkv_a0, rwkv_k_k, rwkv_k_a, rwkv_w2, rwkv_a2, rwkv_g2, rwkv_r_k, rwkv_ln, w_branch, w_out, ffn_up, ffn_conv_w, ffn_conv_b, ffn_down):
    B, S, D = x.shape
    L = ada_w.shape[0]
    assert S % TQ == 0 and S % TM == 0 and S // SLC_BLOCK <= A_HEAD_DIM and S >= 2 * TQ
    mod = _adaln(c, ada_w, ada_b)
    cb, nsa_tab = _nsa_tables(rel_bias, S)
    caq, cbq, cak, cbk = _rope_tables(S)
    gexp = _gate_expand()
    bd1 = _block_diag_ones(1.0)
    bdm = _block_diag_ones(1.0 / C_HEAD_DIM)
    row = lambda v: v.reshape(1, -1)
    x2 = x.reshape(B * S, D)
    for l in range(L):
        m6 = mod[l].reshape(B, 6, 1, D)
        sh1, sc1, gt1, sh2, sc2, gt2 = (m6[:, i] for i in range(6))
        z = _norm_mod_matmul(x2, norm_gain[l, 0], sc1, sh1, _prep_w_in(w_in[l]), S, tn=Z_COLS // 4)
        kcp, vcp = _compress(z, *_prep_compress(nsa_cmp_pos[l], nsa_cmp_w1[l], nsa_cmp_w2[l]), B, S)
        oc, sb = _cmpsel(z, kcp, vcp, cb, B, S)
        os_, ow = _nsa_flash(z, sb, nsa_tab, B, S)
        wqa, wqb, wk, wv = _prep_mla(mla_w_uq[l], mla_w_ukv[l])
        q, k, v = _mla_proj(z, row(mla_q_norm[l]), row(mla_kv_norm[l]), wqa, wqb, wk, wv,
                            caq, cbq, cak, cbk, B, S)
        yb = _mla_flash(q, k, v, B, S)
        mu = rwkv_mu[l]
        o = 3 * C_WIDTH
        mus = [row(mu[:C_WIDTH]), row(mu[C_WIDTH:2 * C_WIDTH]), row(mu[2 * C_WIDTH:o]),
               row(mu[o:o + LANES]), row(mu[o + LANES:])]
        w2p = jnp.concatenate([rwkv_w2[l], jnp.zeros_like(rwkv_a2[l])], axis=0).astype(BF16)
        a2p = jnp.concatenate([jnp.zeros_like(rwkv_w2[l]), rwkv_a2[l]], axis=0).astype(BF16)
        rr, k2, vv, kk, be, ld, gg, bo = _rwkv_prep(
            z, mus, row(rwkv_w0[l]), row(rwkv_a0[l]), row(rwkv_k_k[l]), row(rwkv_k_a[l]),
            row(rwkv_r_k[l]), w2p, a2p, rwkv_g2[l].astype(BF16), bd1, B, S)
        qa_, bkt, yu, vp, pcb = _rwkv_chunk(rr, k2, vv, kk, be, ld, B, S)
        yr = _rwkv_state(qa_, bkt, yu, vp, pcb, B, S)
        x2 = _merge(oc, os_, ow, z, yb, yr, bo, gg, x2, w_branch[l].astype(BF16), w_out[l].astype(BF16),
                    gexp, bdm, rwkv_ln[l], norm_gain[l, 1], gt1, S)
        a = _ffn_up(x2, norm_gain[l, 2], sc2, sh2, ffn_up[l].astype(BF16), ffn_conv_w[l], ffn_conv_b[l], S)
        x2 = _ffn_down(a, ffn_down[l].astype(BF16), x2, norm_gain[l, 3], gt2, S)
    return x2.reshape(B, S, D)
```

```python
import functools
import math

import jax
import jax.numpy as jnp
import numpy as np
from jax import lax
from jax.experimental import pallas as pl
from jax.experimental.pallas import tpu as pltpu

F32 = jnp.float32
BF16 = jnp.bfloat16

D_MODEL = 1024
A_HEADS, A_KV_GROUPS, A_HEAD_DIM = 8, 2, 64
CMP_BLOCK, CMP_STRIDE = 32, 16
SLC_BLOCK, SLC_TOPK, N_LOCAL_BLOCKS = 64, 16, 2
WINDOW = 512
FORCE_SCORE = 1e9
B_HEADS, Q_LORA, KV_LORA, NOPE_DIM, ROPE_DIM, V_DIM = 8, 256, 128, 64, 32, 64
ROPE_THETA = 10000.0
C_HEADS, C_HEAD_DIM = 8, 64
C_WIDTH = C_HEADS * C_HEAD_DIM
DECAY_LORA, AAA_LORA, GATE_LORA = 64, 64, 128
GN_EPS = 64e-5
N_BUCKETS, MAX_DISTANCE = 32, 128
D_FF = 2816
EPS = 1e-6
NEG = -1e30
LOG2E = math.log2(math.e)

A_Q = A_HEADS * A_HEAD_DIM
A_KV = A_KV_GROUPS * A_HEAD_DIM
A_GATE = 3 * A_HEADS
A_COLS = A_Q + 6 * A_KV + A_GATE
B_COLS = Q_LORA + KV_LORA + ROPE_DIM
C_COLS = 3 * C_WIDTH + DECAY_LORA + AAA_LORA + GATE_LORA

LANES = 128
FFN_SUB = 256
VMEM_LIMIT = 48 * 1024 * 1024

Z_AQ = 0
Z_QA = 512
Z_AKC, Z_AVC, Z_AKS, Z_AVS, Z_AKW, Z_AVW = 768, 896, 1024, 1152, 1280, 1408
Z_AG = 1536
Z_KVA = 1664
Z_KRA = 1792
Z_KRB = 1920
Z_CR, Z_CK, Z_CV = 2048, 2560, 3072
Z_CWA = 3584
Z_CG = 3712
Z_ZG = 4096
Z_COLS = 7168

TQ = 256
TM = 512
MLA_PAIRS = 4
CH = 64
RW_SUB = 4


def _cparams(*sem):
    return pltpu.CompilerParams(dimension_semantics=sem, vmem_limit_bytes=VMEM_LIMIT)


def _gelu(x):
    return 0.5 * x * (1.0 + jnp.tanh(0.7978845608028654 * (x + 0.044715 * (x * x * x))))


def _dot(a, b):
    return jnp.dot(a, b, preferred_element_type=F32)


def _split3(x):
    hi = x.astype(BF16)
    r = x - hi.astype(F32)
    mid = r.astype(BF16)
    return hi, mid, (r - mid.astype(F32)).astype(BF16)


def _dot3(x, w):
    return sum(_dot(t, w) for t in _split3(x))


def _dot2(x, w):
    hi, mid, _ = _split3(x)
    return _dot(hi, w) + _dot(mid, w)


def _dot_t(a, b):
    return lax.dot_general(a, b, (((1,), (1,)), ((), ())), preferred_element_type=F32)


def _adaln_kernel(c_ref, w_ref, b_ref, o_ref):
    c = c_ref[...]
    cond = c * jax.nn.sigmoid(c)
    o_ref[0] = _dot(cond.astype(BF16), w_ref[0].astype(BF16)) + b_ref[0]


def _adaln(c, ada_w, ada_b):
    L, D, N6 = ada_w.shape
    B = c.shape[0]
    tn = 1536
    return pl.pallas_call(
        _adaln_kernel,
        grid=(L, N6 // tn),
        in_specs=[pl.BlockSpec((B, D), lambda l, j: (0, 0)),
                  pl.BlockSpec((1, D, tn), lambda l, j: (l, 0, j)),
                  pl.BlockSpec((1, 1, tn), lambda l, j: (l, 0, j))],
        out_specs=pl.BlockSpec((1, B, tn), lambda l, j: (l, 0, j)),
        out_shape=jax.ShapeDtypeStruct((L, B, N6), F32),
        compiler_params=_cparams("parallel", "parallel"),
        name="adaln",
    )(c, ada_w, ada_b.reshape(L, 1, N6))


def _nmm_kernel(x_ref, g_ref, sc_ref, sh_ref, w_ref, o_ref, h_ref):
    @pl.when(pl.program_id(1) == 0)
    def _():
        x = x_ref[...]
        y = x * lax.rsqrt(jnp.mean(x * x, axis=-1, keepdims=True) + EPS)
        h = (y * g_ref[...]) * (1.0 + sc_ref[0]) + sh_ref[0]
        h_ref[...] = h.astype(BF16)

    o_ref[...] = _dot(h_ref[...], w_ref[...]).astype(o_ref.dtype)


def _norm_mod_matmul(x2, gain, sc, sh, w, S, tn, out_dtype=BF16):
    N, D = x2.shape
    NC = w.shape[1]
    tm = min(1024, S)
    tpb = S // tm
    return pl.pallas_call(
        _nmm_kernel,
        grid=(N // tm, NC // tn),
        in_specs=[pl.BlockSpec((tm, D), lambda i, j: (i, 0)),
                  pl.BlockSpec((1, D), lambda i, j: (0, 0)),
                  pl.BlockSpec((1, 1, D), lambda i, j: (i // tpb, 0, 0)),
                  pl.BlockSpec((1, 1, D), lambda i, j: (i // tpb, 0, 0)),
                  pl.BlockSpec((D, tn), lambda i, j: (0, j))],
        out_specs=pl.BlockSpec((tm, tn), lambda i, j: (i, j)),
        out_shape=jax.ShapeDtypeStruct((N, NC), out_dtype),
        scratch_shapes=[pltpu.VMEM((tm, D), BF16)],
        compiler_params=_cparams("parallel", "arbitrary"),
        name="norm_mod_matmul",
    )(x2, gain.reshape(1, D), sc, sh, w)


def _compress_kernel(zk_ref, zv_ref, w1_ref, pos_ref, w1c_ref, w2_ref, kc_ref, vc_ref, zf_ref, *, NC):
    for kind, (z_ref, o_ref) in enumerate(((zk_ref, kc_ref), (zv_ref, vc_ref))):
        pa = jnp.zeros((NC, LANES), F32)
        pb = jnp.zeros((NC, LANES), F32)
        zf_ref[...] = z_ref[...].astype(F32)
        for l in range(CMP_STRIDE):
            xl = zf_ref[pl.ds(l, NC, stride=CMP_STRIDE), :].astype(BF16)
            pa = pa + _dot(xl, w1_ref[kind, l])
            pb = pb + _dot(xl, w1_ref[kind, CMP_STRIDE + l])
        posb = _dot(pos_ref[kind].astype(BF16), w1c_ref[kind])[0:1, :]
        h = pa + pltpu.roll(pb, NC - 1, axis=0) + posb
        act = _gelu(h).astype(BF16)
        row = lax.broadcasted_iota(jnp.int32, (NC, LANES), 0)
        for g in range(A_KV_GROUPS):
            for eo in range(2):
                out = _dot(act, w2_ref[kind, g, eo])
                out = jnp.where(row < NC - 1, out, 0.0)
                o_ref[g, eo, pl.ds(0, NC), :] = jnp.zeros((NC, LANES), BF16)
                o_ref[g, eo, pl.ds(NC, NC), :] = out.astype(BF16)


def _compress(z, w1bd, posrow, w1cat, w2v, B, S):
    NC = S // CMP_STRIDE
    out = jax.ShapeDtypeStruct((B, A_KV_GROUPS, 2, 2 * NC, LANES), BF16)
    ospec = pl.BlockSpec((None, A_KV_GROUPS, 2, 2 * NC, LANES), lambda b: (b, 0, 0, 0, 0))
    full = lambda a: pl.BlockSpec(a.shape, lambda b: (0,) * a.ndim)
    return pl.pallas_call(
        functools.partial(_compress_kernel, NC=NC),
        grid=(B,),
        in_specs=[pl.BlockSpec((S, LANES), lambda b: (b, Z_AKC // LANES)),
                  pl.BlockSpec((S, LANES), lambda b: (b, Z_AVC // LANES)),
                  full(w1bd), full(posrow), full(w1cat), full(w2v)],
        out_specs=[ospec, ospec],
        out_shape=[out, out],
        scratch_shapes=[pltpu.VMEM((S, LANES), F32)],
        compiler_params=_cparams("parallel"),
        name="nsa_compress",
    )(z, z, w1bd, posrow, w1cat, w2v)


def _stack_pairs(zq, scale):
    q = zq.astype(F32) * scale
    return jnp.concatenate([q[:, :LANES], q[:, LANES:]], axis=0).astype(BF16)


def _cmpsel_kernel(zq_ref, kc_ref, vc_ref, cb_ref, oc_ref, sb_ref, *, NC, NS):
    qt = pl.program_id(1)
    for g in range(A_KV_GROUPS):
        cols = pl.ds(g * 2 * LANES, 2 * LANES)
        _cmpsel_group(zq_ref.at[:, cols], kc_ref.at[g], vc_ref.at[g], cb_ref.at[g],
                      oc_ref.at[:, cols], sb_ref.at[g], qt, NC, NS)


def _cmpsel_group(zq_ref, kc_ref, vc_ref, cb_ref, oc_ref, sb_ref, qt, NC, NS):
    r = TQ // CMP_STRIDE
    lhs = _stack_pairs(zq_ref[...], A_HEAD_DIM ** -0.5)
    st = pl.multiple_of(r * (qt + 1), 16)
    col = lax.broadcasted_iota(jnp.int32, (2 * TQ, NC), 1)
    exists = col >= NC - r * (qt + 1)
    acc = jnp.zeros((2 * TQ, LANES), F32)
    psum = jnp.zeros((TQ, NC), F32)
    for eo in range(2):
        kwin = kc_ref[eo, pl.ds(st, NC), :]
        vwin = vc_ref[eo, pl.ds(st, NC), :]
        logits = _dot_t(lhs, kwin) + cb_ref[eo]
        logits = jnp.where(exists, logits, NEG)
        m = jnp.max(logits, axis=-1, keepdims=True)
        p = jnp.where(logits > 0.1 * NEG, jnp.exp(logits - m), 0.0)
        l = jnp.maximum(jnp.sum(p, axis=-1, keepdims=True), 1e-30)
        p = p * (1.0 / l)
        acc = acc + _dot(p.astype(BF16), vwin)
        psum = psum + p[:TQ] + p[TQ:]
    oc_ref[:, :LANES] = acc[:TQ]
    oc_ref[:, LANES:] = acc[TQ:]

    NR = A_HEAD_DIM
    jj = lax.broadcasted_iota(jnp.int32, (NR, NC), 0)
    nn = lax.broadcasted_iota(jnp.int32, (NR, NC), 1) + (r * (qt + 1) - NC)
    delta = 4 * jj - nn
    mt = jnp.where((delta == 0) | (delta == 4), 1.0, 0.0) + jnp.where((delta >= 1) & (delta <= 3), 2.0, 0.0)
    mt = jnp.where(jj < NS, mt, 0.0)
    mtb = mt.astype(BF16)
    imp_t = sum(_dot_t(mtb, t) for t in _split3(psum))
    jb = lax.broadcasted_iota(jnp.int32, (NR, TQ), 0)
    tpos = qt * TQ + lax.broadcasted_iota(jnp.int32, (NR, TQ), 1)
    back = (tpos >> 6) - jb
    forced = (jb == 0) | ((back >= 0) & (back < N_LOCAL_BLOCKS))
    score = jnp.where(forced, FORCE_SCORE, jnp.where(back >= 0, imp_t, -1.0))
    score = jnp.where(jb < NS, score, -2.0)
    ngrp = -(-NS // 8)
    grp = [score[8 * m:8 * m + 8, :] for m in range(ngrp)]
    rank = [jnp.zeros((8, TQ), F32) for _ in range(ngrp)]
    j8 = lax.broadcasted_iota(jnp.int32, (8, TQ), 0)
    for jp in range(NS):
        row = score[jp:jp + 1, :]
        for m in range(ngrp):
            if m < jp // 8:
                beats = row > grp[m]
            elif m > jp // 8:
                beats = row >= grp[m]
            else:
                beats = (row > grp[m]) | ((row == grp[m]) & (j8 > jp - 8 * m))
            rank[m] = rank[m] + jnp.where(beats, 1.0, 0.0)
    rank = jnp.concatenate(rank + [jnp.zeros((NR - 8 * ngrp, TQ), F32)] * (NR > 8 * ngrp), axis=0)
    sel = (rank < float(min(SLC_TOPK, NS))) & (back >= 0) & (jb < NS)
    sb = jnp.where(sel, 0.0, jnp.where(jb < NS, NEG, 0.0))
    sb_ref[...] = jnp.concatenate([sb, jnp.zeros((LANES - NR, TQ), F32)], axis=0).astype(BF16)


def _cmpsel(z, kcp, vcp, cb, B, S):
    NC = S // CMP_STRIDE
    NS = S // SLC_BLOCK
    nq = S // TQ
    G = A_KV_GROUPS
    kspec = pl.BlockSpec((None, G, 2, 2 * NC, LANES), lambda b, q: (b, 0, 0, 0, 0))
    return pl.pallas_call(
        functools.partial(_cmpsel_kernel, NC=NC, NS=NS),
        grid=(B, nq),
        in_specs=[pl.BlockSpec((TQ, A_Q), lambda b, q: (b * nq + q, 0)),
                  kspec, kspec,
                  pl.BlockSpec((G, 2, 2 * TQ, NC), lambda b, q: (0, 0, 0, 0))],
        out_specs=[pl.BlockSpec((TQ, A_Q), lambda b, q: (b * nq + q, 0)),
                   pl.BlockSpec((None, G, LANES, TQ), lambda b, q: (b, 0, 0, q))],
        out_shape=[jax.ShapeDtypeStruct((B * S, A_Q), F32),
                   jax.ShapeDtypeStruct((B, G, LANES, S), BF16)],
        compiler_params=_cparams("parallel", "parallel"),
        name="nsa_cmp_select",
    )(z, kcp, vcp, cb)


def _flash_update_t(s, vt, state, acc_ref):
    m, l = state
    m_new = jnp.maximum(m, jnp.max(s, axis=0, keepdims=True))
    p = jnp.exp2(s - m_new)
    alpha = jnp.exp2(m - m_new)
    acc_ref[...] = alpha * acc_ref[...] + _dot(vt, p.astype(BF16))
    return m_new, alpha * l + jnp.sum(p, axis=0, keepdims=True)


def _flash_pipeline(qk, softmax, lo, hi, state, s_ref, diag_in_loop):
    s_ref[0] = qk(0, lo)

    def step(kc, st):
        s_ref[1] = qk(1, kc)
        st0 = softmax(0, s_ref[0], kc, st[0], False)
        s_ref[0] = qk(0, jnp.minimum(kc + 1, hi))
        st1 = softmax(1, s_ref[1], kc, st[1], False)
        return st0, st1

    end = hi + 1 if diag_in_loop else hi
    npair = (end - lo) // 2
    st = lax.fori_loop(0, npair, lambda i, st: step(lo + 2 * i + 1, step(lo + 2 * i, st)), state)
    st = lax.fori_loop(lo + 2 * npair, end, step, st)
    if diag_in_loop:
        return st
    s_ref[1] = qk(1, hi)
    st0 = softmax(0, s_ref[0], hi, st[0], True)
    st1 = softmax(1, s_ref[1], hi, st[1], True)
    return st0, st1


def _split_heads_kv(slab, g):
    lane = lax.broadcasted_iota(jnp.int32, slab.shape, 1)
    own = jnp.where((lane < A_HEAD_DIM) == (g == 0), slab, 0.0)
    other = pltpu.roll(own, A_HEAD_DIM, axis=1)
    is0 = g == 0
    return jnp.where(is0, own, other), jnp.where(is0, other, own)


def _nsa_flash_kernel(zq_ref, sbt_ref, zks_ref, zvs_ref, zkw_ref, zvw_ref, tab_ref, os_ref, ow_ref,
                      kse_ref, kso_ref, kwe_ref, kwo_ref, vts_ref, vtw_ref, acc_ref, s_ref, *, S):
    g = pl.program_id(1)
    qt = pl.program_id(2)

    @pl.when(qt == 0)
    def _():
        rowb = lax.broadcasted_iota(jnp.int32, (S, LANES), 0) >> 6
        lane = lax.broadcasted_iota(jnp.int32, (S, LANES), 1)
        onehot = jnp.where(rowb == lane, 1.0, 0.0).astype(BF16)
        for zk_ref, zv_ref, ke_ref, ko_ref, vt_ref in ((zks_ref, zvs_ref, kse_ref, kso_ref, vts_ref),
                                                      (zkw_ref, zvw_ref, kwe_ref, kwo_ref, vtw_ref)):
            ke, ko = _split_heads_kv(zk_ref[...].astype(F32), g)
            ke_ref[:, :LANES] = ke.astype(BF16)
            ko_ref[:, :LANES] = ko.astype(BF16)
            for c in range(S // TQ):
                vt_ref[c] = zv_ref[c * TQ:(c + 1) * TQ, :].astype(F32).T.astype(BF16)
        kse_ref[:, LANES:] = onehot
        kso_ref[:, LANES:] = onehot

    zq = zq_ref[...].astype(F32) * (A_HEAD_DIM ** -0.5 * LOG2E)
    rhs_q = jnp.concatenate([zq[:, :LANES], zq[:, LANES:]], axis=0).T.astype(BF16)
    sbt = sbt_ref[...]
    rhs_sel = jnp.concatenate([rhs_q, jnp.concatenate([sbt, sbt], axis=1)], axis=0)
    vrow = pl.multiple_of(g * A_HEAD_DIM, A_HEAD_DIM)
    init = (jnp.full((1, 2 * TQ), 0.5 * NEG, F32), jnp.zeros((1, 2 * TQ), F32))

    def branch(k_refs, vt_ref, rhs, lo, table_of, o_ref):
        acc_ref[...] = jnp.zeros(acc_ref.shape, F32)

        def qk(eo, kc):
            ks = pl.multiple_of(kc * TQ, TQ)
            return _dot(k_refs[eo][pl.ds(ks, TQ), :], rhs)

        def softmax(eo, s, kc, state, diag):
            vt = vt_ref[kc, pl.ds(vrow, A_HEAD_DIM), :]
            return _flash_update_t(s + tab_ref[eo, table_of(qt - kc)], vt, state, acc_ref.at[eo])

        carry = _flash_pipeline(qk, softmax, lo, qt, (init, init), s_ref, True)
        out_t = jnp.concatenate([acc_ref[0] * (1.0 / carry[0][1]), acc_ref[1] * (1.0 / carry[1][1])], axis=0)
        out = out_t.T
        o_ref[:, :LANES] = out[:TQ]
        o_ref[:, LANES:] = out[TQ:]

    branch((kse_ref, kso_ref), vts_ref, rhs_sel, 0, lambda d: jnp.minimum(d, 2), os_ref)
    branch((kwe_ref, kwo_ref), vtw_ref, rhs_q, jnp.maximum(qt - 2, 0),
           lambda d: jnp.where(d == 2, 3, d), ow_ref)


def _nsa_flash(z, sb, tab, B, S):
    nq = S // TQ
    zslab = lambda col: pl.BlockSpec((S, LANES), lambda b, g, q: (b, col // LANES))
    ospec = pl.BlockSpec((TQ, 2 * LANES), lambda b, g, q: (b * nq + q, g))
    oshape = jax.ShapeDtypeStruct((B * S, A_Q), F32)
    return pl.pallas_call(
        functools.partial(_nsa_flash_kernel, S=S),
        grid=(B, A_KV_GROUPS, nq),
        in_specs=[pl.BlockSpec((TQ, 2 * LANES), lambda b, g, q: (b * nq + q, g)),
                  pl.BlockSpec((None, None, LANES, TQ), lambda b, g, q: (b, g, 0, q)),
                  zslab(Z_AKS), zslab(Z_AVS), zslab(Z_AKW), zslab(Z_AVW),
                  pl.BlockSpec((None, 2, 4, TQ, 2 * TQ), lambda b, g, q: (g, 0, 0, 0, 0))],
        out_specs=[ospec, ospec],
        out_shape=[oshape, oshape],
        scratch_shapes=[pltpu.VMEM((S, 2 * LANES), BF16), pltpu.VMEM((S, 2 * LANES), BF16),
                        pltpu.VMEM((S, LANES), BF16), pltpu.VMEM((S, LANES), BF16),
                        pltpu.VMEM((S // TQ, LANES, TQ), BF16), pltpu.VMEM((S // TQ, LANES, TQ), BF16),
                        pltpu.VMEM((2, A_HEAD_DIM, 2 * TQ), F32),
                        pltpu.VMEM((2, TQ, 2 * TQ), F32)],
        compiler_params=_cparams("parallel", "parallel", "arbitrary"),
        name="nsa_selected_window",
    )(z, sb, z, z, z, z, tab)


def _mla_proj_kernel(zqa_ref, zkva_ref, zkra_ref, zkrb_ref, qn_ref, kvn_ref, wqa_ref, wqb_ref,
                     wk_ref, wv_ref, caq_ref, cbq_ref, cak_ref, cbk_ref, q_ref, k_ref, v_ref):
    def rms(x, gain):
        return x * lax.rsqrt(jnp.mean(x * x, axis=-1, keepdims=True) + EPS) * gain

    nq = rms(zqa_ref[...].astype(F32), qn_ref[...]).astype(BF16)
    nkv = rms(zkva_ref[...].astype(F32), kvn_ref[...]).astype(BF16)
    qa = _dot_t(wqa_ref[...], nq)
    qb = _dot_t(wqb_ref[...], nq)
    kn = _dot(nkv, wk_ref[...])
    vt = _dot_t(wv_ref[...], nkv)
    caq, cbq = caq_ref[...], cbq_ref[...]
    kr = zkra_ref[...].astype(F32) * cak_ref[...] + zkrb_ref[...].astype(F32) * cbk_ref[...]
    for h in range(B_HEADS):
        sl = slice(h * LANES, (h + 1) * LANES)
        q_ref[h] = (qa[sl, :] * caq + qb[sl, :] * cbq).astype(BF16)
        k_ref[h] = (kn[:, sl] + kr).astype(BF16)
        v_ref[h] = vt[h * V_DIM:(h + 1) * V_DIM, :].astype(BF16)


def _mla_proj(z, qn, kvn, wqa, wqb, wk, wv, caq, cbq, cak, cbk, B, S):
    tm = TM
    nt = S // tm
    zspec = lambda w, col: pl.BlockSpec((tm, w), lambda b, i: (b * nt + i, col // w))
    full = lambda a: pl.BlockSpec(a.shape, lambda b, i: (0,) * a.ndim)
    tspec = pl.BlockSpec((tm, LANES), lambda b, i: (i, 0))
    tspec_t = pl.BlockSpec((LANES, tm), lambda b, i: (0, i))
    return pl.pallas_call(
        _mla_proj_kernel,
        grid=(B, nt),
        in_specs=[zspec(Q_LORA, Z_QA), zspec(LANES, Z_KVA), zspec(LANES, Z_KRA), zspec(LANES, Z_KRB),
                  full(qn), full(kvn), full(wqa), full(wqb), full(wk), full(wv),
                  tspec_t, tspec_t, tspec, tspec],
        out_specs=[pl.BlockSpec((None, B_HEADS, LANES, tm), lambda b, i: (b, 0, 0, i)),
                   pl.BlockSpec((None, B_HEADS, tm, LANES), lambda b, i: (b, 0, i, 0)),
                   pl.BlockSpec((None, B_HEADS, None, V_DIM, tm), lambda b, i: (b, 0, i, 0, 0))],
        out_shape=[jax.ShapeDtypeStruct((B, B_HEADS, LANES, S), BF16),
                   jax.ShapeDtypeStruct((B, B_HEADS, S, LANES), BF16),
                   jax.ShapeDtypeStruct((B, B_HEADS, nt, V_DIM, tm), BF16)],
        compiler_params=_cparams("parallel", "parallel"),
        name="mla_proj",
    )(z, z, z, z, qn, kvn, wqa, wqb, wk, wv, caq, cbq, cak, cbk)


def _mla_flash_kernel(qt_ref, k_ref, vt_ref, o_ref, acc_ref, s_ref):
    qi = pl.program_id(2)
    ri = lax.broadcasted_iota(jnp.int32, (TM, TM), 0)
    ci = lax.broadcasted_iota(jnp.int32, (TM, TM), 1)
    causal = ri <= ci
    init = (jnp.full((1, TM), 0.5 * NEG, F32), jnp.zeros((1, TM), F32))

    for pr in range(MLA_PAIRS):
        acc_ref[...] = jnp.zeros(acc_ref.shape, F32)

        def qk(hh, kc, pr=pr):
            ks = pl.multiple_of(kc * TM, TM)
            return _dot(k_ref[2 * pr + hh, pl.ds(ks, TM), :], qt_ref[2 * pr + hh])

        def softmax(hh, s, kc, state, diag, pr=pr):
            if diag:
                s = jnp.where(causal, s, NEG)
            return _flash_update_t(s, vt_ref[2 * pr + hh, kc], state, acc_ref.at[hh])

        carry = _flash_pipeline(qk, softmax, 0, qi, (init, init), s_ref, False)
        out_t = jnp.concatenate([acc_ref[0] * (1.0 / carry[0][1]), acc_ref[1] * (1.0 / carry[1][1])], axis=0)
        o_ref[:, pr * LANES:(pr + 1) * LANES] = out_t.T


def _mla_flash(qt, k, vt, B, S):
    nq = S // TM
    hb = 2 * MLA_PAIRS
    hp = B_HEADS // hb
    return pl.pallas_call(
        _mla_flash_kernel,
        grid=(B, hp, nq),
        in_specs=[pl.BlockSpec((None, hb, LANES, TM), lambda b, h, i: (b, h, 0, i)),
                  pl.BlockSpec((None, hb, S, LANES), lambda b, h, i: (b, h, 0, 0)),
                  pl.BlockSpec((None, hb, nq, V_DIM, TM), lambda b, h, i: (b, h, 0, 0, 0))],
        out_specs=pl.BlockSpec((TM, MLA_PAIRS * LANES), lambda b, h, i: (b * nq + i, h)),
        out_shape=jax.ShapeDtypeStruct((B * S, B_HEADS * V_DIM), F32),
        scratch_shapes=[pltpu.VMEM((2, V_DIM, TM), F32), pltpu.VMEM((2, TM, TM), F32)],
        compiler_params=_cparams("parallel", "parallel", "arbitrary"),
        name="mla_flash",
    )(qt, k, vt)


def _rwkv_prep_kernel(zr_ref, zk_ref, zv_ref, zwa_ref, zg_ref, mu_r, mu_k, mu_v, mu_wa, mu_g,
                      w0_ref, a0_ref, kk_ref, ka_ref, rk_ref, w2_ref, a2_ref, g2_ref, bd_ref,
                      r_o, k_o, v_o, kk_o, be_o, ld_o, g_o, bo_o,
                      c_r, c_k, c_v, c_wa, c_g):
    t = pl.program_id(1)
    tm = zr_ref.shape[0]

    def shifted(z_ref, mu_ref, c_ref):
        x = z_ref[...].astype(F32)
        row = lax.broadcasted_iota(jnp.int32, x.shape, 0)
        prev = jnp.where(t == 0, 0.0, c_ref[0:1, :])
        xs = jnp.where(row == 0, prev, pltpu.roll(x, 1, axis=0))
        c_ref[0:1, :] = x[tm - 1:tm, :]
        return x + (xs - x) * mu_ref[...]

    r = shifted(zr_ref, mu_r, c_r)
    k = shifted(zk_ref, mu_k, c_k)
    v = shifted(zv_ref, mu_v, c_v)
    wa = shifted(zwa_ref, mu_wa, c_wa)
    gd = shifted(zg_ref, mu_g, c_g)
    w = w0_ref[...] + _dot(jnp.tanh(wa).astype(BF16), w2_ref[...])
    ld_o[...] = -jax.nn.sigmoid(w) * math.exp(-0.5)
    a = jax.nn.sigmoid(a0_ref[...] + _dot(wa.astype(BF16), a2_ref[...]))
    g_o[...] = _dot(jax.nn.sigmoid(gd).astype(BF16), g2_ref[...])
    kk = k * kk_ref[...]
    nsq = _dot3(kk * kk, bd_ref[...])
    kk = kk / jnp.maximum(jnp.sqrt(nsq), 1e-12)
    k2 = k * (1.0 + (a - 1.0) * ka_ref[...])
    rks = _dot3(r * k2 * rk_ref[...], bd_ref[...])
    r_o[...] = r
    k_o[...] = k2
    v_o[...] = v
    kk_o[...] = kk
    be_o[...] = kk * a
    bo_o[...] = rks * v


def _rwkv_prep(z, mus, w0, a0, k_k, k_a, r_k, w2p, a2p, g2, bd, B, S, tm=512):
    nt = S // tm
    W = C_WIDTH
    zspec = lambda w, col: pl.BlockSpec((tm, w), lambda b, i: (b * nt + i, col // w))
    full = lambda a: pl.BlockSpec(a.shape, lambda b, i: (0,) * a.ndim)
    ospec = pl.BlockSpec((tm, W), lambda b, i: (b * nt + i, 0))
    oshape = jax.ShapeDtypeStruct((B * S, W), F32)
    consts = list(mus) + [w0, a0, k_k, k_a, r_k, w2p, a2p, g2, bd]
    return pl.pallas_call(
        _rwkv_prep_kernel,
        grid=(B, nt),
        in_specs=[zspec(W, Z_CR), zspec(W, Z_CK), zspec(W, Z_CV), zspec(LANES, Z_CWA), zspec(LANES, Z_CG)]
                 + [full(a) for a in consts],
        out_specs=[ospec] * 8,
        out_shape=[oshape] * 8,
        scratch_shapes=[pltpu.VMEM((8, W), F32)] * 3 + [pltpu.VMEM((8, LANES), F32)] * 2,
        compiler_params=_cparams("parallel", "arbitrary"),
        name="rwkv_prep",
    )(z, z, z, z, z, *consts)


def _rwkv_chunk_kernel(r_ref, k_ref, v_ref, kk_ref, be_ref, ld_ref,
                       qa_o, bkt_o, yu_o, vp_o, pcb_o):
    C = CH
    R = ld_ref.shape[0]
    ld = ld_ref[...]
    rr = lax.broadcasted_iota(jnp.int32, (R, R), 0)
    cc = lax.broadcasted_iota(jnp.int32, (R, R), 1)
    sh = int(math.log2(C))
    tri = jnp.where((cc <= rr) & ((cc >> sh) == (rr >> sh)), 1.0, 0.0).astype(BF16)
    cs = sum(_dot(tri, t) for t in _split3(ld))
    cl = jnp.concatenate([jnp.broadcast_to(cs[(s + 1) * C - 1:(s + 1) * C, :], (C, C_WIDTH))
                          for s in range(R // C)], axis=0)
    ex, exn, exx, exc = jnp.exp(cs), jnp.exp(-cs), jnp.exp(cs - ld), jnp.exp(cl - cs)
    kk, be, k2, v = kk_ref[...], be_ref[...], k_ref[...], v_ref[...]
    at = -kk * exx
    rt = r_ref[...] * ex
    bt = be * exn
    kt = k2 * exn
    btc = be * exc
    ktc = k2 * exc
    pc = jnp.exp(cl)
    ri = lax.broadcasted_iota(jnp.int32, (C, C), 0)
    ci = lax.broadcasted_iota(jnp.int32, (C, C), 1)
    incl = ci <= ri
    strict = ci < ri
    eye = jnp.where(ci == ri, 1.0, 0.0)
    lane = lax.broadcasted_iota(jnp.int32, (C, LANES), 1)
    P = C_HEADS // 2
    pairs = [(s, p) for s in range(R // C) for p in range(P)]
    heads = [(i, hh) for i in range(len(pairs)) for hh in range(2)]
    blk = lambda x, i: x[pairs[i][0] * C:(pairs[i][0] + 1) * C, pairs[i][1] * LANES:(pairs[i][1] + 1) * LANES]
    lhs2 = [jnp.concatenate([blk(at, i), blk(rt, i)], axis=0).astype(BF16) for i in range(len(pairs))]
    bm, km, vm, am = [], [], [], []
    for i, hh in heads:
        msk = (lane < C_HEAD_DIM) if hh == 0 else (lane >= C_HEAD_DIM)
        bm.append(jnp.where(msk, blk(bt, i), 0.0).astype(BF16))
        km.append(jnp.where(msk, blk(kt, i), 0.0).astype(BF16))
        vm.append(jnp.where(msk, blk(v, i), 0.0).astype(BF16))
        am.append(jnp.where(msk, blk(at, i), 0.0).astype(BF16))
    g1 = [_dot_t(lhs2[i], bm[j]) for j, (i, hh) in enumerate(heads)]
    g2 = [_dot_t(lhs2[i], km[j]) for j, (i, hh) in enumerate(heads)]
    lab = [jnp.where(strict, g[:C], 0.0) for g in g1]
    mrb = [jnp.where(incl, g[C:], 0.0).astype(BF16) for g in g1]
    lak = [jnp.where(strict, g[:C], 0.0).astype(BF16) for g in g2]
    mrk = [jnp.where(incl, g[C:], 0.0).astype(BF16) for g in g2]
    w2 = [_dot(a, b) for a, b in zip(lak, vm)]
    yk = [_dot(a, b) for a, b in zip(mrk, vm)]
    tinv = [eye + x for x in lab]
    lp = lab
    for _ in range(int(math.log2(C)) - 1):
        lpb = [x.astype(BF16) for x in lp]
        lp = [_dot(x, x) for x in lpb]
        tinv = [t + _dot(t.astype(BF16), x.astype(BF16)) for t, x in zip(tinv, lp)]
    au = [_dot(t.astype(BF16), jnp.concatenate([a, w.astype(BF16)], axis=1)) for t, a, w in zip(tinv, am, w2)]
    qy = [_dot(m, x.astype(BF16)) for m, x in zip(mrb, au)]
    for i, (s, p) in enumerate(pairs):
        e, o = 2 * i, 2 * i + 1
        ahat = au[e][:, :LANES] + au[o][:, :LANES]
        uhat = au[e][:, LANES:] + au[o][:, LANES:]
        qhat = blk(rt, i) + qy[e][:, :LANES] + qy[o][:, :LANES]
        yi = qy[e][:, LANES:] + qy[o][:, LANES:] + yk[e] + yk[o]
        qa_o[s, p] = jnp.concatenate([qhat, ahat], axis=0).astype(BF16)
        yu_o[s, p] = jnp.concatenate([yi, uhat], axis=0)
        bkt_o[s, p] = jnp.concatenate([blk(btc, i), blk(ktc, i)], axis=0).T.astype(BF16)
        vp_o[s, p] = blk(v, i).astype(BF16)
        pcb_o[s, p] = jnp.concatenate([blk(pc, i), blk(pc, i)], axis=0).T


def _rwkv_chunk(r, k2, v, kk, be, ld, B, S):
    nch = S // CH
    W = C_WIDTH
    P = C_HEADS // 2
    ns = RW_SUB
    ispec = pl.BlockSpec((ns * CH, W), lambda b, c: (b * (nch // ns) + c, 0))
    sq = lambda rows: pl.BlockSpec((None, ns, P, rows, LANES), lambda b, c: (b, c, 0, 0, 0))
    shp = lambda rows, dt: jax.ShapeDtypeStruct((B, nch, P, rows, LANES), dt)
    return pl.pallas_call(
        _rwkv_chunk_kernel,
        grid=(B, nch // ns),
        in_specs=[ispec] * 6,
        out_specs=[sq(2 * CH), sq(LANES), sq(2 * CH), sq(CH), sq(LANES)],
        out_shape=[shp(2 * CH, BF16), shp(LANES, BF16), shp(2 * CH, F32), shp(CH, BF16), shp(LANES, F32)],
        compiler_params=_cparams("parallel", "parallel"),
        name="rwkv_chunk",
    )(r, k2, v, kk, be, ld)


def _rwkv_state_kernel(qa_ref, bkt_ref, yu_ref, vp_ref, pcb_ref, y_ref, ap_ref):
    @pl.when(pl.program_id(1) == 0)
    def _():
        ap_ref[...] = jnp.zeros(ap_ref.shape, F32)

    ri = lax.broadcasted_iota(jnp.int32, (LANES, LANES), 0)
    ci = lax.broadcasted_iota(jnp.int32, (LANES, LANES), 1)
    same_head = (ri < C_HEAD_DIM) == (ci < C_HEAD_DIM)
    idx = [(b, p) for b in range(qa_ref.shape[0]) for p in range(C_HEADS // 2)]
    a = [ap_ref[b, p] for b, p in idx]
    for c in range(qa_ref.shape[1]):
        x = [_dot(qa_ref[b, c, p], s.astype(BF16)) for (b, p), s in zip(idx, a)]
        yu = [yu_ref[b, c, p] for b, p in idx]
        for (b, p), xi, yi in zip(idx, x, yu):
            y_ref[b, c * CH:(c + 1) * CH, p * LANES:(p + 1) * LANES] = xi[:CH] + yi[:CH]
        uv = [jnp.concatenate([(xi[CH:] + yi[CH:]).astype(BF16), vp_ref[b, c, p]], axis=0)
              for (b, p), xi, yi in zip(idx, x, yu)]
        upd = [_dot(bkt_ref[b, c, p], t) for (b, p), t in zip(idx, uv)]
        a = [pcb_ref[b, c, p] * s + jnp.where(same_head, t, 0.0) for (b, p), s, t in zip(idx, a, upd)]
    for (b, p), s in zip(idx, a):
        ap_ref[b, p] = s


def _rwkv_state(qa, bkt, yu, vp, pcb, B, S):
    nch = S // CH
    P = C_HEADS // 2
    nb = 2 if B % 2 == 0 else 1
    nc = 4 if nch % 4 == 0 else 1
    sq = lambda rows: pl.BlockSpec((nb, nc, P, rows, LANES), lambda b, c: (b, c, 0, 0, 0))
    return pl.pallas_call(
        _rwkv_state_kernel,
        grid=(B // nb, nch // nc),
        in_specs=[sq(2 * CH), sq(LANES), sq(2 * CH), sq(CH), sq(LANES)],
        out_specs=pl.BlockSpec((nb, nc * CH, C_WIDTH), lambda b, c: (b, c, 0)),
        out_shape=jax.ShapeDtypeStruct((B, S, C_WIDTH), F32),
        scratch_shapes=[pltpu.VMEM((nb, P, LANES, LANES), F32)],
        compiler_params=_cparams("parallel", "arbitrary"),
        name="rwkv_state",
    )(qa, bkt, yu, vp, pcb).reshape(B * S, C_WIDTH)


def _merge_kernel(oc_ref, os_ref, ow_ref, zag_ref, yb_ref, yr_ref, bo_ref, gg_ref,
                  zga_ref, zgb_ref, zgc_ref, x_ref, wbr_ref, wout_ref, ge_ref, bdm_ref,
                  ln_ref, gain_ref, gt_ref, o_ref):
    sg = jax.nn.sigmoid(zag_ref[...].astype(F32))
    ya = (_dot2(sg, ge_ref[0]) * oc_ref[...] + _dot2(sg, ge_ref[1]) * os_ref[...]
          + _dot2(sg, ge_ref[2]) * ow_ref[...])
    yr = yr_ref[...]
    mean = _dot3(yr, bdm_ref[...])
    d = yr - mean
    var = _dot2(d * d, bdm_ref[...])
    yn = d * lax.rsqrt(var + GN_EPS) * ln_ref[0:1, :] + ln_ref[1:2, :]
    yc = (yn + bo_ref[...]) * gg_ref[...]
    sig = lambda ref: jax.nn.sigmoid(ref[...].astype(F32))
    merged = (sig(zga_ref) * _dot(ya.astype(BF16), wbr_ref[0:A_Q, :])
              + sig(zgb_ref) * _dot(yb_ref[...].astype(BF16), wbr_ref[A_Q:2 * A_Q, :])
              + sig(zgc_ref) * _dot(yc.astype(BF16), wbr_ref[2 * A_Q:3 * A_Q, :]))
    y = _dot(merged.astype(BF16), wout_ref[...])
    yn2 = y * lax.rsqrt(jnp.mean(y * y, axis=-1, keepdims=True) + EPS) * gain_ref[...]
    o_ref[...] = x_ref[...] + gt_ref[0] * yn2


def _merge(oc, os_, ow, z, yb, yr, bo, gg, x2, wbr, wout, gexp, bdm, ln, gain, gt, S, tm=512):
    N, D = x2.shape
    tpb = S // tm
    W = A_Q
    row = lambda w: pl.BlockSpec((tm, w), lambda i: (i, 0))
    zspec = lambda w, col: pl.BlockSpec((tm, w), lambda i: (i, col // w))
    full = lambda a: pl.BlockSpec(a.shape, lambda i: (0,) * a.ndim)
    return pl.pallas_call(
        _merge_kernel,
        grid=(N // tm,),
        in_specs=[row(W), row(W), row(W), zspec(LANES, Z_AG), row(W), row(W), row(W), row(W),
                  zspec(D, Z_ZG), zspec(D, Z_ZG + D), zspec(D, Z_ZG + 2 * D), row(D),
                  full(wbr), full(wout), full(gexp), full(bdm), full(ln),
                  pl.BlockSpec((1, D), lambda i: (0, 0)),
                  pl.BlockSpec((1, 1, D), lambda i: (i // tpb, 0, 0))],
        out_specs=row(D),
        out_shape=jax.ShapeDtypeStruct((N, D), F32),
        compiler_params=_cparams("parallel"),
        name="merge_out",
    )(oc, os_, ow, z, yb, yr, bo, gg, z, z, z, x2, wbr, wout, gexp, bdm, ln, gain.reshape(1, D), gt)


def _ffn_up_kernel(x_ref, g_ref, sc_ref, sh_ref, wg_ref, wv_ref, cw_ref, cb_ref, o_ref,
                   cg_ref, cv_ref, *, tpb):
    i = pl.program_id(0)
    tm = x_ref.shape[0]
    fc = FFN_SUB
    first = (i % tpb) == 0
    x = x_ref[...]
    y = x * lax.rsqrt(jnp.mean(x * x, axis=-1, keepdims=True) + EPS)
    h = ((y * g_ref[...]) * (1.0 + sc_ref[0]) + sh_ref[0]).astype(BF16)
    row = lax.broadcasted_iota(jnp.int32, (8, fc), 0)

    def up(c):
        sl = slice(c * fc, (c + 1) * fc)
        return _dot(h, wg_ref[:, sl]), _dot(h, wv_ref[:, sl])

    def conv(u, c_ref, c, off):
        sl = slice(c * fc, (c + 1) * fc)
        wl = slice(off + c * fc, off + (c + 1) * fc)
        prev = c_ref[:, sl]
        p1 = jnp.where(first, 0.0, prev[7:8])
        p2 = jnp.where(first, 0.0, prev[6:7])
        c_ref[:, sl] = u[tm - 8:tm, :]
        r1 = pltpu.roll(u, 1, axis=0)
        r2 = pltpu.roll(u, 2, axis=0)
        t1 = jnp.where(row == 0, p1, r1[0:8])
        t2 = jnp.where(row == 0, p2, jnp.where(row == 1, p1, r2[0:8]))
        u1 = jnp.concatenate([t1, r1[8:]], axis=0)
        u2 = jnp.concatenate([t2, r2[8:]], axis=0)
        return cw_ref[0:1, wl] * u2 + cw_ref[1:2, wl] * u1 + cw_ref[2:3, wl] * u + cb_ref[:, wl]

    nsub = D_FF // fc
    cur = up(0)
    for c in range(nsub):
        nxt = up(c + 1) if c + 1 < nsub else None
        a = _gelu(conv(cur[0], cg_ref, c, 0)) * conv(cur[1], cv_ref, c, D_FF)
        o_ref[:, c * fc:(c + 1) * fc] = a.astype(BF16)
        cur = nxt


def _ffn_up(x2, gain, sc, sh, w_up, conv_w, conv_b, S, tm=256):
    N, D = x2.shape
    tpb = S // tm
    full = lambda a: pl.BlockSpec(a.shape, lambda i: (0,) * a.ndim)
    return pl.pallas_call(
        functools.partial(_ffn_up_kernel, tpb=tpb),
        grid=(N // tm,),
        in_specs=[pl.BlockSpec((tm, D), lambda i: (i, 0)),
                  pl.BlockSpec((1, D), lambda i: (0, 0)),
                  pl.BlockSpec((1, 1, D), lambda i: (i // tpb, 0, 0)),
                  pl.BlockSpec((1, 1, D), lambda i: (i // tpb, 0, 0)),
                  pl.BlockSpec((D, D_FF), lambda i: (0, 0)),
                  pl.BlockSpec((D, D_FF), lambda i: (0, 1)),
                  full(conv_w), pl.BlockSpec((1, 2 * D_FF), lambda i: (0, 0))],
        out_specs=pl.BlockSpec((tm, D_FF), lambda i: (i, 0)),
        out_shape=jax.ShapeDtypeStruct((N, D_FF), BF16),
        scratch_shapes=[pltpu.VMEM((8, D_FF), F32), pltpu.VMEM((8, D_FF), F32)],
        compiler_params=_cparams("arbitrary"),
        name="ffn_up",
    )(x2, gain.reshape(1, D), sc, sh, w_up, w_up, conv_w, conv_b.reshape(1, 2 * D_FF))


def _ffn_down_kernel(a_ref, wd_ref, x_ref, gain_ref, gt_ref, o_ref):
    f = _dot(a_ref[...], wd_ref[...])
    fn = f * lax.rsqrt(jnp.mean(f * f, axis=-1, keepdims=True) + EPS) * gain_ref[...]
    o_ref[...] = x_ref[...] + gt_ref[0] * fn


def _ffn_down(a, wd, x2, gain, gt, S, tm=512):
    N, D = x2.shape
    tpb = S // tm
    return pl.pallas_call(
        _ffn_down_kernel,
        grid=(N // tm,),
        in_specs=[pl.BlockSpec((tm, D_FF), lambda i: (i, 0)),
                  pl.BlockSpec((D_FF, D), lambda i: (0, 0)),
                  pl.BlockSpec((tm, D), lambda i: (i, 0)),
                  pl.BlockSpec((1, D), lambda i: (0, 0)),
                  pl.BlockSpec((1, 1, D), lambda i: (i // tpb, 0, 0))],
        out_specs=pl.BlockSpec((tm, D), lambda i: (i, 0)),
        out_shape=jax.ShapeDtypeStruct((N, D), F32),
        compiler_params=_cparams("parallel"),
        name="ffn_down",
    )(a, wd, x2, gain.reshape(1, D), gt)


def _t5_bucket_np(dist):
    d = np.maximum(dist, 0)
    max_exact = N_BUCKETS // 2
    large = max_exact + (np.log(np.maximum(d, 1).astype(np.float32) / max_exact)
                         / math.log(MAX_DISTANCE / max_exact) * (N_BUCKETS - max_exact)).astype(np.int32)
    return np.where(d < max_exact, d, np.minimum(large, N_BUCKETS - 1))


def _bias_table_kernel(scale_ref, rb_ref, code_ref, o_ref):
    code = code_ref[0]
    sc = scale_ref[pl.program_id(0)]
    for h in range(A_HEADS):
        acc = jnp.full(code.shape, NEG, F32)
        for b in range(N_BUCKETS):
            acc = jnp.where(code == b, rb_ref[b, h], acc)
        o_ref[0, h] = acc * sc


def _bias_tables(rel_bias, codes, scales):
    nt, R, C = codes.shape
    smem = pl.BlockSpec(memory_space=pltpu.SMEM)
    return pl.pallas_call(
        _bias_table_kernel,
        grid=(nt,),
        in_specs=[smem, smem, pl.BlockSpec((1, R, C), lambda t: (t, 0, 0))],
        out_specs=pl.BlockSpec((1, A_HEADS, R, C), lambda t: (t, 0, 0, 0)),
        out_shape=jax.ShapeDtypeStruct((nt, A_HEADS, R, C), F32),
        compiler_params=_cparams("parallel"),
        name="bias_tables",
    )(jnp.asarray(scales, F32), rel_bias, jnp.asarray(codes))


def _codes(dist, lo, hi):
    return np.where((dist >= lo) & (dist < hi), _t5_bucket_np(dist), N_BUCKETS).astype(np.int32)


def _nsa_tables(rel_bias, S):
    NC = S // CMP_STRIDE
    G, HPG = A_KV_GROUPS, A_HEADS // A_KV_GROUPS
    big = 1 << 30
    i = np.arange(TQ)[:, None]
    dc = i - CMP_STRIDE * np.arange(NC)[None, :] + CMP_STRIDE * NC - TQ - (CMP_BLOCK - 1)
    cb = _bias_tables(rel_bias, _codes(dc, 0, big)[None], [1.0])[0]
    cb = cb.reshape(G, HPG // 2, 2, TQ, NC).transpose(0, 2, 1, 3, 4).reshape(G, 2, 2 * TQ, NC)
    dt = np.arange(TQ)[None, :] - np.arange(TQ)[:, None]
    far = np.full((TQ, TQ), MAX_DISTANCE)
    codes = np.stack([_codes(dt, 0, WINDOW), _codes(TQ + dt, 0, WINDOW), _codes(far, 0, big),
                      _codes(2 * TQ + dt, 0, WINDOW)])
    t = _bias_tables(rel_bias, codes, [LOG2E] * 4)
    t = t.reshape(4, G, HPG // 2, 2, TQ, TQ).transpose(1, 3, 0, 4, 2, 5).reshape(G, 2, 4, TQ, 2 * TQ)
    return cb, t


def _pad_cols(w, n):
    return jnp.pad(w, ((0, 0), (0, n - w.shape[1])))


def _prep_w_in(w):
    D = w.shape[0]
    za, zb, zc, zg = jnp.split(w, np.cumsum([A_COLS, B_COLS, C_COLS]).tolist(), axis=1)
    a_parts = jnp.split(za, np.cumsum([A_Q] + [A_KV] * 6).tolist(), axis=1)
    qa, kva, kr = jnp.split(zb, [Q_LORA, Q_LORA + KV_LORA], axis=1)
    half = ROPE_DIM // 2
    kr_rot = jnp.concatenate([-kr[:, half:], kr[:, :half]], axis=1)
    z64 = jnp.zeros((D, NOPE_DIM), w.dtype)
    z32 = jnp.zeros((D, LANES - NOPE_DIM - ROPE_DIM), w.dtype)
    c_r, c_k, c_v, c_wd, c_ad, c_gd = jnp.split(
        zc, np.cumsum([C_WIDTH] * 3 + [DECAY_LORA, AAA_LORA]).tolist(), axis=1)
    cols = [a_parts[0], qa] + a_parts[1:7] + [
        _pad_cols(a_parts[7], LANES), kva,
        jnp.concatenate([z64, kr, z32], axis=1),
        jnp.concatenate([z64, kr_rot, z32], axis=1),
        c_r, c_k, c_v, c_wd, c_ad, c_gd, jnp.zeros((D, Z_ZG - Z_CG - LANES), w.dtype), zg]
    out = jnp.concatenate(cols, axis=1)
    assert out.shape[1] == Z_COLS
    return out.astype(BF16)


def _prep_mla(w_uq, w_ukv):
    dq = NOPE_DIM + ROPE_DIM
    half = ROPE_DIM // 2
    wq = w_uq.reshape(Q_LORA, B_HEADS, dq)
    nope, r1, r2 = wq[..., :NOPE_DIM], wq[..., NOPE_DIM:NOPE_DIM + half], wq[..., NOPE_DIM + half:]
    zq = jnp.zeros((Q_LORA, B_HEADS, LANES - dq), w_uq.dtype)
    wqa = jnp.concatenate([nope, r1, r2, zq], axis=-1).reshape(Q_LORA, B_HEADS * LANES)
    wqb = jnp.concatenate([jnp.zeros_like(nope), -r2, r1, zq], axis=-1).reshape(Q_LORA, B_HEADS * LANES)
    wkv = w_ukv.reshape(KV_LORA, B_HEADS, NOPE_DIM + V_DIM)
    kn, vv = wkv[..., :NOPE_DIM], wkv[..., NOPE_DIM:]
    wk = jnp.concatenate([kn, jnp.zeros_like(kn)], axis=-1).reshape(KV_LORA, B_HEADS * LANES)
    wv = vv.reshape(KV_LORA, B_HEADS * V_DIM)
    return wqa.T.astype(BF16), wqb.T.astype(BF16), wk.astype(BF16), wv.T.astype(BF16)


def _rope_tables(S):
    half = ROPE_DIM // 2
    inv = ROPE_THETA ** (-jnp.arange(half, dtype=F32) / half)
    ang = jnp.arange(S, dtype=F32)[:, None] * inv
    cos2 = jnp.tile(jnp.cos(ang), (1, 2))
    sin2 = jnp.tile(jnp.sin(ang), (1, 2))
    scale = (NOPE_DIM + ROPE_DIM) ** -0.5 * LOG2E
    one = jnp.ones((S, NOPE_DIM), F32)
    z64 = jnp.zeros((S, NOPE_DIM), F32)
    z32 = jnp.zeros((S, LANES - NOPE_DIM - ROPE_DIM), F32)
    caq = (jnp.concatenate([one, cos2, z32], axis=1) * scale).T
    cbq = (jnp.concatenate([z64, sin2, z32], axis=1) * scale).T
    cak = jnp.concatenate([z64, cos2, z32], axis=1)
    cbk = jnp.concatenate([z64, sin2, z32], axis=1)
    return caq, cbq, cak, cbk


def _prep_compress(cmp_pos, cmp_w1, cmp_w2):
    Dh = A_HEAD_DIM
    w1 = cmp_w1.reshape(2, CMP_BLOCK, Dh, Dh)
    z = jnp.zeros_like(w1)
    w1bd = jnp.concatenate([jnp.concatenate([w1, z], axis=-1), jnp.concatenate([z, w1], axis=-1)], axis=-2)
    posrow = jnp.broadcast_to(cmp_pos.reshape(2, 1, CMP_BLOCK * Dh), (2, 8, CMP_BLOCK * Dh))
    w1cat = jnp.concatenate([cmp_w1, cmp_w1], axis=-1)
    zz = jnp.zeros((2, Dh, Dh), cmp_w2.dtype)
    blk = lambda a, b, c, d: jnp.concatenate(
        [jnp.concatenate([a, b], axis=-1), jnp.concatenate([c, d], axis=-1)], axis=-2)
    w2v = jnp.stack([jnp.stack([blk(cmp_w2, zz, zz, zz), blk(zz, cmp_w2, zz, zz)], axis=1),
                     jnp.stack([blk(zz, zz, cmp_w2, zz), blk(zz, zz, zz, cmp_w2)], axis=1)], axis=1)
    return w1bd.astype(BF16), posrow, w1cat.astype(BF16), w2v.astype(BF16)


def _gate_expand():
    e = np.zeros((3, LANES, A_Q), np.float32)
    for h in range(A_HEADS):
        for r in range(3):
            e[r, 3 * h + r, h * A_HEAD_DIM:(h + 1) * A_HEAD_DIM] = 1.0
    return jnp.asarray(e, dtype=BF16)


def _block_diag_ones(scale):
    idx = np.arange(C_WIDTH) // C_HEAD_DIM
    return jnp.asarray((idx[:, None] == idx[None, :]).astype(np.float32) * scale, dtype=BF16)


def kernel(x, c, rel_bias, ada_w, ada_b, norm_gain, w_in, nsa_cmp_pos, nsa_cmp_w1, nsa_cmp_w2, mla_q_norm, mla_kv_norm, mla_w_uq, mla_w_ukv, rwkv_mu, rwkv_w0, rwkv_a0, rwkv_k_k, rwkv_k_a, rwkv_w2, rwkv_a2, rwkv_g2, rwkv_r_k, rwkv_ln, w_branch, w_out, ffn_up, ffn_conv_w, ffn_conv_b, ffn_down):
    B, S, D = x.shape
    L = ada_w.shape[0]
    assert S % TQ == 0 and S % TM == 0 and S // SLC_BLOCK <= A_HEAD_DIM and S >= 2 * TQ
    mod = _adaln(c, ada_w, ada_b)
    cb, nsa_tab = _nsa_tables(rel_bias, S)
    caq, cbq, cak, cbk = _rope_tables(S)
    gexp = _gate_expand()
    bd1 = _block_diag_ones(1.0)
    bdm = _block_diag_ones(1.0 / C_HEAD_DIM)
    row = lambda v: v.reshape(1, -1)
    x2 = x.reshape(B * S, D)
    for l in range(L):
        m6 = mod[l].reshape(B, 6, 1, D)
        sh1, sc1, gt1, sh2, sc2, gt2 = (m6[:, i] for i in range(6))
        z = _norm_mod_matmul(x2, norm_gain[l, 0], sc1, sh1, _prep_w_in(w_in[l]), S, tn=Z_COLS // 4)
        kcp, vcp = _compress(z, *_prep_compress(nsa_cmp_pos[l], nsa_cmp_w1[l], nsa_cmp_w2[l]), B, S)
        oc, sb = _cmpsel(z, kcp, vcp, cb, B, S)
        os_, ow = _nsa_flash(z, sb, nsa_tab, B, S)
        wqa, wqb, wk, wv = _prep_mla(mla_w_uq[l], mla_w_ukv[l])
        q, k, v = _mla_proj(z, row(mla_q_norm[l]), row(mla_kv_norm[l]), wqa, wqb, wk, wv,
                            caq, cbq, cak, cbk, B, S)
        yb = _mla_flash(q, k, v, B, S)
        mu = rwkv_mu[l]
        o = 3 * C_WIDTH
        mus = [row(mu[:C_WIDTH]), row(mu[C_WIDTH:2 * C_WIDTH]), row(mu[2 * C_WIDTH:o]),
               row(mu[o:o + LANES]), row(mu[o + LANES:])]
        w2p = jnp.concatenate([rwkv_w2[l], jnp.zeros_like(rwkv_a2[l])], axis=0).astype(BF16)
        a2p = jnp.concatenate([jnp.zeros_like(rwkv_w2[l]), rwkv_a2[l]], axis=0).astype(BF16)
        rr, k2, vv, kk, be, ld, gg, bo = _rwkv_prep(
            z, mus, row(rwkv_w0[l]), row(rwkv_a0[l]), row(rwkv_k_k[l]), row(rwkv_k_a[l]),
            row(rwkv_r_k[l]), w2p, a2p, rwkv_g2[l].astype(BF16), bd1, B, S)
        qa_, bkt, yu, vp, pcb = _rwkv_chunk(rr, k2, vv, kk, be, ld, B, S)
        yr = _rwkv_state(qa_, bkt, yu, vp, pcb, B, S)
        x2 = _merge(oc, os_, ow, z, yb, yr, bo, gg, x2, w_branch[l].astype(BF16), w_out[l].astype(BF16),
                    gexp, bdm, rwkv_ln[l], norm_gain[l, 1], gt1, S)
        a = _ffn_up(x2, norm_gain[l, 2], sc2, sh2, ffn_up[l].astype(BF16), ffn_conv_w[l], ffn_conv_b[l], S)
        x2 = _ffn_down(a, ffn_down[l].astype(BF16), x2, norm_gain[l, 3], gt2, S)
    return x2.reshape(B, S, D)
```
